```python
import jax, jax.numpy as jnp
from jax import lax
import numpy as np

D_MODEL = 1024
BATCH = 16
SEQ = 256
DEPTH = 2
DEC_BATCH = 4
DEC_SEQ = 4096
PAST_LEN = 512

GRID_W = 64
MIX_W = D_MODEL
GLA_W = MIX_W // 4
GLA_HEADS = 4
GLA_DK = GLA_W // GLA_HEADS
GLA_DV = GLA_W // GLA_HEADS
GLA_LR = 16
GLA_TAU = 16.0
GLA_CHUNK = 64
GMLP_W = MIX_W // 4
GMLP_GROUPS = 4
GMLP_CHUNK = 128
CONV_W = MIX_W // 4
FNET_W = MIX_W - GLA_W - GMLP_W - CONV_W
FNET_GROUPS = 4
IN_WIDTHS = (GLA_W, GLA_W, GLA_W, GLA_W, GLA_LR, GLA_LR, GMLP_W, GMLP_W, CONV_W, CONV_W, CONV_W, FNET_W)
IN_SPLITS = tuple(sum(IN_WIDTHS[:i + 1]) for i in range(len(IN_WIDTHS) - 1))
IN_W = sum(IN_WIDTHS)
N_EXPERTS = 64
TOP_K = 8
N_GROUPS = 8
TOPK_GROUPS = 4
EXPERTS_PER_GROUP = N_EXPERTS // N_GROUPS
EXPERT_FF = D_MODEL // 4
SHARED_FF = D_MODEL // 4
ROUTED_SCALE = 2.5
MOE_BLOCK = 128
DEEPNORM_ALPHA = (2 * DEPTH) ** 0.25
DEEPNORM_BETA = (8 * DEPTH) ** -0.25
LN_EPS = 1e-5
RMS_EPS = 1e-6
POS_BASE = 10000.0

kernel_name = "hybrid_gla_sgu_conv_fnet_moe_prefix_dit_step"


def layer_norm(x, g=None, b=None):
    xf = x.astype(jnp.float32)
    mu = jnp.mean(xf, axis=-1, keepdims=True)
    var = jnp.mean(jnp.square(xf - mu), axis=-1, keepdims=True)
    y = (xf - mu) * lax.rsqrt(var + LN_EPS)
    if g is not None:
        y = y * g.astype(jnp.float32) + b.astype(jnp.float32)
    return y.astype(x.dtype)


def grid_sincos(rows, d):
    quarter = d // 4
    omega = 1.0 / (POS_BASE ** (jnp.arange(quarter, dtype=jnp.float32) / quarter))
    r = jnp.repeat(jnp.arange(rows, dtype=jnp.float32), GRID_W)
    col = jnp.tile(jnp.arange(GRID_W, dtype=jnp.float32), rows)
    er = r[:, None] * omega
    ec = col[:, None] * omega
    return jnp.concatenate([jnp.sin(er), jnp.cos(er), jnp.sin(ec), jnp.cos(ec)], axis=-1)


def gla_chunked(q, k, v, log_a, s0):
    bsz, nh, n, dk = q.shape
    nc = n // GLA_CHUNK
    rs = lambda t: t.astype(jnp.float32).reshape(bsz, nh, nc, GLA_CHUNK, t.shape[-1])
    q, k, v, log_a = rs(q) * (dk ** -0.5), rs(k), rs(v), rs(log_a)
    b = jnp.cumsum(log_a, axis=3)
    b_last = b[:, :, :, -1:, :]
    q_dec = q * jnp.exp(b)
    k_inv = k * jnp.exp(-b)
    k_end = k * jnp.exp(b_last - b)
    mask = jnp.tril(jnp.ones((GLA_CHUNK, GLA_CHUNK), jnp.float32))
    attn = jnp.einsum('bhcld,bhcmd->bhclm', q_dec, k_inv) * mask
    o_intra = jnp.einsum('bhclm,bhcmv->bhclv', attn, v)
    chunk_kv = jnp.einsum('bhcld,bhclv->bhcdv', k_end, v)
    decay = jnp.exp(b_last[:, :, :, 0, :])

    def step(s, inp):
        qd, kv, dc = inp
        o = jnp.einsum('bhld,bhdv->bhlv', qd, s)
        return dc[..., None] * s + kv, o

    s_fin, o_inter = lax.scan(step, s0.astype(jnp.float32),
                              (jnp.moveaxis(q_dec, 2, 0), jnp.moveaxis(chunk_kv, 2, 0), jnp.moveaxis(decay, 2, 0)))
    o = o_intra + jnp.moveaxis(o_inter, 0, 2)
    return o.reshape(bsz, nh, n, -1), s_fin


def mixer(h, p, s0_fwd, s0_bwd):
    bsz, n, _ = h.shape
    proj = h @ p["w_in"]
    q, k, v, g_out, lr_f, lr_b, sg_u, sg_v, cv_b, cv_c, cv_x, ft = jnp.split(proj, IN_SPLITS, axis=-1)

    heads = lambda t: t.reshape(bsz, n, GLA_HEADS, -1).transpose(0, 2, 1, 3)
    flip = lambda t: jnp.flip(t, axis=2)
    log_a_f = jax.nn.log_sigmoid((lr_f @ p["w_gla_a"][0] + p["b_gla_a"][0]).astype(jnp.float32)) / GLA_TAU
    log_a_b = jax.nn.log_sigmoid((lr_b @ p["w_gla_a"][1] + p["b_gla_a"][1]).astype(jnp.float32)) / GLA_TAU
    qh, kh, vh = heads(q), heads(k), heads(v)
    o_f, s_f = gla_chunked(qh, kh, vh, heads(log_a_f), s0_fwd)
    o_b, s_b = gla_chunked(flip(qh), flip(kh), flip(vh), flip(heads(log_a_b)), s0_bwd)
    o = o_f + flip(o_b)
    o = o * lax.rsqrt(jnp.mean(jnp.square(o), axis=-1, keepdims=True) + RMS_EPS)
    o = o * p["gla_norm_g"].astype(jnp.float32).reshape(GLA_HEADS, 1, GLA_DV)
    y_gla = o.transpose(0, 2, 1, 3).reshape(bsz, n, GLA_W).astype(h.dtype) * jax.nn.silu(g_out)

    vn = layer_norm(sg_v, p["sgu_norm_g"], p["sgu_norm_b"])
    vn = vn.reshape(bsz, n // GMLP_CHUNK, GMLP_CHUNK, GMLP_GROUPS, -1)
    sp = jnp.einsum('gst,bctgd->bcsgd', p["w_sgu"], vn) + p["b_sgu"].T[:, :, None]
    y_sgu = sg_u * sp.reshape(bsz, n, GMLP_W)

    z = cv_c * cv_x
    zp = jnp.pad(z, ((0, 0), (1, 1), (0, 0)))
    wc = p["w_conv"]
    conv = wc[0] * zp[:, :-2] + wc[1] * zp[:, 1:-1] + wc[2] * zp[:, 2:]
    y_conv = cv_b * conv

    f = ft.reshape(bsz, n, FNET_GROUPS, -1).astype(jnp.float32)
    y_ft = jnp.fft.fft2(f, axes=(1, 3), norm="ortho").real.reshape(bsz, n, FNET_W).astype(h.dtype)

    y = jnp.concatenate([y_gla, y_sgu, y_conv, y_ft], axis=-1) @ p["w_out"]
    return y, s_f, s_b


def moe(h, p):
    bsz, n, d = h.shape
    x = h.reshape(-1, d)
    t = x.shape[0]
    scores = jax.nn.sigmoid((x @ p["w_router"]).astype(jnp.float32))
    biased = scores + p["router_bias"].astype(jnp.float32)
    grp_score = lax.top_k(biased.reshape(t, N_GROUPS, EXPERTS_PER_GROUP), 2)[0].sum(-1)
    _, top_g = lax.top_k(grp_score, TOPK_GROUPS)
    gmask = jnp.any(top_g[:, :, None] == jnp.arange(N_GROUPS)[None, None, :], axis=1)
    masked = jnp.where(jnp.repeat(gmask, EXPERTS_PER_GROUP, axis=1), biased, -jnp.inf)
    _, idx = lax.top_k(masked, TOP_K)
    w = jnp.take_along_axis(scores, idx, axis=1)
    w = w / jnp.sum(w, axis=-1, keepdims=True) * ROUTED_SCALE

    n_assign = t * TOP_K
    flat_e = idx.reshape(-1)
    order = jnp.argsort(flat_e)
    sorted_e = flat_e[order]
    tok = order // TOP_K
    w_sorted = w.reshape(-1)[order]
    counts = jnp.bincount(flat_e, length=N_EXPERTS)
    padded = ((counts + MOE_BLOCK - 1) // MOE_BLOCK) * MOE_BLOCK
    ends_padded = jnp.cumsum(padded)
    starts_padded = ends_padded - padded
    starts = jnp.cumsum(counts) - counts
    dest = starts_padded[sorted_e] + (jnp.arange(n_assign) - starts[sorted_e])
    n_blocks = -(-n_assign // MOE_BLOCK) + N_EXPERTS
    x_pad = jnp.zeros((n_blocks * MOE_BLOCK, d), x.dtype).at[dest].set(x[tok])
    block_expert = jnp.minimum(
        jnp.searchsorted(ends_padded, jnp.arange(n_blocks) * MOE_BLOCK, side='right'), N_EXPERTS - 1)

    def expert_block(args):
        xb, e = args
        return (jax.nn.silu(xb @ p["w_exp_gate"][e]) * (xb @ p["w_exp_up"][e])) @ p["w_exp_down"][e]

    y_blocks = lax.map(expert_block, (x_pad.reshape(n_blocks, MOE_BLOCK, d), block_expert))
    y_assign = y_blocks.reshape(-1, d)[dest] * w_sorted[:, None]
    routed = jax.ops.segment_sum(y_assign, tok, num_segments=t).astype(x.dtype)
    shared = (jax.nn.silu(x @ p["w_sh_gate"]) * (x @ p["w_sh_up"])) @ p["w_sh_down"]
    return (routed + shared).reshape(bsz, n, d)


def trunk_layer(x, cond, s0, p):
    mod = jax.nn.silu(cond) @ p["w_ada"] + p["b_ada"]
    sh1, sc1, g1, sh2, sc2, g2 = jnp.split(mod[:, None, :], 6, axis=-1)
    h = layer_norm(x) * (1 + sc1) + sh1
    y, s_f, s_b = mixer(h, p, s0[:, 0], s0[:, 1])
    x = layer_norm(DEEPNORM_ALPHA * x + g1 * y, p["ln1_g"], p["ln1_b"])
    h = layer_norm(x) * (1 + sc2) + sh2
    x = layer_norm(DEEPNORM_ALPHA * x + g2 * moe(h, p), p["ln2_g"], p["ln2_b"])
    return x, jnp.stack([s_f, s_b], axis=1)


def setup_inputs(seed: int = 0) -> dict:
    key = jax.random.key(seed)
    ks = iter(jax.random.split(key, 32))
    nrm = lambda shape, scale: scale * jax.random.normal(next(ks), shape, jnp.float32)
    L, D, E, F = DEPTH, D_MODEL, N_EXPERTS, EXPERT_FF
    return {
        "x_prompt": nrm((BATCH, SEQ, D), 1.0),
        "x_sample": nrm((DEC_BATCH, DEC_SEQ, D), 1.0),
        "c": nrm((DEC_BATCH, D), 1.0),
        "state_gla": nrm((DEC_BATCH, L, 2, GLA_HEADS, GLA_DK, GLA_DV), 0.5),
        "c_ctx": nrm((D,), 1.0),
        "w_ada": nrm((L, D, 6 * D), D ** -0.5),
        "b_ada": nrm((L, 6 * D), 0.02),
        "w_in": nrm((L, D, IN_W), D ** -0.5),
        "w_gla_a": nrm((L, 2, GLA_LR, GLA_W), GLA_LR ** -0.5),
        "b_gla_a": nrm((L, 2, GLA_W), 0.1),
        "gla_norm_g": 1.0 + nrm((L, GLA_W), 0.02),
        "sgu_norm_g": 1.0 + nrm((L, GMLP_W), 0.02),
        "sgu_norm_b": nrm((L, GMLP_W), 0.02),
        "w_sgu": nrm((L, GMLP_GROUPS, GMLP_CHUNK, GMLP_CHUNK), GMLP_CHUNK ** -0.5),
        "b_sgu": 1.0 + nrm((L, GMLP_GROUPS, GMLP_CHUNK), 0.02),
        "w_conv": nrm((L, 3, CONV_W), 3 ** -0.5),
        "w_out": nrm((L, MIX_W, D), DEEPNORM_BETA * MIX_W ** -0.5),
        "ln1_g": 1.0 + nrm((L, D), 0.02),
        "ln1_b": nrm((L, D), 0.02),
        "ln2_g": 1.0 + nrm((L, D), 0.02),
        "ln2_b": nrm((L, D), 0.02),
        "w_router": nrm((L, D, E), D ** -0.5),
        "router_bias": nrm((L, E), 0.01),
        "w_exp_gate": nrm((L, E, D, F), D ** -0.5),
        "w_exp_up": nrm((L, E, D, F), D ** -0.5),
        "w_exp_down": nrm((L, E, F, D), DEEPNORM_BETA * F ** -0.5),
        "w_sh_gate": nrm((L, D, SHARED_FF), D ** -0.5),
        "w_sh_up": nrm((L, D, SHARED_FF), D ** -0.5),
        "w_sh_down": nrm((L, SHARED_FF, D), DEEPNORM_BETA * SHARED_FF ** -0.5),
    }


def reference(x_prompt, x_sample, c, state_gla, c_ctx, w_ada, b_ada, w_in, w_gla_a, b_gla_a,
              gla_norm_g, sgu_norm_g, sgu_norm_b, w_sgu, b_sgu, w_conv, w_out,
              ln1_g, ln1_b, ln2_g, ln2_b, w_router, router_bias,
              w_exp_gate, w_exp_up, w_exp_down, w_sh_gate, w_sh_up, w_sh_down):
    def layer_params(l):
        return {
            "w_ada": w_ada[l], "b_ada": b_ada[l], "w_in": w_in[l],
            "w_gla_a": w_gla_a[l], "b_gla_a": b_gla_a[l], "gla_norm_g": gla_norm_g[l],
            "sgu_norm_g": sgu_norm_g[l], "sgu_norm_b": sgu_norm_b[l],
            "w_sgu": w_sgu[l], "b_sgu": b_sgu[l], "w_conv": w_conv[l], "w_out": w_out[l],
            "ln1_g": ln1_g[l], "ln1_b": ln1_b[l], "ln2_g": ln2_g[l], "ln2_b": ln2_b[l],
            "w_router": w_router[l], "router_bias": router_bias[l],
            "w_exp_gate": w_exp_gate[l], "w_exp_up": w_exp_up[l], "w_exp_down": w_exp_down[l],
            "w_sh_gate": w_sh_gate[l], "w_sh_up": w_sh_up[l], "w_sh_down": w_sh_down[l],
        }

    y_prompt = x_prompt
    s_zero = jnp.zeros((x_prompt.shape[0], 2, GLA_HEADS, GLA_DK, GLA_DV), jnp.float32)
    ctx_states = []
    for l in range(DEPTH):
        y_prompt, s_l = trunk_layer(y_prompt, c_ctx[None, :], s_zero, layer_params(l))
        ctx_states.append(s_l)
    new_state_gla = jnp.stack(ctx_states, axis=1).astype(x_prompt.dtype)

    rows = x_sample.shape[1] // GRID_W
    y_sample = x_sample + grid_sincos(rows, D_MODEL).astype(x_sample.dtype)[None]
    for l in range(DEPTH):
        y_sample, _ = trunk_layer(y_sample, c, state_gla[:, l], layer_params(l))

    return (y_prompt, y_sample, new_state_gla)
```

```python
import functools
import math

import numpy as np
import jax
import jax.numpy as jnp
from jax import lax
from jax.experimental import pallas as pl
from jax.experimental.pallas import tpu as pltpu

F32 = jnp.float32
BF16 = jnp.bfloat16

D_MODEL = 1024
DEPTH = 2
GRID_W = 64
HEAD_W = 256
GLA_HEADS = 4
GLA_DK = 64
GLA_LR = 16
GLA_TAU = 16.0
GLA_CHUNK = 64
GMLP_GROUPS = 4
GMLP_CHUNK = 128
FNET_GROUPS = 4
FNET_CH = 64
N_EXPERTS = 64
TOP_K = 8
N_GROUPS = 8
TOPK_GROUPS = 4
EXPERT_FF = 256
ROUTED_SCALE = 2.5
DEEPNORM_ALPHA = (2 * DEPTH) ** 0.25
LN_EPS = 1e-5
RMS_EPS = 1e-6
POS_BASE = 10000.0

COL_Q, COL_K, COL_V, COL_G, COL_SGU, COL_SGV, COL_CVB, COL_CVC, COL_CVX, COL_FT = range(10)
LR_W = 128
PROJ_W = 10 * HEAD_W + LR_W
COL_LR = (10 * HEAD_W) // LR_W

SEG = 256
FFT_N1 = 64
VMEM_LIMIT = 56 * 1024 * 1024


def _cparams(*sem):
    return pltpu.CompilerParams(dimension_semantics=sem, vmem_limit_bytes=VMEM_LIMIT)


def _ln(x):
    mu = jnp.mean(x, axis=-1, keepdims=True)
    xc = x - mu
    var = jnp.mean(xc * xc, axis=-1, keepdims=True)
    return xc * lax.rsqrt(var + LN_EPS)


def _sigmoid(x):
    return 1.0 / (1.0 + jnp.exp(-x))


def _silu(x):
    return x * _sigmoid(x)


def _dot(a, b):
    return jnp.dot(a, b, preferred_element_type=F32)


def _dot_nt(a, b):
    return lax.dot_general(a, b, (((1,), (1,)), ((), ())), preferred_element_type=F32)


def _dot_tn(a, b):
    return lax.dot_general(a, b, (((0,), (0,)), ((), ())), preferred_element_type=F32)


def _ada_kernel(c_ref, w_ref, b_ref, o_ref):
    c = c_ref[...]
    o_ref[...] = _dot(_silu(c).astype(BF16), w_ref[...].astype(BF16)) + b_ref[...]


def _ada_mod(cond8, w_ada, b_ada):
    n_l, d, w6 = w_ada.shape
    tn = 1536
    return pl.pallas_call(
        _ada_kernel,
        grid=(n_l, w6 // tn),
        in_specs=[pl.BlockSpec((8, d), lambda l, j: (0, 0)),
                  pl.BlockSpec((None, d, tn), lambda l, j: (l, 0, j)),
                  pl.BlockSpec((None, 1, tn), lambda l, j: (l, 0, j))],
        out_specs=pl.BlockSpec((None, 8, tn), lambda l, j: (l, 0, j)),
        out_shape=jax.ShapeDtypeStruct((n_l, 8, w6), F32),
        compiler_params=_cparams("parallel", "parallel"),
        name="ada_mod",
    )(cond8, w_ada, b_ada.reshape(n_l, 1, w6))


def _in_proj_kernel(*refs, has_pos):
    if has_pos:
        x_ref, pos_ref, mod_ref, w_ref, cs_ref, proj_ref, zr_ref, zi_ref, x0_ref = refs
        x = x_ref[...] + pos_ref[...]
        x0_ref[...] = x
    else:
        x_ref, mod_ref, w_ref, cs_ref, proj_ref, zr_ref, zi_ref = refs
        x = x_ref[...]
    mod = mod_ref[...]
    sh1 = mod[:, 0:D_MODEL]
    sc1 = mod[:, D_MODEL:2 * D_MODEL]
    h = _ln(x) * (1.0 + sc1) + sh1
    proj = _dot(h.astype(BF16), w_ref[...])
    proj_ref[...] = proj
    ft = proj[:, COL_FT * HEAD_W:(COL_FT + 1) * HEAD_W].astype(BF16)
    z = _dot(ft, cs_ref[...])
    zr_ref[...] = z[:, :HEAD_W].astype(BF16)
    zi_ref[...] = z[:, HEAD_W:].astype(BF16)


def _in_proj(x2d, pos, mod3, mod_row, w_in_p, cs, tm):
    t = x2d.shape[0]
    tm = min(tm, t)
    in_specs = [pl.BlockSpec((tm, D_MODEL), lambda i: (i, 0))]
    args = [x2d]
    out_shape = [jax.ShapeDtypeStruct((t, PROJ_W), F32),
                 jax.ShapeDtypeStruct((t, HEAD_W), BF16),
                 jax.ShapeDtypeStruct((t, HEAD_W), BF16)]
    out_specs = [pl.BlockSpec((tm, PROJ_W), lambda i: (i, 0)),
                 pl.BlockSpec((tm, HEAD_W), lambda i: (i, 0)),
                 pl.BlockSpec((tm, HEAD_W), lambda i: (i, 0))]
    if pos is not None:
        n_pos = pos.shape[0] // tm
        in_specs.append(pl.BlockSpec((tm, D_MODEL), lambda i: (i % n_pos, 0)))
        args.append(pos)
        out_shape.append(jax.ShapeDtypeStruct((t, D_MODEL), F32))
        out_specs.append(pl.BlockSpec((tm, D_MODEL), lambda i: (i, 0)))
    in_specs += [pl.BlockSpec((None, 1, 6 * D_MODEL), lambda i: (mod_row(i, tm), 0, 0)),
                 pl.BlockSpec((D_MODEL, PROJ_W), lambda i: (0, 0)),
                 pl.BlockSpec((HEAD_W, 2 * HEAD_W), lambda i: (0, 0))]
    args += [mod3, w_in_p, cs]
    return pl.pallas_call(
        functools.partial(_in_proj_kernel, has_pos=pos is not None),
        grid=(t // tm,),
        in_specs=in_specs,
        out_specs=out_specs,
        out_shape=out_shape,
        compiler_params=_cparams("parallel"),
        name="in_proj",
    )(*args)


def _gla_segment(q, k, v, pre, st_ref, o_ref, reverse):
    seg = q.shape[0]
    n_chunks = seg // GLA_CHUNK
    la = (jnp.minimum(pre, 0.0) - jnp.log1p(jnp.exp(-jnp.abs(pre)))) * (1.0 / GLA_TAU)

    r = lax.broadcasted_iota(jnp.int32, (seg, seg), 0)
    c = lax.broadcasted_iota(jnp.int32, (seg, seg), 1)
    same_chunk = (r >> 6) == (c >> 6)
    tri = same_chunk & ((c >= r) if reverse else (c <= r))
    tri_m = jnp.where(tri, 1.0, 0.0).astype(BF16)
    ones_m = jnp.where(same_chunk, 1.0, 0.0).astype(BF16)
    hi = la.astype(BF16)
    lo = (la - hi.astype(F32)).astype(BF16)
    b = _dot(tri_m, hi) + _dot(tri_m, lo)
    btot = _dot(ones_m, hi) + _dot(ones_m, lo)

    q_dec = q * (GLA_DK ** -0.5) * jnp.exp(b)
    k_inv = (k * jnp.exp(-b)).astype(BF16)
    k_end = (k * jnp.exp(btot - b)).astype(BF16)
    dec = jnp.exp(btot)
    vb = v.astype(BF16)

    bd = (r >> 6) == (c >> 6)
    lane_h = lax.broadcasted_iota(jnp.int32, (GLA_CHUNK, seg), 1) >> 6
    l_idx = lax.broadcasted_iota(jnp.int32, (seg, GLA_CHUNK), 0) & (GLA_CHUNK - 1)
    m_idx = lax.broadcasted_iota(jnp.int32, (seg, GLA_CHUNK), 1)
    causal = (m_idx >= l_idx) if reverse else (m_idx <= l_idx)

    st = st_ref[...]
    order = range(n_chunks - 1, -1, -1) if reverse else range(n_chunks)
    for ci in order:
        sl = slice(ci * GLA_CHUNK, (ci + 1) * GLA_CHUNK)
        qd = q_dec[sl]
        qbd = jnp.where(bd, jnp.concatenate([qd] * GLA_HEADS, axis=0), 0.0).astype(BF16)
        a = _dot_nt(qbd, k_inv[sl])
        a = jnp.where(causal, a, 0.0)
        rr = _dot(a.astype(BF16), vb[sl])
        o = _dot_nt(qd.astype(BF16), st.astype(BF16))
        for h in range(GLA_HEADS):
            o = o + jnp.where(lane_h == h, rr[h * GLA_CHUNK:(h + 1) * GLA_CHUNK], 0.0)
        o_ref[sl, :] = o
        kvt = _dot_tn(vb[sl], k_end[sl])
        st = st * dec[ci * GLA_CHUNK:ci * GLA_CHUNK + 1, :] + jnp.where(bd, kvt, 0.0)
    st_ref[...] = st


def _gla_kernel(*refs, has_init, emit_final):
    qf, kf, vf, lrf, qb, kb, vb, lrb, wa_ref, ba_ref = refs[:10]
    rest = refs[10:]
    if has_init:
        s0f, s0b = rest[:2]
        rest = rest[2:]
    of_ref, ob_ref = rest[:2]
    rest = rest[2:]
    if emit_final:
        sff, sfb = rest[:2]
        rest = rest[2:]
    stf, stb = rest

    s = pl.program_id(1)

    @pl.when(s == 0)
    def _():
        if has_init:
            stf[...] = s0f[...]
            stb[...] = s0b[...]
        else:
            stf[...] = jnp.zeros_like(stf)
            stb[...] = jnp.zeros_like(stb)

    wa = wa_ref[...]
    ba = ba_ref[...]
    pre_f = jnp.dot(lrf[...], wa, preferred_element_type=F32, precision=lax.Precision.HIGHEST) + ba
    pre_b = jnp.dot(lrb[...], wa, preferred_element_type=F32, precision=lax.Precision.HIGHEST) + ba
    _gla_segment(qf[...], kf[...], vf[...], pre_f[:, :HEAD_W], stf, of_ref, reverse=False)
    _gla_segment(qb[...], kb[...], vb[...], pre_b[:, HEAD_W:], stb, ob_ref, reverse=True)

    if emit_final:
        @pl.when(s == pl.num_programs(1) - 1)
        def _():
            sff[...] = stf[...]
            sfb[...] = stb[...]


def _gla(proj3, wa_pad, ba, st0, emit_final):
    bsz, n, _ = proj3.shape
    nseg = n // SEG

    def col(cb, width=HEAD_W, rev=False):
        if rev:
            return pl.BlockSpec((None, SEG, width), lambda b, s: (b, nseg - 1 - s, cb))
        return pl.BlockSpec((None, SEG, width), lambda b, s: (b, s, cb))

    in_specs = [col(COL_Q), col(COL_K), col(COL_V), col(COL_LR, LR_W),
                col(COL_Q, rev=True), col(COL_K, rev=True), col(COL_V, rev=True), col(COL_LR, LR_W, rev=True),
                pl.BlockSpec((LR_W, 2 * HEAD_W), lambda b, s: (0, 0)),
                pl.BlockSpec((1, 2 * HEAD_W), lambda b, s: (0, 0))]
    args = [proj3] * 8 + [wa_pad, ba]
    st_spec = pl.BlockSpec((None, HEAD_W, HEAD_W), lambda b, s: (b, 0, 0))
    if st0 is not None:
        in_specs += [st_spec, st_spec]
        args += [st0[0], st0[1]]
    out_shape = [jax.ShapeDtypeStruct((bsz, n, HEAD_W), F32)] * 2
    out_specs = [pl.BlockSpec((None, SEG, HEAD_W), lambda b, s: (b, s, 0)),
                 pl.BlockSpec((None, SEG, HEAD_W), lambda b, s: (b, nseg - 1 - s, 0))]
    if emit_final:
        out_shape += [jax.ShapeDtypeStruct((bsz, HEAD_W, HEAD_W), F32)] * 2
        out_specs += [st_spec, st_spec]
    return pl.pallas_call(
        functools.partial(_gla_kernel, has_init=st0 is not None, emit_final=emit_final),
        grid=(bsz, nseg),
        in_specs=in_specs,
        out_specs=out_specs,
        out_shape=out_shape,
        scratch_shapes=[pltpu.VMEM((HEAD_W, HEAD_W), F32), pltpu.VMEM((HEAD_W, HEAD_W), F32)],
        compiler_params=_cparams("parallel", "arbitrary"),
        name="gla",
    )(*args)


def _fft_direct_kernel(zr_ref, zi_ref, cn_ref, sn_ref, o_ref):
    o_ref[...] = _dot(cn_ref[...], zr_ref[...]) + _dot(sn_ref[...], zi_ref[...])


def _fft_direct(zr3, zi3, cn, sn):
    bsz, n, w = zr3.shape
    blk = pl.BlockSpec((None, n, w), lambda b: (b, 0, 0))
    tab = pl.BlockSpec((n, n), lambda b: (0, 0))
    return pl.pallas_call(
        _fft_direct_kernel,
        grid=(bsz,),
        in_specs=[blk, blk, tab, tab],
        out_specs=blk,
        out_shape=jax.ShapeDtypeStruct((bsz, n, w), F32),
        compiler_params=_cparams("parallel"),
        name="fft_direct",
    )(zr3, zi3, cn, sn)


def _fft_a_kernel(zr_ref, zi_ref, c_ref, s_ref, gr_ref, gi_ref):
    zr = zr_ref[...]
    zi = zi_ref[...]
    cm = c_ref[...]
    sm = s_ref[...]
    gr_ref[...] = (_dot(cm, zr) + _dot(sm, zi)).astype(BF16)
    gi_ref[...] = (_dot(cm, zi) - _dot(sm, zr)).astype(BF16)


def _fft_c_kernel(gr_ref, gi_ref, mc_ref, ms_ref, o_ref):
    for j in range(gr_ref.shape[0]):
        o_ref[:, j, :] = _dot(mc_ref[j], gr_ref[j]) + _dot(ms_ref[j], gi_ref[j])


def _fft_two_stage(zr3, zi3, tabs):
    bsz, n, w = zr3.shape
    n1 = FFT_N1
    n2 = n // n1
    c1, s1, mc, ms = tabs
    tn = 2048
    wide = n2 * w
    blk = pl.BlockSpec((None, n1, tn), lambda b, j: (b, 0, j))
    tab = pl.BlockSpec((n1, n1), lambda b, j: (0, 0))
    gr, gi = pl.pallas_call(
        _fft_a_kernel,
        grid=(bsz, wide // tn),
        in_specs=[blk, blk, tab, tab],
        out_specs=[blk, blk],
        out_shape=[jax.ShapeDtypeStruct((bsz, n1, wide), BF16)] * 2,
        compiler_params=_cparams("parallel", "parallel"),
        name="fft_stage_a",
    )(zr3.reshape(bsz, n1, wide), zi3.reshape(bsz, n1, wide), c1, s1)
    kb = 8
    gblk = pl.BlockSpec((None, kb, n2, w), lambda b, j: (b, j, 0, 0))
    mblk = pl.BlockSpec((kb, n2, n2), lambda b, j: (j, 0, 0))
    out = pl.pallas_call(
        _fft_c_kernel,
        grid=(bsz, n1 // kb),
        in_specs=[gblk, gblk, mblk, mblk],
        out_specs=pl.BlockSpec((None, n2, kb, w), lambda b, j: (b, 0, j, 0)),
        out_shape=jax.ShapeDtypeStruct((bsz, n2, n1, w), F32),
        compiler_params=_cparams("parallel", "parallel"),
        name="fft_stage_c",
    )(gr.reshape(bsz, n1, n2, w), gi.reshape(bsz, n1, n2, w), mc, ms)
    return out.reshape(bsz, n, w)


def _route(logits, bias):
    tm = logits.shape[1]
    s = _sigmoid(logits)
    biased = s + bias
    neg = -jnp.inf
    rows = lax.broadcasted_iota(jnp.int32, (8, tm), 0)

    def first_argmax(x, ids, sentinel):
        m = jnp.max(x, axis=0, keepdims=True)
        return m, jnp.min(jnp.where(x == m, ids, sentinel), axis=0, keepdims=True)

    gs_rows = []
    for g in range(N_GROUPS):
        x = biased[8 * g:8 * g + 8]
        m1, i1 = first_argmax(x, rows, 8)
        m2 = jnp.max(jnp.where(rows == i1, neg, x), axis=0, keepdims=True)
        gs_rows.append(m1 + m2)
    gs = jnp.concatenate(gs_rows, axis=0)
    gsel = jnp.zeros((N_GROUPS, tm), F32)
    for _ in range(TOPK_GROUPS):
        _, i = first_argmax(gs, rows, 8)
        hit = rows == i
        gsel = jnp.where(hit, 1.0, gsel)
        gs = jnp.where(hit, neg, gs)

    xs = [jnp.where(gsel[g:g + 1] > 0.0, biased[8 * g:8 * g + 8], neg) for g in range(N_GROUPS)]
    ids = [rows + 8 * g for g in range(N_GROUPS)]
    sel = [jnp.zeros((8, tm), F32) for _ in range(N_GROUPS)]
    for _ in range(TOP_K):
        m = xs[0]
        for g in range(1, N_GROUPS):
            m = jnp.maximum(m, xs[g])
        m = jnp.max(m, axis=0, keepdims=True)
        cand = jnp.where(xs[0] == m, ids[0], N_EXPERTS)
        for g in range(1, N_GROUPS):
            cand = jnp.minimum(cand, jnp.where(xs[g] == m, ids[g], N_EXPERTS))
        i = jnp.min(cand, axis=0, keepdims=True)
        for g in range(N_GROUPS):
            hit = ids[g] == i
            sel[g] = jnp.where(hit, 1.0, sel[g])
            xs[g] = jnp.where(hit, neg, xs[g])

    picked = [sel[g] * s[8 * g:8 * g + 8] for g in range(N_GROUPS)]
    tot = picked[0]
    for g in range(1, N_GROUPS):
        tot = tot + picked[g]
    tot = jnp.sum(tot, axis=0, keepdims=True)
    return jnp.concatenate([p / tot * ROUTED_SCALE for p in picked], axis=0)


def _mix_out_kernel(x_ref, of_ref, ob_ref, g_ref, su_ref, sv_ref, cb_ref, cc_ref, cx_ref,
                    ccp_ref, cxp_ref, ccn_ref, cxn_ref, ft_ref, mod_ref,
                    glag_ref, sgng_ref, sgnb_ref, wsgu_ref, bsgu_ref, wconv_ref, wout_ref,
                    ln1g_ref, ln1b_ref, wr_ref, rb_ref,
                    x1_ref, h2_ref, wc_ref, *, seq_len):
    tm = x_ref.shape[0]
    i = pl.program_id(0)
    mod = mod_ref[...]
    g1 = mod[:, 2 * D_MODEL:3 * D_MODEL]
    sh2 = mod[:, 3 * D_MODEL:4 * D_MODEL]
    sc2 = mod[:, 4 * D_MODEL:5 * D_MODEL]

    o = of_ref[...] + ob_ref[...]
    hr = lax.broadcasted_iota(jnp.int32, (HEAD_W, HEAD_W), 0) >> 6
    hc = lax.broadcasted_iota(jnp.int32, (HEAD_W, HEAD_W), 1) >> 6
    head_mean = jnp.where(hr == hc, 1.0 / GLA_DK, 0.0).astype(BF16)
    o2 = o * o
    o2_hi = o2.astype(BF16)
    o2_lo = (o2 - o2_hi.astype(F32)).astype(BF16)
    ms = _dot(o2_hi, head_mean) + _dot(o2_lo, head_mean)
    y_gla = o * lax.rsqrt(ms + RMS_EPS) * glag_ref[...] * _silu(g_ref[...])

    vn = _ln(sv_ref[...]) * sgng_ref[...] + sgnb_ref[...]
    br = lax.broadcasted_iota(jnp.int32, (GMLP_GROUPS * GMLP_CHUNK, HEAD_W), 0) >> 7
    bc = lax.broadcasted_iota(jnp.int32, (GMLP_GROUPS * GMLP_CHUNK, HEAD_W), 1) >> 6
    sgu_bd = br == bc
    sp_parts = []
    for j in range(tm // GMLP_CHUNK):
        vc = vn[j * GMLP_CHUNK:(j + 1) * GMLP_CHUNK]
        vbd = jnp.where(sgu_bd, jnp.concatenate([vc] * GMLP_GROUPS, axis=0), 0.0).astype(BF16)
        sp_parts.append(_dot(wsgu_ref[...], vbd) + bsgu_ref[...])
    y_sgu = su_ref[...] * jnp.concatenate(sp_parts, axis=0)

    z = cc_ref[...] * cx_ref[...]
    z_before = ccp_ref[7:8, :] * cxp_ref[7:8, :]
    z_after = ccn_ref[0:1, :] * cxn_ref[0:1, :]
    row = lax.broadcasted_iota(jnp.int32, (tm, HEAD_W), 0)
    pos = (i * tm + row) & (seq_len - 1)
    z_prev = jnp.where(row == 0, z_before, pltpu.roll(z, 1, 0))
    z_next = jnp.where(row == tm - 1, z_after, pltpu.roll(z, tm - 1, 0))
    z_prev = jnp.where(pos == 0, 0.0, z_prev)
    z_next = jnp.where(pos == seq_len - 1, 0.0, z_next)
    wconv = wconv_ref[...]
    y_conv = cb_ref[...] * (wconv[0:1] * z_prev + wconv[1:2] * z + wconv[2:3] * z_next)

    y = (_dot(y_gla.astype(BF16), wout_ref[0:HEAD_W, :])
         + _dot(y_sgu.astype(BF16), wout_ref[HEAD_W:2 * HEAD_W, :])
         + _dot(y_conv.astype(BF16), wout_ref[2 * HEAD_W:3 * HEAD_W, :])
         + _dot(ft_ref[...].astype(BF16), wout_ref[3 * HEAD_W:4 * HEAD_W, :]))
    x1 = _ln(DEEPNORM_ALPHA * x_ref[...] + g1 * y) * ln1g_ref[...] + ln1b_ref[...]
    x1_ref[...] = x1
    h2 = _ln(x1) * (1.0 + sc2) + sh2
    h2_ref[...] = h2.astype(BF16)

    logits = jnp.dot(h2, wr_ref[...], preferred_element_type=F32, precision=lax.Precision.HIGHEST)
    wc = _route(logits.T[:N_EXPERTS], rb_ref[...])
    wc_ref[...] = jnp.concatenate([wc, jnp.zeros_like(wc)], axis=0).T


def _mix_out(x2d, of2d, ob2d, proj, yft2d, mod3, mod_row, lw, seq_len, tm):
    t = x2d.shape[0]
    tm = min(tm, t)
    nt8 = t // 8
    rows8 = tm // 8

    def col(cb):
        return pl.BlockSpec((tm, HEAD_W), lambda i: (i, cb))

    def halo_prev(cb):
        return pl.BlockSpec((8, HEAD_W), lambda i: (jnp.maximum(i * rows8 - 1, 0), cb))

    def halo_next(cb):
        return pl.BlockSpec((8, HEAD_W), lambda i: (jnp.minimum((i + 1) * rows8, nt8 - 1), cb))

    def full(a):
        return pl.BlockSpec(a.shape, lambda i: (0,) * a.ndim)

    tok_d = pl.BlockSpec((tm, D_MODEL), lambda i: (i, 0))
    tok_h = pl.BlockSpec((tm, HEAD_W), lambda i: (i, 0))
    weights = [lw["gla_norm_g"], lw["sgu_norm_g"], lw["sgu_norm_b"], lw["w_sgu_cat"], lw["b_sgu_full"],
               lw["w_conv"], lw["w_out"], lw["ln1_g"], lw["ln1_b"], lw["w_router_pad"], lw["router_bias"]]
    in_specs = ([tok_d, tok_h, tok_h, col(COL_G), col(COL_SGU), col(COL_SGV), col(COL_CVB), col(COL_CVC),
                 col(COL_CVX), halo_prev(COL_CVC), halo_prev(COL_CVX), halo_next(COL_CVC), halo_next(COL_CVX),
                 tok_h, pl.BlockSpec((None, 1, 6 * D_MODEL), lambda i: (mod_row(i, tm), 0, 0))]
                + [full(w) for w in weights])
    args = [x2d, of2d, ob2d] + [proj] * 10 + [yft2d, mod3] + weights
    return pl.pallas_call(
        functools.partial(_mix_out_kernel, seq_len=seq_len),
        grid=(t // tm,),
        in_specs=in_specs,
        out_specs=[tok_d, tok_d, pl.BlockSpec((tm, 2 * N_EXPERTS), lambda i: (i, 0))],
        out_shape=[jax.ShapeDtypeStruct((t, D_MODEL), F32),
                   jax.ShapeDtypeStruct((t, D_MODEL), BF16),
                   jax.ShapeDtypeStruct((t, 2 * N_EXPERTS), F32)],
        compiler_params=_cparams("parallel"),
        name="mix_out",
    )(*args)


def _moe_kernel(h_ref, wc_ref, x1_ref, mod_ref, wg_ref, wu_ref, wd_ref, sg_ref, su_ref, sd_ref,
                ln2g_ref, ln2b_ref, o_ref, acc_ref):
    e = pl.program_id(1)
    h = h_ref[...]

    @pl.when(e == 0)
    def _():
        a = _silu(_dot(h, sg_ref[...])) * _dot(h, su_ref[...])
        acc_ref[...] = _dot(a.astype(BF16), sd_ref[...])

    a = _silu(_dot(h, wg_ref[...].astype(BF16))) * _dot(h, wu_ref[...].astype(BF16))
    lane = lax.broadcasted_iota(jnp.int32, wc_ref.shape, 1)
    wcol = jnp.sum(jnp.where(lane == e, wc_ref[...], 0.0), axis=1, keepdims=True)
    acc_ref[...] += _dot((a * wcol).astype(BF16), wd_ref[...].astype(BF16))

    @pl.when(e == pl.num_programs(1) - 1)
    def _():
        g2 = mod_ref[...][:, 5 * D_MODEL:6 * D_MODEL]
        u = DEEPNORM_ALPHA * x1_ref[...] + g2 * acc_ref[...]
        o_ref[...] = _ln(u) * ln2g_ref[...] + ln2b_ref[...]


def _moe(h2, wc, x1, mod3, mod_row, lw, layer, tm):
    t = h2.shape[0]
    tm = min(tm, t)
    ff = EXPERT_FF

    def full(a):
        return pl.BlockSpec(a.shape, lambda i, e: (0,) * a.ndim)

    in_specs = [pl.BlockSpec((tm, D_MODEL), lambda i, e: (i, 0)),
                pl.BlockSpec((tm, 2 * N_EXPERTS), lambda i, e: (i, 0)),
                pl.BlockSpec((tm, D_MODEL), lambda i, e: (i, 0)),
                pl.BlockSpec((None, 1, 6 * D_MODEL), lambda i, e: (mod_row(i, tm), 0, 0)),
                pl.BlockSpec((None, None, D_MODEL, ff), lambda i, e: (layer, e, 0, 0)),
                pl.BlockSpec((None, None, D_MODEL, ff), lambda i, e: (layer, e, 0, 0)),
                pl.BlockSpec((None, None, ff, D_MODEL), lambda i, e: (layer, e, 0, 0)),
                full(lw["w_sh_gate"]), full(lw["w_sh_up"]), full(lw["w_sh_down"]),
                full(lw["ln2_g"]), full(lw["ln2_b"])]
    return pl.pallas_call(
        _moe_kernel,
        grid=(t // tm, N_EXPERTS),
        in_specs=in_specs,
        out_specs=pl.BlockSpec((tm, D_MODEL), lambda i, e: (i, 0)),
        out_shape=jax.ShapeDtypeStruct((t, D_MODEL), F32),
        scratch_shapes=[pltpu.VMEM((tm, D_MODEL), F32)],
        compiler_params=_cparams("parallel", "arbitrary"),
        name="moe",
    )(h2, wc, x1, mod3, lw["w_exp_gate"], lw["w_exp_up"], lw["w_exp_down"],
      lw["w_sh_gate"], lw["w_sh_up"], lw["w_sh_down"], lw["ln2_g"], lw["ln2_b"])


def _channel_dft_table():
    k = np.arange(FNET_CH, dtype=np.float64)
    ang = 2.0 * np.pi * np.outer(k, k) / FNET_CH
    eye = np.eye(FNET_GROUPS)
    return np.concatenate([np.kron(eye, np.cos(ang)), -np.kron(eye, np.sin(ang))], axis=1)


def _direct_dft_tables(n):
    k = np.arange(n, dtype=np.float64)
    ang = 2.0 * np.pi * (np.outer(k, k) % n) / n
    scale = 1.0 / math.sqrt(n * FNET_CH)
    return np.cos(ang) * scale, np.sin(ang) * scale


def _two_stage_dft_tables(n):
    n1 = FFT_N1
    n2 = n // n1
    a = np.arange(n1, dtype=np.float64)
    ang1 = 2.0 * np.pi * (np.outer(a, a) % n1) / n1
    k1 = np.arange(n1).reshape(n1, 1, 1)
    k2 = np.arange(n2).reshape(1, n2, 1)
    m2 = np.arange(n2).reshape(1, 1, n2)
    ang2 = 2.0 * np.pi * ((m2 * (k1 + n1 * k2)) % n) / n
    scale = 1.0 / math.sqrt(n * FNET_CH)
    return np.cos(ang1), np.sin(ang1), np.cos(ang2) * scale, np.sin(ang2) * scale


def _grid_sincos_table(rows, d):
    quarter = d // 4
    omega = 1.0 / (POS_BASE ** (np.arange(quarter, dtype=np.float64) / quarter))
    r = np.arange(rows, dtype=np.float64)[:, None] * omega
    c = np.arange(GRID_W, dtype=np.float64)[:, None] * omega
    return (np.concatenate([np.sin(r), np.cos(r)], axis=-1).astype(np.float32),
            np.concatenate([np.sin(c), np.cos(c)], axis=-1).astype(np.float32))


def _layer_weights(l, w_in, w_gla_a, b_gla_a, gla_norm_g, sgu_norm_g, sgu_norm_b, w_sgu, b_sgu, w_conv,
                   w_out, ln1_g, ln1_b, ln2_g, ln2_b, w_router, router_bias,
                   w_exp_gate, w_exp_up, w_exp_down, w_sh_gate, w_sh_up, w_sh_down):
    wi = w_in[l]
    lr0 = 4 * HEAD_W
    w_in_p = jnp.concatenate(
        [wi[:, :lr0], wi[:, lr0 + 2 * GLA_LR:], wi[:, lr0:lr0 + 2 * GLA_LR],
         jnp.zeros((D_MODEL, LR_W - 2 * GLA_LR), F32)], axis=1).astype(BF16)
    wa_pad = jnp.zeros((LR_W, 2 * HEAD_W), F32)
    wa_pad = wa_pad.at[:GLA_LR, :HEAD_W].set(w_gla_a[l, 0])
    wa_pad = wa_pad.at[GLA_LR:2 * GLA_LR, HEAD_W:].set(w_gla_a[l, 1])
    row = lambda a: a[l].reshape(1, -1)
    return {
        "w_in_p": w_in_p,
        "wa_pad": wa_pad,
        "ba": jnp.concatenate([b_gla_a[l, 0], b_gla_a[l, 1]]).reshape(1, 2 * HEAD_W),
        "gla_norm_g": row(gla_norm_g), "sgu_norm_g": row(sgu_norm_g), "sgu_norm_b": row(sgu_norm_b),
        "w_sgu_cat": jnp.concatenate([w_sgu[l, g] for g in range(GMLP_GROUPS)], axis=1).astype(BF16),
        "b_sgu_full": jnp.repeat(b_sgu[l].T, HEAD_W // GMLP_GROUPS, axis=1),
        "w_conv": w_conv[l],
        "w_out": w_out[l].astype(BF16),
        "ln1_g": row(ln1_g), "ln1_b": row(ln1_b), "ln2_g": row(ln2_g), "ln2_b": row(ln2_b),
        "w_router_pad": jnp.concatenate([w_router[l], jnp.zeros((D_MODEL, N_EXPERTS), F32)], axis=1),
        "router_bias": router_bias[l].reshape(N_EXPERTS, 1),
        "w_exp_gate": w_exp_gate, "w_exp_up": w_exp_up, "w_exp_down": w_exp_down,
        "w_sh_gate": w_sh_gate[l].astype(BF16), "w_sh_up": w_sh_up[l].astype(BF16),
        "w_sh_down": w_sh_down[l].astype(BF16),
    }


def _state_to_blockdiag_t(s):
    bsz = s.shape[0]
    st = jnp.swapaxes(s, 2, 3)
    eye = jnp.eye(GLA_HEADS, dtype=s.dtype)
    return jnp.einsum("bhvd,hg->bhvgd", st, eye).reshape(bsz, HEAD_W, HEAD_W)


def _blockdiag_t_to_state(st):
    bsz = st.shape[0]
    s5 = st.reshape(bsz, GLA_HEADS, GLA_DK, GLA_HEADS, GLA_DK)
    diag = jnp.stack([s5[:, h, :, h, :] for h in range(GLA_HEADS)], axis=1)
    return jnp.swapaxes(diag, 2, 3)


def _trunk_layer(x3, pos, mod3, mod_row, lw, layer, st0, emit_final, tabs):
    bsz, n, _ = x3.shape
    t = bsz * n
    outs = _in_proj(x3.reshape(t, D_MODEL), pos, mod3, mod_row, lw["w_in_p"], tabs["cs"], tm=256)
    if pos is not None:
        proj, zr, zi, x2d = outs
    else:
        proj, zr, zi = outs
        x2d = x3.reshape(t, D_MODEL)
    gla_out = _gla(proj.reshape(bsz, n, PROJ_W), lw["wa_pad"], lw["ba"], st0, emit_final)
    o_f, o_b = gla_out[:2]
    zr3 = zr.reshape(bsz, n, HEAD_W)
    zi3 = zi.reshape(bsz, n, HEAD_W)
    if "two_stage" in tabs:
        yft = _fft_two_stage(zr3, zi3, tabs["two_stage"])
    else:
        yft = _fft_direct(zr3, zi3, *tabs["direct"])
    x1, h2, wc = _mix_out(x2d, o_f.reshape(t, HEAD_W), o_b.reshape(t, HEAD_W), proj, yft.reshape(t, HEAD_W),
                          mod3, mod_row, lw, seq_len=n, tm=256)
    x2 = _moe(h2, wc, x1, mod3, mod_row, lw, layer, tm=1024)
    return x2.reshape(bsz, n, D_MODEL), gla_out[2:]


def kernel(x_prompt, x_sample, c, state_gla, c_ctx, w_ada, b_ada, w_in, w_gla_a, b_gla_a, gla_norm_g, sgu_norm_g, sgu_norm_b, w_sgu, b_sgu, w_conv, w_out, ln1_g, ln1_b, ln2_g, ln2_b, w_router, router_bias, w_exp_gate, w_exp_up, w_exp_down, w_sh_gate, w_sh_up, w_sh_down):
    n_layers = w_ada.shape[0]
    bp, np_, _ = x_prompt.shape
    bs, ns, _ = x_sample.shape
    assert bs <= 7

    cond8 = jnp.concatenate([c_ctx[None, :], c, jnp.zeros((7 - bs, D_MODEL), F32)], axis=0)
    mod = _ada_mod(cond8, w_ada, b_ada)

    tabs_p = {"cs": jnp.asarray(_channel_dft_table(), BF16),
              "direct": tuple(jnp.asarray(a, BF16) for a in _direct_dft_tables(np_))}
    tabs_s = {"cs": tabs_p["cs"],
              "two_stage": tuple(jnp.asarray(a, BF16) for a in _two_stage_dft_tables(ns))}
    rtab, ctab = _grid_sincos_table(ns // GRID_W, D_MODEL)
    pos = jnp.concatenate([jnp.repeat(jnp.asarray(rtab), GRID_W, axis=0),
                           jnp.tile(jnp.asarray(ctab), (ns // GRID_W, 1))], axis=-1)

    prompt_row = lambda i, tm: 0
    sample_row = lambda i, tm: 1 + (i * tm) // ns

    y_p = x_prompt
    y_s = x_sample
    finals = []
    for l in range(n_layers):
        lw = _layer_weights(l, w_in, w_gla_a, b_gla_a, gla_norm_g, sgu_norm_g, sgu_norm_b, w_sgu, b_sgu,
                            w_conv, w_out, ln1_g, ln1_b, ln2_g, ln2_b, w_router, router_bias,
                            w_exp_gate, w_exp_up, w_exp_down, w_sh_gate, w_sh_up, w_sh_down)
        mod3 = mod[l].reshape(8, 1, 6 * D_MODEL)
        y_p, fin = _trunk_layer(y_p, None, mod3, prompt_row, lw, l, None, True, tabs_p)
        finals.append(jnp.stack([_blockdiag_t_to_state(fin[0]), _blockdiag_t_to_state(fin[1])], axis=1))
        st0 = jnp.stack([_state_to_blockdiag_t(state_gla[:, l, 0]), _state_to_blockdiag_t(state_gla[:, l, 1])])
        y_s, _ = _trunk_layer(y_s, pos if l == 0 else None, mod3, sample_row, lw, l, st0, False, tabs_s)
    new_state = jnp.stack(finals, axis=1).astype(x_prompt.dtype)
    return (y_p, y_s, new_state)
```

```python
import functools
import math

import numpy as np
import jax
import jax.numpy as jnp
from jax import lax
from jax.experimental import pallas as pl
from jax.experimental.pallas import tpu as pltpu

F32 = jnp.float32
BF16 = jnp.bfloat16

D_MODEL = 1024
DEPTH = 2
GRID_W = 64
HEAD_W = 256
GLA_HEADS = 4
GLA_DK = 64
GLA_LR = 16
GLA_TAU = 16.0
GLA_CHUNK = 64
GMLP_GROUPS = 4
GMLP_CHUNK = 128
FNET_GROUPS = 4
FNET_CH = 64
N_EXPERTS = 64
TOP_K = 8
N_GROUPS = 8
TOPK_GROUPS = 4
EXPERT_FF = 256
ROUTED_SCALE = 2.5
DEEPNORM_ALPHA = (2 * DEPTH) ** 0.25
LN_EPS = 1e-5
RMS_EPS = 1e-6
POS_BASE = 10000.0

COL_Q, COL_K, COL_V, COL_G, COL_SGU, COL_SGV, COL_CVB, COL_CVC, COL_CVX, COL_FT = range(10)
LR_W = 128
PROJ_W = 10 * HEAD_W + LR_W
COL_LR = (10 * HEAD_W) // LR_W

LANES = 128
PACK_W = D_MODEL // 2
SLAB = PACK_W // LANES
EBLK = 256
SEG = 256
FFT_N1 = 64
VMEM_LIMIT = 56 * 1024 * 1024


def _cparams(*sem):
    return pltpu.CompilerParams(dimension_semantics=sem, vmem_limit_bytes=VMEM_LIMIT)


def _ln(x):
    mu = jnp.mean(x, axis=-1, keepdims=True)
    xc = x - mu
    var = jnp.mean(xc * xc, axis=-1, keepdims=True)
    return xc * lax.rsqrt(var + LN_EPS)


def _sigmoid(x):
    return 1.0 / (1.0 + jnp.exp(-x))


def _silu(x):
    return x * _sigmoid(x)


def _pack_bf16_pairs(x):
    w = x.shape[1] // 2
    lo = lax.bitcast_convert_type(x[:, :w].astype(BF16).astype(F32), jnp.uint32)
    hi = lax.bitcast_convert_type(x[:, w:].astype(BF16).astype(F32), jnp.uint32)
    return (lo >> 16) | (hi & jnp.uint32(0xFFFF0000))


def _unpack_bf16_pairs(u):
    lo = lax.bitcast_convert_type(u << 16, F32)
    hi = lax.bitcast_convert_type(u & jnp.uint32(0xFFFF0000), F32)
    return jnp.concatenate([lo, hi], axis=1)


def _store_rows(ref, packed):
    m = packed.shape[0]
    for c in range(SLAB):
        ref[pl.ds(c, m, stride=SLAB), :] = packed[:, c * LANES:(c + 1) * LANES]


def _load_rows(ref):
    m = ref.shape[0] // SLAB
    return jnp.concatenate([ref[pl.ds(c, m, stride=SLAB), :] for c in range(SLAB)], axis=1)


def _dot(a, b):
    return jnp.dot(a, b, preferred_element_type=F32)


def _dot_nt(a, b):
    return lax.dot_general(a, b, (((1,), (1,)), ((), ())), preferred_element_type=F32)


def _dot_tn(a, b):
    return lax.dot_general(a, b, (((0,), (0,)), ((), ())), preferred_element_type=F32)


def _ada_kernel(c_ref, w_ref, b_ref, o_ref):
    c = c_ref[...]
    o_ref[...] = _dot(_silu(c).astype(BF16), w_ref[...].astype(BF16)) + b_ref[...]


def _ada_mod(cond8, w_ada, b_ada):
    n_l, d, w6 = w_ada.shape
    tn = 1536
    return pl.pallas_call(
        _ada_kernel,
        grid=(n_l, w6 // tn),
        in_specs=[pl.BlockSpec((8, d), lambda l, j: (0, 0)),
                  pl.BlockSpec((None, d, tn), lambda l, j: (l, 0, j)),
                  pl.BlockSpec((None, 1, tn), lambda l, j: (l, 0, j))],
        out_specs=pl.BlockSpec((None, 8, tn), lambda l, j: (l, 0, j)),
        out_shape=jax.ShapeDtypeStruct((n_l, 8, w6), F32),
        compiler_params=_cparams("parallel", "parallel"),
        name="ada_mod",
    )(cond8, w_ada, b_ada.reshape(n_l, 1, w6))


def _in_proj_kernel(*refs, has_pos):
    if has_pos:
        x_ref, pos_ref, mod_ref, w_ref, cs_ref, proj_ref, zr_ref, zi_ref, x0_ref = refs
        x = x_ref[...] + pos_ref[...]
        x0_ref[...] = x
    else:
        x_ref, mod_ref, w_ref, cs_ref, proj_ref, zr_ref, zi_ref = refs
        x = x_ref[...]
    mod = mod_ref[...]
    sh1 = mod[:, 0:D_MODEL]
    sc1 = mod[:, D_MODEL:2 * D_MODEL]
    h = _ln(x) * (1.0 + sc1) + sh1
    proj = _dot(h.astype(BF16), w_ref[...])
    proj_ref[...] = proj
    ft = proj[:, COL_FT * HEAD_W:(COL_FT + 1) * HEAD_W].astype(BF16)
    z = _dot(ft, cs_ref[...])
    zr_ref[...] = z[:, :HEAD_W].astype(BF16)
    zi_ref[...] = z[:, HEAD_W:].astype(BF16)


def _in_proj(x2d, pos, mod3, mod_row, w_in_p, cs, tm):
    t = x2d.shape[0]
    tm = min(tm, t)
    in_specs = [pl.BlockSpec((tm, D_MODEL), lambda i: (i, 0))]
    args = [x2d]
    out_shape = [jax.ShapeDtypeStruct((t, PROJ_W), F32),
                 jax.ShapeDtypeStruct((t, HEAD_W), BF16),
                 jax.ShapeDtypeStruct((t, HEAD_W), BF16)]
    out_specs = [pl.BlockSpec((tm, PROJ_W), lambda i: (i, 0)),
                 pl.BlockSpec((tm, HEAD_W), lambda i: (i, 0)),
                 pl.BlockSpec((tm, HEAD_W), lambda i: (i, 0))]
    if pos is not None:
        n_pos = pos.shape[0] // tm
        in_specs.append(pl.BlockSpec((tm, D_MODEL), lambda i: (i % n_pos, 0)))
        args.append(pos)
        out_shape.append(jax.ShapeDtypeStruct((t, D_MODEL), F32))
        out_specs.append(pl.BlockSpec((tm, D_MODEL), lambda i: (i, 0)))
    in_specs += [pl.BlockSpec((None, 1, 6 * D_MODEL), lambda i: (mod_row(i, tm), 0, 0)),
                 pl.BlockSpec((D_MODEL, PROJ_W), lambda i: (0, 0)),
                 pl.BlockSpec((HEAD_W, 2 * HEAD_W), lambda i: (0, 0))]
    args += [mod3, w_in_p, cs]
    return pl.pallas_call(
        functools.partial(_in_proj_kernel, has_pos=pos is not None),
        grid=(t // tm,),
        in_specs=in_specs,
        out_specs=out_specs,
        out_shape=out_shape,
        compiler_params=_cparams("parallel"),
        name="in_proj",
    )(*args)


def _gla_segment(q, k, v, pre, st_ref, o_ref, reverse):
    seg = q.shape[0]
    n_chunks = seg // GLA_CHUNK
    la = (jnp.minimum(pre, 0.0) - jnp.log1p(jnp.exp(-jnp.abs(pre)))) * (1.0 / GLA_TAU)

    r = lax.broadcasted_iota(jnp.int32, (seg, seg), 0)
    c = lax.broadcasted_iota(jnp.int32, (seg, seg), 1)
    same_chunk = (r >> 6) == (c >> 6)
    tri = same_chunk & ((c >= r) if reverse else (c <= r))
    tri_m = jnp.where(tri, 1.0, 0.0).astype(BF16)
    ones_m = jnp.where(same_chunk, 1.0, 0.0).astype(BF16)
    hi = la.astype(BF16)
    lo = (la - hi.astype(F32)).astype(BF16)
    b = _dot(tri_m, hi) + _dot(tri_m, lo)
    btot = _dot(ones_m, hi) + _dot(ones_m, lo)

    q_dec = q * (GLA_DK ** -0.5) * jnp.exp(b)
    k_inv = (k * jnp.exp(-b)).astype(BF16)
    k_end = (k * jnp.exp(btot - b)).astype(BF16)
    dec = jnp.exp(btot)
    vb = v.astype(BF16)

    bd = (r >> 6) == (c >> 6)
    lane_h = lax.broadcasted_iota(jnp.int32, (GLA_CHUNK, seg), 1) >> 6
    l_idx = lax.broadcasted_iota(jnp.int32, (seg, GLA_CHUNK), 0) & (GLA_CHUNK - 1)
    m_idx = lax.broadcasted_iota(jnp.int32, (seg, GLA_CHUNK), 1)
    causal = (m_idx >= l_idx) if reverse else (m_idx <= l_idx)

    st = st_ref[...]
    order = range(n_chunks - 1, -1, -1) if reverse else range(n_chunks)
    for ci in order:
        sl = slice(ci * GLA_CHUNK, (ci + 1) * GLA_CHUNK)
        qd = q_dec[sl]
        qbd = jnp.where(bd, jnp.concatenate([qd] * GLA_HEADS, axis=0), 0.0).astype(BF16)
        a = _dot_nt(qbd, k_inv[sl])
        a = jnp.where(causal, a, 0.0)
        rr = _dot(a.astype(BF16), vb[sl])
        o = _dot_nt(qd.astype(BF16), st.astype(BF16))
        for h in range(GLA_HEADS):
            o = o + jnp.where(lane_h == h, rr[h * GLA_CHUNK:(h + 1) * GLA_CHUNK], 0.0)
        o_ref[sl, :] = o
        kvt = _dot_tn(vb[sl], k_end[sl])
        st = st * dec[ci * GLA_CHUNK:ci * GLA_CHUNK + 1, :] + jnp.where(bd, kvt, 0.0)
    st_ref[...] = st


def _gla_kernel(*refs, has_init, emit_final):
    qf, kf, vf, lrf, qb, kb, vb, lrb, wa_ref, ba_ref = refs[:10]
    rest = refs[10:]
    if has_init:
        s0f, s0b = rest[:2]
        rest = rest[2:]
    of_ref, ob_ref = rest[:2]
    rest = rest[2:]
    if emit_final:
        sff, sfb = rest[:2]
        rest = rest[2:]
    stf, stb = rest

    s = pl.program_id(1)

    @pl.when(s == 0)
    def _():
        if has_init:
            stf[...] = s0f[...]
            stb[...] = s0b[...]
        else:
            stf[...] = jnp.zeros_like(stf)
            stb[...] = jnp.zeros_like(stb)

    wa = wa_ref[...]
    ba = ba_ref[...]
    pre_f = jnp.dot(lrf[...], wa, preferred_element_type=F32, precision=lax.Precision.HIGHEST) + ba
    pre_b = jnp.dot(lrb[...], wa, preferred_element_type=F32, precision=lax.Precision.HIGHEST) + ba
    _gla_segment(qf[...], kf[...], vf[...], pre_f[:, :HEAD_W], stf, of_ref, reverse=False)
    _gla_segment(qb[...], kb[...], vb[...], pre_b[:, HEAD_W:], stb, ob_ref, reverse=True)

    if emit_final:
        @pl.when(s == pl.num_programs(1) - 1)
        def _():
            sff[...] = stf[...]
            sfb[...] = stb[...]


def _gla(proj3, wa_pad, ba, st0, emit_final):
    bsz, n, _ = proj3.shape
    nseg = n // SEG

    def col(cb, width=HEAD_W, rev=False):
        if rev:
            return pl.BlockSpec((None, SEG, width), lambda b, s: (b, nseg - 1 - s, cb))
        return pl.BlockSpec((None, SEG, width), lambda b, s: (b, s, cb))

    in_specs = [col(COL_Q), col(COL_K), col(COL_V), col(COL_LR, LR_W),
                col(COL_Q, rev=True), col(COL_K, rev=True), col(COL_V, rev=True), col(COL_LR, LR_W, rev=True),
                pl.BlockSpec((LR_W, 2 * HEAD_W), lambda b, s: (0, 0)),
                pl.BlockSpec((1, 2 * HEAD_W), lambda b, s: (0, 0))]
    args = [proj3] * 8 + [wa_pad, ba]
    st_spec = pl.BlockSpec((None, HEAD_W, HEAD_W), lambda b, s: (b, 0, 0))
    if st0 is not None:
        in_specs += [st_spec, st_spec]
        args += [st0[0], st0[1]]
    out_shape = [jax.ShapeDtypeStruct((bsz, n, HEAD_W), F32)] * 2
    out_specs = [pl.BlockSpec((None, SEG, HEAD_W), lambda b, s: (b, s, 0)),
                 pl.BlockSpec((None, SEG, HEAD_W), lambda b, s: (b, nseg - 1 - s, 0))]
    if emit_final:
        out_shape += [jax.ShapeDtypeStruct((bsz, HEAD_W, HEAD_W), F32)] * 2
        out_specs += [st_spec, st_spec]
    return pl.pallas_call(
        functools.partial(_gla_kernel, has_init=st0 is not None, emit_final=emit_final),
        grid=(bsz, nseg),
        in_specs=in_specs,
        out_specs=out_specs,
        out_shape=out_shape,
        scratch_shapes=[pltpu.VMEM((HEAD_W, HEAD_W), F32), pltpu.VMEM((HEAD_W, HEAD_W), F32)],
        compiler_params=_cparams("parallel", "arbitrary"),
        name="gla",
    )(*args)


def _fft_direct_kernel(zr_ref, zi_ref, cn_ref, sn_ref, o_ref):
    o_ref[...] = _dot(cn_ref[...], zr_ref[...]) + _dot(sn_ref[...], zi_ref[...])


def _fft_direct(zr3, zi3, cn, sn):
    bsz, n, w = zr3.shape
    blk = pl.BlockSpec((None, n, w), lambda b: (b, 0, 0))
    tab = pl.BlockSpec((n, n), lambda b: (0, 0))
    return pl.pallas_call(
        _fft_direct_kernel,
        grid=(bsz,),
        in_specs=[blk, blk, tab, tab],
        out_specs=blk,
        out_shape=jax.ShapeDtypeStruct((bsz, n, w), F32),
        compiler_params=_cparams("parallel"),
        name="fft_direct",
    )(zr3, zi3, cn, sn)


def _fft_a_kernel(zr_ref, zi_ref, c_ref, s_ref, gr_ref, gi_ref):
    zr = zr_ref[...]
    zi = zi_ref[...]
    cm = c_ref[...]
    sm = s_ref[...]
    gr_ref[...] = (_dot(cm, zr) + _dot(sm, zi)).astype(BF16)
    gi_ref[...] = (_dot(cm, zi) - _dot(sm, zr)).astype(BF16)


def _fft_c_kernel(gr_ref, gi_ref, mc_ref, ms_ref, o_ref):
    for j in range(gr_ref.shape[0]):
        o_ref[:, j, :] = _dot(mc_ref[j], gr_ref[j]) + _dot(ms_ref[j], gi_ref[j])


def _fft_two_stage(zr3, zi3, tabs):
    bsz, n, w = zr3.shape
    n1 = FFT_N1
    n2 = n // n1
    c1, s1, mc, ms = tabs
    tn = 2048
    wide = n2 * w
    blk = pl.BlockSpec((None, n1, tn), lambda b, j: (b, 0, j))
    tab = pl.BlockSpec((n1, n1), lambda b, j: (0, 0))
    gr, gi = pl.pallas_call(
        _fft_a_kernel,
        grid=(bsz, wide // tn),
        in_specs=[blk, blk, tab, tab],
        out_specs=[blk, blk],
        out_shape=[jax.ShapeDtypeStruct((bsz, n1, wide), BF16)] * 2,
        compiler_params=_cparams("parallel", "parallel"),
        name="fft_stage_a",
    )(zr3.reshape(bsz, n1, wide), zi3.reshape(bsz, n1, wide), c1, s1)
    kb = 8
    gblk = pl.BlockSpec((None, kb, n2, w), lambda b, j: (b, j, 0, 0))
    mblk = pl.BlockSpec((kb, n2, n2), lambda b, j: (j, 0, 0))
    out = pl.pallas_call(
        _fft_c_kernel,
        grid=(bsz, n1 // kb),
        in_specs=[gblk, gblk, mblk, mblk],
        out_specs=pl.BlockSpec((None, n2, kb, w), lambda b, j: (b, 0, j, 0)),
        out_shape=jax.ShapeDtypeStruct((bsz, n2, n1, w), F32),
        compiler_params=_cparams("parallel", "parallel"),
        name="fft_stage_c",
    )(gr.reshape(bsz, n1, n2, w), gi.reshape(bsz, n1, n2, w), mc, ms)
    return out.reshape(bsz, n, w)


def _route(logits, bias, carry):
    tm = logits.shape[1]
    s = _sigmoid(logits)
    biased = s + bias
    neg = -jnp.inf
    rows = lax.broadcasted_iota(jnp.int32, (8, tm), 0)

    def first_argmax(x, ids, sentinel):
        m = jnp.max(x, axis=0, keepdims=True)
        return m, jnp.min(jnp.where(x == m, ids, sentinel), axis=0, keepdims=True)

    gs_rows = []
    for g in range(N_GROUPS):
        x = biased[8 * g:8 * g + 8]
        m1, i1 = first_argmax(x, rows, 8)
        m2 = jnp.max(jnp.where(rows == i1, neg, x), axis=0, keepdims=True)
        gs_rows.append(m1 + m2)
    gs = jnp.concatenate(gs_rows, axis=0)
    gsel = jnp.zeros((N_GROUPS, tm), F32)
    for _ in range(TOPK_GROUPS):
        _, i = first_argmax(gs, rows, 8)
        hit = rows == i
        gsel = jnp.where(hit, 1.0, gsel)
        gs = jnp.where(hit, neg, gs)

    xs = [jnp.where(gsel[g:g + 1] > 0.0, biased[8 * g:8 * g + 8], neg) for g in range(N_GROUPS)]
    ids = [rows + 8 * g for g in range(N_GROUPS)]
    sel = [jnp.zeros((8, tm), F32) for _ in range(N_GROUPS)]
    eids = []
    for _ in range(TOP_K):
        m = xs[0]
        for g in range(1, N_GROUPS):
            m = jnp.maximum(m, xs[g])
        m = jnp.max(m, axis=0, keepdims=True)
        cand = jnp.where(xs[0] == m, ids[0], N_EXPERTS)
        for g in range(1, N_GROUPS):
            cand = jnp.minimum(cand, jnp.where(xs[g] == m, ids[g], N_EXPERTS))
        i = jnp.min(cand, axis=0, keepdims=True)
        eids.append(i)
        for g in range(N_GROUPS):
            hit = ids[g] == i
            sel[g] = jnp.where(hit, 1.0, sel[g])
            xs[g] = jnp.where(hit, neg, xs[g])

    sel_all = jnp.concatenate(sel, axis=0)
    tr = lax.broadcasted_iota(jnp.int32, (tm, tm), 0)
    tc = lax.broadcasted_iota(jnp.int32, (tm, tm), 1)
    before = jnp.where(tr < tc, 1.0, 0.0).astype(BF16)
    seen = _dot(sel_all.astype(BF16), before) + carry
    counts = jnp.sum(sel_all, axis=1, keepdims=True)

    def pick(k, table):
        acc = None
        for g in range(N_GROUPS):
            v = jnp.where(ids[g] == eids[k], table[8 * g:8 * g + 8], 0.0)
            acc = v if acc is None else acc + v
        return jnp.sum(acc, axis=0, keepdims=True)

    w_raw = [pick(k, s) for k in range(TOP_K)]
    ranks = [pick(k, seen) for k in range(TOP_K)]
    tot = w_raw[0]
    for k in range(1, TOP_K):
        tot = tot + w_raw[k]
    weights = [w / tot * ROUTED_SCALE for w in w_raw]
    return eids, weights, ranks, counts


def _mix_out_kernel(x_ref, of_ref, ob_ref, g_ref, su_ref, sv_ref, cb_ref, cc_ref, cx_ref,
                    ccp_ref, cxp_ref, ccn_ref, cxn_ref, ft_ref, mod_ref,
                    glag_ref, sgng_ref, sgnb_ref, wsgu_ref, bsgu_ref, wconv_ref, wout_ref,
                    ln1g_ref, ln1b_ref, wr_ref, rb_ref,
                    x1_ref, xw_ref, wk_ref, eid_ref, rank_ref, cnt_ref, carry_ref, *, seq_len):
    tm = x_ref.shape[0]
    i = pl.program_id(0)

    @pl.when(i == 0)
    def _():
        carry_ref[...] = jnp.zeros_like(carry_ref)

    mod = mod_ref[...]
    g1 = mod[:, 2 * D_MODEL:3 * D_MODEL]
    sh2 = mod[:, 3 * D_MODEL:4 * D_MODEL]
    sc2 = mod[:, 4 * D_MODEL:5 * D_MODEL]

    o = of_ref[...] + ob_ref[...]
    hr = lax.broadcasted_iota(jnp.int32, (HEAD_W, HEAD_W), 0) >> 6
    hc = lax.broadcasted_iota(jnp.int32, (HEAD_W, HEAD_W), 1) >> 6
    head_mean = jnp.where(hr == hc, 1.0 / GLA_DK, 0.0).astype(BF16)
    o2 = o * o
    o2_hi = o2.astype(BF16)
    o2_lo = (o2 - o2_hi.astype(F32)).astype(BF16)
    ms = _dot(o2_hi, head_mean) + _dot(o2_lo, head_mean)
    y_gla = o * lax.rsqrt(ms + RMS_EPS) * glag_ref[...] * _silu(g_ref[...])

    vn = _ln(sv_ref[...]) * sgng_ref[...] + sgnb_ref[...]
    br = lax.broadcasted_iota(jnp.int32, (GMLP_GROUPS * GMLP_CHUNK, HEAD_W), 0) >> 7
    bc = lax.broadcasted_iota(jnp.int32, (GMLP_GROUPS * GMLP_CHUNK, HEAD_W), 1) >> 6
    sgu_bd = br == bc
    sp_parts = []
    for j in range(tm // GMLP_CHUNK):
        vc = vn[j * GMLP_CHUNK:(j + 1) * GMLP_CHUNK]
        vbd = jnp.where(sgu_bd, jnp.concatenate([vc] * GMLP_GROUPS, axis=0), 0.0).astype(BF16)
        sp_parts.append(_dot(wsgu_ref[...], vbd) + bsgu_ref[...])
    y_sgu = su_ref[...] * jnp.concatenate(sp_parts, axis=0)

    z = cc_ref[...] * cx_ref[...]
    z_before = ccp_ref[7:8, :] * cxp_ref[7:8, :]
    z_after = ccn_ref[0:1, :] * cxn_ref[0:1, :]
    row = lax.broadcasted_iota(jnp.int32, (tm, HEAD_W), 0)
    pos = (i * tm + row) & (seq_len - 1)
    z_prev = jnp.where(row == 0, z_before, pltpu.roll(z, 1, 0))
    z_next = jnp.where(row == tm - 1, z_after, pltpu.roll(z, tm - 1, 0))
    z_prev = jnp.where(pos == 0, 0.0, z_prev)
    z_next = jnp.where(pos == seq_len - 1, 0.0, z_next)
    wconv = wconv_ref[...]
    y_conv = cb_ref[...] * (wconv[0:1] * z_prev + wconv[1:2] * z + wconv[2:3] * z_next)

    y = (_dot(y_gla.astype(BF16), wout_ref[0:HEAD_W, :])
         + _dot(y_sgu.astype(BF16), wout_ref[HEAD_W:2 * HEAD_W, :])
         + _dot(y_conv.astype(BF16), wout_ref[2 * HEAD_W:3 * HEAD_W, :])
         + _dot(ft_ref[...].astype(BF16), wout_ref[3 * HEAD_W:4 * HEAD_W, :]))
    x1 = _ln(DEEPNORM_ALPHA * x_ref[...] + g1 * y) * ln1g_ref[...] + ln1b_ref[...]
    x1_ref[...] = x1
    h2 = _ln(x1) * (1.0 + sc2) + sh2
    _store_rows(xw_ref, _pack_bf16_pairs(h2))

    logits = jnp.dot(h2, wr_ref[...], preferred_element_type=F32, precision=lax.Precision.HIGHEST)
    carry = carry_ref[:, 0:1]
    eids, weights, ranks, counts = _route(logits.T[:N_EXPERTS], rb_ref[...], carry)
    eid_ref[...] = jnp.concatenate(eids, axis=0)
    rank_ref[...] = jnp.concatenate(ranks, axis=0).astype(jnp.int32)
    wk = jnp.concatenate(weights + [jnp.zeros((LANES - TOP_K, tm), F32)], axis=0)
    wk_ref[...] = wk.T
    new_carry = carry_ref[...] + counts
    carry_ref[...] = new_carry
    cnt_ref[...] = new_carry


def _mix_out(x2d, of2d, ob2d, proj, yft2d, mod3, mod_row, lw, seq_len, tm):
    t = x2d.shape[0]
    tm = min(tm, t)
    nt8 = t // 8
    rows8 = tm // 8

    def col(cb):
        return pl.BlockSpec((tm, HEAD_W), lambda i: (i, cb))

    def halo_prev(cb):
        return pl.BlockSpec((8, HEAD_W), lambda i: (jnp.maximum(i * rows8 - 1, 0), cb))

    def halo_next(cb):
        return pl.BlockSpec((8, HEAD_W), lambda i: (jnp.minimum((i + 1) * rows8, nt8 - 1), cb))

    def full(a):
        return pl.BlockSpec(a.shape, lambda i: (0,) * a.ndim)

    tok_d = pl.BlockSpec((tm, D_MODEL), lambda i: (i, 0))
    tok_h = pl.BlockSpec((tm, HEAD_W), lambda i: (i, 0))
    weights = [lw["gla_norm_g"], lw["sgu_norm_g"], lw["sgu_norm_b"], lw["w_sgu_cat"], lw["b_sgu_full"],
               lw["w_conv"], lw["w_out"], lw["ln1_g"], lw["ln1_b"], lw["w_router_pad"], lw["router_bias"]]
    in_specs = ([tok_d, tok_h, tok_h, col(COL_G), col(COL_SGU), col(COL_SGV), col(COL_CVB), col(COL_CVC),
                 col(COL_CVX), halo_prev(COL_CVC), halo_prev(COL_CVX), halo_next(COL_CVC), halo_next(COL_CVX),
                 tok_h, pl.BlockSpec((None, 1, 6 * D_MODEL), lambda i: (mod_row(i, tm), 0, 0))]
                + [full(w) for w in weights])
    args = [x2d, of2d, ob2d] + [proj] * 10 + [yft2d, mod3] + weights
    return pl.pallas_call(
        functools.partial(_mix_out_kernel, seq_len=seq_len),
        grid=(t // tm,),
        in_specs=in_specs,
        out_specs=[tok_d,
                   pl.BlockSpec((tm * SLAB, LANES), lambda i: (i, 0)),
                   pl.BlockSpec((tm, LANES), lambda i: (i, 0)),
                   pl.BlockSpec((None, TOP_K, tm), lambda i: (i, 0, 0)),
                   pl.BlockSpec((None, TOP_K, tm), lambda i: (i, 0, 0)),
                   pl.BlockSpec((N_EXPERTS, LANES), lambda i: (0, 0))],
        out_shape=[jax.ShapeDtypeStruct((t, D_MODEL), F32),
                   jax.ShapeDtypeStruct((t * SLAB, LANES), jnp.uint32),
                   jax.ShapeDtypeStruct((t, LANES), F32),
                   jax.ShapeDtypeStruct((t // tm, TOP_K, tm), jnp.int32),
                   jax.ShapeDtypeStruct((t // tm, TOP_K, tm), jnp.int32),
                   jax.ShapeDtypeStruct((N_EXPERTS, LANES), F32)],
        scratch_shapes=[pltpu.VMEM((N_EXPERTS, LANES), F32)],
        compiler_params=_cparams("arbitrary"),
        name="mix_out",
    )(*args)


def _dest_kernel(start_ref, eid_ref, rank_ref, o_ref):
    eid = eid_ref[...]
    base = jnp.zeros_like(eid)
    for e in range(N_EXPERTS):
        base = jnp.where(eid == e, start_ref[e], base)
    o_ref[...] = (base + rank_ref[...]) * SLAB


def _dest_rows(start_pad, eid, rank):
    n_tiles, _, tm = eid.shape
    blk = pl.BlockSpec((None, TOP_K, tm), lambda i, *_: (i, 0, 0))
    return pl.pallas_call(
        _dest_kernel,
        grid_spec=pltpu.PrefetchScalarGridSpec(num_scalar_prefetch=1, grid=(n_tiles,), in_specs=[blk, blk],
                                               out_specs=blk),
        out_shape=jax.ShapeDtypeStruct(eid.shape, jnp.int32),
        compiler_params=_cparams("parallel"),
        name="moe_dest",
    )(start_pad, eid, rank)


def _row_copy(src_hbm, dst_hbm, sem, src_row0, dst_row0):
    src = src_hbm.at[pl.ds(pl.multiple_of(src_row0, SLAB), SLAB)]
    dst = dst_hbm.at[pl.ds(pl.multiple_of(dst_row0, SLAB), SLAB)]
    return pltpu.make_async_copy(src, dst, sem)


def _row_move_kernel(start_ref, padded_ref, dest_ref, zero_hbm, src_hbm, dst_hbm, sem, zsem, *, tm, gather):
    i = pl.program_id(0)
    n = pl.num_programs(0)

    if not gather:
        def zero_copy(e):
            end = start_ref[e] + padded_ref[e]
            off = pl.multiple_of(jnp.maximum(end - EBLK, 0) * SLAB, EBLK * SLAB)
            return pltpu.make_async_copy(zero_hbm, dst_hbm.at[pl.ds(off, EBLK * SLAB)], zsem)

        @pl.when(i == 0)
        def _():
            def z_start(e, c):
                @pl.when(padded_ref[e] > 0)
                def _():
                    zero_copy(e).start()
                return c

            def z_wait(e, c):
                @pl.when(padded_ref[e] > 0)
                def _():
                    zero_copy(e).wait()
                return c

            lax.fori_loop(0, N_EXPERTS, z_start, 0)
            lax.fori_loop(0, N_EXPERTS, z_wait, 0)

    slot_rows = n * tm * SLAB
    group = 16

    def issue(j, c):
        k = j & (TOP_K - 1)
        t0 = pl.multiple_of((j >> 3) * group, group)
        base = pl.multiple_of(k * tm + t0, group)
        pos = [dest_ref[0, base + u] for u in range(group)]
        tok_row0 = (i * tm + t0) * SLAB
        for u in range(group):
            if gather:
                _row_copy(src_hbm, dst_hbm, sem, pos[u], k * slot_rows + tok_row0 + u * SLAB).start(priority=u % 2)
            else:
                _row_copy(src_hbm, dst_hbm, sem, tok_row0 + u * SLAB, pos[u]).start(priority=u % 2)
        return c

    def drain(t, c):
        for k in range(TOP_K):
            _row_copy(src_hbm, dst_hbm, sem, 0, 0).wait()
        return c

    lax.fori_loop(0, TOP_K * (tm // group), issue, 0)

    @pl.when(i > 0)
    def _():
        lax.fori_loop(0, tm, drain, 0)

    @pl.when(i == n - 1)
    def _():
        lax.fori_loop(0, tm, drain, 0)


def _row_move(start_pad, padded, dest, zeros_blk, src, dst_shape, gather):
    n_tiles, _, tm = dest.shape
    dest = dest.reshape(n_tiles, 1, TOP_K * tm)
    smem_blk = pl.BlockSpec((None, 1, TOP_K * tm), lambda i, *_: (i, 0, 0), memory_space=pltpu.SMEM)
    any_spec = pl.BlockSpec(memory_space=pl.ANY)
    return pl.pallas_call(
        functools.partial(_row_move_kernel, tm=tm, gather=gather),
        grid_spec=pltpu.PrefetchScalarGridSpec(
            num_scalar_prefetch=2,
            grid=(n_tiles,),
            in_specs=[smem_blk, any_spec, any_spec],
            out_specs=any_spec,
            scratch_shapes=[pltpu.SemaphoreType.DMA, pltpu.SemaphoreType.DMA]),
        out_shape=jax.ShapeDtypeStruct(dst_shape, jnp.uint32),
        compiler_params=_cparams("arbitrary"),
        name="moe_gather" if gather else "moe_scatter",
    )(start_pad, padded, dest, zeros_blk, src)


def _expert_kernel(be_ref, nu_ref, xs_ref, wg_ref, wu_ref, wd_ref, ys_ref):
    j = pl.program_id(0)

    @pl.when(j < nu_ref[0])
    def _():
        x = _unpack_bf16_pairs(_load_rows(xs_ref)).astype(BF16)
        a = _silu(_dot(x, wg_ref[...].astype(BF16))) * _dot(x, wu_ref[...].astype(BF16))
        _store_rows(ys_ref, _pack_bf16_pairs(_dot(a.astype(BF16), wd_ref[...].astype(BF16))))


def _experts(xs, block_expert, n_used, lw, layer):
    n_rows = xs.shape[0] // SLAB
    ff = EXPERT_FF

    def blk(j, be, nu):
        return (jnp.minimum(j, jnp.maximum(nu[0] - 1, 0)), 0)

    def wblk(j, be, nu):
        return (layer, be[j], 0, 0)

    return pl.pallas_call(
        _expert_kernel,
        grid_spec=pltpu.PrefetchScalarGridSpec(
            num_scalar_prefetch=2,
            grid=(n_rows // EBLK,),
            in_specs=[pl.BlockSpec((EBLK * SLAB, LANES), blk),
                      pl.BlockSpec((None, None, D_MODEL, ff), wblk),
                      pl.BlockSpec((None, None, D_MODEL, ff), wblk),
                      pl.BlockSpec((None, None, ff, D_MODEL), wblk)],
            out_specs=pl.BlockSpec((EBLK * SLAB, LANES), blk)),
        out_shape=jax.ShapeDtypeStruct((n_rows * SLAB, LANES), jnp.uint32),
        compiler_params=_cparams("arbitrary"),
        name="moe_experts",
    )(block_expert, n_used, xs, lw["w_exp_gate"], lw["w_exp_up"], lw["w_exp_down"])


def _moe_final_kernel(x1_ref, xw_ref, *refs):
    g_refs = refs[:TOP_K]
    wk_ref, mod_ref, sg_ref, su_ref, sd_ref, ln2g_ref, ln2b_ref, o_ref = refs[TOP_K:]
    h = _unpack_bf16_pairs(_load_rows(xw_ref)).astype(BF16)
    a = _silu(_dot(h, sg_ref[...])) * _dot(h, su_ref[...])
    acc = _dot(a.astype(BF16), sd_ref[...])
    wk = wk_ref[...]
    for k in range(TOP_K):
        acc = acc + wk[:, k:k + 1] * _unpack_bf16_pairs(_load_rows(g_refs[k]))
    g2 = mod_ref[...][:, 5 * D_MODEL:6 * D_MODEL]
    u = DEEPNORM_ALPHA * x1_ref[...] + g2 * acc
    o_ref[...] = _ln(u) * ln2g_ref[...] + ln2b_ref[...]


def _moe_final(x1, xw, gathered, wk, mod3, mod_row, lw, tm):
    t = x1.shape[0]
    tm = min(tm, t)
    n_tiles = t // tm

    def full(a):
        return pl.BlockSpec(a.shape, lambda i: (0,) * a.ndim)

    def slot(k):
        return pl.BlockSpec((tm * SLAB, LANES), lambda i: (k * n_tiles + i, 0))

    return pl.pallas_call(
        _moe_final_kernel,
        grid=(n_tiles,),
        in_specs=[pl.BlockSpec((tm, D_MODEL), lambda i: (i, 0)),
                  pl.BlockSpec((tm * SLAB, LANES), lambda i: (i, 0))]
        + [slot(k) for k in range(TOP_K)]
        + [pl.BlockSpec((tm, LANES), lambda i: (i, 0)),
                  pl.BlockSpec((None, 1, 6 * D_MODEL), lambda i: (mod_row(i, tm), 0, 0)),
                  full(lw["w_sh_gate"]), full(lw["w_sh_up"]), full(lw["w_sh_down"]),
                  full(lw["ln2_g"]), full(lw["ln2_b"])],
        out_specs=pl.BlockSpec((tm, D_MODEL), lambda i: (i, 0)),
        out_shape=jax.ShapeDtypeStruct((t, D_MODEL), F32),
        compiler_params=_cparams("parallel"),
        name="moe_final",
    )(x1, xw, *([gathered] * TOP_K), wk, mod3, lw["w_sh_gate"], lw["w_sh_up"], lw["w_sh_down"], lw["ln2_g"], lw["ln2_b"])


def _moe(x1, xw, wk, eid, rank, counts_f, mod3, mod_row, lw, layer, tm):
    t = x1.shape[0]
    counts = counts_f[:, 0].astype(jnp.int32)
    padded = ((counts + EBLK - 1) // EBLK) * EBLK
    end_pad = jnp.cumsum(padded)
    start_pad = end_pad - padded
    n_blocks = t * TOP_K // EBLK + N_EXPERTS
    n_used = end_pad[-1] // EBLK
    blk_ids = jnp.minimum(jnp.arange(n_blocks, dtype=jnp.int32), jnp.maximum(n_used - 1, 0))
    block_expert = jnp.minimum(jnp.searchsorted(end_pad, blk_ids * EBLK, side="right"),
                               N_EXPERTS - 1).astype(jnp.int32)
    zeros_blk = jnp.zeros((EBLK * SLAB, LANES), jnp.uint32)
    dest = _dest_rows(start_pad, eid, rank)
    xs = _row_move(start_pad, padded, dest, zeros_blk, xw, (n_blocks * EBLK * SLAB, LANES), gather=False)
    ys = _experts(xs, block_expert, n_used.reshape(1), lw, layer)
    gathered = _row_move(start_pad, padded, dest, zeros_blk, ys, (TOP_K * t * SLAB, LANES), gather=True)
    return _moe_final(x1, xw, gathered, wk, mod3, mod_row, lw, tm)


def _channel_dft_table():
    k = np.arange(FNET_CH, dtype=np.float64)
    ang = 2.0 * np.pi * np.outer(k, k) / FNET_CH
    eye = np.eye(FNET_GROUPS)
    return np.concatenate([np.kron(eye, np.cos(ang)), -np.kron(eye, np.sin(ang))], axis=1)


def _direct_dft_tables(n):
    k = np.arange(n, dtype=np.float64)
    ang = 2.0 * np.pi * (np.outer(k, k) % n) / n
    scale = 1.0 / math.sqrt(n * FNET_CH)
    return np.cos(ang) * scale, np.sin(ang) * scale


def _two_stage_dft_tables(n):
    n1 = FFT_N1
    n2 = n // n1
    a = np.arange(n1, dtype=np.float64)
    ang1 = 2.0 * np.pi * (np.outer(a, a) % n1) / n1
    k1 = np.arange(n1).reshape(n1, 1, 1)
    k2 = np.arange(n2).reshape(1, n2, 1)
    m2 = np.arange(n2).reshape(1, 1, n2)
    ang2 = 2.0 * np.pi * ((m2 * (k1 + n1 * k2)) % n) / n
    scale = 1.0 / math.sqrt(n * FNET_CH)
    return np.cos(ang1), np.sin(ang1), np.cos(ang2) * scale, np.sin(ang2) * scale


def _grid_sincos_table(rows, d):
    quarter = d // 4
    omega = 1.0 / (POS_BASE ** (np.arange(quarter, dtype=np.float64) / quarter))
    r = np.arange(rows, dtype=np.float64)[:, None] * omega
    c = np.arange(GRID_W, dtype=np.float64)[:, None] * omega
    return (np.concatenate([np.sin(r), np.cos(r)], axis=-1).astype(np.float32),
            np.concatenate([np.sin(c), np.cos(c)], axis=-1).astype(np.float32))


def _layer_weights(l, w_in, w_gla_a, b_gla_a, gla_norm_g, sgu_norm_g, sgu_norm_b, w_sgu, b_sgu, w_conv,
                   w_out, ln1_g, ln1_b, ln2_g, ln2_b, w_router, router_bias,
                   w_exp_gate, w_exp_up, w_exp_down, w_sh_gate, w_sh_up, w_sh_down):
    wi = w_in[l]
    lr0 = 4 * HEAD_W
    w_in_p = jnp.concatenate(
        [wi[:, :lr0], wi[:, lr0 + 2 * GLA_LR:], wi[:, lr0:lr0 + 2 * GLA_LR],
         jnp.zeros((D_MODEL, LR_W - 2 * GLA_LR), F32)], axis=1).astype(BF16)
    wa_pad = jnp.zeros((LR_W, 2 * HEAD_W), F32)
    wa_pad = wa_pad.at[:GLA_LR, :HEAD_W].set(w_gla_a[l, 0])
    wa_pad = wa_pad.at[GLA_LR:2 * GLA_LR, HEAD_W:].set(w_gla_a[l, 1])
    row = lambda a: a[l].reshape(1, -1)
    return {
        "w_in_p": w_in_p,
        "wa_pad": wa_pad,
        "ba": jnp.concatenate([b_gla_a[l, 0], b_gla_a[l, 1]]).reshape(1, 2 * HEAD_W),
        "gla_norm_g": row(gla_norm_g), "sgu_norm_g": row(sgu_norm_g), "sgu_norm_b": row(sgu_norm_b),
        "w_sgu_cat": jnp.concatenate([w_sgu[l, g] for g in range(GMLP_GROUPS)], axis=1).astype(BF16),
        "b_sgu_full": jnp.repeat(b_sgu[l].T, HEAD_W // GMLP_GROUPS, axis=1),
        "w_conv": w_conv[l],
        "w_out": w_out[l].astype(BF16),
        "ln1_g": row(ln1_g), "ln1_b": row(ln1_b), "ln2_g": row(ln2_g), "ln2_b": row(ln2_b),
        "w_router_pad": jnp.concatenate([w_router[l], jnp.zeros((D_MODEL, N_EXPERTS), F32)], axis=1),
        "router_bias": router_bias[l].reshape(N_EXPERTS, 1),
        "w_exp_gate": w_exp_gate, "w_exp_up": w_exp_up, "w_exp_down": w_exp_down,
        "w_sh_gate": w_sh_gate[l].astype(BF16), "w_sh_up": w_sh_up[l].astype(BF16),
        "w_sh_down": w_sh_down[l].astype(BF16),
    }


def _state_to_blockdiag_t(s):
    bsz = s.shape[0]
    st = jnp.swapaxes(s, 2, 3)
    eye = jnp.eye(GLA_HEADS, dtype=s.dtype)
    return jnp.einsum("bhvd,hg->bhvgd", st, eye).reshape(bsz, HEAD_W, HEAD_W)


def _blockdiag_t_to_state(st):
    bsz = st.shape[0]
    s5 = st.reshape(bsz, GLA_HEADS, GLA_DK, GLA_HEADS, GLA_DK)
    diag = jnp.stack([s5[:, h, :, h, :] for h in range(GLA_HEADS)], axis=1)
    return jnp.swapaxes(diag, 2, 3)


def _trunk_layer(x3, pos, mod3, mod_row, lw, layer, st0, emit_final, tabs):
    bsz, n, _ = x3.shape
    t = bsz * n
    outs = _in_proj(x3.reshape(t, D_MODEL), pos, mod3, mod_row, lw["w_in_p"], tabs["cs"], tm=256)
    if pos is not None:
        proj, zr, zi, x2d = outs
    else:
        proj, zr, zi = outs
        x2d = x3.reshape(t, D_MODEL)
    gla_out = _gla(proj.reshape(bsz, n, PROJ_W), lw["wa_pad"], lw["ba"], st0, emit_final)
    o_f, o_b = gla_out[:2]
    zr3 = zr.reshape(bsz, n, HEAD_W)
    zi3 = zi.reshape(bsz, n, HEAD_W)
    if "two_stage" in tabs:
        yft = _fft_two_stage(zr3, zi3, tabs["two_stage"])
    else:
        yft = _fft_direct(zr3, zi3, *tabs["direct"])
    x1, xw, wk, eid, rank, counts = _mix_out(x2d, o_f.reshape(t, HEAD_W), o_b.reshape(t, HEAD_W), proj,
                                             yft.reshape(t, HEAD_W), mod3, mod_row, lw, seq_len=n, tm=256)
    x2 = _moe(x1, xw, wk, eid, rank, counts, mod3, mod_row, lw, layer, tm=256)
    return x2.reshape(bsz, n, D_MODEL), gla_out[2:]


def kernel(x_prompt, x_sample, c, state_gla, c_ctx, w_ada, b_ada, w_in, w_gla_a, b_gla_a, gla_norm_g, sgu_norm_g, sgu_norm_b, w_sgu, b_sgu, w_conv, w_out, ln1_g, ln1_b, ln2_g, ln2_b, w_router, router_bias, w_exp_gate, w_exp_up, w_exp_down, w_sh_gate, w_sh_up, w_sh_down):
    n_layers = w_ada.shape[0]
    bp, np_, _ = x_prompt.shape
    bs, ns, _ = x_sample.shape
    assert bs <= 7

    cond8 = jnp.concatenate([c_ctx[None, :], c, jnp.zeros((7 - bs, D_MODEL), F32)], axis=0)
    mod = _ada_mod(cond8, w_ada, b_ada)

    tabs_p = {"cs": jnp.asarray(_channel_dft_table(), BF16),
              "direct": tuple(jnp.asarray(a, BF16) for a in _direct_dft_tables(np_))}
    tabs_s = {"cs": tabs_p["cs"],
              "two_stage": tuple(jnp.asarray(a, BF16) for a in _two_stage_dft_tables(ns))}
    rtab, ctab = _grid_sincos_table(ns // GRID_W, D_MODEL)
    pos = jnp.concatenate([jnp.repeat(jnp.asarray(rtab), GRID_W, axis=0),
                           jnp.tile(jnp.asarray(ctab), (ns // GRID_W, 1))], axis=-1)

    prompt_row = lambda i, tm: 0
    sample_row = lambda i, tm: 1 + (i * tm) // ns

    y_p = x_prompt
    y_s = x_sample
    finals = []
    for l in range(n_layers):
        lw = _layer_weights(l, w_in, w_gla_a, b_gla_a, gla_norm_g, sgu_norm_g, sgu_norm_b, w_sgu, b_sgu,
                            w_conv, w_out, ln1_g, ln1_b, ln2_g, ln2_b, w_router, router_bias,
                            w_exp_gate, w_exp_up, w_exp_down, w_sh_gate, w_sh_up, w_sh_down)
        mod3 = mod[l].reshape(8, 1, 6 * D_MODEL)
        y_p, fin = _trunk_layer(y_p, None, mod3, prompt_row, lw, l, None, True, tabs_p)
        finals.append(jnp.stack([_blockdiag_t_to_state(fin[0]), _blockdiag_t_to_state(fin[1])], axis=1))
        st0 = jnp.stack([_state_to_blockdiag_t(state_gla[:, l, 0]), _state_to_blockdiag_t(state_gla[:, l, 1])])
        y_s, _ = _trunk_layer(y_s, pos if l == 0 else None, mod3, sample_row, lw, l, st0, False, tabs_s)
    new_state = jnp.stack(finals, axis=1).astype(x_prompt.dtype)
    return (y_p, y_s, new_state)
```

```python
import functools
import math

import numpy as np
import jax
import jax.numpy as jnp
from jax import lax
from jax.experimental import pallas as pl
from jax.experimental.pallas import tpu as pltpu

F32 = jnp.float32
BF16 = jnp.bfloat16

D_MODEL = 1024
DEPTH = 2
GRID_W = 64
HEAD_W = 256
GLA_HEADS = 4
GLA_DK = 64
GLA_LR = 16
GLA_TAU = 16.0
GLA_CHUNK = 64
GMLP_GROUPS = 4
GMLP_CHUNK = 128
FNET_GROUPS = 4
FNET_CH = 64
N_EXPERTS = 64
TOP_K = 8
N_GROUPS = 8
TOPK_GROUPS = 4
EXPERT_FF = 256
ROUTED_SCALE = 2.5
DEEPNORM_ALPHA = (2 * DEPTH) ** 0.25
LN_EPS = 1e-5
RMS_EPS = 1e-6
POS_BASE = 10000.0

COL_Q, COL_K, COL_V, COL_G, COL_SGU, COL_SGV, COL_CVB, COL_CVC, COL_CVX, COL_FT = range(10)
LR_W = 128
PROJ_W = 10 * HEAD_W + LR_W
COL_LR = (10 * HEAD_W) // LR_W

LANES = 128
PACK_W = D_MODEL // 2
SLAB = PACK_W // LANES
EBLK = 256
CHUNK = 8
SEG = 256
FFT_N1 = 64
VMEM_LIMIT = 56 * 1024 * 1024


def _cparams(*sem):
    return pltpu.CompilerParams(dimension_semantics=sem, vmem_limit_bytes=VMEM_LIMIT)


def _ln(x):
    mu = jnp.mean(x, axis=-1, keepdims=True)
    xc = x - mu
    var = jnp.mean(xc * xc, axis=-1, keepdims=True)
    return xc * lax.rsqrt(var + LN_EPS)


def _sigmoid(x):
    return 1.0 / (1.0 + jnp.exp(-x))


def _silu(x):
    return x * _sigmoid(x)


def _pack_bf16_pairs(x):
    w = x.shape[1] // 2
    lo = lax.bitcast_convert_type(x[:, :w].astype(BF16).astype(F32), jnp.uint32)
    hi = lax.bitcast_convert_type(x[:, w:].astype(BF16).astype(F32), jnp.uint32)
    return (lo >> 16) | (hi & jnp.uint32(0xFFFF0000))


def _pack_exact_bf16_pairs(x):
    w = x.shape[1] // 2
    lo = lax.bitcast_convert_type(x[:, :w], jnp.uint32)
    hi = lax.bitcast_convert_type(x[:, w:], jnp.uint32)
    return (lo >> 16) | hi


def _unpack_bf16_pairs(u):
    lo = lax.bitcast_convert_type(u << 16, F32)
    hi = lax.bitcast_convert_type(u & jnp.uint32(0xFFFF0000), F32)
    return jnp.concatenate([lo, hi], axis=1)


def _store_rows(ref, packed):
    m = packed.shape[0]
    for c in range(SLAB):
        ref[pl.ds(c, m, stride=SLAB), :] = packed[:, c * LANES:(c + 1) * LANES]


def _load_rows(ref):
    m = ref.shape[0] // SLAB
    return jnp.concatenate([ref[pl.ds(c, m, stride=SLAB), :] for c in range(SLAB)], axis=1)


def _dot(a, b):
    return jnp.dot(a, b, preferred_element_type=F32)


def _dot_nt(a, b):
    return lax.dot_general(a, b, (((1,), (1,)), ((), ())), preferred_element_type=F32)


def _dot_tn(a, b):
    return lax.dot_general(a, b, (((0,), (0,)), ((), ())), preferred_element_type=F32)


def _ada_kernel(c_ref, w_ref, b_ref, o_ref):
    c = c_ref[...]
    o_ref[...] = _dot(_silu(c).astype(BF16), w_ref[...].astype(BF16)) + b_ref[...]


def _ada_mod(cond8, w_ada, b_ada):
    n_l, d, w6 = w_ada.shape
    tn = 1536
    return pl.pallas_call(
        _ada_kernel,
        grid=(n_l, w6 // tn),
        in_specs=[pl.BlockSpec((8, d), lambda l, j: (0, 0)),
                  pl.BlockSpec((None, d, tn), lambda l, j: (l, 0, j)),
                  pl.BlockSpec((None, 1, tn), lambda l, j: (l, 0, j))],
        out_specs=pl.BlockSpec((None, 8, tn), lambda l, j: (l, 0, j)),
        out_shape=jax.ShapeDtypeStruct((n_l, 8, w6), F32),
        compiler_params=_cparams("parallel", "parallel"),
        name="ada_mod",
    )(cond8, w_ada, b_ada.reshape(n_l, 1, w6))


def _in_proj_kernel(*refs, has_pos):
    if has_pos:
        x_ref, pos_ref, mod_ref, w_ref, cs_ref, proj_ref, zr_ref, zi_ref, x0_ref = refs
        x = x_ref[...] + pos_ref[...]
        x0_ref[...] = x
    else:
        x_ref, mod_ref, w_ref, cs_ref, proj_ref, zr_ref, zi_ref = refs
        x = x_ref[...]
    mod = mod_ref[...]
    sh1 = mod[:, 0:D_MODEL]
    sc1 = mod[:, D_MODEL:2 * D_MODEL]
    h = _ln(x) * (1.0 + sc1) + sh1
    proj = _dot(h.astype(BF16), w_ref[...])
    proj_ref[...] = proj
    ft = proj[:, COL_FT * HEAD_W:(COL_FT + 1) * HEAD_W].astype(BF16)
    z = _dot(ft, cs_ref[...])
    zr_ref[...] = z[:, :HEAD_W].astype(BF16)
    zi_ref[...] = z[:, HEAD_W:].astype(BF16)


def _in_proj(x2d, pos, mod3, mod_row, w_in_p, cs, tm):
    t = x2d.shape[0]
    tm = min(tm, t)
    in_specs = [pl.BlockSpec((tm, D_MODEL), lambda i: (i, 0))]
    args = [x2d]
    out_shape = [jax.ShapeDtypeStruct((t, PROJ_W), F32),
                 jax.ShapeDtypeStruct((t, HEAD_W), BF16),
                 jax.ShapeDtypeStruct((t, HEAD_W), BF16)]
    out_specs = [pl.BlockSpec((tm, PROJ_W), lambda i: (i, 0)),
                 pl.BlockSpec((tm, HEAD_W), lambda i: (i, 0)),
                 pl.BlockSpec((tm, HEAD_W), lambda i: (i, 0))]
    if pos is not None:
        n_pos = pos.shape[0] // tm
        in_specs.append(pl.BlockSpec((tm, D_MODEL), lambda i: (i % n_pos, 0)))
        args.append(pos)
        out_shape.append(jax.ShapeDtypeStruct((t, D_MODEL), F32))
        out_specs.append(pl.BlockSpec((tm, D_MODEL), lambda i: (i, 0)))
    in_specs += [pl.BlockSpec((None, 1, 6 * D_MODEL), lambda i: (mod_row(i, tm), 0, 0)),
                 pl.BlockSpec((D_MODEL, PROJ_W), lambda i: (0, 0)),
                 pl.BlockSpec((HEAD_W, 2 * HEAD_W), lambda i: (0, 0))]
    args += [mod3, w_in_p, cs]
    return pl.pallas_call(
        functools.partial(_in_proj_kernel, has_pos=pos is not None),
        grid=(t // tm,),
        in_specs=in_specs,
        out_specs=out_specs,
        out_shape=out_shape,
        compiler_params=_cparams("parallel"),
        name="in_proj",
    )(*args)


def _gla_segment(q, k, v, pre, st_ref, o_ref, reverse):
    seg = q.shape[0]
    n_chunks = seg // GLA_CHUNK
    la = (jnp.minimum(pre, 0.0) - jnp.log1p(jnp.exp(-jnp.abs(pre)))) * (1.0 / GLA_TAU)

    r = lax.broadcasted_iota(jnp.int32, (seg, seg), 0)
    c = lax.broadcasted_iota(jnp.int32, (seg, seg), 1)
    same_chunk = (r >> 6) == (c >> 6)
    tri = same_chunk & ((c >= r) if reverse else (c <= r))
    tri_m = jnp.where(tri, 1.0, 0.0).astype(BF16)
    ones_m = jnp.where(same_chunk, 1.0, 0.0).astype(BF16)
    hi = la.astype(BF16)
    lo = (la - hi.astype(F32)).astype(BF16)
    b = _dot(tri_m, hi) + _dot(tri_m, lo)
    btot = _dot(ones_m, hi) + _dot(ones_m, lo)

    q_dec = q * (GLA_DK ** -0.5) * jnp.exp(b)
    k_inv = (k * jnp.exp(-b)).astype(BF16)
    k_end = (k * jnp.exp(btot - b)).astype(BF16)
    dec = jnp.exp(btot)
    vb = v.astype(BF16)

    bd = (r >> 6) == (c >> 6)
    lane_h = lax.broadcasted_iota(jnp.int32, (GLA_CHUNK, seg), 1) >> 6
    l_idx = lax.broadcasted_iota(jnp.int32, (seg, GLA_CHUNK), 0) & (GLA_CHUNK - 1)
    m_idx = lax.broadcasted_iota(jnp.int32, (seg, GLA_CHUNK), 1)
    causal = (m_idx >= l_idx) if reverse else (m_idx <= l_idx)

    st = st_ref[...]
    order = range(n_chunks - 1, -1, -1) if reverse else range(n_chunks)
    for ci in order:
        sl = slice(ci * GLA_CHUNK, (ci + 1) * GLA_CHUNK)
        qd = q_dec[sl]
        qbd = jnp.where(bd, jnp.concatenate([qd] * GLA_HEADS, axis=0), 0.0).astype(BF16)
        a = _dot_nt(qbd, k_inv[sl])
        a = jnp.where(causal, a, 0.0)
        rr = _dot(a.astype(BF16), vb[sl])
        o = _dot_nt(qd.astype(BF16), st.astype(BF16))
        for h in range(GLA_HEADS):
            o = o + jnp.where(lane_h == h, rr[h * GLA_CHUNK:(h + 1) * GLA_CHUNK], 0.0)
        o_ref[sl, :] = o
        kvt = _dot_tn(vb[sl], k_end[sl])
        st = st * dec[ci * GLA_CHUNK:ci * GLA_CHUNK + 1, :] + jnp.where(bd, kvt, 0.0)
    st_ref[...] = st


def _gla_kernel(*refs, has_init, emit_final):
    qf, kf, vf, lrf, qb, kb, vb, lrb, wa_ref, ba_ref = refs[:10]
    rest = refs[10:]
    if has_init:
        s0f, s0b = rest[:2]
        rest = rest[2:]
    of_ref, ob_ref = rest[:2]
    rest = rest[2:]
    if emit_final:
        sff, sfb = rest[:2]
        rest = rest[2:]
    stf, stb = rest

    s = pl.program_id(1)

    @pl.when(s == 0)
    def _():
        if has_init:
            stf[...] = s0f[...]
            stb[...] = s0b[...]
        else:
            stf[...] = jnp.zeros_like(stf)
            stb[...] = jnp.zeros_like(stb)

    wa = wa_ref[...]
    ba = ba_ref[...]
    pre_f = jnp.dot(lrf[...], wa, preferred_element_type=F32, precision=lax.Precision.HIGHEST) + ba
    pre_b = jnp.dot(lrb[...], wa, preferred_element_type=F32, precision=lax.Precision.HIGHEST) + ba
    _gla_segment(qf[...], kf[...], vf[...], pre_f[:, :HEAD_W], stf, of_ref, reverse=False)
    _gla_segment(qb[...], kb[...], vb[...], pre_b[:, HEAD_W:], stb, ob_ref, reverse=True)

    if emit_final:
        @pl.when(s == pl.num_programs(1) - 1)
        def _():
            sff[...] = stf[...]
            sfb[...] = stb[...]


def _gla(proj3, wa_pad, ba, st0, emit_final):
    bsz, n, _ = proj3.shape
    nseg = n // SEG

    def col(cb, width=HEAD_W, rev=False):
        if rev:
            return pl.BlockSpec((None, SEG, width), lambda b, s: (b, nseg - 1 - s, cb))
        return pl.BlockSpec((None, SEG, width), lambda b, s: (b, s, cb))

    in_specs = [col(COL_Q), col(COL_K), col(COL_V), col(COL_LR, LR_W),
                col(COL_Q, rev=True), col(COL_K, rev=True), col(COL_V, rev=True), col(COL_LR, LR_W, rev=True),
                pl.BlockSpec((LR_W, 2 * HEAD_W), lambda b, s: (0, 0)),
                pl.BlockSpec((1, 2 * HEAD_W), lambda b, s: (0, 0))]
    args = [proj3] * 8 + [wa_pad, ba]
    st_spec = pl.BlockSpec((None, HEAD_W, HEAD_W), lambda b, s: (b, 0, 0))
    if st0 is not None:
        in_specs += [st_spec, st_spec]
        args += [st0[0], st0[1]]
    out_shape = [jax.ShapeDtypeStruct((bsz, n, HEAD_W), F32)] * 2
    out_specs = [pl.BlockSpec((None, SEG, HEAD_W), lambda b, s: (b, s, 0)),
                 pl.BlockSpec((None, SEG, HEAD_W), lambda b, s: (b, nseg - 1 - s, 0))]
    if emit_final:
        out_shape += [jax.ShapeDtypeStruct((bsz, HEAD_W, HEAD_W), F32)] * 2
        out_specs += [st_spec, st_spec]
    return pl.pallas_call(
        functools.partial(_gla_kernel, has_init=st0 is not None, emit_final=emit_final),
        grid=(bsz, nseg),
        in_specs=in_specs,
        out_specs=out_specs,
        out_shape=out_shape,
        scratch_shapes=[pltpu.VMEM((HEAD_W, HEAD_W), F32), pltpu.VMEM((HEAD_W, HEAD_W), F32)],
        compiler_params=_cparams("parallel", "arbitrary"),
        name="gla",
    )(*args)


def _fft_direct_kernel(zr_ref, zi_ref, cn_ref, sn_ref, o_ref):
    o_ref[...] = _dot(cn_ref[...], zr_ref[...]) + _dot(sn_ref[...], zi_ref[...])


def _fft_direct(zr3, zi3, cn, sn):
    bsz, n, w = zr3.shape
    blk = pl.BlockSpec((None, n, w), lambda b: (b, 0, 0))
    tab = pl.BlockSpec((n, n), lambda b: (0, 0))
    return pl.pallas_call(
        _fft_direct_kernel,
        grid=(bsz,),
        in_specs=[blk, blk, tab, tab],
        out_specs=blk,
        out_shape=jax.ShapeDtypeStruct((bsz, n, w), F32),
        compiler_params=_cparams("parallel"),
        name="fft_direct",
    )(zr3, zi3, cn, sn)


def _fft_a_kernel(zr_ref, zi_ref, c_ref, s_ref, gr_ref, gi_ref):
    zr = zr_ref[...]
    zi = zi_ref[...]
    cm = c_ref[...]
    sm = s_ref[...]
    gr_ref[...] = (_dot(cm, zr) + _dot(sm, zi)).astype(BF16)
    gi_ref[...] = (_dot(cm, zi) - _dot(sm, zr)).astype(BF16)


def _fft_c_kernel(gr_ref, gi_ref, mc_ref, ms_ref, o_ref):
    for j in range(gr_ref.shape[0]):
        o_ref[:, j, :] = _dot(mc_ref[j], gr_ref[j]) + _dot(ms_ref[j], gi_ref[j])


def _fft_two_stage(zr3, zi3, tabs):
    bsz, n, w = zr3.shape
    n1 = FFT_N1
    n2 = n // n1
    c1, s1, mc, ms = tabs
    tn = 2048
    wide = n2 * w
    blk = pl.BlockSpec((None, n1, tn), lambda b, j: (b, 0, j))
    tab = pl.BlockSpec((n1, n1), lambda b, j: (0, 0))
    gr, gi = pl.pallas_call(
        _fft_a_kernel,
        grid=(bsz, wide // tn),
        in_specs=[blk, blk, tab, tab],
        out_specs=[blk, blk],
        out_shape=[jax.ShapeDtypeStruct((bsz, n1, wide), BF16)] * 2,
        compiler_params=_cparams("parallel", "parallel"),
        name="fft_stage_a",
    )(zr3.reshape(bsz, n1, wide), zi3.reshape(bsz, n1, wide), c1, s1)
    kb = 8
    gblk = pl.BlockSpec((None, kb, n2, w), lambda b, j: (b, j, 0, 0))
    mblk = pl.BlockSpec((kb, n2, n2), lambda b, j: (j, 0, 0))
    out = pl.pallas_call(
        _fft_c_kernel,
        grid=(bsz, n1 // kb),
        in_specs=[gblk, gblk, mblk, mblk],
        out_specs=pl.BlockSpec((None, n2, kb, w), lambda b, j: (b, 0, j, 0)),
        out_shape=jax.ShapeDtypeStruct((bsz, n2, n1, w), F32),
        compiler_params=_cparams("parallel", "parallel"),
        name="fft_stage_c",
    )(gr.reshape(bsz, n1, n2, w), gi.reshape(bsz, n1, n2, w), mc, ms)
    return out.reshape(bsz, n, w)


def _route(logits, bias):
    tm = logits.shape[1]
    s = _sigmoid(logits)
    biased = s + bias
    neg = -jnp.inf
    rows = lax.broadcasted_iota(jnp.int32, (8, tm), 0)

    def first_argmax(x, ids, sentinel):
        m = jnp.max(x, axis=0, keepdims=True)
        return m, jnp.min(jnp.where(x == m, ids, sentinel), axis=0, keepdims=True)

    gs_rows = []
    for g in range(N_GROUPS):
        x = biased[8 * g:8 * g + 8]
        m1, i1 = first_argmax(x, rows, 8)
        m2 = jnp.max(jnp.where(rows == i1, neg, x), axis=0, keepdims=True)
        gs_rows.append(m1 + m2)
    gs = jnp.concatenate(gs_rows, axis=0)
    gsel = jnp.zeros((N_GROUPS, tm), F32)
    for _ in range(TOPK_GROUPS):
        _, i = first_argmax(gs, rows, 8)
        hit = rows == i
        gsel = jnp.where(hit, 1.0, gsel)
        gs = jnp.where(hit, neg, gs)

    xs = [jnp.where(gsel[g:g + 1] > 0.0, biased[8 * g:8 * g + 8], neg) for g in range(N_GROUPS)]
    ids = [rows + 8 * g for g in range(N_GROUPS)]
    sel = [jnp.zeros((8, tm), F32) for _ in range(N_GROUPS)]
    eids = []
    for _ in range(TOP_K):
        m = xs[0]
        for g in range(1, N_GROUPS):
            m = jnp.maximum(m, xs[g])
        m = jnp.max(m, axis=0, keepdims=True)
        cand = jnp.where(xs[0] == m, ids[0], N_EXPERTS)
        for g in range(1, N_GROUPS):
            cand = jnp.minimum(cand, jnp.where(xs[g] == m, ids[g], N_EXPERTS))
        i = jnp.min(cand, axis=0, keepdims=True)
        eids.append(i)
        for g in range(N_GROUPS):
            hit = ids[g] == i
            sel[g] = jnp.where(hit, 1.0, sel[g])
            xs[g] = jnp.where(hit, neg, xs[g])

    sel_all = jnp.concatenate(sel, axis=0)
    tr = lax.broadcasted_iota(jnp.int32, (tm, tm), 0)
    tc = lax.broadcasted_iota(jnp.int32, (tm, tm), 1)
    before = jnp.where(tr < tc, 1.0, 0.0).astype(BF16)
    seen = _dot(sel_all.astype(BF16), before)
    counts = jnp.sum(sel_all, axis=1, keepdims=True)

    n_chunks = jnp.ceil(counts * (1.0 / CHUNK))
    er = lax.broadcasted_iota(jnp.int32, (N_EXPERTS, N_EXPERTS), 0)
    ec = lax.broadcasted_iota(jnp.int32, (N_EXPERTS, N_EXPERTS), 1)
    lower = jnp.where(ec < er, 1.0, 0.0).astype(BF16)
    run_start = _dot(lower, jnp.broadcast_to(n_chunks, (N_EXPERTS, LANES)).astype(BF16))[:, 0:1] * CHUNK
    local_pos = seen + run_start

    def pick(k, table):
        acc = None
        for g in range(N_GROUPS):
            v = jnp.where(ids[g] == eids[k], table[8 * g:8 * g + 8], 0.0)
            acc = v if acc is None else acc + v
        return jnp.sum(acc, axis=0, keepdims=True)

    w_raw = [pick(k, s) for k in range(TOP_K)]
    lpos = [pick(k, local_pos) for k in range(TOP_K)]
    tot = w_raw[0]
    for k in range(1, TOP_K):
        tot = tot + w_raw[k]
    weights = [w / tot * ROUTED_SCALE for w in w_raw]
    return weights, lpos, n_chunks, run_start


def _mix_out_kernel(x_ref, of_ref, ob_ref, g_ref, su_ref, sv_ref, cb_ref, cc_ref, cx_ref,
                    ccp_ref, cxp_ref, ccn_ref, cxn_ref, ft_ref, mod_ref,
                    glag_ref, sgng_ref, sgnb_ref, wsgu_ref, bsgu_ref, wconv_ref, wout_ref,
                    ln1g_ref, ln1b_ref, wr_ref, rb_ref,
                    x1_ref, xw_ref, wrow_ref, lpos_ref, tab_ref, cnt_ref, carry_ref, *, seq_len):
    tm = x_ref.shape[0]
    i = pl.program_id(0)

    @pl.when(i == 0)
    def _():
        carry_ref[...] = jnp.zeros_like(carry_ref)

    mod = mod_ref[...]
    g1 = mod[:, 2 * D_MODEL:3 * D_MODEL]
    sh2 = mod[:, 3 * D_MODEL:4 * D_MODEL]
    sc2 = mod[:, 4 * D_MODEL:5 * D_MODEL]

    o = of_ref[...] + ob_ref[...]
    hr = lax.broadcasted_iota(jnp.int32, (HEAD_W, HEAD_W), 0) >> 6
    hc = lax.broadcasted_iota(jnp.int32, (HEAD_W, HEAD_W), 1) >> 6
    head_mean = jnp.where(hr == hc, 1.0 / GLA_DK, 0.0).astype(BF16)
    o2 = o * o
    o2_hi = o2.astype(BF16)
    o2_lo = (o2 - o2_hi.astype(F32)).astype(BF16)
    ms = _dot(o2_hi, head_mean) + _dot(o2_lo, head_mean)
    y_gla = o * lax.rsqrt(ms + RMS_EPS) * glag_ref[...] * _silu(g_ref[...])

    vn = _ln(sv_ref[...]) * sgng_ref[...] + sgnb_ref[...]
    br = lax.broadcasted_iota(jnp.int32, (GMLP_GROUPS * GMLP_CHUNK, HEAD_W), 0) >> 7
    bc = lax.broadcasted_iota(jnp.int32, (GMLP_GROUPS * GMLP_CHUNK, HEAD_W), 1) >> 6
    sgu_bd = br == bc
    sp_parts = []
    for j in range(tm // GMLP_CHUNK):
        vc = vn[j * GMLP_CHUNK:(j + 1) * GMLP_CHUNK]
        vbd = jnp.where(sgu_bd, jnp.concatenate([vc] * GMLP_GROUPS, axis=0), 0.0).astype(BF16)
        sp_parts.append(_dot(wsgu_ref[...], vbd) + bsgu_ref[...])
    y_sgu = su_ref[...] * jnp.concatenate(sp_parts, axis=0)

    z = cc_ref[...] * cx_ref[...]
    z_before = ccp_ref[7:8, :] * cxp_ref[7:8, :]
    z_after = ccn_ref[0:1, :] * cxn_ref[0:1, :]
    row = lax.broadcasted_iota(jnp.int32, (tm, HEAD_W), 0)
    pos = (i * tm + row) & (seq_len - 1)
    z_prev = jnp.where(row == 0, z_before, pltpu.roll(z, 1, 0))
    z_next = jnp.where(row == tm - 1, z_after, pltpu.roll(z, tm - 1, 0))
    z_prev = jnp.where(pos == 0, 0.0, z_prev)
    z_next = jnp.where(pos == seq_len - 1, 0.0, z_next)
    wconv = wconv_ref[...]
    y_conv = cb_ref[...] * (wconv[0:1] * z_prev + wconv[1:2] * z + wconv[2:3] * z_next)

    y = (_dot(y_gla.astype(BF16), wout_ref[0:HEAD_W, :])
         + _dot(y_sgu.astype(BF16), wout_ref[HEAD_W:2 * HEAD_W, :])
         + _dot(y_conv.astype(BF16), wout_ref[2 * HEAD_W:3 * HEAD_W, :])
         + _dot(ft_ref[...].astype(BF16), wout_ref[3 * HEAD_W:4 * HEAD_W, :]))
    x1 = _ln(DEEPNORM_ALPHA * x_ref[...] + g1 * y) * ln1g_ref[...] + ln1b_ref[...]
    x1_ref[...] = x1
    h2 = _ln(x1) * (1.0 + sc2) + sh2
    xw_ref[...] = _pack_bf16_pairs(h2)

    logits = jnp.dot(h2, wr_ref[...], preferred_element_type=F32, precision=lax.Precision.HIGHEST)
    weights, lpos, n_chunks, run_start = _route(logits.T[:N_EXPERTS], rb_ref[...])
    lpos_ref[...] = jnp.concatenate(lpos, axis=0).astype(jnp.int32)
    wrow_ref[...] = jnp.concatenate(weights, axis=0)
    carry = carry_ref[:, 0:1]
    lane = lax.broadcasted_iota(jnp.int32, (N_EXPERTS, LANES), 1)
    cols = jnp.where(lane == 0, n_chunks, jnp.where(lane == 1, run_start, jnp.where(lane == 2, carry, 0.0)))
    tab = jnp.concatenate([cols, jnp.zeros((LANES - N_EXPERTS, LANES), F32)], axis=0).T
    tab_ref[...] = tab[0:8].astype(jnp.int32)
    new_carry = carry_ref[...] + n_chunks
    carry_ref[...] = new_carry
    cnt_ref[...] = new_carry


def _mix_out(x2d, of2d, ob2d, proj, yft2d, mod3, mod_row, lw, seq_len, tm):
    t = x2d.shape[0]
    tm = min(tm, t)
    nt8 = t // 8
    rows8 = tm // 8

    def col(cb):
        return pl.BlockSpec((tm, HEAD_W), lambda i: (i, cb))

    def halo_prev(cb):
        return pl.BlockSpec((8, HEAD_W), lambda i: (jnp.maximum(i * rows8 - 1, 0), cb))

    def halo_next(cb):
        return pl.BlockSpec((8, HEAD_W), lambda i: (jnp.minimum((i + 1) * rows8, nt8 - 1), cb))

    def full(a):
        return pl.BlockSpec(a.shape, lambda i: (0,) * a.ndim)

    tok_d = pl.BlockSpec((tm, D_MODEL), lambda i: (i, 0))
    tok_h = pl.BlockSpec((tm, HEAD_W), lambda i: (i, 0))
    weights = [lw["gla_norm_g"], lw["sgu_norm_g"], lw["sgu_norm_b"], lw["w_sgu_cat"], lw["b_sgu_full"],
               lw["w_conv"], lw["w_out"], lw["ln1_g"], lw["ln1_b"], lw["w_router_pad"], lw["router_bias"]]
    in_specs = ([tok_d, tok_h, tok_h, col(COL_G), col(COL_SGU), col(COL_SGV), col(COL_CVB), col(COL_CVC),
                 col(COL_CVX), halo_prev(COL_CVC), halo_prev(COL_CVX), halo_next(COL_CVC), halo_next(COL_CVX),
                 tok_h, pl.BlockSpec((None, 1, 6 * D_MODEL), lambda i: (mod_row(i, tm), 0, 0))]
                + [full(w) for w in weights])
    args = [x2d, of2d, ob2d] + [proj] * 10 + [yft2d, mod3] + weights
    return pl.pallas_call(
        functools.partial(_mix_out_kernel, seq_len=seq_len),
        grid=(t // tm,),
        in_specs=in_specs,
        out_specs=[tok_d,
                   pl.BlockSpec((tm, PACK_W), lambda i: (i, 0)),
                   pl.BlockSpec((None, TOP_K, tm), lambda i: (i, 0, 0)),
                   pl.BlockSpec((None, TOP_K, tm), lambda i: (i, 0, 0)),
                   pl.BlockSpec((None, 8, LANES), lambda i: (i, 0, 0)),
                   pl.BlockSpec((N_EXPERTS, LANES), lambda i: (0, 0))],
        out_shape=[jax.ShapeDtypeStruct((t, D_MODEL), F32),
                   jax.ShapeDtypeStruct((t, PACK_W), jnp.uint32),
                   jax.ShapeDtypeStruct((t // tm, TOP_K, tm), F32),
                   jax.ShapeDtypeStruct((t // tm, TOP_K, tm), jnp.int32),
                   jax.ShapeDtypeStruct((t // tm, 8, LANES), jnp.int32),
                   jax.ShapeDtypeStruct((N_EXPERTS, LANES), F32)],
        scratch_shapes=[pltpu.VMEM((N_EXPERTS, LANES), F32)],
        compiler_params=_cparams("arbitrary"),
        name="mix_out",
    )(*args)


def _dest_kernel(start_ref, eid_ref, rank_ref, o_ref):
    eid = eid_ref[...]
    base = jnp.zeros_like(eid)
    for e in range(N_EXPERTS):
        base = jnp.where(eid == e, start_ref[e], base)
    o_ref[...] = (base + rank_ref[...]) * SLAB


def _dest_rows(start_pad, eid, rank):
    n_tiles, _, tm = eid.shape
    blk = pl.BlockSpec((None, TOP_K, tm), lambda i, *_: (i, 0, 0))
    return pl.pallas_call(
        _dest_kernel,
        grid_spec=pltpu.PrefetchScalarGridSpec(num_scalar_prefetch=1, grid=(n_tiles,), in_specs=[blk, blk],
                                               out_specs=blk),
        out_shape=jax.ShapeDtypeStruct(eid.shape, jnp.int32),
        compiler_params=_cparams("parallel"),
        name="moe_dest",
    )(start_pad, eid, rank)


def _row_copy(src_hbm, dst_hbm, sem, src_row0, dst_row0):
    src = src_hbm.at[pl.ds(pl.multiple_of(src_row0, SLAB), SLAB)]
    dst = dst_hbm.at[pl.ds(pl.multiple_of(dst_row0, SLAB), SLAB)]
    return pltpu.make_async_copy(src, dst, sem)


def _row_move_kernel(start_ref, padded_ref, dest_ref, zero_hbm, src_hbm, dst_hbm, sem, zsem, *, tm, gather):
    i = pl.program_id(0)
    n = pl.num_programs(0)

    if not gather:
        def zero_copy(e):
            end = start_ref[e] + padded_ref[e]
            off = pl.multiple_of(jnp.maximum(end - EBLK, 0) * SLAB, EBLK * SLAB)
            return pltpu.make_async_copy(zero_hbm, dst_hbm.at[pl.ds(off, EBLK * SLAB)], zsem)

        @pl.when(i == 0)
        def _():
            def z_start(e, c):
                @pl.when(padded_ref[e] > 0)
                def _():
                    zero_copy(e).start()
                return c

            def z_wait(e, c):
                @pl.when(padded_ref[e] > 0)
                def _():
                    zero_copy(e).wait()
                return c

            lax.fori_loop(0, N_EXPERTS, z_start, 0)
            lax.fori_loop(0, N_EXPERTS, z_wait, 0)

    slot_rows = n * tm * SLAB
    group = 16

    def issue(j, c):
        k = j & (TOP_K - 1)
        t0 = pl.multiple_of((j >> 3) * group, group)
        base = pl.multiple_of(k * tm + t0, group)
        pos = [dest_ref[0, base + u] for u in range(group)]
        tok_row0 = (i * tm + t0) * SLAB
        for u in range(group):
            if gather:
                _row_copy(src_hbm, dst_hbm, sem, pos[u], k * slot_rows + tok_row0 + u * SLAB).start(priority=u % 2)
            else:
                _row_copy(src_hbm, dst_hbm, sem, tok_row0 + u * SLAB, pos[u]).start(priority=u % 2)
        return c

    def drain(t, c):
        for k in range(TOP_K):
            _row_copy(src_hbm, dst_hbm, sem, 0, 0).wait()
        return c

    lax.fori_loop(0, TOP_K * (tm // group), issue, 0)

    @pl.when(i > 0)
    def _():
        lax.fori_loop(0, tm, drain, 0)

    @pl.when(i == n - 1)
    def _():
        lax.fori_loop(0, tm, drain, 0)


def _row_move(start_pad, padded, dest, zeros_blk, src, dst_shape, gather):
    n_tiles, _, tm = dest.shape
    dest = dest.reshape(n_tiles, 1, TOP_K * tm)
    smem_blk = pl.BlockSpec((None, 1, TOP_K * tm), lambda i, *_: (i, 0, 0), memory_space=pltpu.SMEM)
    any_spec = pl.BlockSpec(memory_space=pl.ANY)
    return pl.pallas_call(
        functools.partial(_row_move_kernel, tm=tm, gather=gather),
        grid_spec=pltpu.PrefetchScalarGridSpec(
            num_scalar_prefetch=2,
            grid=(n_tiles,),
            in_specs=[smem_blk, any_spec, any_spec],
            out_specs=any_spec,
            scratch_shapes=[pltpu.SemaphoreType.DMA, pltpu.SemaphoreType.DMA]),
        out_shape=jax.ShapeDtypeStruct(dst_shape, jnp.uint32),
        compiler_params=_cparams("arbitrary"),
        name="moe_gather" if gather else "moe_scatter",
    )(start_pad, padded, dest, zeros_blk, src)


def _expert_kernel(be_ref, nu_ref, xs_ref, wg_ref, wu_ref, wd_ref, ys_ref):
    j = pl.program_id(0)

    @pl.when(j < nu_ref[0])
    def _():
        x = _unpack_bf16_pairs(_load_rows(xs_ref)).astype(BF16)
        a = _silu(_dot(x, wg_ref[...].astype(BF16))) * _dot(x, wu_ref[...].astype(BF16))
        _store_rows(ys_ref, _pack_bf16_pairs(_dot(a.astype(BF16), wd_ref[...].astype(BF16))))


def _experts(xs, block_expert, n_used, lw, layer):
    n_rows = xs.shape[0] // SLAB
    ff = EXPERT_FF

    def blk(j, be, nu):
        return (jnp.minimum(j, jnp.maximum(nu[0] - 1, 0)), 0)

    def wblk(j, be, nu):
        return (layer, be[j], 0, 0)

    return pl.pallas_call(
        _expert_kernel,
        grid_spec=pltpu.PrefetchScalarGridSpec(
            num_scalar_prefetch=2,
            grid=(n_rows // EBLK,),
            in_specs=[pl.BlockSpec((EBLK * SLAB, LANES), blk),
                      pl.BlockSpec((None, None, D_MODEL, ff), wblk),
                      pl.BlockSpec((None, None, D_MODEL, ff), wblk),
                      pl.BlockSpec((None, None, ff, D_MODEL), wblk)],
            out_specs=pl.BlockSpec((EBLK * SLAB, LANES), blk)),
        out_shape=jax.ShapeDtypeStruct((n_rows * SLAB, LANES), jnp.uint32),
        compiler_params=_cparams("arbitrary"),
        name="moe_experts",
    )(block_expert, n_used, xs, lw["w_exp_gate"], lw["w_exp_up"], lw["w_exp_down"])


def _moe_final_kernel(x1_ref, xw_ref, *refs):
    g_refs = refs[:TOP_K]
    wk_ref, mod_ref, sg_ref, su_ref, sd_ref, ln2g_ref, ln2b_ref, o_ref = refs[TOP_K:]
    h = _unpack_bf16_pairs(_load_rows(xw_ref)).astype(BF16)
    a = _silu(_dot(h, sg_ref[...])) * _dot(h, su_ref[...])
    acc = _dot(a.astype(BF16), sd_ref[...])
    wk = wk_ref[...]
    for k in range(TOP_K):
        acc = acc + wk[:, k:k + 1] * _unpack_bf16_pairs(_load_rows(g_refs[k]))
    g2 = mod_ref[...][:, 5 * D_MODEL:6 * D_MODEL]
    u = DEEPNORM_ALPHA * x1_ref[...] + g2 * acc
    o_ref[...] = _ln(u) * ln2g_ref[...] + ln2b_ref[...]


def _moe_final(x1, xw, gathered, wk, mod3, mod_row, lw, tm):
    t = x1.shape[0]
    tm = min(tm, t)
    n_tiles = t // tm

    def full(a):
        return pl.BlockSpec(a.shape, lambda i: (0,) * a.ndim)

    def slot(k):
        return pl.BlockSpec((tm * SLAB, LANES), lambda i: (k * n_tiles + i, 0))

    return pl.pallas_call(
        _moe_final_kernel,
        grid=(n_tiles,),
        in_specs=[pl.BlockSpec((tm, D_MODEL), lambda i: (i, 0)),
                  pl.BlockSpec((tm * SLAB, LANES), lambda i: (i, 0))]
        + [slot(k) for k in range(TOP_K)]
        + [pl.BlockSpec((tm, LANES), lambda i: (i, 0)),
                  pl.BlockSpec((None, 1, 6 * D_MODEL), lambda i: (mod_row(i, tm), 0, 0)),
                  full(lw["w_sh_gate"]), full(lw["w_sh_up"]), full(lw["w_sh_down"]),
                  full(lw["ln2_g"]), full(lw["ln2_b"])],
        out_specs=pl.BlockSpec((tm, D_MODEL), lambda i: (i, 0)),
        out_shape=jax.ShapeDtypeStruct((t, D_MODEL), F32),
        compiler_params=_cparams("parallel"),
        name="moe_final",
    )(x1, xw, *([gathered] * TOP_K), wk, mod3, lw["w_sh_gate"], lw["w_sh_up"], lw["w_sh_down"], lw["ln2_g"], lw["ln2_b"])


def _moe(x1, xw, wk, eid, rank, counts_f, mod3, mod_row, lw, layer, tm):
    t = x1.shape[0]
    counts = counts_f[:, 0].astype(jnp.int32)
    padded = ((counts + EBLK - 1) // EBLK) * EBLK
    end_pad = jnp.cumsum(padded)
    start_pad = end_pad - padded
    n_blocks = t * TOP_K // EBLK + N_EXPERTS
    n_used = end_pad[-1] // EBLK
    blk_ids = jnp.minimum(jnp.arange(n_blocks, dtype=jnp.int32), jnp.maximum(n_used - 1, 0))
    block_expert = jnp.minimum(jnp.searchsorted(end_pad, blk_ids * EBLK, side="right"),
                               N_EXPERTS - 1).astype(jnp.int32)
    zeros_blk = jnp.zeros((EBLK * SLAB, LANES), jnp.uint32)
    dest = _dest_rows(start_pad, eid, rank)
    xs = _row_move(start_pad, padded, dest, zeros_blk, xw, (n_blocks * EBLK * SLAB, LANES), gather=False)
    ys = _experts(xs, block_expert, n_used.reshape(1), lw, layer)
    gathered = _row_move(start_pad, padded, dest, zeros_blk, ys, (TOP_K * t * SLAB, LANES), gather=True)
    return _moe_final(x1, xw, gathered, wk, mod3, mod_row, lw, tm)


def _local_rows(tm):
    need = tm * TOP_K + N_EXPERTS * (CHUNK - 1)
    return -(-need // EBLK) * EBLK


def _run_copy(local_ref, sorted_hbm, sem, local_row, sorted_row, to_sorted):
    loc = local_ref.at[pl.ds(pl.multiple_of(local_row, CHUNK), CHUNK)]
    srt = sorted_hbm.at[pl.ds(pl.multiple_of(sorted_row, CHUNK), CHUNK)]
    return pltpu.make_async_copy(loc, srt, sem) if to_sorted else pltpu.make_async_copy(srt, loc, sem)


def _start_run_copies(tab_ref, gstart_ref, local_ref, sorted_hbm, sem, to_sorted):
    def per_expert(e, total):
        n = tab_ref[0, e]
        l0 = tab_ref[1, e]
        g0 = (gstart_ref[e] + tab_ref[2, e]) * CHUNK

        def per_chunk(j, c):
            _run_copy(local_ref, sorted_hbm, sem, l0 + j * CHUNK, g0 + j * CHUNK, to_sorted).start()
            return c

        lax.fori_loop(0, n, per_chunk, 0)
        return total + n

    return lax.fori_loop(0, N_EXPERTS, per_expert, 0)


def _wait_run_copies(count, local_ref, sorted_hbm, sem, to_sorted):
    def body(j, c):
        _run_copy(local_ref, sorted_hbm, sem, 0, 0, to_sorted).wait()
        return c

    lax.fori_loop(0, count, body, 0)


def _moe_scatter_kernel(gstart_ref, gtail_ref, xw_ref, lpos_ref, tab_ref, zero_hbm, sorted_hbm,
                        local_ref, sem, zsem, pending_ref):
    i = pl.program_id(0)
    n = pl.num_programs(0)
    slot = i & 1
    tm = xw_ref.shape[0]
    lrows = local_ref.shape[1]

    def zero_copy(e):
        off = pl.multiple_of(jnp.maximum(gtail_ref[e], 0), EBLK)
        return pltpu.make_async_copy(zero_hbm, sorted_hbm.at[pl.ds(off, EBLK)], zsem)

    @pl.when(i == 0)
    def _():
        pending_ref[0] = 0
        pending_ref[1] = 0

        def z_start(e, c):
            @pl.when(gtail_ref[e] >= 0)
            def _():
                zero_copy(e).start()
            return c

        def z_wait(e, c):
            @pl.when(gtail_ref[e] >= 0)
            def _():
                zero_copy(e).wait()
            return c

        lax.fori_loop(0, N_EXPERTS, z_start, 0)
        lax.fori_loop(0, N_EXPERTS, z_wait, 0)

    local = local_ref.at[slot]
    _wait_run_copies(pending_ref[slot], local, sorted_hbm, sem.at[slot], True)

    x = _unpack_bf16_pairs(xw_ref[...]).astype(BF16)
    lpos = lpos_ref[...].astype(jnp.int16)
    one = jnp.ones((EBLK, tm), BF16)
    for b in range(lrows // EBLK):
        riota = lax.broadcasted_iota(jnp.int16, (EBLK, tm), 0) + b * EBLK
        p = jnp.zeros((EBLK, tm), BF16)
        for k in range(TOP_K):
            p = jnp.where(riota == lpos[k:k + 1, :], one, p)
        local[b * EBLK:(b + 1) * EBLK, :] = _pack_exact_bf16_pairs(_dot(p, x))

    pending_ref[slot] = _start_run_copies(tab_ref, gstart_ref, local, sorted_hbm, sem.at[slot], True)

    @pl.when(i == n - 1)
    def _():
        for s in range(2):
            _wait_run_copies(pending_ref[s], local_ref.at[s], sorted_hbm, sem.at[s], True)


def _moe_scatter(gstart, gtail, xw, lpos, tab, n_rows, tm):
    n_tiles = xw.shape[0] // tm
    lrows = _local_rows(tm)
    zeros_blk = jnp.zeros((EBLK, PACK_W), jnp.uint32)
    any_spec = pl.BlockSpec(memory_space=pl.ANY)
    return pl.pallas_call(
        _moe_scatter_kernel,
        grid_spec=pltpu.PrefetchScalarGridSpec(
            num_scalar_prefetch=2,
            grid=(n_tiles,),
            in_specs=[pl.BlockSpec((tm, PACK_W), lambda i, *_: (i, 0)),
                      pl.BlockSpec((None, TOP_K, tm), lambda i, *_: (i, 0, 0)),
                      pl.BlockSpec((None, 8, LANES), lambda i, *_: (i, 0, 0), memory_space=pltpu.SMEM),
                      any_spec],
            out_specs=any_spec,
            scratch_shapes=[pltpu.VMEM((2, lrows, PACK_W), jnp.uint32), pltpu.SemaphoreType.DMA((2,)),
                            pltpu.SemaphoreType.DMA, pltpu.SMEM((2,), jnp.int32)]),
        out_shape=jax.ShapeDtypeStruct((n_rows, PACK_W), jnp.uint32),
        compiler_params=_cparams("arbitrary"),
        name="moe_scatter",
    )(gstart, gtail, xw, lpos, tab, zeros_blk)


def _expert2_kernel(be_ref, nu_ref, xs_ref, wg_ref, wu_ref, wd_ref, ys_ref):
    j = pl.program_id(0)

    @pl.when(j < nu_ref[0])
    def _():
        x = _unpack_bf16_pairs(xs_ref[...]).astype(BF16)
        a = _silu(_dot(x, wg_ref[...].astype(BF16))) * _dot(x, wu_ref[...].astype(BF16))
        ys_ref[...] = _pack_bf16_pairs(_dot(a.astype(BF16), wd_ref[...].astype(BF16)))


def _experts2(xs, block_expert, n_used, lw, layer):
    n_rows = xs.shape[0]
    ff = EXPERT_FF

    def blk(j, be, nu):
        return (jnp.minimum(j, jnp.maximum(nu[0] - 1, 0)), 0)

    def wblk(j, be, nu):
        return (layer, be[j], 0, 0)

    return pl.pallas_call(
        _expert2_kernel,
        grid_spec=pltpu.PrefetchScalarGridSpec(
            num_scalar_prefetch=2,
            grid=(n_rows // EBLK,),
            in_specs=[pl.BlockSpec((EBLK, PACK_W), blk),
                      pl.BlockSpec((None, None, D_MODEL, ff), wblk),
                      pl.BlockSpec((None, None, D_MODEL, ff), wblk),
                      pl.BlockSpec((None, None, ff, D_MODEL), wblk)],
            out_specs=pl.BlockSpec((EBLK, PACK_W), blk)),
        out_shape=jax.ShapeDtypeStruct((n_rows, PACK_W), jnp.uint32),
        compiler_params=_cparams("arbitrary"),
        name="moe_experts",
    )(block_expert, n_used, xs, lw["w_exp_gate"], lw["w_exp_up"], lw["w_exp_down"])


def _moe_combine_kernel(gstart_ref, x1_ref, xw_ref, lpos_ref, wrow_ref, tab_ref, tab_next_ref, mod_ref,
                        sg_ref, su_ref, sd_ref, ln2g_ref, ln2b_ref, sorted_hbm, o_ref,
                        local_ref, sem, pending_ref):
    i = pl.program_id(0)
    n = pl.num_programs(0)
    slot = i & 1
    tm = x1_ref.shape[0]
    lrows = local_ref.shape[1]

    @pl.when(i == 0)
    def _():
        local_ref[...] = jnp.zeros_like(local_ref)
        pending_ref[0] = _start_run_copies(tab_ref, gstart_ref, local_ref.at[0], sorted_hbm, sem.at[0], False)

    @pl.when(i + 1 < n)
    def _():
        nxt = 1 - slot
        pending_ref[nxt] = _start_run_copies(tab_next_ref, gstart_ref, local_ref.at[nxt], sorted_hbm,
                                             sem.at[nxt], False)

    h = _unpack_bf16_pairs(xw_ref[...]).astype(BF16)
    a = _silu(_dot(h, sg_ref[...])) * _dot(h, su_ref[...])
    acc = _dot(a.astype(BF16), sd_ref[...])

    local = local_ref.at[slot]
    _wait_run_copies(pending_ref[slot], local, sorted_hbm, sem.at[slot], False)

    lpos = lpos_ref[...].astype(jnp.int16)
    wrow = wrow_ref[...].astype(BF16)
    for b in range(lrows // EBLK):
        riota = lax.broadcasted_iota(jnp.int16, (EBLK, tm), 0) + b * EBLK
        q = jnp.zeros((EBLK, tm), BF16)
        for k in range(TOP_K):
            q = jnp.where(riota == lpos[k:k + 1, :], jnp.broadcast_to(wrow[k:k + 1, :], (EBLK, tm)), q)
        y = _unpack_bf16_pairs(local[b * EBLK:(b + 1) * EBLK, :]).astype(BF16)
        acc = acc + _dot_tn(q, y)

    g2 = mod_ref[...][:, 5 * D_MODEL:6 * D_MODEL]
    u = DEEPNORM_ALPHA * x1_ref[...] + g2 * acc
    o_ref[...] = _ln(u) * ln2g_ref[...] + ln2b_ref[...]


def _moe_combine(gstart, x1, xw, lpos, wrow, tab, ys, mod3, mod_row, lw, tm):
    t = x1.shape[0]
    n_tiles = t // tm
    lrows = _local_rows(tm)

    def full(a):
        return pl.BlockSpec(a.shape, lambda i, *_: (0,) * a.ndim)

    tab_blk = lambda f: pl.BlockSpec((None, 8, LANES), f, memory_space=pltpu.SMEM)
    return pl.pallas_call(
        _moe_combine_kernel,
        grid_spec=pltpu.PrefetchScalarGridSpec(
            num_scalar_prefetch=1,
            grid=(n_tiles,),
            in_specs=[pl.BlockSpec((tm, D_MODEL), lambda i, *_: (i, 0)),
                      pl.BlockSpec((tm, PACK_W), lambda i, *_: (i, 0)),
                      pl.BlockSpec((None, TOP_K, tm), lambda i, *_: (i, 0, 0)),
                      pl.BlockSpec((None, TOP_K, tm), lambda i, *_: (i, 0, 0)),
                      tab_blk(lambda i, *_: (i, 0, 0)),
                      tab_blk(lambda i, *_: (jnp.minimum(i + 1, n_tiles - 1), 0, 0)),
                      pl.BlockSpec((None, 1, 6 * D_MODEL), lambda i, *_: (mod_row(i, tm), 0, 0)),
                      full(lw["w_sh_gate"]), full(lw["w_sh_up"]), full(lw["w_sh_down"]),
                      full(lw["ln2_g"]), full(lw["ln2_b"]),
                      pl.BlockSpec(memory_space=pl.ANY)],
            out_specs=pl.BlockSpec((tm, D_MODEL), lambda i, *_: (i, 0)),
            scratch_shapes=[pltpu.VMEM((2, lrows, PACK_W), jnp.uint32), pltpu.SemaphoreType.DMA((2,)),
                            pltpu.SMEM((2,), jnp.int32)]),
        out_shape=jax.ShapeDtypeStruct((t, D_MODEL), F32),
        compiler_params=_cparams("arbitrary"),
        name="moe_combine",
    )(gstart, x1, xw, lpos, wrow, tab, tab, mod3, lw["w_sh_gate"], lw["w_sh_up"], lw["w_sh_down"],
      lw["ln2_g"], lw["ln2_b"], ys)


def _moe2(x1, xw, wrow, lpos, tab, chunks_f, mod3, mod_row, lw, layer, tm):
    t = x1.shape[0]
    n_tiles = t // tm
    region_rows = chunks_f[:, 0].astype(jnp.int32) * CHUNK
    padded = ((region_rows + EBLK - 1) // EBLK) * EBLK
    end = jnp.cumsum(padded)
    start = end - padded
    n_blocks = -(-(t * TOP_K + n_tiles * N_EXPERTS * (CHUNK - 1)) // EBLK) + N_EXPERTS
    n_used = end[-1] // EBLK
    blk_row = jnp.minimum(jnp.arange(n_blocks, dtype=jnp.int32), jnp.maximum(n_used - 1, 0)) * EBLK
    block_expert = jnp.minimum(jnp.sum((end[None, :] <= blk_row[:, None]).astype(jnp.int32), axis=1),
                               N_EXPERTS - 1).astype(jnp.int32)
    gstart = start // CHUNK
    gtail = jnp.where(padded > 0, end - EBLK, -1)
    xs = _moe_scatter(gstart, gtail, xw, lpos, tab, n_blocks * EBLK, tm)
    ys = _experts2(xs, block_expert, n_used.reshape(1), lw, layer)
    return _moe_combine(gstart, x1, xw, lpos, wrow, tab, ys, mod3, mod_row, lw, tm)


def _channel_dft_table():
    k = np.arange(FNET_CH, dtype=np.float64)
    ang = 2.0 * np.pi * np.outer(k, k) / FNET_CH
    eye = np.eye(FNET_GROUPS)
    return np.concatenate([np.kron(eye, np.cos(ang)), -np.kron(eye, np.sin(ang))], axis=1)


def _direct_dft_tables(n):
    k = np.arange(n, dtype=np.float64)
    ang = 2.0 * np.pi * (np.outer(k, k) % n) / n
    scale = 1.0 / math.sqrt(n * FNET_CH)
    return np.cos(ang) * scale, np.sin(ang) * scale


def _two_stage_dft_tables(n):
    n1 = FFT_N1
    n2 = n // n1
    a = np.arange(n1, dtype=np.float64)
    ang1 = 2.0 * np.pi * (np.outer(a, a) % n1) / n1
    k1 = np.arange(n1).reshape(n1, 1, 1)
    k2 = np.arange(n2).reshape(1, n2, 1)
    m2 = np.arange(n2).reshape(1, 1, n2)
    ang2 = 2.0 * np.pi * ((m2 * (k1 + n1 * k2)) % n) / n
    scale = 1.0 / math.sqrt(n * FNET_CH)
    return np.cos(ang1), np.sin(ang1), np.cos(ang2) * scale, np.sin(ang2) * scale


def _grid_sincos_table(rows, d):
    quarter = d // 4
    omega = 1.0 / (POS_BASE ** (np.arange(quarter, dtype=np.float64) / quarter))
    r = np.arange(rows, dtype=np.float64)[:, None] * omega
    c = np.arange(GRID_W, dtype=np.float64)[:, None] * omega
    return (np.concatenate([np.sin(r), np.cos(r)], axis=-1).astype(np.float32),
            np.concatenate([np.sin(c), np.cos(c)], axis=-1).astype(np.float32))


def _layer_weights(l, w_in, w_gla_a, b_gla_a, gla_norm_g, sgu_norm_g, sgu_norm_b, w_sgu, b_sgu, w_conv,
                   w_out, ln1_g, ln1_b, ln2_g, ln2_b, w_router, router_bias,
                   w_exp_gate, w_exp_up, w_exp_down, w_sh_gate, w_sh_up, w_sh_down):
    wi = w_in[l]
    lr0 = 4 * HEAD_W
    w_in_p = jnp.concatenate(
        [wi[:, :lr0], wi[:, lr0 + 2 * GLA_LR:], wi[:, lr0:lr0 + 2 * GLA_LR],
         jnp.zeros((D_MODEL, LR_W - 2 * GLA_LR), F32)], axis=1).astype(BF16)
    wa_pad = jnp.zeros((LR_W, 2 * HEAD_W), F32)
    wa_pad = wa_pad.at[:GLA_LR, :HEAD_W].set(w_gla_a[l, 0])
    wa_pad = wa_pad.at[GLA_LR:2 * GLA_LR, HEAD_W:].set(w_gla_a[l, 1])
    row = lambda a: a[l].reshape(1, -1)
    return {
        "w_in_p": w_in_p,
        "wa_pad": wa_pad,
        "ba": jnp.concatenate([b_gla_a[l, 0], b_gla_a[l, 1]]).reshape(1, 2 * HEAD_W),
        "gla_norm_g": row(gla_norm_g), "sgu_norm_g": row(sgu_norm_g), "sgu_norm_b": row(sgu_norm_b),
        "w_sgu_cat": jnp.concatenate([w_sgu[l, g] for g in range(GMLP_GROUPS)], axis=1).astype(BF16),
        "b_sgu_full": jnp.repeat(b_sgu[l].T, HEAD_W // GMLP_GROUPS, axis=1),
        "w_conv": w_conv[l],
        "w_out": w_out[l].astype(BF16),
        "ln1_g": row(ln1_g), "ln1_b": row(ln1_b), "ln2_g": row(ln2_g), "ln2_b": row(ln2_b),
        "w_router_pad": jnp.concatenate([w_router[l], jnp.zeros((D_MODEL, N_EXPERTS), F32)], axis=1),
        "router_bias": router_bias[l].reshape(N_EXPERTS, 1),
        "w_exp_gate": w_exp_gate, "w_exp_up": w_exp_up, "w_exp_down": w_exp_down,
        "w_sh_gate": w_sh_gate[l].astype(BF16), "w_sh_up": w_sh_up[l].astype(BF16),
        "w_sh_down": w_sh_down[l].astype(BF16),
    }


def _state_to_blockdiag_t(s):
    bsz = s.shape[0]
    st = jnp.swapaxes(s, 2, 3)
    eye = jnp.eye(GLA_HEADS, dtype=s.dtype)
    return jnp.einsum("bhvd,hg->bhvgd", st, eye).reshape(bsz, HEAD_W, HEAD_W)


def _blockdiag_t_to_state(st):
    bsz = st.shape[0]
    s5 = st.reshape(bsz, GLA_HEADS, GLA_DK, GLA_HEADS, GLA_DK)
    diag = jnp.stack([s5[:, h, :, h, :] for h in range(GLA_HEADS)], axis=1)
    return jnp.swapaxes(diag, 2, 3)


def _trunk_layer(x3, pos, mod3, mod_row, lw, layer, st0, emit_final, tabs):
    bsz, n, _ = x3.shape
    t = bsz * n
    outs = _in_proj(x3.reshape(t, D_MODEL), pos, mod3, mod_row, lw["w_in_p"], tabs["cs"], tm=256)
    if pos is not None:
        proj, zr, zi, x2d = outs
    else:
        proj, zr, zi = outs
        x2d = x3.reshape(t, D_MODEL)
    gla_out = _gla(proj.reshape(bsz, n, PROJ_W), lw["wa_pad"], lw["ba"], st0, emit_final)
    o_f, o_b = gla_out[:2]
    zr3 = zr.reshape(bsz, n, HEAD_W)
    zi3 = zi.reshape(bsz, n, HEAD_W)
    if "two_stage" in tabs:
        yft = _fft_two_stage(zr3, zi3, tabs["two_stage"])
    else:
        yft = _fft_direct(zr3, zi3, *tabs["direct"])
    x1, xw, wrow, lpos, tab, chunks = _mix_out(x2d, o_f.reshape(t, HEAD_W), o_b.reshape(t, HEAD_W), proj,
                                             yft.reshape(t, HEAD_W), mod3, mod_row, lw, seq_len=n, tm=256)
    x2 = _moe2(x1, xw, wrow, lpos, tab, chunks, mod3, mod_row, lw, layer, tm=min(256, t))
    return x2.reshape(bsz, n, D_MODEL), gla_out[2:]


def kernel(x_prompt, x_sample, c, state_gla, c_ctx, w_ada, b_ada, w_in, w_gla_a, b_gla_a, gla_norm_g, sgu_norm_g, sgu_norm_b, w_sgu, b_sgu, w_conv, w_out, ln1_g, ln1_b, ln2_g, ln2_b, w_router, router_bias, w_exp_gate, w_exp_up, w_exp_down, w_sh_gate, w_sh_up, w_sh_down):
    n_layers = w_ada.shape[0]
    bp, np_, _ = x_prompt.shape
    bs, ns, _ = x_sample.shape
    assert bs <= 7

    cond8 = jnp.concatenate([c_ctx[None, :], c, jnp.zeros((7 - bs, D_MODEL), F32)], axis=0)
    mod = _ada_mod(cond8, w_ada, b_ada)

    tabs_p = {"cs": jnp.asarray(_channel_dft_table(), BF16),
              "direct": tuple(jnp.asarray(a, BF16) for a in _direct_dft_tables(np_))}
    tabs_s = {"cs": tabs_p["cs"],
              "two_stage": tuple(jnp.asarray(a, BF16) for a in _two_stage_dft_tables(ns))}
    rtab, ctab = _grid_sincos_table(ns // GRID_W, D_MODEL)
    pos = jnp.concatenate([jnp.repeat(jnp.asarray(rtab), GRID_W, axis=0),
                           jnp.tile(jnp.asarray(ctab), (ns // GRID_W, 1))], axis=-1)

    prompt_row = lambda i, tm: 0
    sample_row = lambda i, tm: 1 + (i * tm) // ns

    y_p = x_prompt
    y_s = x_sample
    finals = []
    for l in range(n_layers):
        lw = _layer_weights(l, w_in, w_gla_a, b_gla_a, gla_norm_g, sgu_norm_g, sgu_norm_b, w_sgu, b_sgu,
                            w_conv, w_out, ln1_g, ln1_b, ln2_g, ln2_b, w_router, router_bias,
                            w_exp_gate, w_exp_up, w_exp_down, w_sh_gate, w_sh_up, w_sh_down)
        mod3 = mod[l].reshape(8, 1, 6 * D_MODEL)
        y_p, fin = _trunk_layer(y_p, None, mod3, prompt_row, lw, l, None, True, tabs_p)
        finals.append(jnp.stack([_blockdiag_t_to_state(fin[0]), _blockdiag_t_to_state(fin[1])], axis=1))
        st0 = jnp.stack([_state_to_blockdiag_t(state_gla[:, l, 0]), _state_to_blockdiag_t(state_gla[:, l, 1])])
        y_s, _ = _trunk_layer(y_s, pos if l == 0 else None, mod3, sample_row, lw, l, st0, False, tabs_s)
    new_state = jnp.stack(finals, axis=1).astype(x_prompt.dtype)
    return (y_p, y_s, new_state)
```

```python
import functools
import math

import numpy as np
import jax
import jax.numpy as jnp
from jax import lax
from jax.experimental import pallas as pl
from jax.experimental.pallas import tpu as pltpu

F32 = jnp.float32
BF16 = jnp.bfloat16

D_MODEL = 1024
DEPTH = 2
GRID_W = 64
HEAD_W = 256
GLA_HEADS = 4
GLA_DK = 64
GLA_LR = 16
GLA_TAU = 16.0
GLA_CHUNK = 64
GMLP_GROUPS = 4
GMLP_CHUNK = 128
FNET_GROUPS = 4
FNET_CH = 64
N_EXPERTS = 64
TOP_K = 8
N_GROUPS = 8
TOPK_GROUPS = 4
EXPERT_FF = 256
ROUTED_SCALE = 2.5
DEEPNORM_ALPHA = (2 * DEPTH) ** 0.25
LN_EPS = 1e-5
RMS_EPS = 1e-6
POS_BASE = 10000.0

COL_Q, COL_K, COL_V, COL_G, COL_SGU, COL_SGV, COL_CVB, COL_CVC, COL_CVX, COL_FT = range(10)
LR_W = 128
PROJ_W = 10 * HEAD_W + LR_W
COL_LR = (10 * HEAD_W) // LR_W

LANES = 128
PACK_W = D_MODEL // 2
EBLK = 512
LBLK = 256
CHUNK = 8
SEG = 256
FFT_N1 = 64
VMEM_LIMIT = 56 * 1024 * 1024


def _cparams(*sem):
    return pltpu.CompilerParams(dimension_semantics=sem, vmem_limit_bytes=VMEM_LIMIT)


def _ln(x):
    mu = jnp.mean(x, axis=-1, keepdims=True)
    xc = x - mu
    var = jnp.mean(xc * xc, axis=-1, keepdims=True)
    return xc * lax.rsqrt(var + LN_EPS)


def _sigmoid(x):
    return 1.0 / (1.0 + jnp.exp(-x))


def _silu(x):
    return x * _sigmoid(x)


def _pack_bf16_pairs(x):
    w = x.shape[1] // 2
    lo = lax.bitcast_convert_type(x[:, :w].astype(BF16).astype(F32), jnp.uint32)
    hi = lax.bitcast_convert_type(x[:, w:].astype(BF16).astype(F32), jnp.uint32)
    return (lo >> 16) | (hi & jnp.uint32(0xFFFF0000))


def _pack_exact_bf16_pairs(x):
    w = x.shape[1] // 2
    lo = lax.bitcast_convert_type(x[:, :w], jnp.uint32)
    hi = lax.bitcast_convert_type(x[:, w:], jnp.uint32)
    return (lo >> 16) | hi


def _unpack_bf16_pairs(u):
    lo = lax.bitcast_convert_type(u << 16, F32)
    hi = lax.bitcast_convert_type(u & jnp.uint32(0xFFFF0000), F32)
    return jnp.concatenate([lo, hi], axis=1)


def _dot(a, b):
    return jnp.dot(a, b, preferred_element_type=F32)


def _dot_nt(a, b):
    return lax.dot_general(a, b, (((1,), (1,)), ((), ())), preferred_element_type=F32)


def _dot_tn(a, b):
    return lax.dot_general(a, b, (((0,), (0,)), ((), ())), preferred_element_type=F32)


def _ada_kernel(c_ref, w_ref, b_ref, o_ref):
    c = c_ref[...]
    o_ref[...] = _dot(_silu(c).astype(BF16), w_ref[...].astype(BF16)) + b_ref[...]


def _ada_mod(cond8, w_ada, b_ada):
    n_l, d, w6 = w_ada.shape
    tn = 1536
    return pl.pallas_call(
        _ada_kernel,
        grid=(n_l, w6 // tn),
        in_specs=[pl.BlockSpec((8, d), lambda l, j: (0, 0)),
                  pl.BlockSpec((None, d, tn), lambda l, j: (l, 0, j)),
                  pl.BlockSpec((None, 1, tn), lambda l, j: (l, 0, j))],
        out_specs=pl.BlockSpec((None, 8, tn), lambda l, j: (l, 0, j)),
        out_shape=jax.ShapeDtypeStruct((n_l, 8, w6), F32),
        compiler_params=_cparams("parallel", "parallel"),
        name="ada_mod",
    )(cond8, w_ada, b_ada.reshape(n_l, 1, w6))


def _in_proj_kernel(*refs, has_pos):
    if has_pos:
        x_ref, pos_ref, mod_ref, w_ref, cs_ref, proj_ref, zr_ref, zi_ref, x0_ref = refs
        x = x_ref[...] + pos_ref[...]
        x0_ref[...] = x
    else:
        x_ref, mod_ref, w_ref, cs_ref, proj_ref, zr_ref, zi_ref = refs
        x = x_ref[...]
    mod = mod_ref[...]
    sh1 = mod[:, 0:D_MODEL]
    sc1 = mod[:, D_MODEL:2 * D_MODEL]
    h = _ln(x) * (1.0 + sc1) + sh1
    proj = _dot(h.astype(BF16), w_ref[...])
    proj_ref[...] = proj
    ft = proj[:, COL_FT * HEAD_W:(COL_FT + 1) * HEAD_W].astype(BF16)
    z = _dot(ft, cs_ref[...])
    zr_ref[...] = z[:, :HEAD_W].astype(BF16)
    zi_ref[...] = z[:, HEAD_W:].astype(BF16)


def _in_proj(x2d, pos, mod3, mod_row, w_in_p, cs, tm):
    t = x2d.shape[0]
    tm = min(tm, t)
    in_specs = [pl.BlockSpec((tm, D_MODEL), lambda i: (i, 0))]
    args = [x2d]
    out_shape = [jax.ShapeDtypeStruct((t, PROJ_W), F32),
                 jax.ShapeDtypeStruct((t, HEAD_W), BF16),
                 jax.ShapeDtypeStruct((t, HEAD_W), BF16)]
    out_specs = [pl.BlockSpec((tm, PROJ_W), lambda i: (i, 0)),
                 pl.BlockSpec((tm, HEAD_W), lambda i: (i, 0)),
                 pl.BlockSpec((tm, HEAD_W), lambda i: (i, 0))]
    if pos is not None:
        n_pos = pos.shape[0] // tm
        in_specs.append(pl.BlockSpec((tm, D_MODEL), lambda i: (i % n_pos, 0)))
        args.append(pos)
        out_shape.append(jax.ShapeDtypeStruct((t, D_MODEL), F32))
        out_specs.append(pl.BlockSpec((tm, D_MODEL), lambda i: (i, 0)))
    in_specs += [pl.BlockSpec((None, 1, 6 * D_MODEL), lambda i: (mod_row(i, tm), 0, 0)),
                 pl.BlockSpec((D_MODEL, PROJ_W), lambda i: (0, 0)),
                 pl.BlockSpec((HEAD_W, 2 * HEAD_W), lambda i: (0, 0))]
    args += [mod3, w_in_p, cs]
    return pl.pallas_call(
        functools.partial(_in_proj_kernel, has_pos=pos is not None),
        grid=(t // tm,),
        in_specs=in_specs,
        out_specs=out_specs,
        out_shape=out_shape,
        compiler_params=_cparams("parallel"),
        name="in_proj",
    )(*args)


def _gla_segment(q, k, v, pre, st_ref, o_ref, reverse):
    seg = q.shape[0]
    n_chunks = seg // GLA_CHUNK
    la = (jnp.minimum(pre, 0.0) - jnp.log1p(jnp.exp(-jnp.abs(pre)))) * (1.0 / GLA_TAU)

    r = lax.broadcasted_iota(jnp.int32, (seg, seg), 0)
    c = lax.broadcasted_iota(jnp.int32, (seg, seg), 1)
    same_chunk = (r >> 6) == (c >> 6)
    tri = same_chunk & ((c >= r) if reverse else (c <= r))
    tri_m = jnp.where(tri, 1.0, 0.0).astype(BF16)
    ones_m = jnp.where(same_chunk, 1.0, 0.0).astype(BF16)
    hi = la.astype(BF16)
    lo = (la - hi.astype(F32)).astype(BF16)
    b = _dot(tri_m, hi) + _dot(tri_m, lo)
    btot = _dot(ones_m, hi) + _dot(ones_m, lo)

    q_dec = q * (GLA_DK ** -0.5) * jnp.exp(b)
    k_inv = (k * jnp.exp(-b)).astype(BF16)
    k_end = (k * jnp.exp(btot - b)).astype(BF16)
    dec = jnp.exp(btot)
    vb = v.astype(BF16)

    bd = (r >> 6) == (c >> 6)
    lane_h = lax.broadcasted_iota(jnp.int32, (GLA_CHUNK, seg), 1) >> 6
    l_idx = lax.broadcasted_iota(jnp.int32, (seg, GLA_CHUNK), 0) & (GLA_CHUNK - 1)
    m_idx = lax.broadcasted_iota(jnp.int32, (seg, GLA_CHUNK), 1)
    causal = (m_idx >= l_idx) if reverse else (m_idx <= l_idx)

    st = st_ref[...]
    order = range(n_chunks - 1, -1, -1) if reverse else range(n_chunks)
    for ci in order:
        sl = slice(ci * GLA_CHUNK, (ci + 1) * GLA_CHUNK)
        qd = q_dec[sl]
        qbd = jnp.where(bd, jnp.concatenate([qd] * GLA_HEADS, axis=0), 0.0).astype(BF16)
        a = _dot_nt(qbd, k_inv[sl])
        a = jnp.where(causal, a, 0.0)
        rr = _dot(a.astype(BF16), vb[sl])
        o = _dot_nt(qd.astype(BF16), st.astype(BF16))
        for h in range(GLA_HEADS):
            o = o + jnp.where(lane_h == h, rr[h * GLA_CHUNK:(h + 1) * GLA_CHUNK], 0.0)
        o_ref[sl, :] = o
        kvt = _dot_tn(vb[sl], k_end[sl])
        st = st * dec[ci * GLA_CHUNK:ci * GLA_CHUNK + 1, :] + jnp.where(bd, kvt, 0.0)
    st_ref[...] = st


def _gla_kernel(*refs, has_init, emit_final):
    qf, kf, vf, lrf, qb, kb, vb, lrb, wa_ref, ba_ref = refs[:10]
    rest = refs[10:]
    if has_init:
        s0f, s0b = rest[:2]
        rest = rest[2:]
    of_ref, ob_ref = rest[:2]
    rest = rest[2:]
    if emit_final:
        sff, sfb = rest[:2]
        rest = rest[2:]
    stf, stb = rest

    s = pl.program_id(1)

    @pl.when(s == 0)
    def _():
        if has_init:
            stf[...] = s0f[...]
            stb[...] = s0b[...]
        else:
            stf[...] = jnp.zeros_like(stf)
            stb[...] = jnp.zeros_like(stb)

    wa = wa_ref[...]
    ba = ba_ref[...]
    pre_f = jnp.dot(lrf[...], wa, preferred_element_type=F32, precision=lax.Precision.HIGHEST) + ba
    pre_b = jnp.dot(lrb[...], wa, preferred_element_type=F32, precision=lax.Precision.HIGHEST) + ba
    _gla_segment(qf[...], kf[...], vf[...], pre_f[:, :HEAD_W], stf, of_ref, reverse=False)
    _gla_segment(qb[...], kb[...], vb[...], pre_b[:, HEAD_W:], stb, ob_ref, reverse=True)

    if emit_final:
        @pl.when(s == pl.num_programs(1) - 1)
        def _():
            sff[...] = stf[...]
            sfb[...] = stb[...]


def _gla(proj3, wa_pad, ba, st0, emit_final):
    bsz, n, _ = proj3.shape
    nseg = n // SEG

    def col(cb, width=HEAD_W, rev=False):
        if rev:
            return pl.BlockSpec((None, SEG, width), lambda b, s: (b, nseg - 1 - s, cb))
        return pl.BlockSpec((None, SEG, width), lambda b, s: (b, s, cb))

    in_specs = [col(COL_Q), col(COL_K), col(COL_V), col(COL_LR, LR_W),
                col(COL_Q, rev=True), col(COL_K, rev=True), col(COL_V, rev=True), col(COL_LR, LR_W, rev=True),
                pl.BlockSpec((LR_W, 2 * HEAD_W), lambda b, s: (0, 0)),
                pl.BlockSpec((1, 2 * HEAD_W), lambda b, s: (0, 0))]
    args = [proj3] * 8 + [wa_pad, ba]
    st_spec = pl.BlockSpec((None, HEAD_W, HEAD_W), lambda b, s: (b, 0, 0))
    if st0 is not None:
        in_specs += [st_spec, st_spec]
        args += [st0[0], st0[1]]
    out_shape = [jax.ShapeDtypeStruct((bsz, n, HEAD_W), F32)] * 2
    out_specs = [pl.BlockSpec((None, SEG, HEAD_W), lambda b, s: (b, s, 0)),
                 pl.BlockSpec((None, SEG, HEAD_W), lambda b, s: (b, nseg - 1 - s, 0))]
    if emit_final:
        out_shape += [jax.ShapeDtypeStruct((bsz, HEAD_W, HEAD_W), F32)] * 2
        out_specs += [st_spec, st_spec]
    return pl.pallas_call(
        functools.partial(_gla_kernel, has_init=st0 is not None, emit_final=emit_final),
        grid=(bsz, nseg),
        in_specs=in_specs,
        out_specs=out_specs,
        out_shape=out_shape,
        scratch_shapes=[pltpu.VMEM((HEAD_W, HEAD_W), F32), pltpu.VMEM((HEAD_W, HEAD_W), F32)],
        compiler_params=_cparams("parallel", "arbitrary"),
        name="gla",
    )(*args)


def _fft_direct_kernel(zr_ref, zi_ref, cn_ref, sn_ref, o_ref):
    o_ref[...] = _dot(cn_ref[...], zr_ref[...]) + _dot(sn_ref[...], zi_ref[...])


def _fft_direct(zr3, zi3, cn, sn):
    bsz, n, w = zr3.shape
    blk = pl.BlockSpec((None, n, w), lambda b: (b, 0, 0))
    tab = pl.BlockSpec((n, n), lambda b: (0, 0))
    return pl.pallas_call(
        _fft_direct_kernel,
        grid=(bsz,),
        in_specs=[blk, blk, tab, tab],
        out_specs=blk,
        out_shape=jax.ShapeDtypeStruct((bsz, n, w), F32),
        compiler_params=_cparams("parallel"),
        name="fft_direct",
    )(zr3, zi3, cn, sn)


def _fft_a_kernel(zr_ref, zi_ref, c_ref, s_ref, gr_ref, gi_ref):
    zr = zr_ref[...]
    zi = zi_ref[...]
    cm = c_ref[...]
    sm = s_ref[...]
    gr_ref[...] = (_dot(cm, zr) + _dot(sm, zi)).astype(BF16)
    gi_ref[...] = (_dot(cm, zi) - _dot(sm, zr)).astype(BF16)


def _fft_c_kernel(gr_ref, gi_ref, mc_ref, ms_ref, o_ref):
    for j in range(gr_ref.shape[0]):
        o_ref[:, j, :] = _dot(mc_ref[j], gr_ref[j]) + _dot(ms_ref[j], gi_ref[j])


def _fft_two_stage(zr3, zi3, tabs):
    bsz, n, w = zr3.shape
    n1 = FFT_N1
    n2 = n // n1
    c1, s1, mc, ms = tabs
    tn = 2048
    wide = n2 * w
    blk = pl.BlockSpec((None, n1, tn), lambda b, j: (b, 0, j))
    tab = pl.BlockSpec((n1, n1), lambda b, j: (0, 0))
    gr, gi = pl.pallas_call(
        _fft_a_kernel,
        grid=(bsz, wide // tn),
        in_specs=[blk, blk, tab, tab],
        out_specs=[blk, blk],
        out_shape=[jax.ShapeDtypeStruct((bsz, n1, wide), BF16)] * 2,
        compiler_params=_cparams("parallel", "parallel"),
        name="fft_stage_a",
    )(zr3.reshape(bsz, n1, wide), zi3.reshape(bsz, n1, wide), c1, s1)
    kb = 8
    gblk = pl.BlockSpec((None, kb, n2, w), lambda b, j: (b, j, 0, 0))
    mblk = pl.BlockSpec((kb, n2, n2), lambda b, j: (j, 0, 0))
    out = pl.pallas_call(
        _fft_c_kernel,
        grid=(bsz, n1 // kb),
        in_specs=[gblk, gblk, mblk, mblk],
        out_specs=pl.BlockSpec((None, n2, kb, w), lambda b, j: (b, 0, j, 0)),
        out_shape=jax.ShapeDtypeStruct((bsz, n2, n1, w), F32),
        compiler_params=_cparams("parallel", "parallel"),
        name="fft_stage_c",
    )(gr.reshape(bsz, n1, n2, w), gi.reshape(bsz, n1, n2, w), mc, ms)
    return out.reshape(bsz, n, w)


def _route(logits, bias):
    tm = logits.shape[1]
    s = _sigmoid(logits)
    biased = s + bias
    neg = -jnp.inf
    rows = lax.broadcasted_iota(jnp.int32, (8, tm), 0)

    def first_argmax(x, ids, sentinel):
        m = jnp.max(x, axis=0, keepdims=True)
        return m, jnp.min(jnp.where(x == m, ids, sentinel), axis=0, keepdims=True)

    gs_rows = []
    for g in range(N_GROUPS):
        x = biased[8 * g:8 * g + 8]
        m1, i1 = first_argmax(x, rows, 8)
        m2 = jnp.max(jnp.where(rows == i1, neg, x), axis=0, keepdims=True)
        gs_rows.append(m1 + m2)
    gs = jnp.concatenate(gs_rows, axis=0)
    gsel = jnp.zeros((N_GROUPS, tm), F32)
    for _ in range(TOPK_GROUPS):
        _, i = first_argmax(gs, rows, 8)
        hit = rows == i
        gsel = jnp.where(hit, 1.0, gsel)
        gs = jnp.where(hit, neg, gs)

    xs = [jnp.where(gsel[g:g + 1] > 0.0, biased[8 * g:8 * g + 8], neg) for g in range(N_GROUPS)]
    ids = [rows + 8 * g for g in range(N_GROUPS)]
    sel = [jnp.zeros((8, tm), F32) for _ in range(N_GROUPS)]
    eids = []
    for _ in range(TOP_K):
        m = xs[0]
        for g in range(1, N_GROUPS):
            m = jnp.maximum(m, xs[g])
        m = jnp.max(m, axis=0, keepdims=True)
        cand = jnp.where(xs[0] == m, ids[0], N_EXPERTS)
        for g in range(1, N_GROUPS):
            cand = jnp.minimum(cand, jnp.where(xs[g] == m, ids[g], N_EXPERTS))
        i = jnp.min(cand, axis=0, keepdims=True)
        eids.append(i)
        for g in range(N_GROUPS):
            hit = ids[g] == i
            sel[g] = jnp.where(hit, 1.0, sel[g])
            xs[g] = jnp.where(hit, neg, xs[g])

    sel_all = jnp.concatenate(sel, axis=0)
    tr = lax.broadcasted_iota(jnp.int32, (tm, tm), 0)
    tc = lax.broadcasted_iota(jnp.int32, (tm, tm), 1)
    before = jnp.where(tr < tc, 1.0, 0.0).astype(BF16)
    seen = _dot(sel_all.astype(BF16), before)
    counts = jnp.sum(sel_all, axis=1, keepdims=True)

    n_chunks = jnp.ceil(counts * (1.0 / CHUNK))
    er = lax.broadcasted_iota(jnp.int32, (N_EXPERTS, N_EXPERTS), 0)
    ec = lax.broadcasted_iota(jnp.int32, (N_EXPERTS, N_EXPERTS), 1)
    lower = jnp.where(ec < er, 1.0, 0.0).astype(BF16)
    run_start = _dot(lower, jnp.broadcast_to(n_chunks, (N_EXPERTS, LANES)).astype(BF16))[:, 0:1] * CHUNK
    local_pos = seen + run_start

    def pick(k, table):
        acc = None
        for g in range(N_GROUPS):
            v = jnp.where(ids[g] == eids[k], table[8 * g:8 * g + 8], 0.0)
            acc = v if acc is None else acc + v
        return jnp.sum(acc, axis=0, keepdims=True)

    w_raw = [pick(k, s) for k in range(TOP_K)]
    lpos = [pick(k, local_pos) for k in range(TOP_K)]
    tot = w_raw[0]
    for k in range(1, TOP_K):
        tot = tot + w_raw[k]
    weights = [w / tot * ROUTED_SCALE for w in w_raw]
    return weights, lpos, n_chunks, run_start


def _mix_out_kernel(x_ref, of_ref, ob_ref, g_ref, su_ref, sv_ref, cb_ref, cc_ref, cx_ref,
                    ccp_ref, cxp_ref, ccn_ref, cxn_ref, ft_ref, mod_ref,
                    glag_ref, sgng_ref, sgnb_ref, wsgu_ref, bsgu_ref, wconv_ref, wout_ref,
                    ln1g_ref, ln1b_ref, wr_ref, rb_ref,
                    x1_ref, xw_ref, wrow_ref, lpos_ref, tab_ref, cnt_ref, carry_ref, *, seq_len):
    tm = x_ref.shape[0]
    i = pl.program_id(0)

    @pl.when(i == 0)
    def _():
        carry_ref[...] = jnp.zeros_like(carry_ref)

    mod = mod_ref[...]
    g1 = mod[:, 2 * D_MODEL:3 * D_MODEL]
    sh2 = mod[:, 3 * D_MODEL:4 * D_MODEL]
    sc2 = mod[:, 4 * D_MODEL:5 * D_MODEL]

    o = of_ref[...] + ob_ref[...]
    hr = lax.broadcasted_iota(jnp.int32, (HEAD_W, HEAD_W), 0) >> 6
    hc = lax.broadcasted_iota(jnp.int32, (HEAD_W, HEAD_W), 1) >> 6
    head_mean = jnp.where(hr == hc, 1.0 / GLA_DK, 0.0).astype(BF16)
    o2 = o * o
    o2_hi = o2.astype(BF16)
    o2_lo = (o2 - o2_hi.astype(F32)).astype(BF16)
    ms = _dot(o2_hi, head_mean) + _dot(o2_lo, head_mean)
    y_gla = o * lax.rsqrt(ms + RMS_EPS) * glag_ref[...] * _silu(g_ref[...])

    vn = _ln(sv_ref[...]) * sgng_ref[...] + sgnb_ref[...]
    br = lax.broadcasted_iota(jnp.int32, (GMLP_GROUPS * GMLP_CHUNK, HEAD_W), 0) >> 7
    bc = lax.broadcasted_iota(jnp.int32, (GMLP_GROUPS * GMLP_CHUNK, HEAD_W), 1) >> 6
    sgu_bd = br == bc
    sp_parts = []
    for j in range(tm // GMLP_CHUNK):
        vc = vn[j * GMLP_CHUNK:(j + 1) * GMLP_CHUNK]
        vbd = jnp.where(sgu_bd, jnp.concatenate([vc] * GMLP_GROUPS, axis=0), 0.0).astype(BF16)
        sp_parts.append(_dot(wsgu_ref[...], vbd) + bsgu_ref[...])
    y_sgu = su_ref[...] * jnp.concatenate(sp_parts, axis=0)

    z = cc_ref[...] * cx_ref[...]
    z_before = ccp_ref[7:8, :] * cxp_ref[7:8, :]
    z_after = ccn_ref[0:1, :] * cxn_ref[0:1, :]
    row = lax.broadcasted_iota(jnp.int32, (tm, HEAD_W), 0)
    pos = (i * tm + row) & (seq_len - 1)
    z_prev = jnp.where(row == 0, z_before, pltpu.roll(z, 1, 0))
    z_next = jnp.where(row == tm - 1, z_after, pltpu.roll(z, tm - 1, 0))
    z_prev = jnp.where(pos == 0, 0.0, z_prev)
    z_next = jnp.where(pos == seq_len - 1, 0.0, z_next)
    wconv = wconv_ref[...]
    y_conv = cb_ref[...] * (wconv[0:1] * z_prev + wconv[1:2] * z + wconv[2:3] * z_next)

    y = (_dot(y_gla.astype(BF16), wout_ref[0:HEAD_W, :])
         + _dot(y_sgu.astype(BF16), wout_ref[HEAD_W:2 * HEAD_W, :])
         + _dot(y_conv.astype(BF16), wout_ref[2 * HEAD_W:3 * HEAD_W, :])
         + _dot(ft_ref[...].astype(BF16), wout_ref[3 * HEAD_W:4 * HEAD_W, :]))
    x1 = _ln(DEEPNORM_ALPHA * x_ref[...] + g1 * y) * ln1g_ref[...] + ln1b_ref[...]
    x1_ref[...] = x1
    h2 = _ln(x1) * (1.0 + sc2) + sh2
    xw_ref[...] = _pack_bf16_pairs(h2)

    logits = jnp.dot(h2, wr_ref[...], preferred_element_type=F32, precision=lax.Precision.HIGHEST)
    weights, lpos, n_chunks, run_start = _route(logits.T[:N_EXPERTS], rb_ref[...])
    lpos_ref[...] = jnp.concatenate(lpos, axis=0).astype(jnp.int32)
    wrow_ref[...] = jnp.concatenate(weights, axis=0)
    carry = carry_ref[:, 0:1]
    lane = lax.broadcasted_iota(jnp.int32, (N_EXPERTS, LANES), 1)
    cols = jnp.where(lane == 0, n_chunks, jnp.where(lane == 1, run_start, jnp.where(lane == 2, carry, 0.0)))
    tab = jnp.concatenate([cols, jnp.zeros((LANES - N_EXPERTS, LANES), F32)], axis=0).T
    tab_ref[...] = tab[0:8].astype(jnp.int32)
    new_carry = carry_ref[...] + n_chunks
    carry_ref[...] = new_carry
    cnt_ref[...] = new_carry


def _mix_out(x2d, of2d, ob2d, proj, yft2d, mod3, mod_row, lw, seq_len, tm):
    t = x2d.shape[0]
    tm = min(tm, t)
    nt8 = t // 8
    rows8 = tm // 8

    def col(cb):
        return pl.BlockSpec((tm, HEAD_W), lambda i: (i, cb))

    def halo_prev(cb):
        return pl.BlockSpec((8, HEAD_W), lambda i: (jnp.maximum(i * rows8 - 1, 0), cb))

    def halo_next(cb):
        return pl.BlockSpec((8, HEAD_W), lambda i: (jnp.minimum((i + 1) * rows8, nt8 - 1), cb))

    def full(a):
        return pl.BlockSpec(a.shape, lambda i: (0,) * a.ndim)

    tok_d = pl.BlockSpec((tm, D_MODEL), lambda i: (i, 0))
    tok_h = pl.BlockSpec((tm, HEAD_W), lambda i: (i, 0))
    weights = [lw["gla_norm_g"], lw["sgu_norm_g"], lw["sgu_norm_b"], lw["w_sgu_cat"], lw["b_sgu_full"],
               lw["w_conv"], lw["w_out"], lw["ln1_g"], lw["ln1_b"], lw["w_router_pad"], lw["router_bias"]]
    in_specs = ([tok_d, tok_h, tok_h, col(COL_G), col(COL_SGU), col(COL_SGV), col(COL_CVB), col(COL_CVC),
                 col(COL_CVX), halo_prev(COL_CVC), halo_prev(COL_CVX), halo_next(COL_CVC), halo_next(COL_CVX),
                 tok_h, pl.BlockSpec((None, 1, 6 * D_MODEL), lambda i: (mod_row(i, tm), 0, 0))]
                + [full(w) for w in weights])
    args = [x2d, of2d, ob2d] + [proj] * 10 + [yft2d, mod3] + weights
    return pl.pallas_call(
        functools.partial(_mix_out_kernel, seq_len=seq_len),
        grid=(t // tm,),
        in_specs=in_specs,
        out_specs=[tok_d,
                   pl.BlockSpec((tm, PACK_W), lambda i: (i, 0)),
                   pl.BlockSpec((None, TOP_K, tm), lambda i: (i, 0, 0)),
                   pl.BlockSpec((None, TOP_K, tm), lambda i: (i, 0, 0)),
                   pl.BlockSpec((None, 8, LANES), lambda i: (i, 0, 0)),
                   pl.BlockSpec((N_EXPERTS, LANES), lambda i: (0, 0))],
        out_shape=[jax.ShapeDtypeStruct((t, D_MODEL), F32),
                   jax.ShapeDtypeStruct((t, PACK_W), jnp.uint32),
                   jax.ShapeDtypeStruct((t // tm, TOP_K, tm), F32),
                   jax.ShapeDtypeStruct((t // tm, TOP_K, tm), jnp.int32),
                   jax.ShapeDtypeStruct((t // tm, 8, LANES), jnp.int32),
                   jax.ShapeDtypeStruct((N_EXPERTS, LANES), F32)],
        scratch_shapes=[pltpu.VMEM((N_EXPERTS, LANES), F32)],
        compiler_params=_cparams("arbitrary"),
        name="mix_out",
    )(*args)


def _local_rows(tm):
    need = tm * TOP_K + N_EXPERTS * (CHUNK - 1)
    return -(-need // LBLK) * LBLK


def _run_copy(local_ref, sorted_hbm, sem, local_row, sorted_row, to_sorted):
    loc = local_ref.at[pl.ds(pl.multiple_of(local_row, CHUNK), CHUNK)]
    srt = sorted_hbm.at[pl.ds(pl.multiple_of(sorted_row, CHUNK), CHUNK)]
    return pltpu.make_async_copy(loc, srt, sem) if to_sorted else pltpu.make_async_copy(srt, loc, sem)


def _start_run_copies(tab_ref, gstart_ref, local_ref, sorted_hbm, sem, to_sorted):
    def per_expert(e, total):
        n = tab_ref[0, e]
        l0 = tab_ref[1, e]
        g0 = (gstart_ref[e] + tab_ref[2, e]) * CHUNK

        def per_chunk(j, c):
            _run_copy(local_ref, sorted_hbm, sem, l0 + j * CHUNK, g0 + j * CHUNK, to_sorted).start()
            return c

        lax.fori_loop(0, n, per_chunk, 0)
        return total + n

    return lax.fori_loop(0, N_EXPERTS, per_expert, 0)


def _wait_run_copies(count, local_ref, sorted_hbm, sem, to_sorted):
    batch = 16

    def wait_batch(j, c):
        for _ in range(batch):
            _run_copy(local_ref, sorted_hbm, sem, 0, 0, to_sorted).wait()
        return c

    def wait_one(j, c):
        _run_copy(local_ref, sorted_hbm, sem, 0, 0, to_sorted).wait()
        return c

    lax.fori_loop(0, count >> 4, wait_batch, 0)
    lax.fori_loop(0, count & (batch - 1), wait_one, 0)


def _moe_scatter_kernel(gstart_ref, gtail_ref, xw_ref, lpos_ref, tab_ref, sorted_hbm,
                        local_ref, sem, zsem, pending_ref):
    i = pl.program_id(0)
    n = pl.num_programs(0)
    slot = i & 1
    tm = xw_ref.shape[0]
    lrows = local_ref.shape[1]

    def zero_copy(e):
        off = pl.multiple_of(jnp.maximum(gtail_ref[e], 0), EBLK)
        return pltpu.make_async_copy(local_ref.at[1, pl.ds(0, EBLK)], sorted_hbm.at[pl.ds(off, EBLK)], zsem)

    @pl.when(i == 0)
    def _():
        pending_ref[0] = 0
        pending_ref[1] = 0
        local_ref[1, 0:EBLK, :] = jnp.zeros((EBLK, PACK_W), jnp.uint32)

        def z_start(e, c):
            @pl.when(gtail_ref[e] >= 0)
            def _():
                zero_copy(e).start()
            return c

        def z_wait(e, c):
            @pl.when(gtail_ref[e] >= 0)
            def _():
                zero_copy(e).wait()
            return c

        lax.fori_loop(0, N_EXPERTS, z_start, 0)
        lax.fori_loop(0, N_EXPERTS, z_wait, 0)

    local = local_ref.at[slot]
    _wait_run_copies(pending_ref[slot], local, sorted_hbm, sem.at[slot], True)

    x = _unpack_bf16_pairs(xw_ref[...]).astype(BF16)
    lpos = lpos_ref[...].astype(jnp.int16)
    one = jnp.ones((LBLK, tm), BF16)
    for b in range(lrows // LBLK):
        riota = lax.broadcasted_iota(jnp.int16, (LBLK, tm), 0) + b * LBLK
        p = jnp.zeros((LBLK, tm), BF16)
        for k in range(TOP_K):
            p = jnp.where(riota == lpos[k:k + 1, :], one, p)
        local[b * LBLK:(b + 1) * LBLK, :] = _pack_exact_bf16_pairs(_dot(p, x))

    pending_ref[slot] = _start_run_copies(tab_ref, gstart_ref, local, sorted_hbm, sem.at[slot], True)

    @pl.when(i == n - 1)
    def _():
        for s in range(2):
            _wait_run_copies(pending_ref[s], local_ref.at[s], sorted_hbm, sem.at[s], True)


def _moe_scatter(gstart, gtail, xw, lpos, tab, n_rows, tm):
    n_tiles = xw.shape[0] // tm
    lrows = _local_rows(tm)
    any_spec = pl.BlockSpec(memory_space=pl.ANY)
    return pl.pallas_call(
        _moe_scatter_kernel,
        grid_spec=pltpu.PrefetchScalarGridSpec(
            num_scalar_prefetch=2,
            grid=(n_tiles,),
            in_specs=[pl.BlockSpec((tm, PACK_W), lambda i, *_: (i, 0)),
                      pl.BlockSpec((None, TOP_K, tm), lambda i, *_: (i, 0, 0)),
                      pl.BlockSpec((None, 8, LANES), lambda i, *_: (i, 0, 0), memory_space=pltpu.SMEM)],
            out_specs=any_spec,
            scratch_shapes=[pltpu.VMEM((2, lrows, PACK_W), jnp.uint32), pltpu.SemaphoreType.DMA((2,)),
                            pltpu.SemaphoreType.DMA, pltpu.SMEM((2,), jnp.int32)]),
        out_shape=jax.ShapeDtypeStruct((n_rows, PACK_W), jnp.uint32),
        compiler_params=_cparams("arbitrary"),
        name="moe_scatter",
    )(gstart, gtail, xw, lpos, tab)


def _expert2_kernel(be_ref, nu_ref, xs_ref, wg_ref, wu_ref, wd_ref, ys_ref):
    j = pl.program_id(0)

    @pl.when(j < nu_ref[0])
    def _():
        x = _unpack_bf16_pairs(xs_ref[...]).astype(BF16)
        a = _silu(_dot(x, wg_ref[...].astype(BF16))) * _dot(x, wu_ref[...].astype(BF16))
        ys_ref[...] = _pack_bf16_pairs(_dot(a.astype(BF16), wd_ref[...].astype(BF16)))


def _experts2(xs, block_expert, n_used, lw, layer):
    n_rows = xs.shape[0]
    ff = EXPERT_FF

    def blk(j, be, nu):
        return (jnp.minimum(j, jnp.maximum(nu[0] - 1, 0)), 0)

    def wblk(j, be, nu):
        return (layer, be[j], 0, 0)

    return pl.pallas_call(
        _expert2_kernel,
        grid_spec=pltpu.PrefetchScalarGridSpec(
            num_scalar_prefetch=2,
            grid=(n_rows // EBLK,),
            in_specs=[pl.BlockSpec((EBLK, PACK_W), blk),
                      pl.BlockSpec((None, None, D_MODEL, ff), wblk),
                      pl.BlockSpec((None, None, D_MODEL, ff), wblk),
                      pl.BlockSpec((None, None, ff, D_MODEL), wblk)],
            out_specs=pl.BlockSpec((EBLK, PACK_W), blk)),
        out_shape=jax.ShapeDtypeStruct((n_rows, PACK_W), jnp.uint32),
        compiler_params=_cparams("arbitrary"),
        name="moe_experts",
    )(block_expert, n_used, xs, lw["w_exp_gate"], lw["w_exp_up"], lw["w_exp_down"])


def _moe_combine_kernel(gstart_ref, x1_ref, xw_ref, lpos_ref, wrow_ref, tab_ref, tab_next_ref, mod_ref,
                        sg_ref, su_ref, sd_ref, ln2g_ref, ln2b_ref, sorted_hbm, o_ref,
                        local_ref, sem, pending_ref):
    i = pl.program_id(0)
    n = pl.num_programs(0)
    slot = i & 1
    tm = x1_ref.shape[0]
    lrows = local_ref.shape[1]

    @pl.when(i == 0)
    def _():
        local_ref[...] = jnp.zeros_like(local_ref)
        pending_ref[0] = _start_run_copies(tab_ref, gstart_ref, local_ref.at[0], sorted_hbm, sem.at[0], False)

    @pl.when(i + 1 < n)
    def _():
        nxt = 1 - slot
        pending_ref[nxt] = _start_run_copies(tab_next_ref, gstart_ref, local_ref.at[nxt], sorted_hbm,
                                             sem.at[nxt], False)

    h = _unpack_bf16_pairs(xw_ref[...]).astype(BF16)
    a = _silu(_dot(h, sg_ref[...])) * _dot(h, su_ref[...])
    acc = _dot(a.astype(BF16), sd_ref[...])

    local = local_ref.at[slot]
    _wait_run_copies(pending_ref[slot], local, sorted_hbm, sem.at[slot], False)

    lpos = lpos_ref[...].astype(jnp.int16)
    wrow = wrow_ref[...].astype(BF16)
    for b in range(lrows // LBLK):
        riota = lax.broadcasted_iota(jnp.int16, (LBLK, tm), 0) + b * LBLK
        q = jnp.zeros((LBLK, tm), BF16)
        for k in range(TOP_K):
            q = jnp.where(riota == lpos[k:k + 1, :], jnp.broadcast_to(wrow[k:k + 1, :], (LBLK, tm)), q)
        y = _unpack_bf16_pairs(local[b * LBLK:(b + 1) * LBLK, :]).astype(BF16)
        acc = acc + _dot_tn(q, y)

    g2 = mod_ref[...][:, 5 * D_MODEL:6 * D_MODEL]
    u = DEEPNORM_ALPHA * x1_ref[...] + g2 * acc
    o_ref[...] = _ln(u) * ln2g_ref[...] + ln2b_ref[...]


def _moe_combine(gstart, x1, xw, lpos, wrow, tab, ys, mod3, mod_row, lw, tm):
    t = x1.shape[0]
    n_tiles = t // tm
    lrows = _local_rows(tm)

    def full(a):
        return pl.BlockSpec(a.shape, lambda i, *_: (0,) * a.ndim)

    tab_blk = lambda f: pl.BlockSpec((None, 8, LANES), f, memory_space=pltpu.SMEM)
    return pl.pallas_call(
        _moe_combine_kernel,
        grid_spec=pltpu.PrefetchScalarGridSpec(
            num_scalar_prefetch=1,
            grid=(n_tiles,),
            in_specs=[pl.BlockSpec((tm, D_MODEL), lambda i, *_: (i, 0)),
                      pl.BlockSpec((tm, PACK_W), lambda i, *_: (i, 0)),
                      pl.BlockSpec((None, TOP_K, tm), lambda i, *_: (i, 0, 0)),
                      pl.BlockSpec((None, TOP_K, tm), lambda i, *_: (i, 0, 0)),
                      tab_blk(lambda i, *_: (i, 0, 0)),
                      tab_blk(lambda i, *_: (jnp.minimum(i + 1, n_tiles - 1), 0, 0)),
                      pl.BlockSpec((None, 1, 6 * D_MODEL), lambda i, *_: (mod_row(i, tm), 0, 0)),
                      full(lw["w_sh_gate"]), full(lw["w_sh_up"]), full(lw["w_sh_down"]),
                      full(lw["ln2_g"]), full(lw["ln2_b"]),
                      pl.BlockSpec(memory_space=pl.ANY)],
            out_specs=pl.BlockSpec((tm, D_MODEL), lambda i, *_: (i, 0)),
            scratch_shapes=[pltpu.VMEM((2, lrows, PACK_W), jnp.uint32), pltpu.SemaphoreType.DMA((2,)),
                            pltpu.SMEM((2,), jnp.int32)]),
        out_shape=jax.ShapeDtypeStruct((t, D_MODEL), F32),
        compiler_params=_cparams("arbitrary"),
        name="moe_combine",
    )(gstart, x1, xw, lpos, wrow, tab, tab, mod3, lw["w_sh_gate"], lw["w_sh_up"], lw["w_sh_down"],
      lw["ln2_g"], lw["ln2_b"], ys)


def _moe2(x1, xw, wrow, lpos, tab, chunks_f, mod3, mod_row, lw, layer, tm):
    t = x1.shape[0]
    n_tiles = t // tm
    region_rows = chunks_f[:, 0].astype(jnp.int32) * CHUNK
    padded = ((region_rows + EBLK - 1) // EBLK) * EBLK
    end = jnp.cumsum(padded)
    start = end - padded
    n_blocks = -(-(t * TOP_K + n_tiles * N_EXPERTS * (CHUNK - 1)) // EBLK) + N_EXPERTS
    n_used = end[-1] // EBLK
    blk_row = jnp.minimum(jnp.arange(n_blocks, dtype=jnp.int32), jnp.maximum(n_used - 1, 0)) * EBLK
    block_expert = jnp.minimum(jnp.sum((end[None, :] <= blk_row[:, None]).astype(jnp.int32), axis=1),
                               N_EXPERTS - 1).astype(jnp.int32)
    gstart = start // CHUNK
    gtail = jnp.where(padded > 0, end - EBLK, -1)
    xs = _moe_scatter(gstart, gtail, xw, lpos, tab, n_blocks * EBLK, tm)
    ys = _experts2(xs, block_expert, n_used.reshape(1), lw, layer)
    return _moe_combine(gstart, x1, xw, lpos, wrow, tab, ys, mod3, mod_row, lw, tm)


def _channel_dft_table():
    k = np.arange(FNET_CH, dtype=np.float64)
    ang = 2.0 * np.pi * np.outer(k, k) / FNET_CH
    eye = np.eye(FNET_GROUPS)
    return np.concatenate([np.kron(eye, np.cos(ang)), -np.kron(eye, np.sin(ang))], axis=1)


def _direct_dft_tables(n):
    k = np.arange(n, dtype=np.float64)
    ang = 2.0 * np.pi * (np.outer(k, k) % n) / n
    scale = 1.0 / math.sqrt(n * FNET_CH)
    return np.cos(ang) * scale, np.sin(ang) * scale


def _two_stage_dft_tables(n):
    n1 = FFT_N1
    n2 = n // n1
    a = np.arange(n1, dtype=np.float64)
    ang1 = 2.0 * np.pi * (np.outer(a, a) % n1) / n1
    k1 = np.arange(n1).reshape(n1, 1, 1)
    k2 = np.arange(n2).reshape(1, n2, 1)
    m2 = np.arange(n2).reshape(1, 1, n2)
    ang2 = 2.0 * np.pi * ((m2 * (k1 + n1 * k2)) % n) / n
    scale = 1.0 / math.sqrt(n * FNET_CH)
    return np.cos(ang1), np.sin(ang1), np.cos(ang2) * scale, np.sin(ang2) * scale


def _grid_sincos_table(rows, d):
    quarter = d // 4
    omega = 1.0 / (POS_BASE ** (np.arange(quarter, dtype=np.float64) / quarter))
    r = np.arange(rows, dtype=np.float64)[:, None] * omega
    c = np.arange(GRID_W, dtype=np.float64)[:, None] * omega
    return (np.concatenate([np.sin(r), np.cos(r)], axis=-1).astype(np.float32),
            np.concatenate([np.sin(c), np.cos(c)], axis=-1).astype(np.float32))


def _layer_weights(l, w_in, w_gla_a, b_gla_a, gla_norm_g, sgu_norm_g, sgu_norm_b, w_sgu, b_sgu, w_conv,
                   w_out, ln1_g, ln1_b, ln2_g, ln2_b, w_router, router_bias,
                   w_exp_gate, w_exp_up, w_exp_down, w_sh_gate, w_sh_up, w_sh_down):
    wi = w_in[l]
    lr0 = 4 * HEAD_W
    w_in_p = jnp.concatenate(
        [wi[:, :lr0], wi[:, lr0 + 2 * GLA_LR:], wi[:, lr0:lr0 + 2 * GLA_LR],
         jnp.zeros((D_MODEL, LR_W - 2 * GLA_LR), F32)], axis=1).astype(BF16)
    wa_pad = jnp.zeros((LR_W, 2 * HEAD_W), F32)
    wa_pad = wa_pad.at[:GLA_LR, :HEAD_W].set(w_gla_a[l, 0])
    wa_pad = wa_pad.at[GLA_LR:2 * GLA_LR, HEAD_W:].set(w_gla_a[l, 1])
    row = lambda a: a[l].reshape(1, -1)
    return {
        "w_in_p": w_in_p,
        "wa_pad": wa_pad,
        "ba": jnp.concatenate([b_gla_a[l, 0], b_gla_a[l, 1]]).reshape(1, 2 * HEAD_W),
        "gla_norm_g": row(gla_norm_g), "sgu_norm_g": row(sgu_norm_g), "sgu_norm_b": row(sgu_norm_b),
        "w_sgu_cat": jnp.concatenate([w_sgu[l, g] for g in range(GMLP_GROUPS)], axis=1).astype(BF16),
        "b_sgu_full": jnp.repeat(b_sgu[l].T, HEAD_W // GMLP_GROUPS, axis=1),
        "w_conv": w_conv[l],
        "w_out": w_out[l].astype(BF16),
        "ln1_g": row(ln1_g), "ln1_b": row(ln1_b), "ln2_g": row(ln2_g), "ln2_b": row(ln2_b),
        "w_router_pad": jnp.concatenate([w_router[l], jnp.zeros((D_MODEL, N_EXPERTS), F32)], axis=1),
        "router_bias": router_bias[l].reshape(N_EXPERTS, 1),
        "w_exp_gate": w_exp_gate, "w_exp_up": w_exp_up, "w_exp_down": w_exp_down,
        "w_sh_gate": w_sh_gate[l].astype(BF16), "w_sh_up": w_sh_up[l].astype(BF16),
        "w_sh_down": w_sh_down[l].astype(BF16),
    }


def _state_to_blockdiag_t(s):
    bsz = s.shape[0]
    st = jnp.swapaxes(s, 2, 3)
    eye = jnp.eye(GLA_HEADS, dtype=s.dtype)
    return jnp.einsum("bhvd,hg->bhvgd", st, eye).reshape(bsz, HEAD_W, HEAD_W)


def _blockdiag_t_to_state(st):
    bsz = st.shape[0]
    s5 = st.reshape(bsz, GLA_HEADS, GLA_DK, GLA_HEADS, GLA_DK)
    diag = jnp.stack([s5[:, h, :, h, :] for h in range(GLA_HEADS)], axis=1)
    return jnp.swapaxes(diag, 2, 3)


def _trunk_layer(x3, pos, mod3, mod_row, lw, layer, st0, emit_final, tabs):
    bsz, n, _ = x3.shape
    t = bsz * n
    outs = _in_proj(x3.reshape(t, D_MODEL), pos, mod3, mod_row, lw["w_in_p"], tabs["cs"], tm=512)
    if pos is not None:
        proj, zr, zi, x2d = outs
    else:
        proj, zr, zi = outs
        x2d = x3.reshape(t, D_MODEL)
    gla_out = _gla(proj.reshape(bsz, n, PROJ_W), lw["wa_pad"], lw["ba"], st0, emit_final)
    o_f, o_b = gla_out[:2]
    zr3 = zr.reshape(bsz, n, HEAD_W)
    zi3 = zi.reshape(bsz, n, HEAD_W)
    if "two_stage" in tabs:
        yft = _fft_two_stage(zr3, zi3, tabs["two_stage"])
    else:
        yft = _fft_direct(zr3, zi3, *tabs["direct"])
    x1, xw, wrow, lpos, tab, chunks = _mix_out(x2d, o_f.reshape(t, HEAD_W), o_b.reshape(t, HEAD_W), proj,
                                             yft.reshape(t, HEAD_W), mod3, mod_row, lw, seq_len=n, tm=256)
    x2 = _moe2(x1, xw, wrow, lpos, tab, chunks, mod3, mod_row, lw, layer, tm=min(256, t))
    return x2.reshape(bsz, n, D_MODEL), gla_out[2:]


def kernel(x_prompt, x_sample, c, state_gla, c_ctx, w_ada, b_ada, w_in, w_gla_a, b_gla_a, gla_norm_g, sgu_norm_g, sgu_norm_b, w_sgu, b_sgu, w_conv, w_out, ln1_g, ln1_b, ln2_g, ln2_b, w_router, router_bias, w_exp_gate, w_exp_up, w_exp_down, w_sh_gate, w_sh_up, w_sh_down):
    n_layers = w_ada.shape[0]
    bp, np_, _ = x_prompt.shape
    bs, ns, _ = x_sample.shape
    assert bs <= 7

    cond8 = jnp.concatenate([c_ctx[None, :], c, jnp.zeros((7 - bs, D_MODEL), F32)], axis=0)
    mod = _ada_mod(cond8, w_ada, b_ada)

    tabs_p = {"cs": jnp.asarray(_channel_dft_table(), BF16),
              "direct": tuple(jnp.asarray(a, BF16) for a in _direct_dft_tables(np_))}
    tabs_s = {"cs": tabs_p["cs"],
              "two_stage": tuple(jnp.asarray(a, BF16) for a in _two_stage_dft_tables(ns))}
    rtab, ctab = _grid_sincos_table(ns // GRID_W, D_MODEL)
    pos = jnp.concatenate([jnp.repeat(jnp.asarray(rtab), GRID_W, axis=0),
                           jnp.tile(jnp.asarray(ctab), (ns // GRID_W, 1))], axis=-1)

    prompt_row = lambda i, tm: 0
    sample_row = lambda i, tm: 1 + (i * tm) // ns

    y_p = x_prompt
    y_s = x_sample
    finals = []
    for l in range(n_layers):
        lw = _layer_weights(l, w_in, w_gla_a, b_gla_a, gla_norm_g, sgu_norm_g, sgu_norm_b, w_sgu, b_sgu,
                            w_conv, w_out, ln1_g, ln1_b, ln2_g, ln2_b, w_router, router_bias,
                            w_exp_gate, w_exp_up, w_exp_down, w_sh_gate, w_sh_up, w_sh_down)
        mod3 = mod[l].reshape(8, 1, 6 * D_MODEL)
        y_p, fin = _trunk_layer(y_p, None, mod3, prompt_row, lw, l, None, True, tabs_p)
        finals.append(jnp.stack([_blockdiag_t_to_state(fin[0]), _blockdiag_t_to_state(fin[1])], axis=1))
        st0 = jnp.stack([_state_to_blockdiag_t(state_gla[:, l, 0]), _state_to_blockdiag_t(state_gla[:, l, 1])])
        y_s, _ = _trunk_layer(y_s, pos if l == 0 else None, mod3, sample_row, lw, l, st0, False, tabs_s)
    new_state = jnp.stack(finals, axis=1).astype(x_prompt.dtype)
    return (y_p, y_s, new_state)
```

```python
import functools
import math

import numpy as np
import jax
import jax.numpy as jnp
from jax import lax
from jax.experimental import pallas as pl
from jax.experimental.pallas import tpu as pltpu

F32 = jnp.float32
BF16 = jnp.bfloat16

D_MODEL = 1024
DEPTH = 2
GRID_W = 64
HEAD_W = 256
GLA_HEADS = 4
GLA_DK = 64
GLA_LR = 16
GLA_TAU = 16.0
GLA_CHUNK = 64
GMLP_GROUPS = 4
GMLP_CHUNK = 128
FNET_GROUPS = 4
FNET_CH = 64
N_EXPERTS = 64
TOP_K = 8
N_GROUPS = 8
TOPK_GROUPS = 4
EXPERT_FF = 256
ROUTED_SCALE = 2.5
DEEPNORM_ALPHA = (2 * DEPTH) ** 0.25
LN_EPS = 1e-5
RMS_EPS = 1e-6
POS_BASE = 10000.0

COL_Q, COL_K, COL_V, COL_G, COL_SGU, COL_SGV, COL_CVB, COL_CVC, COL_CVX, COL_FT = range(10)
LR_W = 128
PROJ_W = 10 * HEAD_W + LR_W
COL_LR = (10 * HEAD_W) // LR_W

LANES = 128
PACK_W = D_MODEL // 2
EBLK = 512
LBLK = 256
CHUNK = 8
BIG_SHIFT = 2
BIG_ROWS = CHUNK << BIG_SHIFT
SEG = 256
FFT_N1 = 64
VMEM_LIMIT = 56 * 1024 * 1024


def _cparams(*sem):
    return pltpu.CompilerParams(dimension_semantics=sem, vmem_limit_bytes=VMEM_LIMIT)


def _ln(x):
    mu = jnp.mean(x, axis=-1, keepdims=True)
    xc = x - mu
    var = jnp.mean(xc * xc, axis=-1, keepdims=True)
    return xc * lax.rsqrt(var + LN_EPS)


def _sigmoid(x):
    return 1.0 / (1.0 + jnp.exp(-x))


def _silu(x):
    return x * _sigmoid(x)


def _pack_bf16_pairs(x):
    w = x.shape[1] // 2
    lo = lax.bitcast_convert_type(x[:, :w].astype(BF16).astype(F32), jnp.uint32)
    hi = lax.bitcast_convert_type(x[:, w:].astype(BF16).astype(F32), jnp.uint32)
    return (lo >> 16) | (hi & jnp.uint32(0xFFFF0000))


def _pack_exact_bf16_pairs(x):
    w = x.shape[1] // 2
    lo = lax.bitcast_convert_type(x[:, :w], jnp.uint32)
    hi = lax.bitcast_convert_type(x[:, w:], jnp.uint32)
    return (lo >> 16) | hi


def _unpack_bf16_pairs(u):
    lo = lax.bitcast_convert_type(u << 16, F32)
    hi = lax.bitcast_convert_type(u & jnp.uint32(0xFFFF0000), F32)
    return jnp.concatenate([lo, hi], axis=1)


def _dot(a, b):
    return jnp.dot(a, b, preferred_element_type=F32)


def _dot_nt(a, b):
    return lax.dot_general(a, b, (((1,), (1,)), ((), ())), preferred_element_type=F32)


def _dot_tn(a, b):
    return lax.dot_general(a, b, (((0,), (0,)), ((), ())), preferred_element_type=F32)


def _ada_kernel(c_ref, w_ref, b_ref, o_ref):
    c = c_ref[...]
    o_ref[...] = _dot(_silu(c).astype(BF16), w_ref[...].astype(BF16)) + b_ref[...]


def _ada_mod(cond8, w_ada, b_ada):
    n_l, d, w6 = w_ada.shape
    tn = 1536
    return pl.pallas_call(
        _ada_kernel,
        grid=(n_l, w6 // tn),
        in_specs=[pl.BlockSpec((8, d), lambda l, j: (0, 0)),
                  pl.BlockSpec((None, d, tn), lambda l, j: (l, 0, j)),
                  pl.BlockSpec((None, 1, tn), lambda l, j: (l, 0, j))],
        out_specs=pl.BlockSpec((None, 8, tn), lambda l, j: (l, 0, j)),
        out_shape=jax.ShapeDtypeStruct((n_l, 8, w6), F32),
        compiler_params=_cparams("parallel", "parallel"),
        name="ada_mod",
    )(cond8, w_ada, b_ada.reshape(n_l, 1, w6))


def _in_proj_kernel(*refs, has_pos):
    if has_pos:
        x_ref, pos_ref, mod_ref, w_ref, cs_ref, proj_ref, zr_ref, zi_ref, x0_ref = refs
        x = x_ref[...] + pos_ref[...]
        x0_ref[...] = x
    else:
        x_ref, mod_ref, w_ref, cs_ref, proj_ref, zr_ref, zi_ref = refs
        x = x_ref[...]
    mod = mod_ref[...]
    sh1 = mod[:, 0:D_MODEL]
    sc1 = mod[:, D_MODEL:2 * D_MODEL]
    h = _ln(x) * (1.0 + sc1) + sh1
    proj = _dot(h.astype(BF16), w_ref[...])
    proj_ref[...] = proj
    ft = proj[:, COL_FT * HEAD_W:(COL_FT + 1) * HEAD_W].astype(BF16)
    z = _dot(ft, cs_ref[...])
    zr_ref[...] = z[:, :HEAD_W].astype(BF16)
    zi_ref[...] = z[:, HEAD_W:].astype(BF16)


def _in_proj(x2d, pos, mod3, mod_row, w_in_p, cs, tm):
    t = x2d.shape[0]
    tm = min(tm, t)
    in_specs = [pl.BlockSpec((tm, D_MODEL), lambda i: (i, 0))]
    args = [x2d]
    out_shape = [jax.ShapeDtypeStruct((t, PROJ_W), F32),
                 jax.ShapeDtypeStruct((t, HEAD_W), BF16),
                 jax.ShapeDtypeStruct((t, HEAD_W), BF16)]
    out_specs = [pl.BlockSpec((tm, PROJ_W), lambda i: (i, 0)),
                 pl.BlockSpec((tm, HEAD_W), lambda i: (i, 0)),
                 pl.BlockSpec((tm, HEAD_W), lambda i: (i, 0))]
    if pos is not None:
        n_pos = pos.shape[0] // tm
        in_specs.append(pl.BlockSpec((tm, D_MODEL), lambda i: (i % n_pos, 0)))
        args.append(pos)
        out_shape.append(jax.ShapeDtypeStruct((t, D_MODEL), F32))
        out_specs.append(pl.BlockSpec((tm, D_MODEL), lambda i: (i, 0)))
    in_specs += [pl.BlockSpec((None, 1, 6 * D_MODEL), lambda i: (mod_row(i, tm), 0, 0)),
                 pl.BlockSpec((D_MODEL, PROJ_W), lambda i: (0, 0)),
                 pl.BlockSpec((HEAD_W, 2 * HEAD_W), lambda i: (0, 0))]
    args += [mod3, w_in_p, cs]
    return pl.pallas_call(
        functools.partial(_in_proj_kernel, has_pos=pos is not None),
        grid=(t // tm,),
        in_specs=in_specs,
        out_specs=out_specs,
        out_shape=out_shape,
        compiler_params=_cparams("parallel"),
        name="in_proj",
    )(*args)


def _gla_segment(q, k, v, pre, st_ref, o_ref, reverse):
    seg = q.shape[0]
    n_chunks = seg // GLA_CHUNK
    la = (jnp.minimum(pre, 0.0) - jnp.log1p(jnp.exp(-jnp.abs(pre)))) * (1.0 / GLA_TAU)

    r = lax.broadcasted_iota(jnp.int32, (seg, seg), 0)
    c = lax.broadcasted_iota(jnp.int32, (seg, seg), 1)
    same_chunk = (r >> 6) == (c >> 6)
    tri = same_chunk & ((c >= r) if reverse else (c <= r))
    tri_m = jnp.where(tri, 1.0, 0.0).astype(BF16)
    ones_m = jnp.where(same_chunk, 1.0, 0.0).astype(BF16)
    hi = la.astype(BF16)
    lo = (la - hi.astype(F32)).astype(BF16)
    b = _dot(tri_m, hi) + _dot(tri_m, lo)
    btot = _dot(ones_m, hi) + _dot(ones_m, lo)

    q_dec = q * (GLA_DK ** -0.5) * jnp.exp(b)
    k_inv = (k * jnp.exp(-b)).astype(BF16)
    k_end = (k * jnp.exp(btot - b)).astype(BF16)
    dec = jnp.exp(btot)
    vb = v.astype(BF16)

    bd = (r >> 6) == (c >> 6)
    lane_h = lax.broadcasted_iota(jnp.int32, (GLA_CHUNK, seg), 1) >> 6
    l_idx = lax.broadcasted_iota(jnp.int32, (seg, GLA_CHUNK), 0) & (GLA_CHUNK - 1)
    m_idx = lax.broadcasted_iota(jnp.int32, (seg, GLA_CHUNK), 1)
    causal = (m_idx >= l_idx) if reverse else (m_idx <= l_idx)

    st = st_ref[...]
    order = range(n_chunks - 1, -1, -1) if reverse else range(n_chunks)
    for ci in order:
        sl = slice(ci * GLA_CHUNK, (ci + 1) * GLA_CHUNK)
        qd = q_dec[sl]
        qbd = jnp.where(bd, jnp.concatenate([qd] * GLA_HEADS, axis=0), 0.0).astype(BF16)
        a = _dot_nt(qbd, k_inv[sl])
        a = jnp.where(causal, a, 0.0)
        rr = _dot(a.astype(BF16), vb[sl])
        o = _dot_nt(qd.astype(BF16), st.astype(BF16))
        for h in range(GLA_HEADS):
            o = o + jnp.where(lane_h == h, rr[h * GLA_CHUNK:(h + 1) * GLA_CHUNK], 0.0)
        o_ref[sl, :] = o
        kvt = _dot_tn(vb[sl], k_end[sl])
        st = st * dec[ci * GLA_CHUNK:ci * GLA_CHUNK + 1, :] + jnp.where(bd, kvt, 0.0)
    st_ref[...] = st


def _gla_kernel(*refs, has_init, emit_final):
    qf, kf, vf, lrf, qb, kb, vb, lrb, wa_ref, ba_ref = refs[:10]
    rest = refs[10:]
    if has_init:
        s0f, s0b = rest[:2]
        rest = rest[2:]
    of_ref, ob_ref = rest[:2]
    rest = rest[2:]
    if emit_final:
        sff, sfb = rest[:2]
        rest = rest[2:]
    stf, stb = rest

    s = pl.program_id(1)

    @pl.when(s == 0)
    def _():
        if has_init:
            stf[...] = s0f[...]
            stb[...] = s0b[...]
        else:
            stf[...] = jnp.zeros_like(stf)
            stb[...] = jnp.zeros_like(stb)

    wa = wa_ref[...]
    ba = ba_ref[...]
    pre_f = jnp.dot(lrf[...], wa, preferred_element_type=F32, precision=lax.Precision.HIGHEST) + ba
    pre_b = jnp.dot(lrb[...], wa, preferred_element_type=F32, precision=lax.Precision.HIGHEST) + ba
    _gla_segment(qf[...], kf[...], vf[...], pre_f[:, :HEAD_W], stf, of_ref, reverse=False)
    _gla_segment(qb[...], kb[...], vb[...], pre_b[:, HEAD_W:], stb, ob_ref, reverse=True)

    if emit_final:
        @pl.when(s == pl.num_programs(1) - 1)
        def _():
            sff[...] = stf[...]
            sfb[...] = stb[...]


def _gla(proj3, wa_pad, ba, st0, emit_final):
    bsz, n, _ = proj3.shape
    nseg = n // SEG

    def col(cb, width=HEAD_W, rev=False):
        if rev:
            return pl.BlockSpec((None, SEG, width), lambda b, s: (b, nseg - 1 - s, cb))
        return pl.BlockSpec((None, SEG, width), lambda b, s: (b, s, cb))

    in_specs = [col(COL_Q), col(COL_K), col(COL_V), col(COL_LR, LR_W),
                col(COL_Q, rev=True), col(COL_K, rev=True), col(COL_V, rev=True), col(COL_LR, LR_W, rev=True),
                pl.BlockSpec((LR_W, 2 * HEAD_W), lambda b, s: (0, 0)),
                pl.BlockSpec((1, 2 * HEAD_W), lambda b, s: (0, 0))]
    args = [proj3] * 8 + [wa_pad, ba]
    st_spec = pl.BlockSpec((None, HEAD_W, HEAD_W), lambda b, s: (b, 0, 0))
    if st0 is not None:
        in_specs += [st_spec, st_spec]
        args += [st0[0], st0[1]]
    out_shape = [jax.ShapeDtypeStruct((bsz, n, HEAD_W), F32)] * 2
    out_specs = [pl.BlockSpec((None, SEG, HEAD_W), lambda b, s: (b, s, 0)),
                 pl.BlockSpec((None, SEG, HEAD_W), lambda b, s: (b, nseg - 1 - s, 0))]
    if emit_final:
        out_shape += [jax.ShapeDtypeStruct((bsz, HEAD_W, HEAD_W), F32)] * 2
        out_specs += [st_spec, st_spec]
    return pl.pallas_call(
        functools.partial(_gla_kernel, has_init=st0 is not None, emit_final=emit_final),
        grid=(bsz, nseg),
        in_specs=in_specs,
        out_specs=out_specs,
        out_shape=out_shape,
        scratch_shapes=[pltpu.VMEM((HEAD_W, HEAD_W), F32), pltpu.VMEM((HEAD_W, HEAD_W), F32)],
        compiler_params=_cparams("parallel", "arbitrary"),
        name="gla",
    )(*args)


def _fft_direct_kernel(zr_ref, zi_ref, cn_ref, sn_ref, o_ref):
    o_ref[...] = _dot(cn_ref[...], zr_ref[...]) + _dot(sn_ref[...], zi_ref[...])


def _fft_direct(zr3, zi3, cn, sn):
    bsz, n, w = zr3.shape
    blk = pl.BlockSpec((None, n, w), lambda b: (b, 0, 0))
    tab = pl.BlockSpec((n, n), lambda b: (0, 0))
    return pl.pallas_call(
        _fft_direct_kernel,
        grid=(bsz,),
        in_specs=[blk, blk, tab, tab],
        out_specs=blk,
        out_shape=jax.ShapeDtypeStruct((bsz, n, w), F32),
        compiler_params=_cparams("parallel"),
        name="fft_direct",
    )(zr3, zi3, cn, sn)


def _fft_a_kernel(zr_ref, zi_ref, c_ref, s_ref, gr_ref, gi_ref):
    zr = zr_ref[...]
    zi = zi_ref[...]
    cm = c_ref[...]
    sm = s_ref[...]
    gr_ref[...] = (_dot(cm, zr) + _dot(sm, zi)).astype(BF16)
    gi_ref[...] = (_dot(cm, zi) - _dot(sm, zr)).astype(BF16)


def _fft_c_kernel(gr_ref, gi_ref, mc_ref, ms_ref, o_ref):
    for j in range(gr_ref.shape[0]):
        o_ref[:, j, :] = _dot(mc_ref[j], gr_ref[j]) + _dot(ms_ref[j], gi_ref[j])


def _fft_two_stage(zr3, zi3, tabs):
    bsz, n, w = zr3.shape
    n1 = FFT_N1
    n2 = n // n1
    c1, s1, mc, ms = tabs
    tn = 2048
    wide = n2 * w
    blk = pl.BlockSpec((None, n1, tn), lambda b, j: (b, 0, j))
    tab = pl.BlockSpec((n1, n1), lambda b, j: (0, 0))
    gr, gi = pl.pallas_call(
        _fft_a_kernel,
        grid=(bsz, wide // tn),
        in_specs=[blk, blk, tab, tab],
        out_specs=[blk, blk],
        out_shape=[jax.ShapeDtypeStruct((bsz, n1, wide), BF16)] * 2,
        compiler_params=_cparams("parallel", "parallel"),
        name="fft_stage_a",
    )(zr3.reshape(bsz, n1, wide), zi3.reshape(bsz, n1, wide), c1, s1)
    kb = 8
    gblk = pl.BlockSpec((None, kb, n2, w), lambda b, j: (b, j, 0, 0))
    mblk = pl.BlockSpec((kb, n2, n2), lambda b, j: (j, 0, 0))
    out = pl.pallas_call(
        _fft_c_kernel,
        grid=(bsz, n1 // kb),
        in_specs=[gblk, gblk, mblk, mblk],
        out_specs=pl.BlockSpec((None, n2, kb, w), lambda b, j: (b, 0, j, 0)),
        out_shape=jax.ShapeDtypeStruct((bsz, n2, n1, w), F32),
        compiler_params=_cparams("parallel", "parallel"),
        name="fft_stage_c",
    )(gr.reshape(bsz, n1, n2, w), gi.reshape(bsz, n1, n2, w), mc, ms)
    return out.reshape(bsz, n, w)


def _route(logits, bias):
    tm = logits.shape[1]
    s = _sigmoid(logits)
    biased = s + bias
    neg = -jnp.inf
    rows = lax.broadcasted_iota(jnp.int32, (8, tm), 0)

    def first_argmax(x, ids, sentinel):
        m = jnp.max(x, axis=0, keepdims=True)
        return m, jnp.min(jnp.where(x == m, ids, sentinel), axis=0, keepdims=True)

    gs_rows = []
    for g in range(N_GROUPS):
        x = biased[8 * g:8 * g + 8]
        m1, i1 = first_argmax(x, rows, 8)
        m2 = jnp.max(jnp.where(rows == i1, neg, x), axis=0, keepdims=True)
        gs_rows.append(m1 + m2)
    gs = jnp.concatenate(gs_rows, axis=0)
    gsel = jnp.zeros((N_GROUPS, tm), F32)
    for _ in range(TOPK_GROUPS):
        _, i = first_argmax(gs, rows, 8)
        hit = rows == i
        gsel = jnp.where(hit, 1.0, gsel)
        gs = jnp.where(hit, neg, gs)

    xs = [jnp.where(gsel[g:g + 1] > 0.0, biased[8 * g:8 * g + 8], neg) for g in range(N_GROUPS)]
    ids = [rows + 8 * g for g in range(N_GROUPS)]
    sel = [jnp.zeros((8, tm), F32) for _ in range(N_GROUPS)]
    eids = []
    for _ in range(TOP_K):
        m = xs[0]
        for g in range(1, N_GROUPS):
            m = jnp.maximum(m, xs[g])
        m = jnp.max(m, axis=0, keepdims=True)
        cand = jnp.where(xs[0] == m, ids[0], N_EXPERTS)
        for g in range(1, N_GROUPS):
            cand = jnp.minimum(cand, jnp.where(xs[g] == m, ids[g], N_EXPERTS))
        i = jnp.min(cand, axis=0, keepdims=True)
        eids.append(i)
        for g in range(N_GROUPS):
            hit = ids[g] == i
            sel[g] = jnp.where(hit, 1.0, sel[g])
            xs[g] = jnp.where(hit, neg, xs[g])

    sel_all = jnp.concatenate(sel, axis=0)
    tr = lax.broadcasted_iota(jnp.int32, (tm, tm), 0)
    tc = lax.broadcasted_iota(jnp.int32, (tm, tm), 1)
    before = jnp.where(tr < tc, 1.0, 0.0).astype(BF16)
    seen = _dot(sel_all.astype(BF16), before)
    counts = jnp.sum(sel_all, axis=1, keepdims=True)

    n_chunks = jnp.ceil(counts * (1.0 / CHUNK))
    er = lax.broadcasted_iota(jnp.int32, (N_EXPERTS, N_EXPERTS), 0)
    ec = lax.broadcasted_iota(jnp.int32, (N_EXPERTS, N_EXPERTS), 1)
    lower = jnp.where(ec < er, 1.0, 0.0).astype(BF16)
    run_start = _dot(lower, jnp.broadcast_to(n_chunks, (N_EXPERTS, LANES)).astype(BF16))[:, 0:1] * CHUNK
    local_pos = seen + run_start

    def pick(k, table):
        acc = None
        for g in range(N_GROUPS):
            v = jnp.where(ids[g] == eids[k], table[8 * g:8 * g + 8], 0.0)
            acc = v if acc is None else acc + v
        return jnp.sum(acc, axis=0, keepdims=True)

    w_raw = [pick(k, s) for k in range(TOP_K)]
    lpos = [pick(k, local_pos) for k in range(TOP_K)]
    tot = w_raw[0]
    for k in range(1, TOP_K):
        tot = tot + w_raw[k]
    weights = [w / tot * ROUTED_SCALE for w in w_raw]
    return weights, lpos, n_chunks, run_start


def _mix_out_kernel(x_ref, of_ref, ob_ref, g_ref, su_ref, sv_ref, cb_ref, cc_ref, cx_ref,
                    ccp_ref, cxp_ref, ccn_ref, cxn_ref, ft_ref, mod_ref,
                    glag_ref, sgng_ref, sgnb_ref, wsgu_ref, bsgu_ref, wconv_ref, wout_ref,
                    ln1g_ref, ln1b_ref, wrh_ref, wrl_ref, rb_ref,
                    x1_ref, xw_ref, wrow_ref, lpos_ref, tab_ref, cnt_ref, carry_ref, *, seq_len):
    tm = x_ref.shape[0]
    i = pl.program_id(0)

    @pl.when(i == 0)
    def _():
        carry_ref[...] = jnp.zeros_like(carry_ref)

    mod = mod_ref[...]
    g1 = mod[:, 2 * D_MODEL:3 * D_MODEL]
    sh2 = mod[:, 3 * D_MODEL:4 * D_MODEL]
    sc2 = mod[:, 4 * D_MODEL:5 * D_MODEL]

    o = of_ref[...] + ob_ref[...]
    hr = lax.broadcasted_iota(jnp.int32, (HEAD_W, HEAD_W), 0) >> 6
    hc = lax.broadcasted_iota(jnp.int32, (HEAD_W, HEAD_W), 1) >> 6
    head_mean = jnp.where(hr == hc, 1.0 / GLA_DK, 0.0).astype(BF16)
    o2 = o * o
    o2_hi = o2.astype(BF16)
    o2_lo = (o2 - o2_hi.astype(F32)).astype(BF16)
    ms = _dot(o2_hi, head_mean) + _dot(o2_lo, head_mean)
    y_gla = o * lax.rsqrt(ms + RMS_EPS) * glag_ref[...] * _silu(g_ref[...])

    vn = _ln(sv_ref[...]) * sgng_ref[...] + sgnb_ref[...]
    br = lax.broadcasted_iota(jnp.int32, (GMLP_GROUPS * GMLP_CHUNK, HEAD_W), 0) >> 7
    bc = lax.broadcasted_iota(jnp.int32, (GMLP_GROUPS * GMLP_CHUNK, HEAD_W), 1) >> 6
    sgu_bd = br == bc
    sp_parts = []
    for j in range(tm // GMLP_CHUNK):
        vc = vn[j * GMLP_CHUNK:(j + 1) * GMLP_CHUNK]
        vbd = jnp.where(sgu_bd, jnp.concatenate([vc] * GMLP_GROUPS, axis=0), 0.0).astype(BF16)
        sp_parts.append(_dot(wsgu_ref[...], vbd) + bsgu_ref[...])
    y_sgu = su_ref[...] * jnp.concatenate(sp_parts, axis=0)

    z = cc_ref[...] * cx_ref[...]
    z_before = ccp_ref[7:8, :] * cxp_ref[7:8, :]
    z_after = ccn_ref[0:1, :] * cxn_ref[0:1, :]
    row = lax.broadcasted_iota(jnp.int32, (tm, HEAD_W), 0)
    pos = (i * tm + row) & (seq_len - 1)
    z_prev = jnp.where(row == 0, z_before, pltpu.roll(z, 1, 0))
    z_next = jnp.where(row == tm - 1, z_after, pltpu.roll(z, tm - 1, 0))
    z_prev = jnp.where(pos == 0, 0.0, z_prev)
    z_next = jnp.where(pos == seq_len - 1, 0.0, z_next)
    wconv = wconv_ref[...]
    y_conv = cb_ref[...] * (wconv[0:1] * z_prev + wconv[1:2] * z + wconv[2:3] * z_next)

    y = (_dot(y_gla.astype(BF16), wout_ref[0:HEAD_W, :])
         + _dot(y_sgu.astype(BF16), wout_ref[HEAD_W:2 * HEAD_W, :])
         + _dot(y_conv.astype(BF16), wout_ref[2 * HEAD_W:3 * HEAD_W, :])
         + _dot(ft_ref[...].astype(BF16), wout_ref[3 * HEAD_W:4 * HEAD_W, :]))
    x1 = _ln(DEEPNORM_ALPHA * x_ref[...] + g1 * y) * ln1g_ref[...] + ln1b_ref[...]
    x1_ref[...] = x1
    h2 = _ln(x1) * (1.0 + sc2) + sh2
    xw_ref[...] = _pack_bf16_pairs(h2)

    h2_hi = h2.astype(BF16)
    h2_lo = (h2 - h2_hi.astype(F32)).astype(BF16)
    logits = _dot(h2_hi, wrh_ref[...]) + _dot(h2_hi, wrl_ref[...]) + _dot(h2_lo, wrh_ref[...])
    weights, lpos, n_chunks, run_start = _route(logits.T[:N_EXPERTS], rb_ref[...])
    lpos_ref[...] = jnp.concatenate(lpos, axis=0).astype(jnp.int32)
    wrow_ref[...] = jnp.concatenate(weights, axis=0)
    carry = carry_ref[:, 0:1]
    lane = lax.broadcasted_iota(jnp.int32, (N_EXPERTS, LANES), 1)
    cols = jnp.where(lane == 0, n_chunks, jnp.where(lane == 1, run_start, jnp.where(lane == 2, carry, 0.0)))
    tab = jnp.concatenate([cols, jnp.zeros((LANES - N_EXPERTS, LANES), F32)], axis=0).T
    tab_ref[...] = tab[0:8].astype(jnp.int32)
    new_carry = carry_ref[...] + n_chunks
    carry_ref[...] = new_carry
    cnt_ref[...] = new_carry


def _mix_out(x2d, of2d, ob2d, proj, yft2d, mod3, mod_row, lw, seq_len, tm):
    t = x2d.shape[0]
    tm = min(tm, t)
    nt8 = t // 8
    rows8 = tm // 8

    def col(cb):
        return pl.BlockSpec((tm, HEAD_W), lambda i: (i, cb))

    def halo_prev(cb):
        return pl.BlockSpec((8, HEAD_W), lambda i: (jnp.maximum(i * rows8 - 1, 0), cb))

    def halo_next(cb):
        return pl.BlockSpec((8, HEAD_W), lambda i: (jnp.minimum((i + 1) * rows8, nt8 - 1), cb))

    def full(a):
        return pl.BlockSpec(a.shape, lambda i: (0,) * a.ndim)

    tok_d = pl.BlockSpec((tm, D_MODEL), lambda i: (i, 0))
    tok_h = pl.BlockSpec((tm, HEAD_W), lambda i: (i, 0))
    weights = [lw["gla_norm_g"], lw["sgu_norm_g"], lw["sgu_norm_b"], lw["w_sgu_cat"], lw["b_sgu_full"],
               lw["w_conv"], lw["w_out"], lw["ln1_g"], lw["ln1_b"], lw["w_router_hi"], lw["w_router_lo"], lw["router_bias"]]
    in_specs = ([tok_d, tok_h, tok_h, col(COL_G), col(COL_SGU), col(COL_SGV), col(COL_CVB), col(COL_CVC),
                 col(COL_CVX), halo_prev(COL_CVC), halo_prev(COL_CVX), halo_next(COL_CVC), halo_next(COL_CVX),
                 tok_h, pl.BlockSpec((None, 1, 6 * D_MODEL), lambda i: (mod_row(i, tm), 0, 0))]
                + [full(w) for w in weights])
    args = [x2d, of2d, ob2d] + [proj] * 10 + [yft2d, mod3] + weights
    return pl.pallas_call(
        functools.partial(_mix_out_kernel, seq_len=seq_len),
        grid=(t // tm,),
        in_specs=in_specs,
        out_specs=[tok_d,
                   pl.BlockSpec((tm, PACK_W), lambda i: (i, 0)),
                   pl.BlockSpec((None, TOP_K, tm), lambda i: (i, 0, 0)),
                   pl.BlockSpec((None, TOP_K, tm), lambda i: (i, 0, 0)),
                   pl.BlockSpec((None, 8, LANES), lambda i: (i, 0, 0)),
                   pl.BlockSpec((N_EXPERTS, LANES), lambda i: (0, 0))],
        out_shape=[jax.ShapeDtypeStruct((t, D_MODEL), F32),
                   jax.ShapeDtypeStruct((t, PACK_W), jnp.uint32),
                   jax.ShapeDtypeStruct((t // tm, TOP_K, tm), F32),
                   jax.ShapeDtypeStruct((t // tm, TOP_K, tm), jnp.int32),
                   jax.ShapeDtypeStruct((t // tm, 8, LANES), jnp.int32),
                   jax.ShapeDtypeStruct((N_EXPERTS, LANES), F32)],
        scratch_shapes=[pltpu.VMEM((N_EXPERTS, LANES), F32)],
        compiler_params=_cparams("arbitrary"),
        name="mix_out",
    )(*args)


def _local_rows(tm):
    need = tm * TOP_K + N_EXPERTS * (CHUNK - 1)
    return -(-need // LBLK) * LBLK


def _run_copy(local_ref, sorted_hbm, sem, local_row, sorted_row, to_sorted, rows=CHUNK):
    loc = local_ref.at[pl.ds(pl.multiple_of(local_row, CHUNK), rows)]
    srt = sorted_hbm.at[pl.ds(pl.multiple_of(sorted_row, CHUNK), rows)]
    return pltpu.make_async_copy(loc, srt, sem) if to_sorted else pltpu.make_async_copy(srt, loc, sem)


def _start_run_copies(tab_ref, gstart_ref, local_ref, sorted_hbm, sem, to_sorted):
    def per_expert(e, totals):
        n = tab_ref[0, e]
        l0 = tab_ref[1, e]
        g0 = (gstart_ref[e] + tab_ref[2, e]) * CHUNK
        n_big = n >> BIG_SHIFT
        n_small = n & ((1 << BIG_SHIFT) - 1)

        def big(j, c):
            _run_copy(local_ref, sorted_hbm, sem, l0 + j * BIG_ROWS, g0 + j * BIG_ROWS, to_sorted, BIG_ROWS).start()
            return c

        lax.fori_loop(0, n_big, big, 0)
        l1 = l0 + n_big * BIG_ROWS
        g1 = g0 + n_big * BIG_ROWS

        def small(j, c):
            _run_copy(local_ref, sorted_hbm, sem, l1 + j * CHUNK, g1 + j * CHUNK, to_sorted).start()
            return c

        lax.fori_loop(0, n_small, small, 0)
        return totals[0] + n_big, totals[1] + n_small

    return lax.fori_loop(0, N_EXPERTS, per_expert, (jnp.int32(0), jnp.int32(0)))


def _get_pending(pending_ref, s):
    return pending_ref[s, 0], pending_ref[s, 1]


def _set_pending(pending_ref, s, counts):
    pending_ref[s, 0] = counts[0]
    pending_ref[s, 1] = counts[1]


def _wait_run_copies(counts, local_ref, sorted_hbm, sem, to_sorted):
    batch = 8

    def wait_n(rows, reps):
        def body(j, c):
            for _ in range(reps):
                _run_copy(local_ref, sorted_hbm, sem, 0, 0, to_sorted, rows).wait()
            return c
        return body

    for count, rows in zip(counts, (BIG_ROWS, CHUNK)):
        lax.fori_loop(0, count >> 3, wait_n(rows, batch), 0)
        lax.fori_loop(0, count & (batch - 1), wait_n(rows, 1), 0)


def _moe_scatter_kernel(gstart_ref, gtail_ref, xw_ref, lpos_ref, tab_ref, sorted_hbm,
                        local_ref, sem, zsem, pending_ref):
    i = pl.program_id(0)
    n = pl.num_programs(0)
    slot = i & 1
    tm = xw_ref.shape[0]
    lrows = local_ref.shape[1]

    def zero_copy(e):
        off = pl.multiple_of(jnp.maximum(gtail_ref[e], 0), EBLK)
        return pltpu.make_async_copy(local_ref.at[1, pl.ds(0, EBLK)], sorted_hbm.at[pl.ds(off, EBLK)], zsem)

    @pl.when(i == 0)
    def _():
        _set_pending(pending_ref, 0, (0, 0))
        _set_pending(pending_ref, 1, (0, 0))
        local_ref[1, 0:EBLK, :] = jnp.zeros((EBLK, PACK_W), jnp.uint32)

        def z_start(e, c):
            @pl.when(gtail_ref[e] >= 0)
            def _():
                zero_copy(e).start()
            return c

        def z_wait(e, c):
            @pl.when(gtail_ref[e] >= 0)
            def _():
                zero_copy(e).wait()
            return c

        lax.fori_loop(0, N_EXPERTS, z_start, 0)
        lax.fori_loop(0, N_EXPERTS, z_wait, 0)

    local = local_ref.at[slot]
    _wait_run_copies(_get_pending(pending_ref, slot), local, sorted_hbm, sem.at[slot], True)

    x = _unpack_bf16_pairs(xw_ref[...]).astype(BF16)
    lpos = lpos_ref[...].astype(jnp.int16)
    one = jnp.ones((LBLK, tm), BF16)
    for b in range(lrows // LBLK):
        riota = lax.broadcasted_iota(jnp.int16, (LBLK, tm), 0) + b * LBLK
        p = jnp.zeros((LBLK, tm), BF16)
        for k in range(TOP_K):
            p = jnp.where(riota == lpos[k:k + 1, :], one, p)
        local[b * LBLK:(b + 1) * LBLK, :] = _pack_exact_bf16_pairs(_dot(p, x))

    _set_pending(pending_ref, slot, _start_run_copies(tab_ref, gstart_ref, local, sorted_hbm, sem.at[slot], True))

    @pl.when(i == n - 1)
    def _():
        for s in range(2):
            _wait_run_copies(_get_pending(pending_ref, s), local_ref.at[s], sorted_hbm, sem.at[s], True)


def _moe_scatter(gstart, gtail, xw, lpos, tab, n_rows, tm):
    n_tiles = xw.shape[0] // tm
    lrows = _local_rows(tm)
    any_spec = pl.BlockSpec(memory_space=pl.ANY)
    return pl.pallas_call(
        _moe_scatter_kernel,
        grid_spec=pltpu.PrefetchScalarGridSpec(
            num_scalar_prefetch=2,
            grid=(n_tiles,),
            in_specs=[pl.BlockSpec((tm, PACK_W), lambda i, *_: (i, 0)),
                      pl.BlockSpec((None, TOP_K, tm), lambda i, *_: (i, 0, 0)),
                      pl.BlockSpec((None, 8, LANES), lambda i, *_: (i, 0, 0), memory_space=pltpu.SMEM)],
            out_specs=any_spec,
            scratch_shapes=[pltpu.VMEM((2, lrows, PACK_W), jnp.uint32), pltpu.SemaphoreType.DMA((2,)),
                            pltpu.SemaphoreType.DMA, pltpu.SMEM((2, 2), jnp.int32)]),
        out_shape=jax.ShapeDtypeStruct((n_rows, PACK_W), jnp.uint32),
        compiler_params=_cparams("arbitrary"),
        name="moe_scatter",
    )(gstart, gtail, xw, lpos, tab)


def _expert2_kernel(be_ref, nu_ref, xs_ref, wg_ref, wu_ref, wd_ref, ys_ref, wgu_b, wd_b):
    j = pl.program_id(0)

    @pl.when(jnp.logical_or(j == 0, be_ref[j] != be_ref[jnp.maximum(j - 1, 0)]))
    def _():
        wgu_b[:, :EXPERT_FF] = wg_ref[...].astype(BF16)
        wgu_b[:, EXPERT_FF:] = wu_ref[...].astype(BF16)
        wd_b[...] = wd_ref[...].astype(BF16)

    @pl.when(j < nu_ref[0])
    def _():
        x = _unpack_bf16_pairs(xs_ref[...]).astype(BF16)
        gu = _dot(x, wgu_b[...])
        a = _silu(gu[:, :EXPERT_FF]) * gu[:, EXPERT_FF:]
        ys_ref[...] = _pack_bf16_pairs(_dot(a.astype(BF16), wd_b[...]))


def _experts2(xs, block_expert, n_used, lw, layer):
    n_rows = xs.shape[0]
    ff = EXPERT_FF

    def blk(j, be, nu):
        return (jnp.minimum(j, jnp.maximum(nu[0] - 1, 0)), 0)

    def wblk(j, be, nu):
        return (layer, be[j], 0, 0)

    return pl.pallas_call(
        _expert2_kernel,
        grid_spec=pltpu.PrefetchScalarGridSpec(
            num_scalar_prefetch=2,
            grid=(n_rows // EBLK,),
            in_specs=[pl.BlockSpec((EBLK, PACK_W), blk),
                      pl.BlockSpec((None, None, D_MODEL, ff), wblk),
                      pl.BlockSpec((None, None, D_MODEL, ff), wblk),
                      pl.BlockSpec((None, None, ff, D_MODEL), wblk)],
            out_specs=pl.BlockSpec((EBLK, PACK_W), blk),
            scratch_shapes=[pltpu.VMEM((D_MODEL, 2 * ff), BF16), pltpu.VMEM((ff, D_MODEL), BF16)]),
        out_shape=jax.ShapeDtypeStruct((n_rows, PACK_W), jnp.uint32),
        compiler_params=_cparams("arbitrary"),
        name="moe_experts",
    )(block_expert, n_used, xs, lw["w_exp_gate"], lw["w_exp_up"], lw["w_exp_down"])


def _moe_combine_kernel(gstart_ref, x1_ref, xw_ref, lpos_ref, wrow_ref, tab_ref, tab_next_ref, mod_ref,
                        sg_ref, su_ref, sd_ref, ln2g_ref, ln2b_ref, sorted_hbm, o_ref,
                        local_ref, sem, pending_ref):
    i = pl.program_id(0)
    n = pl.num_programs(0)
    slot = i & 1
    tm = x1_ref.shape[0]
    lrows = local_ref.shape[1]

    @pl.when(i == 0)
    def _():
        local_ref[...] = jnp.zeros_like(local_ref)
        _set_pending(pending_ref, 0,
                     _start_run_copies(tab_ref, gstart_ref, local_ref.at[0], sorted_hbm, sem.at[0], False))

    @pl.when(i + 1 < n)
    def _():
        nxt = 1 - slot
        _set_pending(pending_ref, nxt, _start_run_copies(tab_next_ref, gstart_ref, local_ref.at[nxt], sorted_hbm,
                                                         sem.at[nxt], False))

    h = _unpack_bf16_pairs(xw_ref[...]).astype(BF16)
    a = _silu(_dot(h, sg_ref[...])) * _dot(h, su_ref[...])
    acc = _dot(a.astype(BF16), sd_ref[...])

    local = local_ref.at[slot]
    _wait_run_copies(_get_pending(pending_ref, slot), local, sorted_hbm, sem.at[slot], False)

    lpos = lpos_ref[...].astype(jnp.int16)
    wrow = wrow_ref[...].astype(BF16)
    for b in range(lrows // LBLK):
        riota = lax.broadcasted_iota(jnp.int16, (LBLK, tm), 0) + b * LBLK
        q = jnp.zeros((LBLK, tm), BF16)
        for k in range(TOP_K):
            q = jnp.where(riota == lpos[k:k + 1, :], jnp.broadcast_to(wrow[k:k + 1, :], (LBLK, tm)), q)
        y = _unpack_bf16_pairs(local[b * LBLK:(b + 1) * LBLK, :]).astype(BF16)
        acc = acc + _dot_tn(q, y)

    g2 = mod_ref[...][:, 5 * D_MODEL:6 * D_MODEL]
    u = DEEPNORM_ALPHA * x1_ref[...] + g2 * acc
    o_ref[...] = _ln(u) * ln2g_ref[...] + ln2b_ref[...]


def _moe_combine(gstart, x1, xw, lpos, wrow, tab, ys, mod3, mod_row, lw, tm):
    t = x1.shape[0]
    n_tiles = t // tm
    lrows = _local_rows(tm)

    def full(a):
        return pl.BlockSpec(a.shape, lambda i, *_: (0,) * a.ndim)

    tab_blk = lambda f: pl.BlockSpec((None, 8, LANES), f, memory_space=pltpu.SMEM)
    return pl.pallas_call(
        _moe_combine_kernel,
        grid_spec=pltpu.PrefetchScalarGridSpec(
            num_scalar_prefetch=1,
            grid=(n_tiles,),
            in_specs=[pl.BlockSpec((tm, D_MODEL), lambda i, *_: (i, 0)),
                      pl.BlockSpec((tm, PACK_W), lambda i, *_: (i, 0)),
                      pl.BlockSpec((None, TOP_K, tm), lambda i, *_: (i, 0, 0)),
                      pl.BlockSpec((None, TOP_K, tm), lambda i, *_: (i, 0, 0)),
                      tab_blk(lambda i, *_: (i, 0, 0)),
                      tab_blk(lambda i, *_: (jnp.minimum(i + 1, n_tiles - 1), 0, 0)),
                      pl.BlockSpec((None, 1, 6 * D_MODEL), lambda i, *_: (mod_row(i, tm), 0, 0)),
                      full(lw["w_sh_gate"]), full(lw["w_sh_up"]), full(lw["w_sh_down"]),
                      full(lw["ln2_g"]), full(lw["ln2_b"]),
                      pl.BlockSpec(memory_space=pl.ANY)],
            out_specs=pl.BlockSpec((tm, D_MODEL), lambda i, *_: (i, 0)),
            scratch_shapes=[pltpu.VMEM((2, lrows, PACK_W), jnp.uint32), pltpu.SemaphoreType.DMA((2,)),
                            pltpu.SMEM((2, 2), jnp.int32)]),
        out_shape=jax.ShapeDtypeStruct((t, D_MODEL), F32),
        compiler_params=_cparams("arbitrary"),
        name="moe_combine",
    )(gstart, x1, xw, lpos, wrow, tab, tab, mod3, lw["w_sh_gate"], lw["w_sh_up"], lw["w_sh_down"],
      lw["ln2_g"], lw["ln2_b"], ys)


def _moe2(x1, xw, wrow, lpos, tab, chunks_f, mod3, mod_row, lw, layer, tm):
    t = x1.shape[0]
    n_tiles = t // tm
    region_rows = chunks_f[:, 0].astype(jnp.int32) * CHUNK
    padded = ((region_rows + EBLK - 1) // EBLK) * EBLK
    end = jnp.cumsum(padded)
    start = end - padded
    n_blocks = -(-(t * TOP_K + n_tiles * N_EXPERTS * (CHUNK - 1)) // EBLK) + N_EXPERTS
    n_used = end[-1] // EBLK
    blk_row = jnp.minimum(jnp.arange(n_blocks, dtype=jnp.int32), jnp.maximum(n_used - 1, 0)) * EBLK
    block_expert = jnp.minimum(jnp.sum((end[None, :] <= blk_row[:, None]).astype(jnp.int32), axis=1),
                               N_EXPERTS - 1).astype(jnp.int32)
    gstart = start // CHUNK
    gtail = jnp.where(padded > 0, end - EBLK, -1)
    xs = _moe_scatter(gstart, gtail, xw, lpos, tab, n_blocks * EBLK, tm)
    ys = _experts2(xs, block_expert, n_used.reshape(1), lw, layer)
    return _moe_combine(gstart, x1, xw, lpos, wrow, tab, ys, mod3, mod_row, lw, tm)


def _channel_dft_table():
    k = np.arange(FNET_CH, dtype=np.float64)
    ang = 2.0 * np.pi * np.outer(k, k) / FNET_CH
    eye = np.eye(FNET_GROUPS)
    return np.concatenate([np.kron(eye, np.cos(ang)), -np.kron(eye, np.sin(ang))], axis=1)


def _direct_dft_tables(n):
    k = np.arange(n, dtype=np.float64)
    ang = 2.0 * np.pi * (np.outer(k, k) % n) / n
    scale = 1.0 / math.sqrt(n * FNET_CH)
    return np.cos(ang) * scale, np.sin(ang) * scale


def _two_stage_dft_tables(n):
    n1 = FFT_N1
    n2 = n // n1
    a = np.arange(n1, dtype=np.float64)
    ang1 = 2.0 * np.pi * (np.outer(a, a) % n1) / n1
    k1 = np.arange(n1).reshape(n1, 1, 1)
    k2 = np.arange(n2).reshape(1, n2, 1)
    m2 = np.arange(n2).reshape(1, 1, n2)
    ang2 = 2.0 * np.pi * ((m2 * (k1 + n1 * k2)) % n) / n
    scale = 1.0 / math.sqrt(n * FNET_CH)
    return np.cos(ang1), np.sin(ang1), np.cos(ang2) * scale, np.sin(ang2) * scale


def _grid_sincos_table(rows, d):
    quarter = d // 4
    omega = 1.0 / (POS_BASE ** (np.arange(quarter, dtype=np.float64) / quarter))
    r = np.arange(rows, dtype=np.float64)[:, None] * omega
    c = np.arange(GRID_W, dtype=np.float64)[:, None] * omega
    return (np.concatenate([np.sin(r), np.cos(r)], axis=-1).astype(np.float32),
            np.concatenate([np.sin(c), np.cos(c)], axis=-1).astype(np.float32))


def _layer_weights(l, w_in, w_gla_a, b_gla_a, gla_norm_g, sgu_norm_g, sgu_norm_b, w_sgu, b_sgu, w_conv,
                   w_out, ln1_g, ln1_b, ln2_g, ln2_b, w_router, router_bias,
                   w_exp_gate, w_exp_up, w_exp_down, w_sh_gate, w_sh_up, w_sh_down):
    wi = w_in[l]
    lr0 = 4 * HEAD_W
    w_in_p = jnp.concatenate(
        [wi[:, :lr0], wi[:, lr0 + 2 * GLA_LR:], wi[:, lr0:lr0 + 2 * GLA_LR],
         jnp.zeros((D_MODEL, LR_W - 2 * GLA_LR), F32)], axis=1).astype(BF16)
    wa_pad = jnp.zeros((LR_W, 2 * HEAD_W), F32)
    wa_pad = wa_pad.at[:GLA_LR, :HEAD_W].set(w_gla_a[l, 0])
    wa_pad = wa_pad.at[GLA_LR:2 * GLA_LR, HEAD_W:].set(w_gla_a[l, 1])
    w_router_pad = jnp.concatenate([w_router[l], jnp.zeros((D_MODEL, LANES - N_EXPERTS), F32)], axis=1)
    w_router_hi = w_router_pad.astype(BF16)
    row = lambda a: a[l].reshape(1, -1)
    return {
        "w_in_p": w_in_p,
        "wa_pad": wa_pad,
        "ba": jnp.concatenate([b_gla_a[l, 0], b_gla_a[l, 1]]).reshape(1, 2 * HEAD_W),
        "gla_norm_g": row(gla_norm_g), "sgu_norm_g": row(sgu_norm_g), "sgu_norm_b": row(sgu_norm_b),
        "w_sgu_cat": jnp.concatenate([w_sgu[l, g] for g in range(GMLP_GROUPS)], axis=1).astype(BF16),
        "b_sgu_full": jnp.repeat(b_sgu[l].T, HEAD_W // GMLP_GROUPS, axis=1),
        "w_conv": w_conv[l],
        "w_out": w_out[l].astype(BF16),
        "ln1_g": row(ln1_g), "ln1_b": row(ln1_b), "ln2_g": row(ln2_g), "ln2_b": row(ln2_b),
        "w_router_hi": w_router_hi, "w_router_lo": (w_router_pad - w_router_hi.astype(F32)).astype(BF16),
        "router_bias": router_bias[l].reshape(N_EXPERTS, 1),
        "w_exp_gate": w_exp_gate, "w_exp_up": w_exp_up, "w_exp_down": w_exp_down,
        "w_sh_gate": w_sh_gate[l].astype(BF16), "w_sh_up": w_sh_up[l].astype(BF16),
        "w_sh_down": w_sh_down[l].astype(BF16),
    }


def _state_to_blockdiag_t(s):
    bsz = s.shape[0]
    st = jnp.swapaxes(s, 2, 3)
    eye = jnp.eye(GLA_HEADS, dtype=s.dtype)
    return jnp.einsum("bhvd,hg->bhvgd", st, eye).reshape(bsz, HEAD_W, HEAD_W)


def _blockdiag_t_to_state(st):
    bsz = st.shape[0]
    s5 = st.reshape(bsz, GLA_HEADS, GLA_DK, GLA_HEADS, GLA_DK)
    diag = jnp.stack([s5[:, h, :, h, :] for h in range(GLA_HEADS)], axis=1)
    return jnp.swapaxes(diag, 2, 3)


def _trunk_layer(x3, pos, mod3, mod_row, lw, layer, st0, emit_final, tabs):
    bsz, n, _ = x3.shape
    t = bsz * n
    outs = _in_proj(x3.reshape(t, D_MODEL), pos, mod3, mod_row, lw["w_in_p"], tabs["cs"], tm=512)
    if pos is not None:
        proj, zr, zi, x2d = outs
    else:
        proj, zr, zi = outs
        x2d = x3.reshape(t, D_MODEL)
    gla_out = _gla(proj.reshape(bsz, n, PROJ_W), lw["wa_pad"], lw["ba"], st0, emit_final)
    o_f, o_b = gla_out[:2]
    zr3 = zr.reshape(bsz, n, HEAD_W)
    zi3 = zi.reshape(bsz, n, HEAD_W)
    if "two_stage" in tabs:
        yft = _fft_two_stage(zr3, zi3, tabs["two_stage"])
    else:
        yft = _fft_direct(zr3, zi3, *tabs["direct"])
    x1, xw, wrow, lpos, tab, chunks = _mix_out(x2d, o_f.reshape(t, HEAD_W), o_b.reshape(t, HEAD_W), proj,
                                             yft.reshape(t, HEAD_W), mod3, mod_row, lw, seq_len=n, tm=256)
    x2 = _moe2(x1, xw, wrow, lpos, tab, chunks, mod3, mod_row, lw, layer, tm=min(256, t))
    return x2.reshape(bsz, n, D_MODEL), gla_out[2:]


def kernel(x_prompt, x_sample, c, state_gla, c_ctx, w_ada, b_ada, w_in, w_gla_a, b_gla_a, gla_norm_g, sgu_norm_g, sgu_norm_b, w_sgu, b_sgu, w_conv, w_out, ln1_g, ln1_b, ln2_g, ln2_b, w_router, router_bias, w_exp_gate, w_exp_up, w_exp_down, w_sh_gate, w_sh_up, w_sh_down):
    n_layers = w_ada.shape[0]
    bp, np_, _ = x_prompt.shape
    bs, ns, _ = x_sample.shape
    assert bs <= 7

    cond8 = jnp.concatenate([c_ctx[None, :], c, jnp.zeros((7 - bs, D_MODEL), F32)], axis=0)
    mod = _ada_mod(cond8, w_ada, b_ada)

    tabs_p = {"cs": jnp.asarray(_channel_dft_table(), BF16),
              "direct": tuple(jnp.asarray(a, BF16) for a in _direct_dft_tables(np_))}
    tabs_s = {"cs": tabs_p["cs"],
              "two_stage": tuple(jnp.asarray(a, BF16) for a in _two_stage_dft_tables(ns))}
    rtab, ctab = _grid_sincos_table(ns // GRID_W, D_MODEL)
    pos = jnp.concatenate([jnp.repeat(jnp.asarray(rtab), GRID_W, axis=0),
                           jnp.tile(jnp.asarray(ctab), (ns // GRID_W, 1))], axis=-1)

    prompt_row = lambda i, tm: 0
    sample_row = lambda i, tm: 1 + (i * tm) // ns

    y_p = x_prompt
    y_s = x_sample
    finals = []
    for l in range(n_layers):
        lw = _layer_weights(l, w_in, w_gla_a, b_gla_a, gla_norm_g, sgu_norm_g, sgu_norm_b, w_sgu, b_sgu,
                            w_conv, w_out, ln1_g, ln1_b, ln2_g, ln2_b, w_router, router_bias,
                            w_exp_gate, w_exp_up, w_exp_down, w_sh_gate, w_sh_up, w_sh_down)
        mod3 = mod[l].reshape(8, 1, 6 * D_MODEL)
        y_p, fin = _trunk_layer(y_p, None, mod3, prompt_row, lw, l, None, True, tabs_p)
        finals.append(jnp.stack([_blockdiag_t_to_state(fin[0]), _blockdiag_t_to_state(fin[1])], axis=1))
        st0 = jnp.stack([_state_to_blockdiag_t(state_gla[:, l, 0]), _state_to_blockdiag_t(state_gla[:, l, 1])])
        y_s, _ = _trunk_layer(y_s, pos if l == 0 else None, mod3, sample_row, lw, l, st0, False, tabs_s)
    new_state = jnp.stack(finals, axis=1).astype(x_prompt.dtype)
    return (y_p, y_s, new_state)
```

```python
import functools
import math

import numpy as np
import jax
import jax.numpy as jnp
from jax import lax
from jax.experimental import pallas as pl
from jax.experimental.pallas import tpu as pltpu

F32 = jnp.float32
BF16 = jnp.bfloat16

D_MODEL = 1024
DEPTH = 2
GRID_W = 64
HEAD_W = 256
GLA_HEADS = 4
GLA_DK = 64
GLA_LR = 16
GLA_TAU = 16.0
GLA_CHUNK = 64
GMLP_GROUPS = 4
GMLP_CHUNK = 128
FNET_GROUPS = 4
FNET_CH = 64
N_EXPERTS = 64
TOP_K = 8
N_GROUPS = 8
TOPK_GROUPS = 4
EXPERT_FF = 256
ROUTED_SCALE = 2.5
DEEPNORM_ALPHA = (2 * DEPTH) ** 0.25
LN_EPS = 1e-5
RMS_EPS = 1e-6
POS_BASE = 10000.0

COL_Q, COL_K, COL_V, COL_G, COL_SGU, COL_SGV, COL_CVB, COL_CVC, COL_CVX = range(9)
LR_W = 128
STORE_W = 9 * HEAD_W + LR_W
PROJ_W = STORE_W + HEAD_W
COL_LR = (9 * HEAD_W) // LR_W
HALO = 16

LANES = 128
PACK_W = D_MODEL // 2
EBLK_MAX = 512
LBLK = 256
CHUNK = 8
BIG_SHIFT = 2
BIG_ROWS = CHUNK << BIG_SHIFT
SEG = 256
FFT_N1 = 64
VMEM_LIMIT = 56 * 1024 * 1024


def _cparams(*sem):
    return pltpu.CompilerParams(dimension_semantics=sem, vmem_limit_bytes=VMEM_LIMIT)


def _ln(x):
    mu = jnp.mean(x, axis=-1, keepdims=True)
    xc = x - mu
    var = jnp.mean(xc * xc, axis=-1, keepdims=True)
    return xc * lax.rsqrt(var + LN_EPS)


def _sigmoid(x):
    return 1.0 / (1.0 + jnp.exp(-x))


def _silu(x):
    return x * _sigmoid(x)


def _pack_bf16_pairs(x):
    w = x.shape[1] // 2
    lo = lax.bitcast_convert_type(x[:, :w].astype(BF16).astype(F32), jnp.uint32)
    hi = lax.bitcast_convert_type(x[:, w:].astype(BF16).astype(F32), jnp.uint32)
    return (lo >> 16) | (hi & jnp.uint32(0xFFFF0000))


def _pack_exact_bf16_pairs(x):
    w = x.shape[1] // 2
    lo = lax.bitcast_convert_type(x[:, :w], jnp.uint32)
    hi = lax.bitcast_convert_type(x[:, w:], jnp.uint32)
    return (lo >> 16) | hi


def _unpack_bf16_pairs(u):
    lo = lax.bitcast_convert_type(u << 16, F32)
    hi = lax.bitcast_convert_type(u & jnp.uint32(0xFFFF0000), F32)
    return jnp.concatenate([lo, hi], axis=1)


def _dot(a, b):
    return jnp.dot(a, b, preferred_element_type=F32)


def _dot_nt(a, b):
    return lax.dot_general(a, b, (((1,), (1,)), ((), ())), preferred_element_type=F32)


def _dot_tn(a, b):
    return lax.dot_general(a, b, (((0,), (0,)), ((), ())), preferred_element_type=F32)


def _ada_kernel(c_ref, w_ref, b_ref, o_ref):
    c = c_ref[...]
    o_ref[...] = _dot(_silu(c).astype(BF16), w_ref[...].astype(BF16)) + b_ref[...]


def _ada_mod(cond8, w_ada, b_ada):
    n_l, d, w6 = w_ada.shape
    tn = 1536
    return pl.pallas_call(
        _ada_kernel,
        grid=(n_l, w6 // tn),
        in_specs=[pl.BlockSpec((8, d), lambda l, j: (0, 0)),
                  pl.BlockSpec((None, d, tn), lambda l, j: (l, 0, j)),
                  pl.BlockSpec((None, 1, tn), lambda l, j: (l, 0, j))],
        out_specs=pl.BlockSpec((None, 8, tn), lambda l, j: (l, 0, j)),
        out_shape=jax.ShapeDtypeStruct((n_l, 8, w6), F32),
        compiler_params=_cparams("parallel", "parallel"),
        name="ada_mod",
    )(cond8, w_ada, b_ada.reshape(n_l, 1, w6))


def _in_proj_kernel(*refs, has_pos):
    if has_pos:
        x_ref, pos_ref, mod_ref, w_ref, cs_ref, proj_ref, zr_ref, zi_ref, x0_ref = refs
        x = x_ref[...] + pos_ref[...]
        x0_ref[...] = x
    else:
        x_ref, mod_ref, w_ref, cs_ref, proj_ref, zr_ref, zi_ref = refs
        x = x_ref[...]
    mod = mod_ref[...]
    sh1 = mod[:, 0:D_MODEL]
    sc1 = mod[:, D_MODEL:2 * D_MODEL]
    h = _ln(x) * (1.0 + sc1) + sh1
    proj = _dot(h.astype(BF16), w_ref[...])
    proj_ref[...] = proj[:, :STORE_W].astype(BF16)
    ft = proj[:, STORE_W:].astype(BF16)
    z = _dot(ft, cs_ref[...])
    zr_ref[...] = z[:, :HEAD_W].astype(BF16)
    zi_ref[...] = z[:, HEAD_W:].astype(BF16)


def _in_proj(x2d, pos, mod3, mod_row, w_in_p, cs, tm):
    t = x2d.shape[0]
    tm = min(tm, t)
    in_specs = [pl.BlockSpec((tm, D_MODEL), lambda i: (i, 0))]
    args = [x2d]
    out_shape = [jax.ShapeDtypeStruct((t, STORE_W), BF16),
                 jax.ShapeDtypeStruct((t, HEAD_W), BF16),
                 jax.ShapeDtypeStruct((t, HEAD_W), BF16)]
    out_specs = [pl.BlockSpec((tm, STORE_W), lambda i: (i, 0)),
                 pl.BlockSpec((tm, HEAD_W), lambda i: (i, 0)),
                 pl.BlockSpec((tm, HEAD_W), lambda i: (i, 0))]
    if pos is not None:
        n_pos = pos.shape[0] // tm
        in_specs.append(pl.BlockSpec((tm, D_MODEL), lambda i: (i % n_pos, 0)))
        args.append(pos)
        out_shape.append(jax.ShapeDtypeStruct((t, D_MODEL), F32))
        out_specs.append(pl.BlockSpec((tm, D_MODEL), lambda i: (i, 0)))
    in_specs += [pl.BlockSpec((None, 1, 6 * D_MODEL), lambda i: (mod_row(i, tm), 0, 0)),
                 pl.BlockSpec((D_MODEL, PROJ_W), lambda i: (0, 0)),
                 pl.BlockSpec((HEAD_W, 2 * HEAD_W), lambda i: (0, 0))]
    args += [mod3, w_in_p, cs]
    return pl.pallas_call(
        functools.partial(_in_proj_kernel, has_pos=pos is not None),
        grid=(t // tm,),
        in_specs=in_specs,
        out_specs=out_specs,
        out_shape=out_shape,
        compiler_params=_cparams("parallel"),
        name="in_proj",
    )(*args)


def _gla_masks(reverse):
    i = np.arange(SEG)
    same_chunk = (i[:, None] // GLA_CHUNK) == (i[None, :] // GLA_CHUNK)
    tri = same_chunk & ((i[None, :] >= i[:, None]) if reverse else (i[None, :] <= i[:, None]))
    l_idx = i[:, None] % GLA_CHUNK
    m_idx = np.arange(GLA_CHUNK)[None, :]
    causal = (m_idx >= l_idx) if reverse else (m_idx <= l_idx)
    return (jnp.asarray(tri, BF16), jnp.asarray(same_chunk, BF16), jnp.asarray(same_chunk, F32),
            jnp.asarray(causal, F32))


def _gla_segment(q, k, v, pre, st_ref, o_ref, masks, reverse):
    seg = q.shape[0]
    n_chunks = seg // GLA_CHUNK
    la = (jnp.minimum(pre, 0.0) - jnp.log1p(jnp.exp(-jnp.abs(pre)))) * (1.0 / GLA_TAU)

    tri_m, ones_m, bd, causal = masks
    hi = la.astype(BF16)
    lo = (la - hi.astype(F32)).astype(BF16)
    b = _dot(tri_m, hi) + _dot(tri_m, lo)
    btot = _dot(ones_m, hi) + _dot(ones_m, lo)

    q_dec = q * (GLA_DK ** -0.5) * jnp.exp(b)
    k_inv = (k * jnp.exp(-b)).astype(BF16)
    k_end = (k * jnp.exp(btot - b)).astype(BF16)
    dec = jnp.exp(btot)
    vb = v.astype(BF16)
    keep = causal > 0.5

    st = st_ref[...]
    order = range(n_chunks - 1, -1, -1) if reverse else range(n_chunks)
    for ci in order:
        sl = slice(ci * GLA_CHUNK, (ci + 1) * GLA_CHUNK)
        qd = q_dec[sl]
        qbd = (jnp.concatenate([qd] * GLA_HEADS, axis=0) * bd).astype(BF16)
        a = _dot_nt(qbd, k_inv[sl])
        a = jnp.where(keep, a, 0.0)
        rr = _dot(a.astype(BF16), vb[sl])
        o = _dot_nt(qd.astype(BF16), st.astype(BF16))
        for h in range(GLA_HEADS):
            hs = slice(h * GLA_CHUNK, (h + 1) * GLA_CHUNK)
            o = o + rr[hs] * bd[hs]
        o_ref[sl, :] = o
        kvt = _dot_tn(vb[sl], k_end[sl])
        st = st * dec[ci * GLA_CHUNK:ci * GLA_CHUNK + 1, :] + kvt * bd
    st_ref[...] = st


def _gla_kernel(*refs, has_init, emit_final):
    qf, kf, vf, lrf, qb, kb, vb, lrb, wah_ref, wal_ref, ba_ref = refs[:11]
    mask_refs = refs[11:19]
    rest = refs[19:]
    if has_init:
        s0f, s0b = rest[:2]
        rest = rest[2:]
    of_ref, ob_ref = rest[:2]
    rest = rest[2:]
    if emit_final:
        sff, sfb = rest[:2]
        rest = rest[2:]
    stf, stb = rest

    s = pl.program_id(1)

    @pl.when(s == 0)
    def _():
        if has_init:
            stf[...] = s0f[...]
            stb[...] = s0b[...]
        else:
            stf[...] = jnp.zeros_like(stf)
            stb[...] = jnp.zeros_like(stb)

    def decay_pre(lr_ref):
        lr = lr_ref[...]
        return _dot(lr, wah_ref[...]) + _dot(lr, wal_ref[...]) + ba_ref[...]

    f32 = lambda ref: ref[...].astype(F32)

    masks_f = tuple(m[...] for m in mask_refs[:4])
    masks_b = tuple(m[...] for m in mask_refs[4:])
    _gla_segment(f32(qf), f32(kf), f32(vf), decay_pre(lrf)[:, :HEAD_W], stf, of_ref, masks_f, reverse=False)
    _gla_segment(f32(qb), f32(kb), f32(vb), decay_pre(lrb)[:, HEAD_W:], stb, ob_ref, masks_b, reverse=True)

    if emit_final:
        @pl.when(s == pl.num_programs(1) - 1)
        def _():
            sff[...] = stf[...]
            sfb[...] = stb[...]


def _gla(proj3, wa_hi, wa_lo, ba, st0, emit_final):
    bsz, n, _ = proj3.shape
    nseg = n // SEG

    def col(cb, width=HEAD_W, rev=False):
        if rev:
            return pl.BlockSpec((None, SEG, width), lambda b, s: (b, nseg - 1 - s, cb))
        return pl.BlockSpec((None, SEG, width), lambda b, s: (b, s, cb))

    in_specs = [col(COL_Q), col(COL_K), col(COL_V), col(COL_LR, LR_W),
                col(COL_Q, rev=True), col(COL_K, rev=True), col(COL_V, rev=True), col(COL_LR, LR_W, rev=True),
                pl.BlockSpec((LR_W, 2 * HEAD_W), lambda b, s: (0, 0)),
                pl.BlockSpec((LR_W, 2 * HEAD_W), lambda b, s: (0, 0)),
                pl.BlockSpec((1, 2 * HEAD_W), lambda b, s: (0, 0))]
    masks = _gla_masks(False) + _gla_masks(True)
    in_specs += [pl.BlockSpec(m.shape, lambda b, s: (0, 0)) for m in masks]
    args = [proj3] * 8 + [wa_hi, wa_lo, ba] + list(masks)
    st_spec = pl.BlockSpec((None, HEAD_W, HEAD_W), lambda b, s: (b, 0, 0))
    if st0 is not None:
        in_specs += [st_spec, st_spec]
        args += [st0[0], st0[1]]
    out_shape = [jax.ShapeDtypeStruct((bsz, n, HEAD_W), F32)] * 2
    out_specs = [pl.BlockSpec((None, SEG, HEAD_W), lambda b, s: (b, s, 0)),
                 pl.BlockSpec((None, SEG, HEAD_W), lambda b, s: (b, nseg - 1 - s, 0))]
    if emit_final:
        out_shape += [jax.ShapeDtypeStruct((bsz, HEAD_W, HEAD_W), F32)] * 2
        out_specs += [st_spec, st_spec]
    return pl.pallas_call(
        functools.partial(_gla_kernel, has_init=st0 is not None, emit_final=emit_final),
        grid=(bsz, nseg),
        in_specs=in_specs,
        out_specs=out_specs,
        out_shape=out_shape,
        scratch_shapes=[pltpu.VMEM((HEAD_W, HEAD_W), F32), pltpu.VMEM((HEAD_W, HEAD_W), F32)],
        compiler_params=_cparams("parallel", "arbitrary"),
        name="gla",
    )(*args)


def _fft_direct_kernel(zr_ref, zi_ref, cn_ref, sn_ref, o_ref):
    o_ref[...] = _dot(cn_ref[...], zr_ref[...]) + _dot(sn_ref[...], zi_ref[...])


def _fft_direct(zr3, zi3, cn, sn):
    bsz, n, w = zr3.shape
    blk = pl.BlockSpec((None, n, w), lambda b: (b, 0, 0))
    tab = pl.BlockSpec((n, n), lambda b: (0, 0))
    return pl.pallas_call(
        _fft_direct_kernel,
        grid=(bsz,),
        in_specs=[blk, blk, tab, tab],
        out_specs=blk,
        out_shape=jax.ShapeDtypeStruct((bsz, n, w), F32),
        compiler_params=_cparams("parallel"),
        name="fft_direct",
    )(zr3, zi3, cn, sn)


def _fft_a_kernel(zr_ref, zi_ref, c_ref, s_ref, gr_ref, gi_ref):
    zr = zr_ref[...]
    zi = zi_ref[...]
    cm = c_ref[...]
    sm = s_ref[...]
    gr_ref[...] = (_dot(cm, zr) + _dot(sm, zi)).astype(BF16)
    gi_ref[...] = (_dot(cm, zi) - _dot(sm, zr)).astype(BF16)


def _fft_c_kernel(gr_ref, gi_ref, mc_ref, ms_ref, o_ref):
    for j in range(gr_ref.shape[0]):
        o_ref[:, j, :] = _dot(mc_ref[j], gr_ref[j]) + _dot(ms_ref[j], gi_ref[j])


def _fft_two_stage(zr3, zi3, tabs):
    bsz, n, w = zr3.shape
    n1 = FFT_N1
    n2 = n // n1
    c1, s1, mc, ms = tabs
    tn = 2048
    wide = n2 * w
    blk = pl.BlockSpec((None, n1, tn), lambda b, j: (b, 0, j))
    tab = pl.BlockSpec((n1, n1), lambda b, j: (0, 0))
    gr, gi = pl.pallas_call(
        _fft_a_kernel,
        grid=(bsz, wide // tn),
        in_specs=[blk, blk, tab, tab],
        out_specs=[blk, blk],
        out_shape=[jax.ShapeDtypeStruct((bsz, n1, wide), BF16)] * 2,
        compiler_params=_cparams("parallel", "parallel"),
        name="fft_stage_a",
    )(zr3.reshape(bsz, n1, wide), zi3.reshape(bsz, n1, wide), c1, s1)
    kb = 8
    gblk = pl.BlockSpec((None, kb, n2, w), lambda b, j: (b, j, 0, 0))
    mblk = pl.BlockSpec((kb, n2, n2), lambda b, j: (j, 0, 0))
    out = pl.pallas_call(
        _fft_c_kernel,
        grid=(bsz, n1 // kb),
        in_specs=[gblk, gblk, mblk, mblk],
        out_specs=pl.BlockSpec((None, n2, kb, w), lambda b, j: (b, 0, j, 0)),
        out_shape=jax.ShapeDtypeStruct((bsz, n2, n1, w), F32),
        compiler_params=_cparams("parallel", "parallel"),
        name="fft_stage_c",
    )(gr.reshape(bsz, n1, n2, w), gi.reshape(bsz, n1, n2, w), mc, ms)
    return out.reshape(bsz, n, w)


def _mix_masks(tm):
    h = np.arange(HEAD_W) // GLA_DK
    head_mean = (h[:, None] == h[None, :]) / GLA_DK
    rg = np.arange(GMLP_GROUPS * GMLP_CHUNK) // GMLP_CHUNK
    cg = np.arange(HEAD_W) // (HEAD_W // GMLP_GROUPS)
    t = np.arange(tm)
    e = np.arange(N_EXPERTS)
    return (jnp.asarray(head_mean, BF16), jnp.asarray(rg[:, None] == cg[None, :], F32),
            jnp.asarray(t[:, None] < t[None, :], BF16), jnp.asarray(e[None, :] < e[:, None], BF16))


def _route(logits, bias, before, lower):
    tm = logits.shape[1]
    s = _sigmoid(logits)
    biased = s + bias
    neg = -jnp.inf
    rows = lax.broadcasted_iota(jnp.int32, (8, tm), 0)

    def first_argmax(x, ids, sentinel):
        m = jnp.max(x, axis=0, keepdims=True)
        return m, jnp.min(jnp.where(x == m, ids, sentinel), axis=0, keepdims=True)

    gs_rows = []
    for g in range(N_GROUPS):
        x = biased[8 * g:8 * g + 8]
        m1, i1 = first_argmax(x, rows, 8)
        m2 = jnp.max(jnp.where(rows == i1, neg, x), axis=0, keepdims=True)
        gs_rows.append(m1 + m2)
    gs = jnp.concatenate(gs_rows, axis=0)
    gsel = jnp.zeros((N_GROUPS, tm), F32)
    for _ in range(TOPK_GROUPS):
        _, i = first_argmax(gs, rows, 8)
        hit = rows == i
        gsel = jnp.where(hit, 1.0, gsel)
        gs = jnp.where(hit, neg, gs)

    xs = [jnp.where(gsel[g:g + 1] > 0.0, biased[8 * g:8 * g + 8], neg) for g in range(N_GROUPS)]
    ids = [rows + 8 * g for g in range(N_GROUPS)]
    sel = [jnp.zeros((8, tm), F32) for _ in range(N_GROUPS)]
    eids = []
    for _ in range(TOP_K):
        m = xs[0]
        for g in range(1, N_GROUPS):
            m = jnp.maximum(m, xs[g])
        m = jnp.max(m, axis=0, keepdims=True)
        cand = jnp.where(xs[0] == m, ids[0], N_EXPERTS)
        for g in range(1, N_GROUPS):
            cand = jnp.minimum(cand, jnp.where(xs[g] == m, ids[g], N_EXPERTS))
        i = jnp.min(cand, axis=0, keepdims=True)
        eids.append(i)
        for g in range(N_GROUPS):
            hit = ids[g] == i
            sel[g] = jnp.where(hit, 1.0, sel[g])
            xs[g] = jnp.where(hit, neg, xs[g])

    sel_all = jnp.concatenate(sel, axis=0)
    seen = _dot(sel_all.astype(BF16), before)
    counts = jnp.sum(sel_all, axis=1, keepdims=True)

    n_chunks = jnp.ceil(counts * (1.0 / CHUNK))
    run_start = _dot(lower, jnp.broadcast_to(n_chunks, (N_EXPERTS, LANES)).astype(BF16))[:, 0:1] * CHUNK
    local_pos = seen + run_start

    def pick(k, table):
        acc = None
        for g in range(N_GROUPS):
            v = jnp.where(ids[g] == eids[k], table[8 * g:8 * g + 8], 0.0)
            acc = v if acc is None else acc + v
        return jnp.sum(acc, axis=0, keepdims=True)

    w_raw = [pick(k, s) for k in range(TOP_K)]
    lpos = [pick(k, local_pos) for k in range(TOP_K)]
    tot = w_raw[0]
    for k in range(1, TOP_K):
        tot = tot + w_raw[k]
    weights = [w / tot * ROUTED_SCALE for w in w_raw]
    return weights, lpos, n_chunks, run_start


def _mix_out_kernel(x_ref, of_ref, ob_ref, g_ref, su_ref, sv_ref, cb_ref, cc_ref, cx_ref,
                    ccp_ref, cxp_ref, ccn_ref, cxn_ref, ft_ref, mod_ref,
                    glag_ref, sgng_ref, sgnb_ref, wsgu_ref, bsgu_ref, wconv_ref, wout_ref,
                    ln1g_ref, ln1b_ref, wrh_ref, wrl_ref, rb_ref, hmean_ref, sgubd_ref, before_ref, lower_ref,
                    x1_ref, xw_ref, wrow_ref, lpos_ref, tab_ref, cnt_ref, carry_ref, *, seq_len):
    tm = x_ref.shape[0]
    i = pl.program_id(0)

    @pl.when(i == 0)
    def _():
        carry_ref[...] = jnp.zeros_like(carry_ref)

    mod = mod_ref[...]
    g1 = mod[:, 2 * D_MODEL:3 * D_MODEL]
    sh2 = mod[:, 3 * D_MODEL:4 * D_MODEL]
    sc2 = mod[:, 4 * D_MODEL:5 * D_MODEL]

    o = of_ref[...] + ob_ref[...]
    head_mean = hmean_ref[...]
    o2 = o * o
    o2_hi = o2.astype(BF16)
    o2_lo = (o2 - o2_hi.astype(F32)).astype(BF16)
    ms = _dot(o2_hi, head_mean) + _dot(o2_lo, head_mean)
    y_gla = o * lax.rsqrt(ms + RMS_EPS) * glag_ref[...] * _silu(g_ref[...].astype(F32))

    vn = _ln(sv_ref[...].astype(F32)) * sgng_ref[...] + sgnb_ref[...]
    sgu_bd = sgubd_ref[...]
    sp_parts = []
    for j in range(tm // GMLP_CHUNK):
        vc = vn[j * GMLP_CHUNK:(j + 1) * GMLP_CHUNK]
        vbd = (jnp.concatenate([vc] * GMLP_GROUPS, axis=0) * sgu_bd).astype(BF16)
        sp_parts.append(_dot(wsgu_ref[...], vbd) + bsgu_ref[...])
    y_sgu = su_ref[...].astype(F32) * jnp.concatenate(sp_parts, axis=0)

    z = cc_ref[...].astype(F32) * cx_ref[...].astype(F32)
    z_before = (ccp_ref[...].astype(F32) * cxp_ref[...].astype(F32))[HALO - 1:HALO, :]
    z_after = (ccn_ref[...].astype(F32) * cxn_ref[...].astype(F32))[0:1, :]
    row = lax.broadcasted_iota(jnp.int32, (tm, HEAD_W), 0)
    pos = (i * tm + row) & (seq_len - 1)
    z_prev = jnp.where(row == 0, z_before, pltpu.roll(z, 1, 0))
    z_next = jnp.where(row == tm - 1, z_after, pltpu.roll(z, tm - 1, 0))
    z_prev = jnp.where(pos == 0, 0.0, z_prev)
    z_next = jnp.where(pos == seq_len - 1, 0.0, z_next)
    wconv = wconv_ref[...]
    y_conv = cb_ref[...].astype(F32) * (wconv[0:1] * z_prev + wconv[1:2] * z + wconv[2:3] * z_next)

    y = (_dot(y_gla.astype(BF16), wout_ref[0:HEAD_W, :])
         + _dot(y_sgu.astype(BF16), wout_ref[HEAD_W:2 * HEAD_W, :])
         + _dot(y_conv.astype(BF16), wout_ref[2 * HEAD_W:3 * HEAD_W, :])
         + _dot(ft_ref[...].astype(BF16), wout_ref[3 * HEAD_W:4 * HEAD_W, :]))
    x1 = _ln(DEEPNORM_ALPHA * x_ref[...] + g1 * y) * ln1g_ref[...] + ln1b_ref[...]
    x1_ref[...] = x1
    h2 = _ln(x1) * (1.0 + sc2) + sh2
    xw_ref[...] = _pack_bf16_pairs(h2)

    h2_hi = h2.astype(BF16)
    h2_lo = (h2 - h2_hi.astype(F32)).astype(BF16)
    logits = _dot(h2_hi, wrh_ref[...]) + _dot(h2_hi, wrl_ref[...]) + _dot(h2_lo, wrh_ref[...])
    weights, lpos, n_chunks, run_start = _route(logits.T[:N_EXPERTS], rb_ref[...], before_ref[...], lower_ref[...])
    lpos_ref[...] = jnp.concatenate(lpos, axis=0).astype(jnp.int32)
    wrow_ref[...] = jnp.concatenate(weights, axis=0)
    carry = carry_ref[:, 0:1]
    lane = lax.broadcasted_iota(jnp.int32, (N_EXPERTS, LANES), 1)
    total = jnp.sum(n_chunks, axis=0, keepdims=True)
    cols = jnp.where(lane == 0, n_chunks, jnp.where(lane == 1, run_start,
                                                   jnp.where(lane == 2, carry, jnp.where(lane == 3, total, 0.0))))
    tab = jnp.concatenate([cols, jnp.zeros((LANES - N_EXPERTS, LANES), F32)], axis=0).T
    tab_ref[...] = tab[0:8].astype(jnp.int32)
    new_carry = carry_ref[...] + n_chunks
    carry_ref[...] = new_carry
    cnt_ref[...] = new_carry


def _mix_out(x2d, of2d, ob2d, proj, yft2d, mod3, mod_row, lw, seq_len, tm):
    t = x2d.shape[0]
    tm = min(tm, t)
    nt8 = t // HALO
    rows8 = tm // HALO

    def col(cb):
        return pl.BlockSpec((tm, HEAD_W), lambda i: (i, cb))

    def halo_prev(cb):
        return pl.BlockSpec((HALO, HEAD_W), lambda i: (jnp.maximum(i * rows8 - 1, 0), cb))

    def halo_next(cb):
        return pl.BlockSpec((HALO, HEAD_W), lambda i: (jnp.minimum((i + 1) * rows8, nt8 - 1), cb))

    def full(a):
        return pl.BlockSpec(a.shape, lambda i: (0,) * a.ndim)

    tok_d = pl.BlockSpec((tm, D_MODEL), lambda i: (i, 0))
    tok_h = pl.BlockSpec((tm, HEAD_W), lambda i: (i, 0))
    weights = [lw["gla_norm_g"], lw["sgu_norm_g"], lw["sgu_norm_b"], lw["w_sgu_cat"], lw["b_sgu_full"],
               lw["w_conv"], lw["w_out"], lw["ln1_g"], lw["ln1_b"], lw["w_router_hi"], lw["w_router_lo"], lw["router_bias"]]
    weights += list(_mix_masks(tm))
    in_specs = ([tok_d, tok_h, tok_h, col(COL_G), col(COL_SGU), col(COL_SGV), col(COL_CVB), col(COL_CVC),
                 col(COL_CVX), halo_prev(COL_CVC), halo_prev(COL_CVX), halo_next(COL_CVC), halo_next(COL_CVX),
                 tok_h, pl.BlockSpec((None, 1, 6 * D_MODEL), lambda i: (mod_row(i, tm), 0, 0))]
                + [full(w) for w in weights])
    args = [x2d, of2d, ob2d] + [proj] * 10 + [yft2d, mod3] + weights
    return pl.pallas_call(
        functools.partial(_mix_out_kernel, seq_len=seq_len),
        grid=(t // tm,),
        in_specs=in_specs,
        out_specs=[tok_d,
                   pl.BlockSpec((tm, PACK_W), lambda i: (i, 0)),
                   pl.BlockSpec((None, TOP_K, tm), lambda i: (i, 0, 0)),
                   pl.BlockSpec((None, TOP_K, tm), lambda i: (i, 0, 0)),
                   pl.BlockSpec((None, 8, LANES), lambda i: (i, 0, 0)),
                   pl.BlockSpec((N_EXPERTS, LANES), lambda i: (0, 0))],
        out_shape=[jax.ShapeDtypeStruct((t, D_MODEL), F32),
                   jax.ShapeDtypeStruct((t, PACK_W), jnp.uint32),
                   jax.ShapeDtypeStruct((t // tm, TOP_K, tm), F32),
                   jax.ShapeDtypeStruct((t // tm, TOP_K, tm), jnp.int32),
                   jax.ShapeDtypeStruct((t // tm, 8, LANES), jnp.int32),
                   jax.ShapeDtypeStruct((N_EXPERTS, LANES), F32)],
        scratch_shapes=[pltpu.VMEM((N_EXPERTS, LANES), F32)],
        compiler_params=_cparams("arbitrary"),
        name="mix_out",
    )(*args)


def _local_rows(tm):
    need = tm * TOP_K + N_EXPERTS * (CHUNK - 1)
    return -(-need // LBLK) * LBLK


def _run_copy(local_ref, sorted_hbm, sem, local_row, sorted_row, to_sorted, rows=CHUNK):
    loc = local_ref.at[pl.ds(pl.multiple_of(local_row, CHUNK), rows)]
    srt = sorted_hbm.at[pl.ds(pl.multiple_of(sorted_row, CHUNK), rows)]
    return pltpu.make_async_copy(loc, srt, sem) if to_sorted else pltpu.make_async_copy(srt, loc, sem)


def _start_run_copies(tab_ref, gstart_ref, local_ref, sorted_hbm, sem, to_sorted):
    def per_expert(e, totals):
        n = tab_ref[0, e]
        l0 = tab_ref[1, e]
        g0 = (gstart_ref[e] + tab_ref[2, e]) * CHUNK
        n_big = n >> BIG_SHIFT
        n_small = n & ((1 << BIG_SHIFT) - 1)

        def big(j, c):
            _run_copy(local_ref, sorted_hbm, sem, l0 + j * BIG_ROWS, g0 + j * BIG_ROWS, to_sorted, BIG_ROWS).start()
            return c

        lax.fori_loop(0, n_big, big, 0)
        l1 = l0 + n_big * BIG_ROWS
        g1 = g0 + n_big * BIG_ROWS

        def small(j, c):
            _run_copy(local_ref, sorted_hbm, sem, l1 + j * CHUNK, g1 + j * CHUNK, to_sorted).start()
            return c

        lax.fori_loop(0, n_small, small, 0)
        return totals[0] + n_big, totals[1] + n_small

    return lax.fori_loop(0, N_EXPERTS, per_expert, (jnp.int32(0), jnp.int32(0)))


def _get_pending(pending_ref, s):
    return pending_ref[s, 0], pending_ref[s, 1]


def _set_pending(pending_ref, s, counts):
    pending_ref[s, 0] = counts[0]
    pending_ref[s, 1] = counts[1]


def _wait_run_copies(counts, local_ref, sorted_hbm, sem, to_sorted):
    batch = 8

    def wait_n(rows, reps):
        def body(j, c):
            for _ in range(reps):
                _run_copy(local_ref, sorted_hbm, sem, 0, 0, to_sorted, rows).wait()
            return c
        return body

    for count, rows in zip(counts, (BIG_ROWS, CHUNK)):
        lax.fori_loop(0, count >> 3, wait_n(rows, batch), 0)
        lax.fori_loop(0, count & (batch - 1), wait_n(rows, 1), 0)


def _moe_scatter_kernel(gstart_ref, gtail_ref, xw_ref, lpos_ref, tab_ref, sorted_hbm,
                        local_ref, sem, zsem, pending_ref, *, eblk):
    i = pl.program_id(0)
    n = pl.num_programs(0)
    slot = i & 1
    tm = xw_ref.shape[0]
    lrows = local_ref.shape[1]

    def zero_copy(e):
        off = pl.multiple_of(jnp.maximum(gtail_ref[e], 0), eblk)
        return pltpu.make_async_copy(local_ref.at[1, pl.ds(0, eblk)], sorted_hbm.at[pl.ds(off, eblk)], zsem)

    @pl.when(i == 0)
    def _():
        _set_pending(pending_ref, 0, (0, 0))
        _set_pending(pending_ref, 1, (0, 0))
        local_ref[1, 0:eblk, :] = jnp.zeros((eblk, PACK_W), jnp.uint32)

        def z_start(e, c):
            @pl.when(gtail_ref[e] >= 0)
            def _():
                zero_copy(e).start()
            return c

        def z_wait(e, c):
            @pl.when(gtail_ref[e] >= 0)
            def _():
                zero_copy(e).wait()
            return c

        lax.fori_loop(0, N_EXPERTS, z_start, 0)
        lax.fori_loop(0, N_EXPERTS, z_wait, 0)

    local = local_ref.at[slot]
    _wait_run_copies(_get_pending(pending_ref, slot), local, sorted_hbm, sem.at[slot], True)

    x = _unpack_bf16_pairs(xw_ref[...]).astype(BF16)
    lpos = lpos_ref[...].astype(jnp.int16)
    one = jnp.ones((LBLK, tm), BF16)
    used_rows = tab_ref[3, 0] * CHUNK

    def sort_block(b):
        riota = lax.broadcasted_iota(jnp.int16, (LBLK, tm), 0) + b * LBLK
        p = jnp.zeros((LBLK, tm), BF16)
        for k in range(TOP_K):
            p = jnp.where(riota == lpos[k:k + 1, :], one, p)
        local[b * LBLK:(b + 1) * LBLK, :] = _pack_exact_bf16_pairs(_dot(p, x))

    n_blocks = lrows // LBLK
    for b in range(n_blocks - 1):
        sort_block(b)
    pl.when(used_rows > (n_blocks - 1) * LBLK)(functools.partial(sort_block, n_blocks - 1))

    _set_pending(pending_ref, slot, _start_run_copies(tab_ref, gstart_ref, local, sorted_hbm, sem.at[slot], True))

    @pl.when(i == n - 1)
    def _():
        for s in range(2):
            _wait_run_copies(_get_pending(pending_ref, s), local_ref.at[s], sorted_hbm, sem.at[s], True)


def _moe_scatter(gstart, gtail, xw, lpos, tab, n_rows, tm, eblk):
    n_tiles = xw.shape[0] // tm
    lrows = _local_rows(tm)
    any_spec = pl.BlockSpec(memory_space=pl.ANY)
    return pl.pallas_call(
        functools.partial(_moe_scatter_kernel, eblk=eblk),
        grid_spec=pltpu.PrefetchScalarGridSpec(
            num_scalar_prefetch=2,
            grid=(n_tiles,),
            in_specs=[pl.BlockSpec((tm, PACK_W), lambda i, *_: (i, 0)),
                      pl.BlockSpec((None, TOP_K, tm), lambda i, *_: (i, 0, 0)),
                      pl.BlockSpec((None, 8, LANES), lambda i, *_: (i, 0, 0), memory_space=pltpu.SMEM)],
            out_specs=any_spec,
            scratch_shapes=[pltpu.VMEM((2, lrows, PACK_W), jnp.uint32), pltpu.SemaphoreType.DMA((2,)),
                            pltpu.SemaphoreType.DMA, pltpu.SMEM((2, 2), jnp.int32)]),
        out_shape=jax.ShapeDtypeStruct((n_rows, PACK_W), jnp.uint32),
        compiler_params=_cparams("arbitrary"),
        name="moe_scatter",
    )(gstart, gtail, xw, lpos, tab)


def _expert2_kernel(be_ref, nu_ref, xs_ref, wg_ref, wu_ref, wd_ref, ys_ref, wgu_b, wd_b):
    j = pl.program_id(0)

    @pl.when(jnp.logical_or(j == 0, be_ref[j] != be_ref[jnp.maximum(j - 1, 0)]))
    def _():
        wgu_b[:, :EXPERT_FF] = wg_ref[...].astype(BF16)
        wgu_b[:, EXPERT_FF:] = wu_ref[...].astype(BF16)
        wd_b[...] = wd_ref[...].astype(BF16)

    @pl.when(j < nu_ref[0])
    def _():
        x = _unpack_bf16_pairs(xs_ref[...]).astype(BF16)
        gu = _dot(x, wgu_b[...])
        a = _silu(gu[:, :EXPERT_FF]) * gu[:, EXPERT_FF:]
        ys_ref[...] = _pack_bf16_pairs(_dot(a.astype(BF16), wd_b[...]))


def _experts2(xs, block_expert, n_used, lw, layer, eblk):
    n_rows = xs.shape[0]
    ff = EXPERT_FF

    def blk(j, be, nu):
        return (jnp.minimum(j, jnp.maximum(nu[0] - 1, 0)), 0)

    def wblk(j, be, nu):
        return (layer, be[j], 0, 0)

    return pl.pallas_call(
        _expert2_kernel,
        grid_spec=pltpu.PrefetchScalarGridSpec(
            num_scalar_prefetch=2,
            grid=(n_rows // eblk,),
            in_specs=[pl.BlockSpec((eblk, PACK_W), blk),
                      pl.BlockSpec((None, None, D_MODEL, ff), wblk),
                      pl.BlockSpec((None, None, D_MODEL, ff), wblk),
                      pl.BlockSpec((None, None, ff, D_MODEL), wblk)],
            out_specs=pl.BlockSpec((eblk, PACK_W), blk),
            scratch_shapes=[pltpu.VMEM((D_MODEL, 2 * ff), BF16), pltpu.VMEM((ff, D_MODEL), BF16)]),
        out_shape=jax.ShapeDtypeStruct((n_rows, PACK_W), jnp.uint32),
        compiler_params=_cparams("arbitrary"),
        name="moe_experts",
    )(block_expert, n_used, xs, lw["w_exp_gate"], lw["w_exp_up"], lw["w_exp_down"])


def _moe_combine_kernel(gstart_ref, x1_ref, xw_ref, lpos_ref, wrow_ref, tab_ref, tab_next_ref, mod_ref,
                        sg_ref, su_ref, sd_ref, ln2g_ref, ln2b_ref, sorted_hbm, o_ref,
                        local_ref, sem, pending_ref):
    i = pl.program_id(0)
    n = pl.num_programs(0)
    slot = i & 1
    tm = x1_ref.shape[0]
    lrows = local_ref.shape[1]

    @pl.when(i == 0)
    def _():
        local_ref[...] = jnp.zeros_like(local_ref)
        _set_pending(pending_ref, 0,
                     _start_run_copies(tab_ref, gstart_ref, local_ref.at[0], sorted_hbm, sem.at[0], False))

    @pl.when(i + 1 < n)
    def _():
        nxt = 1 - slot
        _set_pending(pending_ref, nxt, _start_run_copies(tab_next_ref, gstart_ref, local_ref.at[nxt], sorted_hbm,
                                                         sem.at[nxt], False))

    h = _unpack_bf16_pairs(xw_ref[...]).astype(BF16)
    a = _silu(_dot(h, sg_ref[...])) * _dot(h, su_ref[...])
    acc = _dot(a.astype(BF16), sd_ref[...])

    local = local_ref.at[slot]
    _wait_run_copies(_get_pending(pending_ref, slot), local, sorted_hbm, sem.at[slot], False)

    lpos = lpos_ref[...].astype(jnp.int16)
    wrow = wrow_ref[...].astype(BF16)

    def unsort_block(b, acc):
        riota = lax.broadcasted_iota(jnp.int16, (LBLK, tm), 0) + b * LBLK
        q = jnp.zeros((LBLK, tm), BF16)
        for k in range(TOP_K):
            q = jnp.where(riota == lpos[k:k + 1, :], jnp.broadcast_to(wrow[k:k + 1, :], (LBLK, tm)), q)
        y = _unpack_bf16_pairs(local[b * LBLK:(b + 1) * LBLK, :]).astype(BF16)
        return acc + _dot_tn(q, y)

    for b in range(lrows // LBLK):
        acc = unsort_block(b, acc)

    g2 = mod_ref[...][:, 5 * D_MODEL:6 * D_MODEL]
    u = DEEPNORM_ALPHA * x1_ref[...] + g2 * acc
    o_ref[...] = _ln(u) * ln2g_ref[...] + ln2b_ref[...]


def _moe_combine(gstart, x1, xw, lpos, wrow, tab, ys, mod3, mod_row, lw, tm):
    t = x1.shape[0]
    n_tiles = t // tm
    lrows = _local_rows(tm)

    def full(a):
        return pl.BlockSpec(a.shape, lambda i, *_: (0,) * a.ndim)

    tab_blk = lambda f: pl.BlockSpec((None, 8, LANES), f, memory_space=pltpu.SMEM)
    return pl.pallas_call(
        _moe_combine_kernel,
        grid_spec=pltpu.PrefetchScalarGridSpec(
            num_scalar_prefetch=1,
            grid=(n_tiles,),
            in_specs=[pl.BlockSpec((tm, D_MODEL), lambda i, *_: (i, 0)),
                      pl.BlockSpec((tm, PACK_W), lambda i, *_: (i, 0)),
                      pl.BlockSpec((None, TOP_K, tm), lambda i, *_: (i, 0, 0)),
                      pl.BlockSpec((None, TOP_K, tm), lambda i, *_: (i, 0, 0)),
                      tab_blk(lambda i, *_: (i, 0, 0)),
                      tab_blk(lambda i, *_: (jnp.minimum(i + 1, n_tiles - 1), 0, 0)),
                      pl.BlockSpec((None, 1, 6 * D_MODEL), lambda i, *_: (mod_row(i, tm), 0, 0)),
                      full(lw["w_sh_gate"]), full(lw["w_sh_up"]), full(lw["w_sh_down"]),
                      full(lw["ln2_g"]), full(lw["ln2_b"]),
                      pl.BlockSpec(memory_space=pl.ANY)],
            out_specs=pl.BlockSpec((tm, D_MODEL), lambda i, *_: (i, 0)),
            scratch_shapes=[pltpu.VMEM((2, lrows, PACK_W), jnp.uint32), pltpu.SemaphoreType.DMA((2,)),
                            pltpu.SMEM((2, 2), jnp.int32)]),
        out_shape=jax.ShapeDtypeStruct((t, D_MODEL), F32),
        compiler_params=_cparams("arbitrary"),
        name="moe_combine",
    )(gstart, x1, xw, lpos, wrow, tab, tab, mod3, lw["w_sh_gate"], lw["w_sh_up"], lw["w_sh_down"],
      lw["ln2_g"], lw["ln2_b"], ys)


def _moe2(x1, xw, wrow, lpos, tab, chunks_f, mod3, mod_row, lw, layer, tm):
    t = x1.shape[0]
    n_tiles = t // tm
    eblk = max(LBLK, min(EBLK_MAX, (t * TOP_K // N_EXPERTS) // 2))
    region_rows = chunks_f[:, 0].astype(jnp.int32) * CHUNK
    padded = ((region_rows + eblk - 1) // eblk) * eblk
    end = jnp.cumsum(padded)
    start = end - padded
    n_blocks = -(-(t * TOP_K + n_tiles * N_EXPERTS * (CHUNK - 1)) // eblk) + N_EXPERTS
    n_used = end[-1] // eblk
    blk_row = jnp.minimum(jnp.arange(n_blocks, dtype=jnp.int32), jnp.maximum(n_used - 1, 0)) * eblk
    block_expert = jnp.minimum(jnp.sum((end[None, :] <= blk_row[:, None]).astype(jnp.int32), axis=1),
                               N_EXPERTS - 1).astype(jnp.int32)
    gstart = start // CHUNK
    gtail = jnp.where(padded > 0, end - eblk, -1)
    xs = _moe_scatter(gstart, gtail, xw, lpos, tab, n_blocks * eblk, tm, eblk)
    ys = _experts2(xs, block_expert, n_used.reshape(1), lw, layer, eblk)
    return _moe_combine(gstart, x1, xw, lpos, wrow, tab, ys, mod3, mod_row, lw, tm)


def _channel_dft_table():
    k = np.arange(FNET_CH, dtype=np.float64)
    ang = 2.0 * np.pi * np.outer(k, k) / FNET_CH
    eye = np.eye(FNET_GROUPS)
    return np.concatenate([np.kron(eye, np.cos(ang)), -np.kron(eye, np.sin(ang))], axis=1)


def _direct_dft_tables(n):
    k = np.arange(n, dtype=np.float64)
    ang = 2.0 * np.pi * (np.outer(k, k) % n) / n
    scale = 1.0 / math.sqrt(n * FNET_CH)
    return np.cos(ang) * scale, np.sin(ang) * scale


def _two_stage_dft_tables(n):
    n1 = FFT_N1
    n2 = n // n1
    a = np.arange(n1, dtype=np.float64)
    ang1 = 2.0 * np.pi * (np.outer(a, a) % n1) / n1
    k1 = np.arange(n1).reshape(n1, 1, 1)
    k2 = np.arange(n2).reshape(1, n2, 1)
    m2 = np.arange(n2).reshape(1, 1, n2)
    ang2 = 2.0 * np.pi * ((m2 * (k1 + n1 * k2)) % n) / n
    scale = 1.0 / math.sqrt(n * FNET_CH)
    return np.cos(ang1), np.sin(ang1), np.cos(ang2) * scale, np.sin(ang2) * scale


def _grid_sincos_table(rows, d):
    quarter = d // 4
    omega = 1.0 / (POS_BASE ** (np.arange(quarter, dtype=np.float64) / quarter))
    r = np.arange(rows, dtype=np.float64)[:, None] * omega
    c = np.arange(GRID_W, dtype=np.float64)[:, None] * omega
    return (np.concatenate([np.sin(r), np.cos(r)], axis=-1).astype(np.float32),
            np.concatenate([np.sin(c), np.cos(c)], axis=-1).astype(np.float32))


def _layer_weights(l, w_in, w_gla_a, b_gla_a, gla_norm_g, sgu_norm_g, sgu_norm_b, w_sgu, b_sgu, w_conv,
                   w_out, ln1_g, ln1_b, ln2_g, ln2_b, w_router, router_bias,
                   w_exp_gate, w_exp_up, w_exp_down, w_sh_gate, w_sh_up, w_sh_down):
    wi = w_in[l]
    lr0 = 4 * HEAD_W
    w_in_p = jnp.concatenate(
        [wi[:, :lr0], wi[:, lr0 + 2 * GLA_LR:lr0 + 2 * GLA_LR + 5 * HEAD_W], wi[:, lr0:lr0 + 2 * GLA_LR],
         jnp.zeros((D_MODEL, LR_W - 2 * GLA_LR), F32), wi[:, lr0 + 2 * GLA_LR + 5 * HEAD_W:]], axis=1).astype(BF16)
    wa_pad = jnp.zeros((LR_W, 2 * HEAD_W), F32)
    wa_pad = wa_pad.at[:GLA_LR, :HEAD_W].set(w_gla_a[l, 0])
    wa_pad = wa_pad.at[GLA_LR:2 * GLA_LR, HEAD_W:].set(w_gla_a[l, 1])
    w_router_pad = jnp.concatenate([w_router[l], jnp.zeros((D_MODEL, LANES - N_EXPERTS), F32)], axis=1)
    w_router_hi = w_router_pad.astype(BF16)
    row = lambda a: a[l].reshape(1, -1)
    return {
        "w_in_p": w_in_p,
        "wa_hi": wa_pad.astype(BF16), "wa_lo": (wa_pad - wa_pad.astype(BF16).astype(F32)).astype(BF16),
        "ba": jnp.concatenate([b_gla_a[l, 0], b_gla_a[l, 1]]).reshape(1, 2 * HEAD_W),
        "gla_norm_g": row(gla_norm_g), "sgu_norm_g": row(sgu_norm_g), "sgu_norm_b": row(sgu_norm_b),
        "w_sgu_cat": jnp.concatenate([w_sgu[l, g] for g in range(GMLP_GROUPS)], axis=1).astype(BF16),
        "b_sgu_full": jnp.repeat(b_sgu[l].T, HEAD_W // GMLP_GROUPS, axis=1),
        "w_conv": w_conv[l],
        "w_out": w_out[l].astype(BF16),
        "ln1_g": row(ln1_g), "ln1_b": row(ln1_b), "ln2_g": row(ln2_g), "ln2_b": row(ln2_b),
        "w_router_hi": w_router_hi, "w_router_lo": (w_router_pad - w_router_hi.astype(F32)).astype(BF16),
        "router_bias": router_bias[l].reshape(N_EXPERTS, 1),
        "w_exp_gate": w_exp_gate, "w_exp_up": w_exp_up, "w_exp_down": w_exp_down,
        "w_sh_gate": w_sh_gate[l].astype(BF16), "w_sh_up": w_sh_up[l].astype(BF16),
        "w_sh_down": w_sh_down[l].astype(BF16),
    }


def _state_to_blockdiag_t(s):
    bsz = s.shape[0]
    st = jnp.swapaxes(s, 2, 3)
    eye = jnp.eye(GLA_HEADS, dtype=s.dtype)
    return jnp.einsum("bhvd,hg->bhvgd", st, eye).reshape(bsz, HEAD_W, HEAD_W)


def _blockdiag_t_to_state(st):
    bsz = st.shape[0]
    s5 = st.reshape(bsz, GLA_HEADS, GLA_DK, GLA_HEADS, GLA_DK)
    diag = jnp.stack([s5[:, h, :, h, :] for h in range(GLA_HEADS)], axis=1)
    return jnp.swapaxes(diag, 2, 3)


def _trunk_layer(x3, pos, mod3, mod_row, lw, layer, st0, emit_final, tabs):
    bsz, n, _ = x3.shape
    t = bsz * n
    outs = _in_proj(x3.reshape(t, D_MODEL), pos, mod3, mod_row, lw["w_in_p"], tabs["cs"], tm=512)
    if pos is not None:
        proj, zr, zi, x2d = outs
    else:
        proj, zr, zi = outs
        x2d = x3.reshape(t, D_MODEL)
    gla_out = _gla(proj.reshape(bsz, n, STORE_W), lw["wa_hi"], lw["wa_lo"], lw["ba"], st0, emit_final)
    o_f, o_b = gla_out[:2]
    zr3 = zr.reshape(bsz, n, HEAD_W)
    zi3 = zi.reshape(bsz, n, HEAD_W)
    if "two_stage" in tabs:
        yft = _fft_two_stage(zr3, zi3, tabs["two_stage"])
    else:
        yft = _fft_direct(zr3, zi3, *tabs["direct"])
    x1, xw, wrow, lpos, tab, chunks = _mix_out(x2d, o_f.reshape(t, HEAD_W), o_b.reshape(t, HEAD_W), proj,
                                             yft.reshape(t, HEAD_W), mod3, mod_row, lw, seq_len=n, tm=256)
    x2 = _moe2(x1, xw, wrow, lpos, tab, chunks, mod3, mod_row, lw, layer, tm=min(256, t))
    return x2.reshape(bsz, n, D_MODEL), gla_out[2:]


def kernel(x_prompt, x_sample, c, state_gla, c_ctx, w_ada, b_ada, w_in, w_gla_a, b_gla_a, gla_norm_g, sgu_norm_g, sgu_norm_b, w_sgu, b_sgu, w_conv, w_out, ln1_g, ln1_b, ln2_g, ln2_b, w_router, router_bias, w_exp_gate, w_exp_up, w_exp_down, w_sh_gate, w_sh_up, w_sh_down):
    n_layers = w_ada.shape[0]
    bp, np_, _ = x_prompt.shape
    bs, ns, _ = x_sample.shape
    assert bs <= 7

    cond8 = jnp.concatenate([c_ctx[None, :], c, jnp.zeros((7 - bs, D_MODEL), F32)], axis=0)
    mod = _ada_mod(cond8, w_ada, b_ada)

    tabs_p = {"cs": jnp.asarray(_channel_dft_table(), BF16),
              "direct": tuple(jnp.asarray(a, BF16) for a in _direct_dft_tables(np_))}
    tabs_s = {"cs": tabs_p["cs"],
              "two_stage": tuple(jnp.asarray(a, BF16) for a in _two_stage_dft_tables(ns))}
    rtab, ctab = _grid_sincos_table(ns // GRID_W, D_MODEL)
    pos = jnp.concatenate([jnp.repeat(jnp.asarray(rtab), GRID_W, axis=0),
                           jnp.tile(jnp.asarray(ctab), (ns // GRID_W, 1))], axis=-1)

    prompt_row = lambda i, tm: 0
    sample_row = lambda i, tm: 1 + (i * tm) // ns

    y_p = x_prompt
    y_s = x_sample
    finals = []
    for l in range(n_layers):
        lw = _layer_weights(l, w_in, w_gla_a, b_gla_a, gla_norm_g, sgu_norm_g, sgu_norm_b, w_sgu, b_sgu,
                            w_conv, w_out, ln1_g, ln1_b, ln2_g, ln2_b, w_router, router_bias,
                            w_exp_gate, w_exp_up, w_exp_down, w_sh_gate, w_sh_up, w_sh_down)
        mod3 = mod[l].reshape(8, 1, 6 * D_MODEL)
        y_p, fin = _trunk_layer(y_p, None, mod3, prompt_row, lw, l, None, True, tabs_p)
        finals.append(jnp.stack([_blockdiag_t_to_state(fin[0]), _blockdiag_t_to_state(fin[1])], axis=1))
        st0 = jnp.stack([_state_to_blockdiag_t(state_gla[:, l, 0]), _state_to_blockdiag_t(state_gla[:, l, 1])])
        y_s, _ = _trunk_layer(y_s, pos if l == 0 else None, mod3, sample_row, lw, l, st0, False, tabs_s)
    new_state = jnp.stack(finals, axis=1).astype(x_prompt.dtype)
    return (y_p, y_s, new_state)
```

```python
import functools
import math

import numpy as np
import jax
import jax.numpy as jnp
from jax import lax
from jax.experimental import pallas as pl
from jax.experimental.pallas import tpu as pltpu

F32 = jnp.float32
BF16 = jnp.bfloat16

D_MODEL = 1024
DEPTH = 2
GRID_W = 64
HEAD_W = 256
GLA_HEADS = 4
GLA_DK = 64
GLA_LR = 16
GLA_TAU = 16.0
GLA_CHUNK = 64
GMLP_GROUPS = 4
GMLP_CHUNK = 128
FNET_GROUPS = 4
FNET_CH = 64
N_EXPERTS = 64
TOP_K = 8
N_GROUPS = 8
TOPK_GROUPS = 4
EXPERT_FF = 256
ROUTED_SCALE = 2.5
DEEPNORM_ALPHA = (2 * DEPTH) ** 0.25
LN_EPS = 1e-5
RMS_EPS = 1e-6
POS_BASE = 10000.0

COL_Q, COL_K, COL_V, COL_G, COL_SGU, COL_SGV, COL_CVB, COL_CVC, COL_CVX = range(9)
LR_W = 128
STORE_W = 9 * HEAD_W + LR_W
PROJ_W = STORE_W + HEAD_W
COL_LR = (9 * HEAD_W) // LR_W
HALO = 16

LANES = 128
PACK_W = D_MODEL // 2
EBLK_MAX = 512
LBLK = 256
CHUNK = 8
BIG_SHIFT = 2
BIG_ROWS = CHUNK << BIG_SHIFT
SEG = 256
FFT_N1 = 64
VMEM_LIMIT = 56 * 1024 * 1024


def _cparams(*sem):
    return pltpu.CompilerParams(dimension_semantics=sem, vmem_limit_bytes=VMEM_LIMIT)


def _ln(x):
    mu = jnp.mean(x, axis=-1, keepdims=True)
    xc = x - mu
    var = jnp.mean(xc * xc, axis=-1, keepdims=True)
    return xc * lax.rsqrt(var + LN_EPS)


def _sigmoid(x):
    return 1.0 / (1.0 + jnp.exp(-x))


def _silu(x):
    return x * _sigmoid(x)


def _pack_bf16_pairs(x):
    w = x.shape[1] // 2
    lo = lax.bitcast_convert_type(x[:, :w].astype(BF16).astype(F32), jnp.uint32)
    hi = lax.bitcast_convert_type(x[:, w:].astype(BF16).astype(F32), jnp.uint32)
    return (lo >> 16) | (hi & jnp.uint32(0xFFFF0000))


def _pack_exact_bf16_pairs(x):
    w = x.shape[1] // 2
    lo = lax.bitcast_convert_type(x[:, :w], jnp.uint32)
    hi = lax.bitcast_convert_type(x[:, w:], jnp.uint32)
    return (lo >> 16) | hi


def _unpack_bf16_pairs(u):
    lo = lax.bitcast_convert_type(u << 16, F32)
    hi = lax.bitcast_convert_type(u & jnp.uint32(0xFFFF0000), F32)
    return jnp.concatenate([lo, hi], axis=1)


def _dot(a, b):
    return jnp.dot(a, b, preferred_element_type=F32)


def _dot_nt(a, b):
    return lax.dot_general(a, b, (((1,), (1,)), ((), ())), preferred_element_type=F32)


def _dot_tn(a, b):
    return lax.dot_general(a, b, (((0,), (0,)), ((), ())), preferred_element_type=F32)


def _ada_kernel(c_ref, w_ref, b_ref, o_ref):
    c = c_ref[...]
    o_ref[...] = _dot(_silu(c).astype(BF16), w_ref[...].astype(BF16)) + b_ref[...]


def _ada_mod(cond8, w_ada, b_ada):
    n_l, d, w6 = w_ada.shape
    tn = 1536
    return pl.pallas_call(
        _ada_kernel,
        grid=(n_l, w6 // tn),
        in_specs=[pl.BlockSpec((8, d), lambda l, j: (0, 0)),
                  pl.BlockSpec((None, d, tn), lambda l, j: (l, 0, j)),
                  pl.BlockSpec((None, 1, tn), lambda l, j: (l, 0, j))],
        out_specs=pl.BlockSpec((None, 8, tn), lambda l, j: (l, 0, j)),
        out_shape=jax.ShapeDtypeStruct((n_l, 8, w6), F32),
        compiler_params=_cparams("parallel", "parallel"),
        name="ada_mod",
    )(cond8, w_ada, b_ada.reshape(n_l, 1, w6))


def _in_proj_kernel(*refs, has_pos):
    if has_pos:
        x_ref, pos_ref, mod_ref, w_ref, cs_ref, proj_ref, zr_ref, zi_ref, x0_ref = refs
        x = x_ref[...] + pos_ref[...]
        x0_ref[...] = x
    else:
        x_ref, mod_ref, w_ref, cs_ref, proj_ref, zr_ref, zi_ref = refs
        x = x_ref[...]
    mod = mod_ref[...]
    sh1 = mod[:, 0:D_MODEL]
    sc1 = mod[:, D_MODEL:2 * D_MODEL]
    h = _ln(x) * (1.0 + sc1) + sh1
    proj = _dot(h.astype(BF16), w_ref[...])
    proj_ref[...] = proj[:, :STORE_W].astype(BF16)
    ft = proj[:, STORE_W:].astype(BF16)
    z = _dot(ft, cs_ref[...])
    zr_ref[...] = z[:, :HEAD_W].astype(BF16)
    zi_ref[...] = z[:, HEAD_W:].astype(BF16)


def _in_proj(x2d, pos, mod3, mod_row, w_in_p, cs, tm):
    t = x2d.shape[0]
    tm = min(tm, t)
    in_specs = [pl.BlockSpec((tm, D_MODEL), lambda i: (i, 0))]
    args = [x2d]
    out_shape = [jax.ShapeDtypeStruct((t, STORE_W), BF16),
                 jax.ShapeDtypeStruct((t, HEAD_W), BF16),
                 jax.ShapeDtypeStruct((t, HEAD_W), BF16)]
    out_specs = [pl.BlockSpec((tm, STORE_W), lambda i: (i, 0)),
                 pl.BlockSpec((tm, HEAD_W), lambda i: (i, 0)),
                 pl.BlockSpec((tm, HEAD_W), lambda i: (i, 0))]
    if pos is not None:
        n_pos = pos.shape[0] // tm
        in_specs.append(pl.BlockSpec((tm, D_MODEL), lambda i: (i % n_pos, 0)))
        args.append(pos)
        out_shape.append(jax.ShapeDtypeStruct((t, D_MODEL), F32))
        out_specs.append(pl.BlockSpec((tm, D_MODEL), lambda i: (i, 0)))
    in_specs += [pl.BlockSpec((None, 1, 6 * D_MODEL), lambda i: (mod_row(i, tm), 0, 0)),
                 pl.BlockSpec((D_MODEL, PROJ_W), lambda i: (0, 0)),
                 pl.BlockSpec((HEAD_W, 2 * HEAD_W), lambda i: (0, 0))]
    args += [mod3, w_in_p, cs]
    return pl.pallas_call(
        functools.partial(_in_proj_kernel, has_pos=pos is not None),
        grid=(t // tm,),
        in_specs=in_specs,
        out_specs=out_specs,
        out_shape=out_shape,
        compiler_params=_cparams("parallel"),
        name="in_proj",
    )(*args)


def _gla_masks(reverse):
    i = np.arange(SEG)
    same_chunk = (i[:, None] // GLA_CHUNK) == (i[None, :] // GLA_CHUNK)
    tri = same_chunk & ((i[None, :] >= i[:, None]) if reverse else (i[None, :] <= i[:, None]))
    l_idx = i[:, None] % GLA_CHUNK
    m_idx = np.arange(GLA_CHUNK)[None, :]
    causal = (m_idx >= l_idx) if reverse else (m_idx <= l_idx)
    return (jnp.asarray(tri, BF16), jnp.asarray(same_chunk, BF16), jnp.asarray(same_chunk, F32),
            jnp.asarray(causal, F32))


def _gla_segment(q, k, v, pre, st_ref, o_ref, masks, reverse):
    seg = q.shape[0]
    n_chunks = seg // GLA_CHUNK
    la = (jnp.minimum(pre, 0.0) - jnp.log1p(jnp.exp(-jnp.abs(pre)))) * (1.0 / GLA_TAU)

    tri_m, ones_m, bd, causal = masks
    hi = la.astype(BF16)
    lo = (la - hi.astype(F32)).astype(BF16)
    b = _dot(tri_m, hi) + _dot(tri_m, lo)
    btot = _dot(ones_m, hi) + _dot(ones_m, lo)

    q_dec = q * (GLA_DK ** -0.5) * jnp.exp(b)
    k_inv = (k * jnp.exp(-b)).astype(BF16)
    k_end = (k * jnp.exp(btot - b)).astype(BF16)
    dec = jnp.exp(btot)
    vb = v.astype(BF16)
    keep = causal > 0.5

    st = st_ref[...]
    order = range(n_chunks - 1, -1, -1) if reverse else range(n_chunks)
    for ci in order:
        sl = slice(ci * GLA_CHUNK, (ci + 1) * GLA_CHUNK)
        qd = q_dec[sl]
        qbd = (jnp.concatenate([qd] * GLA_HEADS, axis=0) * bd).astype(BF16)
        a = _dot_nt(qbd, k_inv[sl])
        a = jnp.where(keep, a, 0.0)
        rr = _dot(a.astype(BF16), vb[sl])
        o = _dot_nt(qd.astype(BF16), st.astype(BF16))
        for h in range(GLA_HEADS):
            hs = slice(h * GLA_CHUNK, (h + 1) * GLA_CHUNK)
            o = o + rr[hs] * bd[hs]
        o_ref[sl, :] = o
        kvt = _dot_tn(vb[sl], k_end[sl])
        st = st * dec[ci * GLA_CHUNK:ci * GLA_CHUNK + 1, :] + kvt * bd
    st_ref[...] = st


def _gla_kernel(*refs, has_init, emit_final):
    qf, kf, vf, lrf, qb, kb, vb, lrb, wah_ref, wal_ref, ba_ref = refs[:11]
    mask_refs = refs[11:19]
    rest = refs[19:]
    if has_init:
        s0f, s0b = rest[:2]
        rest = rest[2:]
    of_ref, ob_ref = rest[:2]
    rest = rest[2:]
    if emit_final:
        sff, sfb = rest[:2]
        rest = rest[2:]
    stf, stb = rest

    s = pl.program_id(1)

    @pl.when(s == 0)
    def _():
        if has_init:
            stf[...] = s0f[...]
            stb[...] = s0b[...]
        else:
            stf[...] = jnp.zeros_like(stf)
            stb[...] = jnp.zeros_like(stb)

    def decay_pre(lr_ref):
        lr = lr_ref[...]
        return _dot(lr, wah_ref[...]) + _dot(lr, wal_ref[...]) + ba_ref[...]

    f32 = lambda ref: ref[...].astype(F32)

    masks_f = tuple(m[...] for m in mask_refs[:4])
    masks_b = tuple(m[...] for m in mask_refs[4:])
    _gla_segment(f32(qf), f32(kf), f32(vf), decay_pre(lrf)[:, :HEAD_W], stf, of_ref, masks_f, reverse=False)
    _gla_segment(f32(qb), f32(kb), f32(vb), decay_pre(lrb)[:, HEAD_W:], stb, ob_ref, masks_b, reverse=True)

    if emit_final:
        @pl.when(s == pl.num_programs(1) - 1)
        def _():
            sff[...] = stf[...]
            sfb[...] = stb[...]


def _gla(proj3, wa_hi, wa_lo, ba, st0, emit_final):
    bsz, n, _ = proj3.shape
    nseg = n // SEG

    def col(cb, width=HEAD_W, rev=False):
        if rev:
            return pl.BlockSpec((None, SEG, width), lambda b, s: (b, nseg - 1 - s, cb))
        return pl.BlockSpec((None, SEG, width), lambda b, s: (b, s, cb))

    in_specs = [col(COL_Q), col(COL_K), col(COL_V), col(COL_LR, LR_W),
                col(COL_Q, rev=True), col(COL_K, rev=True), col(COL_V, rev=True), col(COL_LR, LR_W, rev=True),
                pl.BlockSpec((LR_W, 2 * HEAD_W), lambda b, s: (0, 0)),
                pl.BlockSpec((LR_W, 2 * HEAD_W), lambda b, s: (0, 0)),
                pl.BlockSpec((1, 2 * HEAD_W), lambda b, s: (0, 0))]
    masks = _gla_masks(False) + _gla_masks(True)
    in_specs += [pl.BlockSpec(m.shape, lambda b, s: (0, 0)) for m in masks]
    args = [proj3] * 8 + [wa_hi, wa_lo, ba] + list(masks)
    st_spec = pl.BlockSpec((None, HEAD_W, HEAD_W), lambda b, s: (b, 0, 0))
    if st0 is not None:
        in_specs += [st_spec, st_spec]
        args += [st0[0], st0[1]]
    out_shape = [jax.ShapeDtypeStruct((bsz, n, HEAD_W), F32)] * 2
    out_specs = [pl.BlockSpec((None, SEG, HEAD_W), lambda b, s: (b, s, 0)),
                 pl.BlockSpec((None, SEG, HEAD_W), lambda b, s: (b, nseg - 1 - s, 0))]
    if emit_final:
        out_shape += [jax.ShapeDtypeStruct((bsz, HEAD_W, HEAD_W), F32)] * 2
        out_specs += [st_spec, st_spec]
    return pl.pallas_call(
        functools.partial(_gla_kernel, has_init=st0 is not None, emit_final=emit_final),
        grid=(bsz, nseg),
        in_specs=in_specs,
        out_specs=out_specs,
        out_shape=out_shape,
        scratch_shapes=[pltpu.VMEM((HEAD_W, HEAD_W), F32), pltpu.VMEM((HEAD_W, HEAD_W), F32)],
        compiler_params=_cparams("parallel", "arbitrary"),
        name="gla",
    )(*args)


def _fft_direct_kernel(zr_ref, zi_ref, cn_ref, sn_ref, o_ref):
    o_ref[...] = _dot(cn_ref[...], zr_ref[...]) + _dot(sn_ref[...], zi_ref[...])


def _fft_direct(zr3, zi3, cn, sn):
    bsz, n, w = zr3.shape
    blk = pl.BlockSpec((None, n, w), lambda b: (b, 0, 0))
    tab = pl.BlockSpec((n, n), lambda b: (0, 0))
    return pl.pallas_call(
        _fft_direct_kernel,
        grid=(bsz,),
        in_specs=[blk, blk, tab, tab],
        out_specs=blk,
        out_shape=jax.ShapeDtypeStruct((bsz, n, w), F32),
        compiler_params=_cparams("parallel"),
        name="fft_direct",
    )(zr3, zi3, cn, sn)


def _fft_a_kernel(zr_ref, zi_ref, c_ref, s_ref, gr_ref, gi_ref):
    zr = zr_ref[...]
    zi = zi_ref[...]
    cm = c_ref[...]
    sm = s_ref[...]
    gr_ref[...] = (_dot(cm, zr) + _dot(sm, zi)).astype(BF16)
    gi_ref[...] = (_dot(cm, zi) - _dot(sm, zr)).astype(BF16)


def _fft_c_kernel(gr_ref, gi_ref, mc_ref, ms_ref, o_ref):
    for j in range(gr_ref.shape[0]):
        o_ref[:, j, :] = _dot(mc_ref[j], gr_ref[j]) + _dot(ms_ref[j], gi_ref[j])


def _fft_two_stage(zr3, zi3, tabs):
    bsz, n, w = zr3.shape
    n1 = FFT_N1
    n2 = n // n1
    c1, s1, mc, ms = tabs
    tn = 2048
    wide = n2 * w
    blk = pl.BlockSpec((None, n1, tn), lambda b, j: (b, 0, j))
    tab = pl.BlockSpec((n1, n1), lambda b, j: (0, 0))
    gr, gi = pl.pallas_call(
        _fft_a_kernel,
        grid=(bsz, wide // tn),
        in_specs=[blk, blk, tab, tab],
        out_specs=[blk, blk],
        out_shape=[jax.ShapeDtypeStruct((bsz, n1, wide), BF16)] * 2,
        compiler_params=_cparams("parallel", "parallel"),
        name="fft_stage_a",
    )(zr3.reshape(bsz, n1, wide), zi3.reshape(bsz, n1, wide), c1, s1)
    kb = 8
    gblk = pl.BlockSpec((None, kb, n2, w), lambda b, j: (b, j, 0, 0))
    mblk = pl.BlockSpec((kb, n2, n2), lambda b, j: (j, 0, 0))
    out = pl.pallas_call(
        _fft_c_kernel,
        grid=(bsz, n1 // kb),
        in_specs=[gblk, gblk, mblk, mblk],
        out_specs=pl.BlockSpec((None, n2, kb, w), lambda b, j: (b, 0, j, 0)),
        out_shape=jax.ShapeDtypeStruct((bsz, n2, n1, w), F32),
        compiler_params=_cparams("parallel", "parallel"),
        name="fft_stage_c",
    )(gr.reshape(bsz, n1, n2, w), gi.reshape(bsz, n1, n2, w), mc, ms)
    return out.reshape(bsz, n, w)


def _mix_masks(tm):
    h = np.arange(HEAD_W) // GLA_DK
    head_mean = (h[:, None] == h[None, :]) / GLA_DK
    rg = np.arange(GMLP_GROUPS * GMLP_CHUNK) // GMLP_CHUNK
    cg = np.arange(HEAD_W) // (HEAD_W // GMLP_GROUPS)
    t = np.arange(tm)
    e = np.arange(N_EXPERTS)
    return (jnp.asarray(head_mean, BF16), jnp.asarray(rg[:, None] == cg[None, :], F32),
            jnp.asarray(t[:, None] < t[None, :], BF16), jnp.asarray(e[None, :] < e[:, None], BF16))


def _route(logits, bias, before, lower):
    tm = logits.shape[1]
    s = _sigmoid(logits)
    biased = s + bias
    neg = -jnp.inf
    rows = lax.broadcasted_iota(jnp.int32, (8, tm), 0)

    def first_argmax(x, ids, sentinel):
        m = jnp.max(x, axis=0, keepdims=True)
        return m, jnp.min(jnp.where(x == m, ids, sentinel), axis=0, keepdims=True)

    gs_rows = []
    for g in range(N_GROUPS):
        x = biased[8 * g:8 * g + 8]
        m1, i1 = first_argmax(x, rows, 8)
        m2 = jnp.max(jnp.where(rows == i1, neg, x), axis=0, keepdims=True)
        gs_rows.append(m1 + m2)
    gs = jnp.concatenate(gs_rows, axis=0)
    gsel = jnp.zeros((N_GROUPS, tm), F32)
    for _ in range(TOPK_GROUPS):
        _, i = first_argmax(gs, rows, 8)
        hit = rows == i
        gsel = jnp.where(hit, 1.0, gsel)
        gs = jnp.where(hit, neg, gs)

    xs = [jnp.where(gsel[g:g + 1] > 0.0, biased[8 * g:8 * g + 8], neg) for g in range(N_GROUPS)]
    ids = [rows + 8 * g for g in range(N_GROUPS)]
    sel = [jnp.zeros((8, tm), F32) for _ in range(N_GROUPS)]
    eids = []
    for _ in range(TOP_K):
        m = xs[0]
        for g in range(1, N_GROUPS):
            m = jnp.maximum(m, xs[g])
        m = jnp.max(m, axis=0, keepdims=True)
        cand = jnp.where(xs[0] == m, ids[0], N_EXPERTS)
        for g in range(1, N_GROUPS):
            cand = jnp.minimum(cand, jnp.where(xs[g] == m, ids[g], N_EXPERTS))
        i = jnp.min(cand, axis=0, keepdims=True)
        eids.append(i)
        for g in range(N_GROUPS):
            hit = ids[g] == i
            sel[g] = jnp.where(hit, 1.0, sel[g])
            xs[g] = jnp.where(hit, neg, xs[g])

    sel_all = jnp.concatenate(sel, axis=0)
    seen = _dot(sel_all.astype(BF16), before)
    counts = jnp.sum(sel_all, axis=1, keepdims=True)

    n_chunks = jnp.ceil(counts * (1.0 / CHUNK))
    run_start = _dot(lower, jnp.broadcast_to(n_chunks, (N_EXPERTS, LANES)).astype(BF16))[:, 0:1] * CHUNK
    local_pos = seen + run_start

    def pick(k, table):
        acc = None
        for g in range(N_GROUPS):
            v = jnp.where(ids[g] == eids[k], table[8 * g:8 * g + 8], 0.0)
            acc = v if acc is None else acc + v
        return jnp.sum(acc, axis=0, keepdims=True)

    w_raw = [pick(k, s) for k in range(TOP_K)]
    lpos = [pick(k, local_pos) for k in range(TOP_K)]
    tot = w_raw[0]
    for k in range(1, TOP_K):
        tot = tot + w_raw[k]
    weights = [w / tot * ROUTED_SCALE for w in w_raw]
    return weights, lpos, n_chunks, run_start


def _mix_out_kernel(x_ref, of_ref, ob_ref, g_ref, su_ref, sv_ref, cb_ref, cc_ref, cx_ref,
                    ccp_ref, cxp_ref, ccn_ref, cxn_ref, ft_ref, mod_ref,
                    glag_ref, sgng_ref, sgnb_ref, wsgu_ref, bsgu_ref, wconv_ref, wout_ref,
                    ln1g_ref, ln1b_ref, wrh_ref, wrl_ref, rb_ref, hmean_ref, sgubd_ref, before_ref, lower_ref,
                    x1_ref, xw_ref, wrow_ref, lpos_ref, tab_ref, cnt_ref, carry_ref, *, seq_len):
    tm = x_ref.shape[0]
    i = pl.program_id(0)

    @pl.when(i == 0)
    def _():
        carry_ref[...] = jnp.zeros_like(carry_ref)

    mod = mod_ref[...]
    g1 = mod[:, 2 * D_MODEL:3 * D_MODEL]
    sh2 = mod[:, 3 * D_MODEL:4 * D_MODEL]
    sc2 = mod[:, 4 * D_MODEL:5 * D_MODEL]

    o = of_ref[...] + ob_ref[...]
    head_mean = hmean_ref[...]
    o2 = o * o
    o2_hi = o2.astype(BF16)
    o2_lo = (o2 - o2_hi.astype(F32)).astype(BF16)
    ms = _dot(o2_hi, head_mean) + _dot(o2_lo, head_mean)
    y_gla = o * lax.rsqrt(ms + RMS_EPS) * glag_ref[...] * _silu(g_ref[...].astype(F32))

    vn = _ln(sv_ref[...].astype(F32)) * sgng_ref[...] + sgnb_ref[...]
    sgu_bd = sgubd_ref[...]
    sp_parts = []
    for j in range(tm // GMLP_CHUNK):
        vc = vn[j * GMLP_CHUNK:(j + 1) * GMLP_CHUNK]
        vbd = (jnp.concatenate([vc] * GMLP_GROUPS, axis=0) * sgu_bd).astype(BF16)
        sp_parts.append(_dot(wsgu_ref[...], vbd) + bsgu_ref[...])
    y_sgu = su_ref[...].astype(F32) * jnp.concatenate(sp_parts, axis=0)

    z = cc_ref[...].astype(F32) * cx_ref[...].astype(F32)
    z_before = (ccp_ref[...].astype(F32) * cxp_ref[...].astype(F32))[HALO - 1:HALO, :]
    z_after = (ccn_ref[...].astype(F32) * cxn_ref[...].astype(F32))[0:1, :]
    row = lax.broadcasted_iota(jnp.int32, (tm, HEAD_W), 0)
    pos = (i * tm + row) & (seq_len - 1)
    z_prev = jnp.where(row == 0, z_before, pltpu.roll(z, 1, 0))
    z_next = jnp.where(row == tm - 1, z_after, pltpu.roll(z, tm - 1, 0))
    z_prev = jnp.where(pos == 0, 0.0, z_prev)
    z_next = jnp.where(pos == seq_len - 1, 0.0, z_next)
    wconv = wconv_ref[...]
    y_conv = cb_ref[...].astype(F32) * (wconv[0:1] * z_prev + wconv[1:2] * z + wconv[2:3] * z_next)

    y = (_dot(y_gla.astype(BF16), wout_ref[0:HEAD_W, :])
         + _dot(y_sgu.astype(BF16), wout_ref[HEAD_W:2 * HEAD_W, :])
         + _dot(y_conv.astype(BF16), wout_ref[2 * HEAD_W:3 * HEAD_W, :])
         + _dot(ft_ref[...].astype(BF16), wout_ref[3 * HEAD_W:4 * HEAD_W, :]))
    x1 = _ln(DEEPNORM_ALPHA * x_ref[...] + g1 * y) * ln1g_ref[...] + ln1b_ref[...]
    x1_ref[...] = x1
    h2 = _ln(x1) * (1.0 + sc2) + sh2
    xw_ref[...] = _pack_bf16_pairs(h2)

    h2_hi = h2.astype(BF16)
    h2_lo = (h2 - h2_hi.astype(F32)).astype(BF16)
    logits = _dot(h2_hi, wrh_ref[...]) + _dot(h2_hi, wrl_ref[...]) + _dot(h2_lo, wrh_ref[...])
    weights, lpos, n_chunks, run_start = _route(logits.T[:N_EXPERTS], rb_ref[...], before_ref[...], lower_ref[...])
    lpos_ref[...] = jnp.concatenate(lpos, axis=0).astype(jnp.int32)
    wrow_ref[...] = jnp.concatenate(weights, axis=0)
    carry = carry_ref[:, 0:1]
    lane = lax.broadcasted_iota(jnp.int32, (N_EXPERTS, LANES), 1)
    total = jnp.sum(n_chunks, axis=0, keepdims=True)
    cols = jnp.where(lane == 0, n_chunks, jnp.where(lane == 1, run_start,
                                                   jnp.where(lane == 2, carry, jnp.where(lane == 3, total, 0.0))))
    tab = jnp.concatenate([cols, jnp.zeros((LANES - N_EXPERTS, LANES), F32)], axis=0).T
    tab_ref[...] = tab[0:8].astype(jnp.int32)
    new_carry = carry_ref[...] + n_chunks
    carry_ref[...] = new_carry
    cnt_ref[...] = new_carry


def _mix_out(x2d, of2d, ob2d, proj, yft2d, mod3, mod_row, lw, seq_len, tm):
    t = x2d.shape[0]
    tm = min(tm, t)
    nt8 = t // HALO
    rows8 = tm // HALO

    def col(cb):
        return pl.BlockSpec((tm, HEAD_W), lambda i: (i, cb))

    def halo_prev(cb):
        return pl.BlockSpec((HALO, HEAD_W), lambda i: (jnp.maximum(i * rows8 - 1, 0), cb))

    def halo_next(cb):
        return pl.BlockSpec((HALO, HEAD_W), lambda i: (jnp.minimum((i + 1) * rows8, nt8 - 1), cb))

    def full(a):
        return pl.BlockSpec(a.shape, lambda i: (0,) * a.ndim)

    tok_d = pl.BlockSpec((tm, D_MODEL), lambda i: (i, 0))
    tok_h = pl.BlockSpec((tm, HEAD_W), lambda i: (i, 0))
    weights = [lw["gla_norm_g"], lw["sgu_norm_g"], lw["sgu_norm_b"], lw["w_sgu_cat"], lw["b_sgu_full"],
               lw["w_conv"], lw["w_out"], lw["ln1_g"], lw["ln1_b"], lw["w_router_hi"], lw["w_router_lo"], lw["router_bias"]]
    weights += list(_mix_masks(tm))
    in_specs = ([tok_d, tok_h, tok_h, col(COL_G), col(COL_SGU), col(COL_SGV), col(COL_CVB), col(COL_CVC),
                 col(COL_CVX), halo_prev(COL_CVC), halo_prev(COL_CVX), halo_next(COL_CVC), halo_next(COL_CVX),
                 tok_h, pl.BlockSpec((None, 1, 6 * D_MODEL), lambda i: (mod_row(i, tm), 0, 0))]
                + [full(w) for w in weights])
    args = [x2d, of2d, ob2d] + [proj] * 10 + [yft2d, mod3] + weights
    return pl.pallas_call(
        functools.partial(_mix_out_kernel, seq_len=seq_len),
        grid=(t // tm,),
        in_specs=in_specs,
        out_specs=[tok_d,
                   pl.BlockSpec((tm, PACK_W), lambda i: (i, 0)),
                   pl.BlockSpec((None, TOP_K, tm), lambda i: (i, 0, 0)),
                   pl.BlockSpec((None, TOP_K, tm), lambda i: (i, 0, 0)),
                   pl.BlockSpec((None, 8, LANES), lambda i: (i, 0, 0)),
                   pl.BlockSpec((N_EXPERTS, LANES), lambda i: (0, 0))],
        out_shape=[jax.ShapeDtypeStruct((t, D_MODEL), F32),
                   jax.ShapeDtypeStruct((t, PACK_W), jnp.uint32),
                   jax.ShapeDtypeStruct((t // tm, TOP_K, tm), F32),
                   jax.ShapeDtypeStruct((t // tm, TOP_K, tm), jnp.int32),
                   jax.ShapeDtypeStruct((t // tm, 8, LANES), jnp.int32),
                   jax.ShapeDtypeStruct((N_EXPERTS, LANES), F32)],
        scratch_shapes=[pltpu.VMEM((N_EXPERTS, LANES), F32)],
        compiler_params=_cparams("arbitrary"),
        name="mix_out",
    )(*args)


def _local_rows(tm):
    need = tm * TOP_K + N_EXPERTS * (CHUNK - 1)
    return -(-need // LBLK) * LBLK


def _run_copy(local_ref, sorted_hbm, sem, local_row, sorted_row, to_sorted, rows=CHUNK):
    loc = local_ref.at[pl.ds(pl.multiple_of(local_row, CHUNK), rows)]
    srt = sorted_hbm.at[pl.ds(pl.multiple_of(sorted_row, CHUNK), rows)]
    return pltpu.make_async_copy(loc, srt, sem) if to_sorted else pltpu.make_async_copy(srt, loc, sem)


def _start_run_copies(tab_ref, gstart_ref, local_ref, sorted_hbm, sem, to_sorted):
    def per_expert(e, totals):
        n = tab_ref[0, e]
        l0 = tab_ref[1, e]
        g0 = (gstart_ref[e] + tab_ref[2, e]) * CHUNK
        n_big = n >> BIG_SHIFT
        n_small = n & ((1 << BIG_SHIFT) - 1)

        def big(j, c):
            _run_copy(local_ref, sorted_hbm, sem, l0 + j * BIG_ROWS, g0 + j * BIG_ROWS, to_sorted, BIG_ROWS).start()
            return c

        lax.fori_loop(0, n_big, big, 0)
        l1 = l0 + n_big * BIG_ROWS
        g1 = g0 + n_big * BIG_ROWS

        def small(j, c):
            _run_copy(local_ref, sorted_hbm, sem, l1 + j * CHUNK, g1 + j * CHUNK, to_sorted).start()
            return c

        lax.fori_loop(0, n_small, small, 0)
        return totals[0] + n_big, totals[1] + n_small

    return lax.fori_loop(0, N_EXPERTS, per_expert, (jnp.int32(0), jnp.int32(0)))


def _get_pending(pending_ref, s):
    return pending_ref[s, 0], pending_ref[s, 1]


def _set_pending(pending_ref, s, counts):
    pending_ref[s, 0] = counts[0]
    pending_ref[s, 1] = counts[1]


def _wait_run_copies(counts, local_ref, sorted_hbm, sem, to_sorted):
    batch = 8

    def wait_n(rows, reps):
        def body(j, c):
            for _ in range(reps):
                _run_copy(local_ref, sorted_hbm, sem, 0, 0, to_sorted, rows).wait()
            return c
        return body

    for count, rows in zip(counts, (BIG_ROWS, CHUNK)):
        lax.fori_loop(0, count >> 3, wait_n(rows, batch), 0)
        lax.fori_loop(0, count & (batch - 1), wait_n(rows, 1), 0)


def _moe_scatter_kernel(gstart_ref, gtail_ref, xw_ref, lpos_ref, tab_ref, sorted_hbm,
                        local_ref, sem, zsem, pending_ref, *, eblk):
    i = pl.program_id(0)
    n = pl.num_programs(0)
    slot = i & 1
    tm = xw_ref.shape[0]
    lrows = local_ref.shape[1]

    def zero_copy(e):
        off = pl.multiple_of(jnp.maximum(gtail_ref[e], 0), eblk)
        return pltpu.make_async_copy(local_ref.at[1, pl.ds(0, eblk)], sorted_hbm.at[pl.ds(off, eblk)], zsem)

    @pl.when(i == 0)
    def _():
        _set_pending(pending_ref, 0, (0, 0))
        _set_pending(pending_ref, 1, (0, 0))
        local_ref[1, 0:eblk, :] = jnp.zeros((eblk, PACK_W), jnp.uint32)

        def z_start(e, c):
            @pl.when(gtail_ref[e] >= 0)
            def _():
                zero_copy(e).start()
            return c

        def z_wait(e, c):
            @pl.when(gtail_ref[e] >= 0)
            def _():
                zero_copy(e).wait()
            return c

        lax.fori_loop(0, N_EXPERTS, z_start, 0)
        lax.fori_loop(0, N_EXPERTS, z_wait, 0)

    local = local_ref.at[slot]
    _wait_run_copies(_get_pending(pending_ref, slot), local, sorted_hbm, sem.at[slot], True)

    x = _unpack_bf16_pairs(xw_ref[...]).astype(BF16)
    lpos = lpos_ref[...].astype(jnp.int16)
    one = jnp.ones((LBLK, tm), BF16)
    used_rows = tab_ref[3, 0] * CHUNK

    def sort_block(b):
        riota = lax.broadcasted_iota(jnp.int16, (LBLK, tm), 0) + b * LBLK
        p = jnp.zeros((LBLK, tm), BF16)
        for k in range(TOP_K):
            p = jnp.where(riota == lpos[k:k + 1, :], one, p)
        local[b * LBLK:(b + 1) * LBLK, :] = _pack_exact_bf16_pairs(_dot(p, x))

    n_blocks = lrows // LBLK
    for b in range(n_blocks - 1):
        sort_block(b)
    pl.when(used_rows > (n_blocks - 1) * LBLK)(functools.partial(sort_block, n_blocks - 1))

    _set_pending(pending_ref, slot, _start_run_copies(tab_ref, gstart_ref, local, sorted_hbm, sem.at[slot], True))

    @pl.when(i == n - 1)
    def _():
        for s in range(2):
            _wait_run_copies(_get_pending(pending_ref, s), local_ref.at[s], sorted_hbm, sem.at[s], True)


def _moe_scatter(gstart, gtail, xw, lpos, tab, n_rows, tm, eblk):
    n_tiles = xw.shape[0] // tm
    lrows = _local_rows(tm)
    any_spec = pl.BlockSpec(memory_space=pl.ANY)
    return pl.pallas_call(
        functools.partial(_moe_scatter_kernel, eblk=eblk),
        grid_spec=pltpu.PrefetchScalarGridSpec(
            num_scalar_prefetch=2,
            grid=(n_tiles,),
            in_specs=[pl.BlockSpec((tm, PACK_W), lambda i, *_: (i, 0)),
                      pl.BlockSpec((None, TOP_K, tm), lambda i, *_: (i, 0, 0)),
                      pl.BlockSpec((None, 8, LANES), lambda i, *_: (i, 0, 0), memory_space=pltpu.SMEM)],
            out_specs=any_spec,
            scratch_shapes=[pltpu.VMEM((2, lrows, PACK_W), jnp.uint32), pltpu.SemaphoreType.DMA((2,)),
                            pltpu.SemaphoreType.DMA, pltpu.SMEM((2, 2), jnp.int32)]),
        out_shape=jax.ShapeDtypeStruct((n_rows, PACK_W), jnp.uint32),
        compiler_params=_cparams("arbitrary"),
        name="moe_scatter",
    )(gstart, gtail, xw, lpos, tab)


def _expert2_kernel(be_ref, nu_ref, xs_ref, wg_ref, wu_ref, wd_ref, ys_ref, wgu_b, wd_b):
    j = pl.program_id(0)

    @pl.when(jnp.logical_or(j == 0, be_ref[j] != be_ref[jnp.maximum(j - 1, 0)]))
    def _():
        wgu_b[:, :EXPERT_FF] = wg_ref[...].astype(BF16)
        wgu_b[:, EXPERT_FF:] = wu_ref[...].astype(BF16)
        wd_b[...] = wd_ref[...].astype(BF16)

    @pl.when(j < nu_ref[0])
    def _():
        x = _unpack_bf16_pairs(xs_ref[...]).astype(BF16)
        gu = _dot(x, wgu_b[...])
        a = _silu(gu[:, :EXPERT_FF]) * gu[:, EXPERT_FF:]
        ys_ref[...] = _pack_bf16_pairs(_dot(a.astype(BF16), wd_b[...]))


def _experts2(xs, block_expert, n_used, lw, layer, eblk):
    n_rows = xs.shape[0]
    ff = EXPERT_FF

    def blk(j, be, nu):
        return (jnp.minimum(j, jnp.maximum(nu[0] - 1, 0)), 0)

    def wblk(j, be, nu):
        return (layer, be[j], 0, 0)

    return pl.pallas_call(
        _expert2_kernel,
        grid_spec=pltpu.PrefetchScalarGridSpec(
            num_scalar_prefetch=2,
            grid=(n_used[0],),
            in_specs=[pl.BlockSpec((eblk, PACK_W), blk),
                      pl.BlockSpec((None, None, D_MODEL, ff), wblk),
                      pl.BlockSpec((None, None, D_MODEL, ff), wblk),
                      pl.BlockSpec((None, None, ff, D_MODEL), wblk)],
            out_specs=pl.BlockSpec((eblk, PACK_W), blk),
            scratch_shapes=[pltpu.VMEM((D_MODEL, 2 * ff), BF16), pltpu.VMEM((ff, D_MODEL), BF16)]),
        out_shape=jax.ShapeDtypeStruct((n_rows, PACK_W), jnp.uint32),
        compiler_params=_cparams("arbitrary"),
        name="moe_experts",
    )(block_expert, n_used, xs, lw["w_exp_gate"], lw["w_exp_up"], lw["w_exp_down"])


def _moe_combine_kernel(gstart_ref, x1_ref, xw_ref, lpos_ref, wrow_ref, tab_ref, tab_next_ref, mod_ref,
                        sg_ref, su_ref, sd_ref, ln2g_ref, ln2b_ref, sorted_hbm, o_ref,
                        local_ref, sem, pending_ref):
    i = pl.program_id(0)
    n = pl.num_programs(0)
    slot = i & 1
    tm = x1_ref.shape[0]
    lrows = local_ref.shape[1]

    @pl.when(i == 0)
    def _():
        local_ref[...] = jnp.zeros_like(local_ref)
        _set_pending(pending_ref, 0,
                     _start_run_copies(tab_ref, gstart_ref, local_ref.at[0], sorted_hbm, sem.at[0], False))

    @pl.when(i + 1 < n)
    def _():
        nxt = 1 - slot
        _set_pending(pending_ref, nxt, _start_run_copies(tab_next_ref, gstart_ref, local_ref.at[nxt], sorted_hbm,
                                                         sem.at[nxt], False))

    h = _unpack_bf16_pairs(xw_ref[...]).astype(BF16)
    a = _silu(_dot(h, sg_ref[...])) * _dot(h, su_ref[...])
    acc = _dot(a.astype(BF16), sd_ref[...])

    local = local_ref.at[slot]
    _wait_run_copies(_get_pending(pending_ref, slot), local, sorted_hbm, sem.at[slot], False)

    lpos = lpos_ref[...].astype(jnp.int16)
    wrow = wrow_ref[...].astype(BF16)

    def unsort_block(b, acc):
        riota = lax.broadcasted_iota(jnp.int16, (LBLK, tm), 0) + b * LBLK
        q = jnp.zeros((LBLK, tm), BF16)
        for k in range(TOP_K):
            q = jnp.where(riota == lpos[k:k + 1, :], jnp.broadcast_to(wrow[k:k + 1, :], (LBLK, tm)), q)
        y = _unpack_bf16_pairs(local[b * LBLK:(b + 1) * LBLK, :]).astype(BF16)
        return acc + _dot_tn(q, y)

    for b in range(lrows // LBLK):
        acc = unsort_block(b, acc)

    g2 = mod_ref[...][:, 5 * D_MODEL:6 * D_MODEL]
    u = DEEPNORM_ALPHA * x1_ref[...] + g2 * acc
    o_ref[...] = _ln(u) * ln2g_ref[...] + ln2b_ref[...]


def _moe_combine(gstart, x1, xw, lpos, wrow, tab, ys, mod3, mod_row, lw, tm):
    t = x1.shape[0]
    n_tiles = t // tm
    lrows = _local_rows(tm)

    def full(a):
        return pl.BlockSpec(a.shape, lambda i, *_: (0,) * a.ndim)

    tab_blk = lambda f: pl.BlockSpec((None, 8, LANES), f, memory_space=pltpu.SMEM)
    return pl.pallas_call(
        _moe_combine_kernel,
        grid_spec=pltpu.PrefetchScalarGridSpec(
            num_scalar_prefetch=1,
            grid=(n_tiles,),
            in_specs=[pl.BlockSpec((tm, D_MODEL), lambda i, *_: (i, 0)),
                      pl.BlockSpec((tm, PACK_W), lambda i, *_: (i, 0)),
                      pl.BlockSpec((None, TOP_K, tm), lambda i, *_: (i, 0, 0)),
                      pl.BlockSpec((None, TOP_K, tm), lambda i, *_: (i, 0, 0)),
                      tab_blk(lambda i, *_: (i, 0, 0)),
                      tab_blk(lambda i, *_: (jnp.minimum(i + 1, n_tiles - 1), 0, 0)),
                      pl.BlockSpec((None, 1, 6 * D_MODEL), lambda i, *_: (mod_row(i, tm), 0, 0)),
                      full(lw["w_sh_gate"]), full(lw["w_sh_up"]), full(lw["w_sh_down"]),
                      full(lw["ln2_g"]), full(lw["ln2_b"]),
                      pl.BlockSpec(memory_space=pl.ANY)],
            out_specs=pl.BlockSpec((tm, D_MODEL), lambda i, *_: (i, 0)),
            scratch_shapes=[pltpu.VMEM((2, lrows, PACK_W), jnp.uint32), pltpu.SemaphoreType.DMA((2,)),
                            pltpu.SMEM((2, 2), jnp.int32)]),
        out_shape=jax.ShapeDtypeStruct((t, D_MODEL), F32),
        compiler_params=_cparams("arbitrary"),
        name="moe_combine",
    )(gstart, x1, xw, lpos, wrow, tab, tab, mod3, lw["w_sh_gate"], lw["w_sh_up"], lw["w_sh_down"],
      lw["ln2_g"], lw["ln2_b"], ys)


def _moe2(x1, xw, wrow, lpos, tab, chunks_f, mod3, mod_row, lw, layer, tm):
    t = x1.shape[0]
    n_tiles = t // tm
    eblk = max(LBLK, min(EBLK_MAX, t * TOP_K // N_EXPERTS))
    region_rows = chunks_f[:, 0].astype(jnp.int32) * CHUNK
    padded = ((region_rows + eblk - 1) // eblk) * eblk
    end = jnp.cumsum(padded)
    start = end - padded
    n_blocks = -(-(t * TOP_K + n_tiles * N_EXPERTS * (CHUNK - 1)) // eblk) + N_EXPERTS
    n_used = end[-1] // eblk
    blk_row = jnp.minimum(jnp.arange(n_blocks, dtype=jnp.int32), jnp.maximum(n_used - 1, 0)) * eblk
    block_expert = jnp.minimum(jnp.sum((end[None, :] <= blk_row[:, None]).astype(jnp.int32), axis=1),
                               N_EXPERTS - 1).astype(jnp.int32)
    gstart = start // CHUNK
    gtail = jnp.where(padded > 0, end - eblk, -1)
    xs = _moe_scatter(gstart, gtail, xw, lpos, tab, n_blocks * eblk, tm, eblk)
    ys = _experts2(xs, block_expert, n_used.reshape(1), lw, layer, eblk)
    return _moe_combine(gstart, x1, xw, lpos, wrow, tab, ys, mod3, mod_row, lw, tm)


def _channel_dft_table():
    k = np.arange(FNET_CH, dtype=np.float64)
    ang = 2.0 * np.pi * np.outer(k, k) / FNET_CH
    eye = np.eye(FNET_GROUPS)
    return np.concatenate([np.kron(eye, np.cos(ang)), -np.kron(eye, np.sin(ang))], axis=1)


def _direct_dft_tables(n):
    k = np.arange(n, dtype=np.float64)
    ang = 2.0 * np.pi * (np.outer(k, k) % n) / n
    scale = 1.0 / math.sqrt(n * FNET_CH)
    return np.cos(ang) * scale, np.sin(ang) * scale


def _two_stage_dft_tables(n):
    n1 = FFT_N1
    n2 = n // n1
    a = np.arange(n1, dtype=np.float64)
    ang1 = 2.0 * np.pi * (np.outer(a, a) % n1) / n1
    k1 = np.arange(n1).reshape(n1, 1, 1)
    k2 = np.arange(n2).reshape(1, n2, 1)
    m2 = np.arange(n2).reshape(1, 1, n2)
    ang2 = 2.0 * np.pi * ((m2 * (k1 + n1 * k2)) % n) / n
    scale = 1.0 / math.sqrt(n * FNET_CH)
    return np.cos(ang1), np.sin(ang1), np.cos(ang2) * scale, np.sin(ang2) * scale


def _grid_sincos_table(rows, d):
    quarter = d // 4
    omega = 1.0 / (POS_BASE ** (np.arange(quarter, dtype=np.float64) / quarter))
    r = np.arange(rows, dtype=np.float64)[:, None] * omega
    c = np.arange(GRID_W, dtype=np.float64)[:, None] * omega
    return (np.concatenate([np.sin(r), np.cos(r)], axis=-1).astype(np.float32),
            np.concatenate([np.sin(c), np.cos(c)], axis=-1).astype(np.float32))


def _layer_weights(l, w_in, w_gla_a, b_gla_a, gla_norm_g, sgu_norm_g, sgu_norm_b, w_sgu, b_sgu, w_conv,
                   w_out, ln1_g, ln1_b, ln2_g, ln2_b, w_router, router_bias,
                   w_exp_gate, w_exp_up, w_exp_down, w_sh_gate, w_sh_up, w_sh_down):
    wi = w_in[l]
    lr0 = 4 * HEAD_W
    w_in_p = jnp.concatenate(
        [wi[:, :lr0], wi[:, lr0 + 2 * GLA_LR:lr0 + 2 * GLA_LR + 5 * HEAD_W], wi[:, lr0:lr0 + 2 * GLA_LR],
         jnp.zeros((D_MODEL, LR_W - 2 * GLA_LR), F32), wi[:, lr0 + 2 * GLA_LR + 5 * HEAD_W:]], axis=1).astype(BF16)
    wa_pad = jnp.zeros((LR_W, 2 * HEAD_W), F32)
    wa_pad = wa_pad.at[:GLA_LR, :HEAD_W].set(w_gla_a[l, 0])
    wa_pad = wa_pad.at[GLA_LR:2 * GLA_LR, HEAD_W:].set(w_gla_a[l, 1])
    w_router_pad = jnp.concatenate([w_router[l], jnp.zeros((D_MODEL, LANES - N_EXPERTS), F32)], axis=1)
    w_router_hi = w_router_pad.astype(BF16)
    row = lambda a: a[l].reshape(1, -1)
    return {
        "w_in_p": w_in_p,
        "wa_hi": wa_pad.astype(BF16), "wa_lo": (wa_pad - wa_pad.astype(BF16).astype(F32)).astype(BF16),
        "ba": jnp.concatenate([b_gla_a[l, 0], b_gla_a[l, 1]]).reshape(1, 2 * HEAD_W),
        "gla_norm_g": row(gla_norm_g), "sgu_norm_g": row(sgu_norm_g), "sgu_norm_b": row(sgu_norm_b),
        "w_sgu_cat": jnp.concatenate([w_sgu[l, g] for g in range(GMLP_GROUPS)], axis=1).astype(BF16),
        "b_sgu_full": jnp.repeat(b_sgu[l].T, HEAD_W // GMLP_GROUPS, axis=1),
        "w_conv": w_conv[l],
        "w_out": w_out[l].astype(BF16),
        "ln1_g": row(ln1_g), "ln1_b": row(ln1_b), "ln2_g": row(ln2_g), "ln2_b": row(ln2_b),
        "w_router_hi": w_router_hi, "w_router_lo": (w_router_pad - w_router_hi.astype(F32)).astype(BF16),
        "router_bias": router_bias[l].reshape(N_EXPERTS, 1),
        "w_exp_gate": w_exp_gate, "w_exp_up": w_exp_up, "w_exp_down": w_exp_down,
        "w_sh_gate": w_sh_gate[l].astype(BF16), "w_sh_up": w_sh_up[l].astype(BF16),
        "w_sh_down": w_sh_down[l].astype(BF16),
    }


def _state_to_blockdiag_t(s):
    bsz = s.shape[0]
    st = jnp.swapaxes(s, 2, 3)
    eye = jnp.eye(GLA_HEADS, dtype=s.dtype)
    return jnp.einsum("bhvd,hg->bhvgd", st, eye).reshape(bsz, HEAD_W, HEAD_W)


def _blockdiag_t_to_state(st):
    bsz = st.shape[0]
    s5 = st.reshape(bsz, GLA_HEADS, GLA_DK, GLA_HEADS, GLA_DK)
    diag = jnp.stack([s5[:, h, :, h, :] for h in range(GLA_HEADS)], axis=1)
    return jnp.swapaxes(diag, 2, 3)


def _trunk_layer(x3, pos, mod3, mod_row, lw, layer, st0, emit_final, tabs):
    bsz, n, _ = x3.shape
    t = bsz * n
    outs = _in_proj(x3.reshape(t, D_MODEL), pos, mod3, mod_row, lw["w_in_p"], tabs["cs"], tm=512)
    if pos is not None:
        proj, zr, zi, x2d = outs
    else:
        proj, zr, zi = outs
        x2d = x3.reshape(t, D_MODEL)
    gla_out = _gla(proj.reshape(bsz, n, STORE_W), lw["wa_hi"], lw["wa_lo"], lw["ba"], st0, emit_final)
    o_f, o_b = gla_out[:2]
    zr3 = zr.reshape(bsz, n, HEAD_W)
    zi3 = zi.reshape(bsz, n, HEAD_W)
    if "two_stage" in tabs:
        yft = _fft_two_stage(zr3, zi3, tabs["two_stage"])
    else:
        yft = _fft_direct(zr3, zi3, *tabs["direct"])
    x1, xw, wrow, lpos, tab, chunks = _mix_out(x2d, o_f.reshape(t, HEAD_W), o_b.reshape(t, HEAD_W), proj,
                                             yft.reshape(t, HEAD_W), mod3, mod_row, lw, seq_len=n, tm=256)
    x2 = _moe2(x1, xw, wrow, lpos, tab, chunks, mod3, mod_row, lw, layer, tm=min(256, t))
    return x2.reshape(bsz, n, D_MODEL), gla_out[2:]


def kernel(x_prompt, x_sample, c, state_gla, c_ctx, w_ada, b_ada, w_in, w_gla_a, b_gla_a, gla_norm_g, sgu_norm_g, sgu_norm_b, w_sgu, b_sgu, w_conv, w_out, ln1_g, ln1_b, ln2_g, ln2_b, w_router, router_bias, w_exp_gate, w_exp_up, w_exp_down, w_sh_gate, w_sh_up, w_sh_down):
    n_layers = w_ada.shape[0]
    bp, np_, _ = x_prompt.shape
    bs, ns, _ = x_sample.shape
    assert bs <= 7

    cond8 = jnp.concatenate([c_ctx[None, :], c, jnp.zeros((7 - bs, D_MODEL), F32)], axis=0)
    mod = _ada_mod(cond8, w_ada, b_ada)

    tabs_p = {"cs": jnp.asarray(_channel_dft_table(), BF16),
              "direct": tuple(jnp.asarray(a, BF16) for a in _direct_dft_tables(np_))}
    tabs_s = {"cs": tabs_p["cs"],
              "two_stage": tuple(jnp.asarray(a, BF16) for a in _two_stage_dft_tables(ns))}
    rtab, ctab = _grid_sincos_table(ns // GRID_W, D_MODEL)
    pos = jnp.concatenate([jnp.repeat(jnp.asarray(rtab), GRID_W, axis=0),
                           jnp.tile(jnp.asarray(ctab), (ns // GRID_W, 1))], axis=-1)

    prompt_row = lambda i, tm: 0
    sample_row = lambda i, tm: 1 + (i * tm) // ns

    y_p = x_prompt
    y_s = x_sample
    finals = []
    for l in range(n_layers):
        lw = _layer_weights(l, w_in, w_gla_a, b_gla_a, gla_norm_g, sgu_norm_g, sgu_norm_b, w_sgu, b_sgu,
                            w_conv, w_out, ln1_g, ln1_b, ln2_g, ln2_b, w_router, router_bias,
                            w_exp_gate, w_exp_up, w_exp_down, w_sh_gate, w_sh_up, w_sh_down)
        mod3 = mod[l].reshape(8, 1, 6 * D_MODEL)
        y_p, fin = _trunk_layer(y_p, None, mod3, prompt_row, lw, l, None, True, tabs_p)
        finals.append(jnp.stack([_blockdiag_t_to_state(fin[0]), _blockdiag_t_to_state(fin[1])], axis=1))
        st0 = jnp.stack([_state_to_blockdiag_t(state_gla[:, l, 0]), _state_to_blockdiag_t(state_gla[:, l, 1])])
        y_s, _ = _trunk_layer(y_s, pos if l == 0 else None, mod3, sample_row, lw, l, st0, False, tabs_s)
    new_state = jnp.stack(finals, axis=1).astype(x_prompt.dtype)
    return (y_p, y_s, new_state)
```

```python
import functools
import math

import numpy as np
import jax
import jax.numpy as jnp
from jax import lax
from jax.experimental import pallas as pl
from jax.experimental.pallas import tpu as pltpu

F32 = jnp.float32
BF16 = jnp.bfloat16

D_MODEL = 1024
DEPTH = 2
GRID_W = 64
HEAD_W = 256
GLA_HEADS = 4
GLA_DK = 64
GLA_LR = 16
GLA_TAU = 16.0
GLA_CHUNK = 64
GMLP_GROUPS = 4
GMLP_CHUNK = 128
FNET_GROUPS = 4
FNET_CH = 64
N_EXPERTS = 64
TOP_K = 8
N_GROUPS = 8
TOPK_GROUPS = 4
EXPERT_FF = 256
ROUTED_SCALE = 2.5
DEEPNORM_ALPHA = (2 * DEPTH) ** 0.25
LN_EPS = 1e-5
RMS_EPS = 1e-6
POS_BASE = 10000.0

COL_Q, COL_K, COL_V, COL_G, COL_SGU, COL_SGV, COL_CVB, COL_CVC, COL_CVX = range(9)
LR_W = 128
STORE_W = 9 * HEAD_W + LR_W
PROJ_W = STORE_W + HEAD_W
COL_LR = (9 * HEAD_W) // LR_W
HALO = 16

LANES = 128
PACK_W = D_MODEL // 2
EBLK_MAX = 1024
LBLK = 256
CHUNK = 8
BIG_SHIFT = 3
BIG_ROWS = CHUNK << BIG_SHIFT
SEG = 256
FFT_N1 = 64
VMEM_LIMIT = 56 * 1024 * 1024


def _cparams(*sem):
    return pltpu.CompilerParams(dimension_semantics=sem, vmem_limit_bytes=VMEM_LIMIT)


def _ln(x):
    mu = jnp.mean(x, axis=-1, keepdims=True)
    xc = x - mu
    var = jnp.mean(xc * xc, axis=-1, keepdims=True)
    return xc * lax.rsqrt(var + LN_EPS)


def _sigmoid(x):
    return 1.0 / (1.0 + jnp.exp(-x))


def _silu(x):
    return x * _sigmoid(x)


def _pack_bf16_pairs(x):
    w = x.shape[1] // 2
    lo = lax.bitcast_convert_type(x[:, :w].astype(BF16).astype(F32), jnp.uint32)
    hi = lax.bitcast_convert_type(x[:, w:].astype(BF16).astype(F32), jnp.uint32)
    return (lo >> 16) | (hi & jnp.uint32(0xFFFF0000))


def _pack_exact_bf16_pairs(x):
    w = x.shape[1] // 2
    lo = lax.bitcast_convert_type(x[:, :w], jnp.uint32)
    hi = lax.bitcast_convert_type(x[:, w:], jnp.uint32)
    return (lo >> 16) | hi


def _unpack_bf16_pairs(u):
    lo = lax.bitcast_convert_type(u << 16, F32)
    hi = lax.bitcast_convert_type(u & jnp.uint32(0xFFFF0000), F32)
    return jnp.concatenate([lo, hi], axis=1)


def _dot(a, b):
    return jnp.dot(a, b, preferred_element_type=F32)


def _dot_nt(a, b):
    return lax.dot_general(a, b, (((1,), (1,)), ((), ())), preferred_element_type=F32)


def _dot_tn(a, b):
    return lax.dot_general(a, b, (((0,), (0,)), ((), ())), preferred_element_type=F32)


def _ada_kernel(c_ref, w_ref, b_ref, o_ref):
    c = c_ref[...]
    o_ref[...] = _dot(_silu(c).astype(BF16), w_ref[...].astype(BF16)) + b_ref[...]


def _ada_mod(cond8, w_ada, b_ada):
    n_l, d, w6 = w_ada.shape
    tn = 1536
    return pl.pallas_call(
        _ada_kernel,
        grid=(n_l, w6 // tn),
        in_specs=[pl.BlockSpec((8, d), lambda l, j: (0, 0)),
                  pl.BlockSpec((None, d, tn), lambda l, j: (l, 0, j)),
                  pl.BlockSpec((None, 1, tn), lambda l, j: (l, 0, j))],
        out_specs=pl.BlockSpec((None, 8, tn), lambda l, j: (l, 0, j)),
        out_shape=jax.ShapeDtypeStruct((n_l, 8, w6), F32),
        compiler_params=_cparams("parallel", "parallel"),
        name="ada_mod",
    )(cond8, w_ada, b_ada.reshape(n_l, 1, w6))


def _in_proj_kernel(*refs, has_pos):
    if has_pos:
        x_ref, pos_ref, mod_ref, w_ref, cs_ref, proj_ref, zr_ref, zi_ref, x0_ref = refs
        x = x_ref[...] + pos_ref[...]
        x0_ref[...] = x
    else:
        x_ref, mod_ref, w_ref, cs_ref, proj_ref, zr_ref, zi_ref = refs
        x = x_ref[...]
    mod = mod_ref[...]
    sh1 = mod[:, 0:D_MODEL]
    sc1 = mod[:, D_MODEL:2 * D_MODEL]
    h = _ln(x) * (1.0 + sc1) + sh1
    proj = _dot(h.astype(BF16), w_ref[...])
    proj_ref[...] = proj[:, :STORE_W].astype(BF16)
    ft = proj[:, STORE_W:].astype(BF16)
    z = _dot(ft, cs_ref[...])
    zr_ref[...] = z[:, :HEAD_W].astype(BF16)
    zi_ref[...] = z[:, HEAD_W:].astype(BF16)


def _in_proj(x2d, pos, mod3, mod_row, w_in_p, cs, tm):
    t = x2d.shape[0]
    tm = min(tm, t)
    in_specs = [pl.BlockSpec((tm, D_MODEL), lambda i: (i, 0))]
    args = [x2d]
    out_shape = [jax.ShapeDtypeStruct((t, STORE_W), BF16),
                 jax.ShapeDtypeStruct((t, HEAD_W), BF16),
                 jax.ShapeDtypeStruct((t, HEAD_W), BF16)]
    out_specs = [pl.BlockSpec((tm, STORE_W), lambda i: (i, 0)),
                 pl.BlockSpec((tm, HEAD_W), lambda i: (i, 0)),
                 pl.BlockSpec((tm, HEAD_W), lambda i: (i, 0))]
    if pos is not None:
        n_pos = pos.shape[0] // tm
        in_specs.append(pl.BlockSpec((tm, D_MODEL), lambda i: (i % n_pos, 0)))
        args.append(pos)
        out_shape.append(jax.ShapeDtypeStruct((t, D_MODEL), F32))
        out_specs.append(pl.BlockSpec((tm, D_MODEL), lambda i: (i, 0)))
    in_specs += [pl.BlockSpec((None, 1, 6 * D_MODEL), lambda i: (mod_row(i, tm), 0, 0)),
                 pl.BlockSpec((D_MODEL, PROJ_W), lambda i: (0, 0)),
                 pl.BlockSpec((HEAD_W, 2 * HEAD_W), lambda i: (0, 0))]
    args += [mod3, w_in_p, cs]
    return pl.pallas_call(
        functools.partial(_in_proj_kernel, has_pos=pos is not None),
        grid=(t // tm,),
        in_specs=in_specs,
        out_specs=out_specs,
        out_shape=out_shape,
        compiler_params=_cparams("parallel"),
        name="in_proj",
    )(*args)


def _gla_masks(reverse):
    i = np.arange(SEG)
    same_chunk = (i[:, None] // GLA_CHUNK) == (i[None, :] // GLA_CHUNK)
    tri = same_chunk & ((i[None, :] >= i[:, None]) if reverse else (i[None, :] <= i[:, None]))
    l_idx = i[:, None] % GLA_CHUNK
    m_idx = np.arange(GLA_CHUNK)[None, :]
    causal = (m_idx >= l_idx) if reverse else (m_idx <= l_idx)
    return (jnp.asarray(tri, BF16), jnp.asarray(same_chunk, BF16), jnp.asarray(same_chunk, F32),
            jnp.asarray(causal, F32))


def _gla_segment(q, k, v, pre, st_ref, o_ref, masks, reverse):
    seg = q.shape[0]
    n_chunks = seg // GLA_CHUNK
    la = (jnp.minimum(pre, 0.0) - jnp.log1p(jnp.exp(-jnp.abs(pre)))) * (1.0 / GLA_TAU)

    tri_m, ones_m, bd, causal = masks
    hi = la.astype(BF16)
    lo = (la - hi.astype(F32)).astype(BF16)
    b = _dot(tri_m, hi) + _dot(tri_m, lo)
    btot = _dot(ones_m, hi) + _dot(ones_m, lo)

    q_dec = q * (GLA_DK ** -0.5) * jnp.exp(b)
    k_inv = (k * jnp.exp(-b)).astype(BF16)
    k_end = (k * jnp.exp(btot - b)).astype(BF16)
    dec = jnp.exp(btot)
    vb = v.astype(BF16)
    keep = causal > 0.5

    st = st_ref[...]
    order = range(n_chunks - 1, -1, -1) if reverse else range(n_chunks)
    for ci in order:
        sl = slice(ci * GLA_CHUNK, (ci + 1) * GLA_CHUNK)
        qd = q_dec[sl]
        qbd = (jnp.concatenate([qd] * GLA_HEADS, axis=0) * bd).astype(BF16)
        a = _dot_nt(qbd, k_inv[sl])
        a = jnp.where(keep, a, 0.0)
        rr = _dot(a.astype(BF16), vb[sl])
        o = _dot_nt(qd.astype(BF16), st.astype(BF16))
        for h in range(GLA_HEADS):
            hs = slice(h * GLA_CHUNK, (h + 1) * GLA_CHUNK)
            o = o + rr[hs] * bd[hs]
        o_ref[sl, :] = o
        kvt = _dot_tn(vb[sl], k_end[sl])
        st = st * dec[ci * GLA_CHUNK:ci * GLA_CHUNK + 1, :] + kvt * bd
    st_ref[...] = st


def _gla_kernel(*refs, has_init, emit_final):
    qf, kf, vf, lrf, qb, kb, vb, lrb, wah_ref, wal_ref, ba_ref = refs[:11]
    mask_refs = refs[11:19]
    rest = refs[19:]
    if has_init:
        s0f, s0b = rest[:2]
        rest = rest[2:]
    of_ref, ob_ref = rest[:2]
    rest = rest[2:]
    if emit_final:
        sff, sfb = rest[:2]
        rest = rest[2:]
    stf, stb = rest

    s = pl.program_id(1)

    @pl.when(s == 0)
    def _():
        if has_init:
            stf[...] = s0f[...]
            stb[...] = s0b[...]
        else:
            stf[...] = jnp.zeros_like(stf)
            stb[...] = jnp.zeros_like(stb)

    def decay_pre(lr_ref):
        lr = lr_ref[...]
        return _dot(lr, wah_ref[...]) + _dot(lr, wal_ref[...]) + ba_ref[...]

    f32 = lambda ref: ref[...].astype(F32)

    masks_f = tuple(m[...] for m in mask_refs[:4])
    masks_b = tuple(m[...] for m in mask_refs[4:])
    _gla_segment(f32(qf), f32(kf), f32(vf), decay_pre(lrf)[:, :HEAD_W], stf, of_ref, masks_f, reverse=False)
    _gla_segment(f32(qb), f32(kb), f32(vb), decay_pre(lrb)[:, HEAD_W:], stb, ob_ref, masks_b, reverse=True)

    if emit_final:
        @pl.when(s == pl.num_programs(1) - 1)
        def _():
            sff[...] = stf[...]
            sfb[...] = stb[...]


def _gla(proj3, wa_hi, wa_lo, ba, st0, emit_final):
    bsz, n, _ = proj3.shape
    nseg = n // SEG

    def col(cb, width=HEAD_W, rev=False):
        if rev:
            return pl.BlockSpec((None, SEG, width), lambda b, s: (b, nseg - 1 - s, cb))
        return pl.BlockSpec((None, SEG, width), lambda b, s: (b, s, cb))

    in_specs = [col(COL_Q), col(COL_K), col(COL_V), col(COL_LR, LR_W),
                col(COL_Q, rev=True), col(COL_K, rev=True), col(COL_V, rev=True), col(COL_LR, LR_W, rev=True),
                pl.BlockSpec((LR_W, 2 * HEAD_W), lambda b, s: (0, 0)),
                pl.BlockSpec((LR_W, 2 * HEAD_W), lambda b, s: (0, 0)),
                pl.BlockSpec((1, 2 * HEAD_W), lambda b, s: (0, 0))]
    masks = _gla_masks(False) + _gla_masks(True)
    in_specs += [pl.BlockSpec(m.shape, lambda b, s: (0, 0)) for m in masks]
    args = [proj3] * 8 + [wa_hi, wa_lo, ba] + list(masks)
    st_spec = pl.BlockSpec((None, HEAD_W, HEAD_W), lambda b, s: (b, 0, 0))
    if st0 is not None:
        in_specs += [st_spec, st_spec]
        args += [st0[0], st0[1]]
    out_shape = [jax.ShapeDtypeStruct((bsz, n, HEAD_W), F32)] * 2
    out_specs = [pl.BlockSpec((None, SEG, HEAD_W), lambda b, s: (b, s, 0)),
                 pl.BlockSpec((None, SEG, HEAD_W), lambda b, s: (b, nseg - 1 - s, 0))]
    if emit_final:
        out_shape += [jax.ShapeDtypeStruct((bsz, HEAD_W, HEAD_W), F32)] * 2
        out_specs += [st_spec, st_spec]
    return pl.pallas_call(
        functools.partial(_gla_kernel, has_init=st0 is not None, emit_final=emit_final),
        grid=(bsz, nseg),
        in_specs=in_specs,
        out_specs=out_specs,
        out_shape=out_shape,
        scratch_shapes=[pltpu.VMEM((HEAD_W, HEAD_W), F32), pltpu.VMEM((HEAD_W, HEAD_W), F32)],
        compiler_params=_cparams("parallel", "arbitrary"),
        name="gla",
    )(*args)


def _fft_direct_kernel(zr_ref, zi_ref, cn_ref, sn_ref, o_ref):
    o_ref[...] = _dot(cn_ref[...], zr_ref[...]) + _dot(sn_ref[...], zi_ref[...])


def _fft_direct(zr3, zi3, cn, sn):
    bsz, n, w = zr3.shape
    blk = pl.BlockSpec((None, n, w), lambda b: (b, 0, 0))
    tab = pl.BlockSpec((n, n), lambda b: (0, 0))
    return pl.pallas_call(
        _fft_direct_kernel,
        grid=(bsz,),
        in_specs=[blk, blk, tab, tab],
        out_specs=blk,
        out_shape=jax.ShapeDtypeStruct((bsz, n, w), F32),
        compiler_params=_cparams("parallel"),
        name="fft_direct",
    )(zr3, zi3, cn, sn)


def _fft_a_kernel(zr_ref, zi_ref, c_ref, s_ref, gr_ref, gi_ref):
    zr = zr_ref[...]
    zi = zi_ref[...]
    cm = c_ref[...]
    sm = s_ref[...]
    gr_ref[...] = (_dot(cm, zr) + _dot(sm, zi)).astype(BF16)
    gi_ref[...] = (_dot(cm, zi) - _dot(sm, zr)).astype(BF16)


def _fft_c_kernel(gr_ref, gi_ref, mc_ref, ms_ref, o_ref):
    for j in range(gr_ref.shape[0]):
        o_ref[:, j, :] = _dot(mc_ref[j], gr_ref[j]) + _dot(ms_ref[j], gi_ref[j])


def _fft_two_stage(zr3, zi3, tabs):
    bsz, n, w = zr3.shape
    n1 = FFT_N1
    n2 = n // n1
    c1, s1, mc, ms = tabs
    tn = 2048
    wide = n2 * w
    blk = pl.BlockSpec((None, n1, tn), lambda b, j: (b, 0, j))
    tab = pl.BlockSpec((n1, n1), lambda b, j: (0, 0))
    gr, gi = pl.pallas_call(
        _fft_a_kernel,
        grid=(bsz, wide // tn),
        in_specs=[blk, blk, tab, tab],
        out_specs=[blk, blk],
        out_shape=[jax.ShapeDtypeStruct((bsz, n1, wide), BF16)] * 2,
        compiler_params=_cparams("parallel", "parallel"),
        name="fft_stage_a",
    )(zr3.reshape(bsz, n1, wide), zi3.reshape(bsz, n1, wide), c1, s1)
    kb = 8
    gblk = pl.BlockSpec((None, kb, n2, w), lambda b, j: (b, j, 0, 0))
    mblk = pl.BlockSpec((kb, n2, n2), lambda b, j: (j, 0, 0))
    out = pl.pallas_call(
        _fft_c_kernel,
        grid=(bsz, n1 // kb),
        in_specs=[gblk, gblk, mblk, mblk],
        out_specs=pl.BlockSpec((None, n2, kb, w), lambda b, j: (b, 0, j, 0)),
        out_shape=jax.ShapeDtypeStruct((bsz, n2, n1, w), F32),
        compiler_params=_cparams("parallel", "parallel"),
        name="fft_stage_c",
    )(gr.reshape(bsz, n1, n2, w), gi.reshape(bsz, n1, n2, w), mc, ms)
    return out.reshape(bsz, n, w)


def _mix_masks(tm):
    h = np.arange(HEAD_W) // GLA_DK
    head_mean = (h[:, None] == h[None, :]) / GLA_DK
    rg = np.arange(GMLP_GROUPS * GMLP_CHUNK) // GMLP_CHUNK
    cg = np.arange(HEAD_W) // (HEAD_W // GMLP_GROUPS)
    t = np.arange(tm)
    e = np.arange(N_EXPERTS)
    return (jnp.asarray(head_mean, BF16), jnp.asarray(rg[:, None] == cg[None, :], F32),
            jnp.asarray(t[:, None] < t[None, :], BF16), jnp.asarray(e[None, :] < e[:, None], BF16))


def _route(logits, bias, before, lower):
    tm = logits.shape[1]
    s = _sigmoid(logits)
    biased = s + bias
    neg = -jnp.inf
    rows = lax.broadcasted_iota(jnp.int32, (8, tm), 0)

    def first_argmax(x, ids, sentinel):
        m = jnp.max(x, axis=0, keepdims=True)
        return m, jnp.min(jnp.where(x == m, ids, sentinel), axis=0, keepdims=True)

    gs_rows = []
    for g in range(N_GROUPS):
        x = biased[8 * g:8 * g + 8]
        m1, i1 = first_argmax(x, rows, 8)
        m2 = jnp.max(jnp.where(rows == i1, neg, x), axis=0, keepdims=True)
        gs_rows.append(m1 + m2)
    gs = jnp.concatenate(gs_rows, axis=0)
    gsel = jnp.zeros((N_GROUPS, tm), F32)
    for _ in range(TOPK_GROUPS):
        _, i = first_argmax(gs, rows, 8)
        hit = rows == i
        gsel = jnp.where(hit, 1.0, gsel)
        gs = jnp.where(hit, neg, gs)

    xs = [jnp.where(gsel[g:g + 1] > 0.0, biased[8 * g:8 * g + 8], neg) for g in range(N_GROUPS)]
    ids = [rows + 8 * g for g in range(N_GROUPS)]
    sel = [jnp.zeros((8, tm), F32) for _ in range(N_GROUPS)]
    eids = []
    for _ in range(TOP_K):
        m = xs[0]
        for g in range(1, N_GROUPS):
            m = jnp.maximum(m, xs[g])
        m = jnp.max(m, axis=0, keepdims=True)
        cand = jnp.where(xs[0] == m, ids[0], N_EXPERTS)
        for g in range(1, N_GROUPS):
            cand = jnp.minimum(cand, jnp.where(xs[g] == m, ids[g], N_EXPERTS))
        i = jnp.min(cand, axis=0, keepdims=True)
        eids.append(i)
        for g in range(N_GROUPS):
            hit = ids[g] == i
            sel[g] = jnp.where(hit, 1.0, sel[g])
            xs[g] = jnp.where(hit, neg, xs[g])

    sel_all = jnp.concatenate(sel, axis=0)
    seen = _dot(sel_all.astype(BF16), before)
    counts = jnp.sum(sel_all, axis=1, keepdims=True)

    n_chunks = jnp.ceil(counts * (1.0 / CHUNK))
    run_start = _dot(lower, jnp.broadcast_to(n_chunks, (N_EXPERTS, LANES)).astype(BF16))[:, 0:1] * CHUNK
    local_pos = seen + run_start

    def pick(k, table):
        acc = None
        for g in range(N_GROUPS):
            v = jnp.where(ids[g] == eids[k], table[8 * g:8 * g + 8], 0.0)
            acc = v if acc is None else acc + v
        return jnp.sum(acc, axis=0, keepdims=True)

    w_raw = [pick(k, s) for k in range(TOP_K)]
    lpos = [pick(k, local_pos) for k in range(TOP_K)]
    tot = w_raw[0]
    for k in range(1, TOP_K):
        tot = tot + w_raw[k]
    weights = [w / tot * ROUTED_SCALE for w in w_raw]
    return weights, lpos, n_chunks, run_start


def _mix_out_kernel(x_ref, of_ref, ob_ref, g_ref, su_ref, sv_ref, cb_ref, cc_ref, cx_ref,
                    ccp_ref, cxp_ref, ccn_ref, cxn_ref, ft_ref, mod_ref,
                    glag_ref, sgng_ref, sgnb_ref, wsgu_ref, bsgu_ref, wconv_ref, wout_ref,
                    ln1g_ref, ln1b_ref, wrh_ref, wrl_ref, rb_ref, hmean_ref, sgubd_ref, before_ref, lower_ref,
                    x1_ref, xw_ref, wrow_ref, lpos_ref, tab_ref, cnt_ref, carry_ref, *, seq_len):
    tm = x_ref.shape[0]
    i = pl.program_id(0)

    @pl.when(i == 0)
    def _():
        carry_ref[...] = jnp.zeros_like(carry_ref)

    mod = mod_ref[...]
    g1 = mod[:, 2 * D_MODEL:3 * D_MODEL]
    sh2 = mod[:, 3 * D_MODEL:4 * D_MODEL]
    sc2 = mod[:, 4 * D_MODEL:5 * D_MODEL]

    o = of_ref[...] + ob_ref[...]
    head_mean = hmean_ref[...]
    o2 = o * o
    o2_hi = o2.astype(BF16)
    o2_lo = (o2 - o2_hi.astype(F32)).astype(BF16)
    ms = _dot(o2_hi, head_mean) + _dot(o2_lo, head_mean)
    y_gla = o * lax.rsqrt(ms + RMS_EPS) * glag_ref[...] * _silu(g_ref[...].astype(F32))

    vn = _ln(sv_ref[...].astype(F32)) * sgng_ref[...] + sgnb_ref[...]
    sgu_bd = sgubd_ref[...]
    sp_parts = []
    for j in range(tm // GMLP_CHUNK):
        vc = vn[j * GMLP_CHUNK:(j + 1) * GMLP_CHUNK]
        vbd = (jnp.concatenate([vc] * GMLP_GROUPS, axis=0) * sgu_bd).astype(BF16)
        sp_parts.append(_dot(wsgu_ref[...], vbd) + bsgu_ref[...])
    y_sgu = su_ref[...].astype(F32) * jnp.concatenate(sp_parts, axis=0)

    z = cc_ref[...].astype(F32) * cx_ref[...].astype(F32)
    z_before = (ccp_ref[...].astype(F32) * cxp_ref[...].astype(F32))[HALO - 1:HALO, :]
    z_after = (ccn_ref[...].astype(F32) * cxn_ref[...].astype(F32))[0:1, :]
    row = lax.broadcasted_iota(jnp.int32, (tm, HEAD_W), 0)
    pos = (i * tm + row) & (seq_len - 1)
    z_prev = jnp.where(row == 0, z_before, pltpu.roll(z, 1, 0))
    z_next = jnp.where(row == tm - 1, z_after, pltpu.roll(z, tm - 1, 0))
    z_prev = jnp.where(pos == 0, 0.0, z_prev)
    z_next = jnp.where(pos == seq_len - 1, 0.0, z_next)
    wconv = wconv_ref[...]
    y_conv = cb_ref[...].astype(F32) * (wconv[0:1] * z_prev + wconv[1:2] * z + wconv[2:3] * z_next)

    y = (_dot(y_gla.astype(BF16), wout_ref[0:HEAD_W, :])
         + _dot(y_sgu.astype(BF16), wout_ref[HEAD_W:2 * HEAD_W, :])
         + _dot(y_conv.astype(BF16), wout_ref[2 * HEAD_W:3 * HEAD_W, :])
         + _dot(ft_ref[...].astype(BF16), wout_ref[3 * HEAD_W:4 * HEAD_W, :]))
    x1 = _ln(DEEPNORM_ALPHA * x_ref[...] + g1 * y) * ln1g_ref[...] + ln1b_ref[...]
    x1_ref[...] = x1
    h2 = _ln(x1) * (1.0 + sc2) + sh2
    xw_ref[...] = _pack_bf16_pairs(h2)

    h2_hi = h2.astype(BF16)
    h2_lo = (h2 - h2_hi.astype(F32)).astype(BF16)
    logits = _dot(h2_hi, wrh_ref[...]) + _dot(h2_hi, wrl_ref[...]) + _dot(h2_lo, wrh_ref[...])
    weights, lpos, n_chunks, run_start = _route(logits.T[:N_EXPERTS], rb_ref[...], before_ref[...], lower_ref[...])
    lpos_ref[...] = jnp.concatenate(lpos, axis=0).astype(jnp.int32)
    wrow_ref[...] = jnp.concatenate(weights, axis=0)
    carry = carry_ref[:, 0:1]
    lane = lax.broadcasted_iota(jnp.int32, (N_EXPERTS, LANES), 1)
    total = jnp.sum(n_chunks, axis=0, keepdims=True)
    cols = jnp.where(lane == 0, n_chunks, jnp.where(lane == 1, run_start,
                                                   jnp.where(lane == 2, carry, jnp.where(lane == 3, total, 0.0))))
    tab = jnp.concatenate([cols, jnp.zeros((LANES - N_EXPERTS, LANES), F32)], axis=0).T
    tab_ref[...] = tab[0:8].astype(jnp.int32)
    new_carry = carry_ref[...] + n_chunks
    carry_ref[...] = new_carry
    cnt_ref[...] = new_carry


def _mix_out(x2d, of2d, ob2d, proj, yft2d, mod3, mod_row, lw, seq_len, tm):
    t = x2d.shape[0]
    tm = min(tm, t)
    nt8 = t // HALO
    rows8 = tm // HALO

    def col(cb):
        return pl.BlockSpec((tm, HEAD_W), lambda i: (i, cb))

    def halo_prev(cb):
        return pl.BlockSpec((HALO, HEAD_W), lambda i: (jnp.maximum(i * rows8 - 1, 0), cb))

    def halo_next(cb):
        return pl.BlockSpec((HALO, HEAD_W), lambda i: (jnp.minimum((i + 1) * rows8, nt8 - 1), cb))

    def full(a):
        return pl.BlockSpec(a.shape, lambda i: (0,) * a.ndim)

    tok_d = pl.BlockSpec((tm, D_MODEL), lambda i: (i, 0))
    tok_h = pl.BlockSpec((tm, HEAD_W), lambda i: (i, 0))
    weights = [lw["gla_norm_g"], lw["sgu_norm_g"], lw["sgu_norm_b"], lw["w_sgu_cat"], lw["b_sgu_full"],
               lw["w_conv"], lw["w_out"], lw["ln1_g"], lw["ln1_b"], lw["w_router_hi"], lw["w_router_lo"], lw["router_bias"]]
    weights += list(_mix_masks(tm))
    in_specs = ([tok_d, tok_h, tok_h, col(COL_G), col(COL_SGU), col(COL_SGV), col(COL_CVB), col(COL_CVC),
                 col(COL_CVX), halo_prev(COL_CVC), halo_prev(COL_CVX), halo_next(COL_CVC), halo_next(COL_CVX),
                 tok_h, pl.BlockSpec((None, 1, 6 * D_MODEL), lambda i: (mod_row(i, tm), 0, 0))]
                + [full(w) for w in weights])
    args = [x2d, of2d, ob2d] + [proj] * 10 + [yft2d, mod3] + weights
    return pl.pallas_call(
        functools.partial(_mix_out_kernel, seq_len=seq_len),
        grid=(t // tm,),
        in_specs=in_specs,
        out_specs=[tok_d,
                   pl.BlockSpec((tm, PACK_W), lambda i: (i, 0)),
                   pl.BlockSpec((None, TOP_K, tm), lambda i: (i, 0, 0)),
                   pl.BlockSpec((None, TOP_K, tm), lambda i: (i, 0, 0)),
                   pl.BlockSpec((None, 8, LANES), lambda i: (i, 0, 0)),
                   pl.BlockSpec((N_EXPERTS, LANES), lambda i: (0, 0))],
        out_shape=[jax.ShapeDtypeStruct((t, D_MODEL), F32),
                   jax.ShapeDtypeStruct((t, PACK_W), jnp.uint32),
                   jax.ShapeDtypeStruct((t // tm, TOP_K, tm), F32),
                   jax.ShapeDtypeStruct((t // tm, TOP_K, tm), jnp.int32),
                   jax.ShapeDtypeStruct((t // tm, 8, LANES), jnp.int32),
                   jax.ShapeDtypeStruct((N_EXPERTS, LANES), F32)],
        scratch_shapes=[pltpu.VMEM((N_EXPERTS, LANES), F32)],
        compiler_params=_cparams("arbitrary"),
        name="mix_out",
    )(*args)


def _local_rows(tm):
    need = tm * TOP_K + N_EXPERTS * (CHUNK - 1)
    return -(-need // LBLK) * LBLK


def _run_copy(local_ref, sorted_hbm, sem, local_row, sorted_row, to_sorted, rows=CHUNK):
    loc = local_ref.at[pl.ds(pl.multiple_of(local_row, CHUNK), rows)]
    srt = sorted_hbm.at[pl.ds(pl.multiple_of(sorted_row, CHUNK), rows)]
    return pltpu.make_async_copy(loc, srt, sem) if to_sorted else pltpu.make_async_copy(srt, loc, sem)


def _start_run_copies(tab_ref, gstart_ref, local_ref, sorted_hbm, sem, to_sorted):
    def per_expert(e, totals):
        n = tab_ref[0, e]
        l0 = tab_ref[1, e]
        g0 = (gstart_ref[e] + tab_ref[2, e]) * CHUNK
        n_big = n >> BIG_SHIFT
        n_small = n & ((1 << BIG_SHIFT) - 1)

        def big(j, c):
            _run_copy(local_ref, sorted_hbm, sem, l0 + j * BIG_ROWS, g0 + j * BIG_ROWS, to_sorted, BIG_ROWS).start()
            return c

        lax.fori_loop(0, n_big, big, 0)
        l1 = l0 + n_big * BIG_ROWS
        g1 = g0 + n_big * BIG_ROWS

        def small(j, c):
            _run_copy(local_ref, sorted_hbm, sem, l1 + j * CHUNK, g1 + j * CHUNK, to_sorted).start()
            return c

        lax.fori_loop(0, n_small, small, 0)
        return totals[0] + n_big, totals[1] + n_small

    return lax.fori_loop(0, N_EXPERTS, per_expert, (jnp.int32(0), jnp.int32(0)))


def _get_pending(pending_ref, s):
    return pending_ref[s, 0], pending_ref[s, 1]


def _set_pending(pending_ref, s, counts):
    pending_ref[s, 0] = counts[0]
    pending_ref[s, 1] = counts[1]


def _wait_run_copies(counts, local_ref, sorted_hbm, sem, to_sorted):
    batch = 8

    def wait_n(rows, reps):
        def body(j, c):
            for _ in range(reps):
                _run_copy(local_ref, sorted_hbm, sem, 0, 0, to_sorted, rows).wait()
            return c
        return body

    for count, rows in zip(counts, (BIG_ROWS, CHUNK)):
        lax.fori_loop(0, count >> 3, wait_n(rows, batch), 0)
        lax.fori_loop(0, count & (batch - 1), wait_n(rows, 1), 0)


def _moe_scatter_kernel(gstart_ref, gtail_ref, xw_ref, lpos_ref, tab_ref, sorted_hbm,
                        local_ref, sem, zsem, pending_ref, *, eblk):
    i = pl.program_id(0)
    n = pl.num_programs(0)
    slot = i & 1
    tm = xw_ref.shape[0]
    lrows = local_ref.shape[1]

    def zero_copy(e):
        off = pl.multiple_of(jnp.maximum(gtail_ref[e], 0), eblk)
        return pltpu.make_async_copy(local_ref.at[1, pl.ds(0, eblk)], sorted_hbm.at[pl.ds(off, eblk)], zsem)

    @pl.when(i == 0)
    def _():
        _set_pending(pending_ref, 0, (0, 0))
        _set_pending(pending_ref, 1, (0, 0))
        local_ref[1, 0:eblk, :] = jnp.zeros((eblk, PACK_W), jnp.uint32)

        def z_start(e, c):
            @pl.when(gtail_ref[e] >= 0)
            def _():
                zero_copy(e).start()
            return c

        def z_wait(e, c):
            @pl.when(gtail_ref[e] >= 0)
            def _():
                zero_copy(e).wait()
            return c

        lax.fori_loop(0, N_EXPERTS, z_start, 0)
        lax.fori_loop(0, N_EXPERTS, z_wait, 0)

    local = local_ref.at[slot]
    _wait_run_copies(_get_pending(pending_ref, slot), local, sorted_hbm, sem.at[slot], True)

    x = _unpack_bf16_pairs(xw_ref[...]).astype(BF16)
    lpos = lpos_ref[...].astype(jnp.int16)
    one = jnp.ones((LBLK, tm), BF16)
    used_rows = tab_ref[3, 0] * CHUNK

    def sort_block(b):
        riota = lax.broadcasted_iota(jnp.int16, (LBLK, tm), 0) + b * LBLK
        p = jnp.zeros((LBLK, tm), BF16)
        for k in range(TOP_K):
            p = jnp.where(riota == lpos[k:k + 1, :], one, p)
        local[b * LBLK:(b + 1) * LBLK, :] = _pack_exact_bf16_pairs(_dot(p, x))

    n_blocks = lrows // LBLK
    for b in range(n_blocks - 1):
        sort_block(b)
    pl.when(used_rows > (n_blocks - 1) * LBLK)(functools.partial(sort_block, n_blocks - 1))

    _set_pending(pending_ref, slot, _start_run_copies(tab_ref, gstart_ref, local, sorted_hbm, sem.at[slot], True))

    @pl.when(i == n - 1)
    def _():
        for s in range(2):
            _wait_run_copies(_get_pending(pending_ref, s), local_ref.at[s], sorted_hbm, sem.at[s], True)


def _moe_scatter(gstart, gtail, xw, lpos, tab, n_rows, tm, eblk):
    n_tiles = xw.shape[0] // tm
    lrows = _local_rows(tm)
    any_spec = pl.BlockSpec(memory_space=pl.ANY)
    return pl.pallas_call(
        functools.partial(_moe_scatter_kernel, eblk=eblk),
        grid_spec=pltpu.PrefetchScalarGridSpec(
            num_scalar_prefetch=2,
            grid=(n_tiles,),
            in_specs=[pl.BlockSpec((tm, PACK_W), lambda i, *_: (i, 0)),
                      pl.BlockSpec((None, TOP_K, tm), lambda i, *_: (i, 0, 0)),
                      pl.BlockSpec((None, 8, LANES), lambda i, *_: (i, 0, 0), memory_space=pltpu.SMEM)],
            out_specs=any_spec,
            scratch_shapes=[pltpu.VMEM((2, lrows, PACK_W), jnp.uint32), pltpu.SemaphoreType.DMA((2,)),
                            pltpu.SemaphoreType.DMA, pltpu.SMEM((2, 2), jnp.int32)]),
        out_shape=jax.ShapeDtypeStruct((n_rows, PACK_W), jnp.uint32),
        compiler_params=_cparams("arbitrary"),
        name="moe_scatter",
    )(gstart, gtail, xw, lpos, tab)


def _expert2_kernel(be_ref, nu_ref, xs_ref, wg_ref, wu_ref, wd_ref, ys_ref, wgu_b, wd_b):
    j = pl.program_id(0)

    @pl.when(jnp.logical_or(j == 0, be_ref[j] != be_ref[jnp.maximum(j - 1, 0)]))
    def _():
        wgu_b[:, :EXPERT_FF] = wg_ref[...].astype(BF16)
        wgu_b[:, EXPERT_FF:] = wu_ref[...].astype(BF16)
        wd_b[...] = wd_ref[...].astype(BF16)

    @pl.when(j < nu_ref[0])
    def _():
        x = _unpack_bf16_pairs(xs_ref[...]).astype(BF16)
        gu = _dot(x, wgu_b[...])
        a = _silu(gu[:, :EXPERT_FF]) * gu[:, EXPERT_FF:]
        ys_ref[...] = _pack_bf16_pairs(_dot(a.astype(BF16), wd_b[...]))


def _experts2(xs, block_expert, n_used, lw, layer, eblk):
    n_rows = xs.shape[0]
    ff = EXPERT_FF

    def blk(j, be, nu):
        return (jnp.minimum(j, jnp.maximum(nu[0] - 1, 0)), 0)

    def wblk(j, be, nu):
        return (layer, be[j], 0, 0)

    return pl.pallas_call(
        _expert2_kernel,
        grid_spec=pltpu.PrefetchScalarGridSpec(
            num_scalar_prefetch=2,
            grid=(n_used[0],),
            in_specs=[pl.BlockSpec((eblk, PACK_W), blk),
                      pl.BlockSpec((None, None, D_MODEL, ff), wblk),
                      pl.BlockSpec((None, None, D_MODEL, ff), wblk),
                      pl.BlockSpec((None, None, ff, D_MODEL), wblk)],
            out_specs=pl.BlockSpec((eblk, PACK_W), blk),
            scratch_shapes=[pltpu.VMEM((D_MODEL, 2 * ff), BF16), pltpu.VMEM((ff, D_MODEL), BF16)]),
        out_shape=jax.ShapeDtypeStruct((n_rows, PACK_W), jnp.uint32),
        compiler_params=_cparams("arbitrary"),
        name="moe_experts",
    )(block_expert, n_used, xs, lw["w_exp_gate"], lw["w_exp_up"], lw["w_exp_down"])


def _moe_combine_kernel(gstart_ref, x1_ref, xw_ref, lpos_ref, wrow_ref, tab_ref, tab_next_ref, mod_ref,
                        sg_ref, su_ref, sd_ref, ln2g_ref, ln2b_ref, sorted_hbm, o_ref,
                        local_ref, sem, pending_ref):
    i = pl.program_id(0)
    n = pl.num_programs(0)
    slot = i & 1
    tm = x1_ref.shape[0]
    lrows = local_ref.shape[1]

    @pl.when(i == 0)
    def _():
        local_ref[...] = jnp.zeros_like(local_ref)
        _set_pending(pending_ref, 0,
                     _start_run_copies(tab_ref, gstart_ref, local_ref.at[0], sorted_hbm, sem.at[0], False))

    @pl.when(i + 1 < n)
    def _():
        nxt = 1 - slot
        _set_pending(pending_ref, nxt, _start_run_copies(tab_next_ref, gstart_ref, local_ref.at[nxt], sorted_hbm,
                                                         sem.at[nxt], False))

    h = _unpack_bf16_pairs(xw_ref[...]).astype(BF16)
    a = _silu(_dot(h, sg_ref[...])) * _dot(h, su_ref[...])
    acc = _dot(a.astype(BF16), sd_ref[...])

    local = local_ref.at[slot]
    _wait_run_copies(_get_pending(pending_ref, slot), local, sorted_hbm, sem.at[slot], False)

    lpos = lpos_ref[...].astype(jnp.int16)
    wrow = wrow_ref[...].astype(BF16)

    def unsort_block(b, acc):
        riota = lax.broadcasted_iota(jnp.int16, (LBLK, tm), 0) + b * LBLK
        q = jnp.zeros((LBLK, tm), BF16)
        for k in range(TOP_K):
            q = jnp.where(riota == lpos[k:k + 1, :], jnp.broadcast_to(wrow[k:k + 1, :], (LBLK, tm)), q)
        y = _unpack_bf16_pairs(local[b * LBLK:(b + 1) * LBLK, :]).astype(BF16)
        return acc + _dot_tn(q, y)

    for b in range(lrows // LBLK):
        acc = unsort_block(b, acc)

    g2 = mod_ref[...][:, 5 * D_MODEL:6 * D_MODEL]
    u = DEEPNORM_ALPHA * x1_ref[...] + g2 * acc
    o_ref[...] = _ln(u) * ln2g_ref[...] + ln2b_ref[...]


def _moe_combine(gstart, x1, xw, lpos, wrow, tab, ys, mod3, mod_row, lw, tm):
    t = x1.shape[0]
    n_tiles = t // tm
    lrows = _local_rows(tm)

    def full(a):
        return pl.BlockSpec(a.shape, lambda i, *_: (0,) * a.ndim)

    tab_blk = lambda f: pl.BlockSpec((None, 8, LANES), f, memory_space=pltpu.SMEM)
    return pl.pallas_call(
        _moe_combine_kernel,
        grid_spec=pltpu.PrefetchScalarGridSpec(
            num_scalar_prefetch=1,
            grid=(n_tiles,),
            in_specs=[pl.BlockSpec((tm, D_MODEL), lambda i, *_: (i, 0)),
                      pl.BlockSpec((tm, PACK_W), lambda i, *_: (i, 0)),
                      pl.BlockSpec((None, TOP_K, tm), lambda i, *_: (i, 0, 0)),
                      pl.BlockSpec((None, TOP_K, tm), lambda i, *_: (i, 0, 0)),
                      tab_blk(lambda i, *_: (i, 0, 0)),
                      tab_blk(lambda i, *_: (jnp.minimum(i + 1, n_tiles - 1), 0, 0)),
                      pl.BlockSpec((None, 1, 6 * D_MODEL), lambda i, *_: (mod_row(i, tm), 0, 0)),
                      full(lw["w_sh_gate"]), full(lw["w_sh_up"]), full(lw["w_sh_down"]),
                      full(lw["ln2_g"]), full(lw["ln2_b"]),
                      pl.BlockSpec(memory_space=pl.ANY)],
            out_specs=pl.BlockSpec((tm, D_MODEL), lambda i, *_: (i, 0)),
            scratch_shapes=[pltpu.VMEM((2, lrows, PACK_W), jnp.uint32), pltpu.SemaphoreType.DMA((2,)),
                            pltpu.SMEM((2, 2), jnp.int32)]),
        out_shape=jax.ShapeDtypeStruct((t, D_MODEL), F32),
        compiler_params=_cparams("arbitrary"),
        name="moe_combine",
    )(gstart, x1, xw, lpos, wrow, tab, tab, mod3, lw["w_sh_gate"], lw["w_sh_up"], lw["w_sh_down"],
      lw["ln2_g"], lw["ln2_b"], ys)


def _moe2(x1, xw, wrow, lpos, tab, chunks_f, mod3, mod_row, lw, layer, tm):
    t = x1.shape[0]
    n_tiles = t // tm
    eblk = max(LBLK, min(EBLK_MAX, t * TOP_K // N_EXPERTS))
    region_rows = chunks_f[:, 0].astype(jnp.int32) * CHUNK
    padded = ((region_rows + eblk - 1) // eblk) * eblk
    end = jnp.cumsum(padded)
    start = end - padded
    n_blocks = -(-(t * TOP_K + n_tiles * N_EXPERTS * (CHUNK - 1)) // eblk) + N_EXPERTS
    n_used = end[-1] // eblk
    blk_row = jnp.minimum(jnp.arange(n_blocks, dtype=jnp.int32), jnp.maximum(n_used - 1, 0)) * eblk
    block_expert = jnp.minimum(jnp.sum((end[None, :] <= blk_row[:, None]).astype(jnp.int32), axis=1),
                               N_EXPERTS - 1).astype(jnp.int32)
    gstart = start // CHUNK
    gtail = jnp.where(padded > 0, end - eblk, -1)
    xs = _moe_scatter(gstart, gtail, xw, lpos, tab, n_blocks * eblk, tm, eblk)
    ys = _experts2(xs, block_expert, n_used.reshape(1), lw, layer, eblk)
    return _moe_combine(gstart, x1, xw, lpos, wrow, tab, ys, mod3, mod_row, lw, tm)


def _channel_dft_table():
    k = np.arange(FNET_CH, dtype=np.float64)
    ang = 2.0 * np.pi * np.outer(k, k) / FNET_CH
    eye = np.eye(FNET_GROUPS)
    return np.concatenate([np.kron(eye, np.cos(ang)), -np.kron(eye, np.sin(ang))], axis=1)


def _direct_dft_tables(n):
    k = np.arange(n, dtype=np.float64)
    ang = 2.0 * np.pi * (np.outer(k, k) % n) / n
    scale = 1.0 / math.sqrt(n * FNET_CH)
    return np.cos(ang) * scale, np.sin(ang) * scale


def _two_stage_dft_tables(n):
    n1 = FFT_N1
    n2 = n // n1
    a = np.arange(n1, dtype=np.float64)
    ang1 = 2.0 * np.pi * (np.outer(a, a) % n1) / n1
    k1 = np.arange(n1).reshape(n1, 1, 1)
    k2 = np.arange(n2).reshape(1, n2, 1)
    m2 = np.arange(n2).reshape(1, 1, n2)
    ang2 = 2.0 * np.pi * ((m2 * (k1 + n1 * k2)) % n) / n
    scale = 1.0 / math.sqrt(n * FNET_CH)
    return np.cos(ang1), np.sin(ang1), np.cos(ang2) * scale, np.sin(ang2) * scale


def _grid_sincos_table(rows, d):
    quarter = d // 4
    omega = 1.0 / (POS_BASE ** (np.arange(quarter, dtype=np.float64) / quarter))
    r = np.arange(rows, dtype=np.float64)[:, None] * omega
    c = np.arange(GRID_W, dtype=np.float64)[:, None] * omega
    return (np.concatenate([np.sin(r), np.cos(r)], axis=-1).astype(np.float32),
            np.concatenate([np.sin(c), np.cos(c)], axis=-1).astype(np.float32))


def _layer_weights(l, w_in, w_gla_a, b_gla_a, gla_norm_g, sgu_norm_g, sgu_norm_b, w_sgu, b_sgu, w_conv,
                   w_out, ln1_g, ln1_b, ln2_g, ln2_b, w_router, router_bias,
                   w_exp_gate, w_exp_up, w_exp_down, w_sh_gate, w_sh_up, w_sh_down):
    wi = w_in[l]
    lr0 = 4 * HEAD_W
    w_in_p = jnp.concatenate(
        [wi[:, :lr0], wi[:, lr0 + 2 * GLA_LR:lr0 + 2 * GLA_LR + 5 * HEAD_W], wi[:, lr0:lr0 + 2 * GLA_LR],
         jnp.zeros((D_MODEL, LR_W - 2 * GLA_LR), F32), wi[:, lr0 + 2 * GLA_LR + 5 * HEAD_W:]], axis=1).astype(BF16)
    wa_pad = jnp.zeros((LR_W, 2 * HEAD_W), F32)
    wa_pad = wa_pad.at[:GLA_LR, :HEAD_W].set(w_gla_a[l, 0])
    wa_pad = wa_pad.at[GLA_LR:2 * GLA_LR, HEAD_W:].set(w_gla_a[l, 1])
    w_router_pad = jnp.concatenate([w_router[l], jnp.zeros((D_MODEL, LANES - N_EXPERTS), F32)], axis=1)
    w_router_hi = w_router_pad.astype(BF16)
    row = lambda a: a[l].reshape(1, -1)
    return {
        "w_in_p": w_in_p,
        "wa_hi": wa_pad.astype(BF16), "wa_lo": (wa_pad - wa_pad.astype(BF16).astype(F32)).astype(BF16),
        "ba": jnp.concatenate([b_gla_a[l, 0], b_gla_a[l, 1]]).reshape(1, 2 * HEAD_W),
        "gla_norm_g": row(gla_norm_g), "sgu_norm_g": row(sgu_norm_g), "sgu_norm_b": row(sgu_norm_b),
        "w_sgu_cat": jnp.concatenate([w_sgu[l, g] for g in range(GMLP_GROUPS)], axis=1).astype(BF16),
        "b_sgu_full": jnp.repeat(b_sgu[l].T, HEAD_W // GMLP_GROUPS, axis=1),
        "w_conv": w_conv[l],
        "w_out": w_out[l].astype(BF16),
        "ln1_g": row(ln1_g), "ln1_b": row(ln1_b), "ln2_g": row(ln2_g), "ln2_b": row(ln2_b),
        "w_router_hi": w_router_hi, "w_router_lo": (w_router_pad - w_router_hi.astype(F32)).astype(BF16),
        "router_bias": router_bias[l].reshape(N_EXPERTS, 1),
        "w_exp_gate": w_exp_gate, "w_exp_up": w_exp_up, "w_exp_down": w_exp_down,
        "w_sh_gate": w_sh_gate[l].astype(BF16), "w_sh_up": w_sh_up[l].astype(BF16),
        "w_sh_down": w_sh_down[l].astype(BF16),
    }


def _state_to_blockdiag_t(s):
    bsz = s.shape[0]
    st = jnp.swapaxes(s, 2, 3)
    eye = jnp.eye(GLA_HEADS, dtype=s.dtype)
    return jnp.einsum("bhvd,hg->bhvgd", st, eye).reshape(bsz, HEAD_W, HEAD_W)


def _blockdiag_t_to_state(st):
    bsz = st.shape[0]
    s5 = st.reshape(bsz, GLA_HEADS, GLA_DK, GLA_HEADS, GLA_DK)
    diag = jnp.stack([s5[:, h, :, h, :] for h in range(GLA_HEADS)], axis=1)
    return jnp.swapaxes(diag, 2, 3)


def _trunk_layer(x3, pos, mod3, mod_row, lw, layer, st0, emit_final, tabs):
    bsz, n, _ = x3.shape
    t = bsz * n
    outs = _in_proj(x3.reshape(t, D_MODEL), pos, mod3, mod_row, lw["w_in_p"], tabs["cs"], tm=512)
    if pos is not None:
        proj, zr, zi, x2d = outs
    else:
        proj, zr, zi = outs
        x2d = x3.reshape(t, D_MODEL)
    gla_out = _gla(proj.reshape(bsz, n, STORE_W), lw["wa_hi"], lw["wa_lo"], lw["ba"], st0, emit_final)
    o_f, o_b = gla_out[:2]
    zr3 = zr.reshape(bsz, n, HEAD_W)
    zi3 = zi.reshape(bsz, n, HEAD_W)
    if "two_stage" in tabs:
        yft = _fft_two_stage(zr3, zi3, tabs["two_stage"])
    else:
        yft = _fft_direct(zr3, zi3, *tabs["direct"])
    x1, xw, wrow, lpos, tab, chunks = _mix_out(x2d, o_f.reshape(t, HEAD_W), o_b.reshape(t, HEAD_W), proj,
                                             yft.reshape(t, HEAD_W), mod3, mod_row, lw, seq_len=n, tm=256)
    x2 = _moe2(x1, xw, wrow, lpos, tab, chunks, mod3, mod_row, lw, layer, tm=min(256, t))
    return x2.reshape(bsz, n, D_MODEL), gla_out[2:]


def kernel(x_prompt, x_sample, c, state_gla, c_ctx, w_ada, b_ada, w_in, w_gla_a, b_gla_a, gla_norm_g, sgu_norm_g, sgu_norm_b, w_sgu, b_sgu, w_conv, w_out, ln1_g, ln1_b, ln2_g, ln2_b, w_router, router_bias, w_exp_gate, w_exp_up, w_exp_down, w_sh_gate, w_sh_up, w_sh_down):
    n_layers = w_ada.shape[0]
    bp, np_, _ = x_prompt.shape
    bs, ns, _ = x_sample.shape
    assert bs <= 7

    cond8 = jnp.concatenate([c_ctx[None, :], c, jnp.zeros((7 - bs, D_MODEL), F32)], axis=0)
    mod = _ada_mod(cond8, w_ada, b_ada)

    tabs_p = {"cs": jnp.asarray(_channel_dft_table(), BF16),
              "direct": tuple(jnp.asarray(a, BF16) for a in _direct_dft_tables(np_))}
    tabs_s = {"cs": tabs_p["cs"],
              "two_stage": tuple(jnp.asarray(a, BF16) for a in _two_stage_dft_tables(ns))}
    rtab, ctab = _grid_sincos_table(ns // GRID_W, D_MODEL)
    pos = jnp.concatenate([jnp.repeat(jnp.asarray(rtab), GRID_W, axis=0),
                           jnp.tile(jnp.asarray(ctab), (ns // GRID_W, 1))], axis=-1)

    prompt_row = lambda i, tm: 0
    sample_row = lambda i, tm: 1 + (i * tm) // ns

    y_p = x_prompt
    y_s = x_sample
    finals = []
    for l in range(n_layers):
        lw = _layer_weights(l, w_in, w_gla_a, b_gla_a, gla_norm_g, sgu_norm_g, sgu_norm_b, w_sgu, b_sgu,
                            w_conv, w_out, ln1_g, ln1_b, ln2_g, ln2_b, w_router, router_bias,
                            w_exp_gate, w_exp_up, w_exp_down, w_sh_gate, w_sh_up, w_sh_down)
        mod3 = mod[l].reshape(8, 1, 6 * D_MODEL)
        y_p, fin = _trunk_layer(y_p, None, mod3, prompt_row, lw, l, None, True, tabs_p)
        finals.append(jnp.stack([_blockdiag_t_to_state(fin[0]), _blockdiag_t_to_state(fin[1])], axis=1))
        st0 = jnp.stack([_state_to_blockdiag_t(state_gla[:, l, 0]), _state_to_blockdiag_t(state_gla[:, l, 1])])
        y_s, _ = _trunk_layer(y_s, pos if l == 0 else None, mod3, sample_row, lw, l, st0, False, tabs_s)
    new_state = jnp.stack(finals, axis=1).astype(x_prompt.dtype)
    return (y_p, y_s, new_state)
```

```python
import functools
import math

import numpy as np
import jax
import jax.numpy as jnp
from jax import lax
from jax.experimental import pallas as pl
from jax.experimental.pallas import tpu as pltpu

F32 = jnp.float32
BF16 = jnp.bfloat16

D_MODEL = 1024
DEPTH = 2
GRID_W = 64
HEAD_W = 256
GLA_HEADS = 4
GLA_DK = 64
GLA_LR = 16
GLA_TAU = 16.0
GLA_CHUNK = 64
GMLP_GROUPS = 4
GMLP_CHUNK = 128
FNET_GROUPS = 4
FNET_CH = 64
N_EXPERTS = 64
TOP_K = 8
N_GROUPS = 8
TOPK_GROUPS = 4
EXPERT_FF = 256
ROUTED_SCALE = 2.5
DEEPNORM_ALPHA = (2 * DEPTH) ** 0.25
LN_EPS = 1e-5
RMS_EPS = 1e-6
POS_BASE = 10000.0

COL_Q, COL_K, COL_V, COL_G, COL_SGU, COL_SGV, COL_CVB, COL_CVC, COL_CVX = range(9)
LR_W = 128
STORE_W = 9 * HEAD_W + LR_W
PROJ_W = STORE_W + HEAD_W
COL_LR = (9 * HEAD_W) // LR_W
HALO = 16

LANES = 128
PACK_W = D_MODEL // 2
EBLK_MAX = 1024
LBLK = 256
CHUNK = 8
BIG_SHIFT = 2
BIG_ROWS = CHUNK << BIG_SHIFT
SEG = 256
FFT_N1 = 64
VMEM_LIMIT = 56 * 1024 * 1024


def _cparams(*sem):
    return pltpu.CompilerParams(dimension_semantics=sem, vmem_limit_bytes=VMEM_LIMIT)


def _ln(x):
    mu = jnp.mean(x, axis=-1, keepdims=True)
    xc = x - mu
    var = jnp.mean(xc * xc, axis=-1, keepdims=True)
    return xc * lax.rsqrt(var + LN_EPS)


def _sigmoid(x):
    return 1.0 / (1.0 + jnp.exp(-x))


def _silu(x):
    return x * _sigmoid(x)


def _pack_bf16_pairs(x):
    w = x.shape[1] // 2
    lo = lax.bitcast_convert_type(x[:, :w].astype(BF16).astype(F32), jnp.uint32)
    hi = lax.bitcast_convert_type(x[:, w:].astype(BF16).astype(F32), jnp.uint32)
    return (lo >> 16) | (hi & jnp.uint32(0xFFFF0000))


def _pack_exact_bf16_pairs(x):
    w = x.shape[1] // 2
    lo = lax.bitcast_convert_type(x[:, :w], jnp.uint32)
    hi = lax.bitcast_convert_type(x[:, w:], jnp.uint32)
    return (lo >> 16) | hi


def _unpack_bf16_pairs(u):
    lo = lax.bitcast_convert_type(u << 16, F32)
    hi = lax.bitcast_convert_type(u & jnp.uint32(0xFFFF0000), F32)
    return jnp.concatenate([lo, hi], axis=1)


def _dot(a, b):
    return jnp.dot(a, b, preferred_element_type=F32)


def _dot_nt(a, b):
    return lax.dot_general(a, b, (((1,), (1,)), ((), ())), preferred_element_type=F32)


def _dot_tn(a, b):
    return lax.dot_general(a, b, (((0,), (0,)), ((), ())), preferred_element_type=F32)


def _ada_kernel(c_ref, w_ref, b_ref, o_ref):
    c = c_ref[...]
    o_ref[...] = _dot(_silu(c).astype(BF16), w_ref[...].astype(BF16)) + b_ref[...]


def _ada_mod(cond8, w_ada, b_ada):
    n_l, d, w6 = w_ada.shape
    tn = 1536
    return pl.pallas_call(
        _ada_kernel,
        grid=(n_l, w6 // tn),
        in_specs=[pl.BlockSpec((8, d), lambda l, j: (0, 0)),
                  pl.BlockSpec((None, d, tn), lambda l, j: (l, 0, j)),
                  pl.BlockSpec((None, 1, tn), lambda l, j: (l, 0, j))],
        out_specs=pl.BlockSpec((None, 8, tn), lambda l, j: (l, 0, j)),
        out_shape=jax.ShapeDtypeStruct((n_l, 8, w6), F32),
        compiler_params=_cparams("parallel", "parallel"),
        name="ada_mod",
    )(cond8, w_ada, b_ada.reshape(n_l, 1, w6))


def _in_proj_kernel(*refs, has_pos):
    if has_pos:
        x_ref, pos_ref, mod_ref, w_ref, cs_ref, proj_ref, zr_ref, zi_ref, x0_ref = refs
        x = x_ref[...] + pos_ref[...]
        x0_ref[...] = x
    else:
        x_ref, mod_ref, w_ref, cs_ref, proj_ref, zr_ref, zi_ref = refs
        x = x_ref[...]
    mod = mod_ref[...]
    sh1 = mod[:, 0:D_MODEL]
    sc1 = mod[:, D_MODEL:2 * D_MODEL]
    h = _ln(x) * (1.0 + sc1) + sh1
    proj = _dot(h.astype(BF16), w_ref[...])
    proj_ref[...] = proj[:, :STORE_W].astype(BF16)
    ft = proj[:, STORE_W:].astype(BF16)
    z = _dot(ft, cs_ref[...])
    zr_ref[...] = z[:, :HEAD_W].astype(BF16)
    zi_ref[...] = z[:, HEAD_W:].astype(BF16)


def _in_proj(x2d, pos, mod3, mod_row, w_in_p, cs, tm):
    t = x2d.shape[0]
    tm = min(tm, t)
    in_specs = [pl.BlockSpec((tm, D_MODEL), lambda i: (i, 0))]
    args = [x2d]
    out_shape = [jax.ShapeDtypeStruct((t, STORE_W), BF16),
                 jax.ShapeDtypeStruct((t, HEAD_W), BF16),
                 jax.ShapeDtypeStruct((t, HEAD_W), BF16)]
    out_specs = [pl.BlockSpec((tm, STORE_W), lambda i: (i, 0)),
                 pl.BlockSpec((tm, HEAD_W), lambda i: (i, 0)),
                 pl.BlockSpec((tm, HEAD_W), lambda i: (i, 0))]
    if pos is not None:
        n_pos = pos.shape[0] // tm
        in_specs.append(pl.BlockSpec((tm, D_MODEL), lambda i: (i % n_pos, 0)))
        args.append(pos)
        out_shape.append(jax.ShapeDtypeStruct((t, D_MODEL), F32))
        out_specs.append(pl.BlockSpec((tm, D_MODEL), lambda i: (i, 0)))
    in_specs += [pl.BlockSpec((None, 1, 6 * D_MODEL), lambda i: (mod_row(i, tm), 0, 0)),
                 pl.BlockSpec((D_MODEL, PROJ_W), lambda i: (0, 0)),
                 pl.BlockSpec((HEAD_W, 2 * HEAD_W), lambda i: (0, 0))]
    args += [mod3, w_in_p, cs]
    return pl.pallas_call(
        functools.partial(_in_proj_kernel, has_pos=pos is not None),
        grid=(t // tm,),
        in_specs=in_specs,
        out_specs=out_specs,
        out_shape=out_shape,
        compiler_params=_cparams("parallel"),
        name="in_proj",
    )(*args)


def _gla_masks(reverse):
    i = np.arange(SEG)
    same_chunk = (i[:, None] // GLA_CHUNK) == (i[None, :] // GLA_CHUNK)
    tri = same_chunk & ((i[None, :] >= i[:, None]) if reverse else (i[None, :] <= i[:, None]))
    l_idx = i[:, None] % GLA_CHUNK
    m_idx = np.arange(GLA_CHUNK)[None, :]
    causal = (m_idx >= l_idx) if reverse else (m_idx <= l_idx)
    return (jnp.asarray(tri, BF16), jnp.asarray(same_chunk, BF16), jnp.asarray(same_chunk, F32),
            jnp.asarray(causal, F32))


def _gla_segment(q, k, v, pre, st_ref, o_ref, masks, reverse):
    seg = q.shape[0]
    n_chunks = seg // GLA_CHUNK
    la = (jnp.minimum(pre, 0.0) - jnp.log1p(jnp.exp(-jnp.abs(pre)))) * (1.0 / GLA_TAU)

    tri_m, ones_m, bd, causal = masks
    hi = la.astype(BF16)
    lo = (la - hi.astype(F32)).astype(BF16)
    b = _dot(tri_m, hi) + _dot(tri_m, lo)
    btot = _dot(ones_m, hi) + _dot(ones_m, lo)

    q_dec = q * (GLA_DK ** -0.5) * jnp.exp(b)
    k_inv = (k * jnp.exp(-b)).astype(BF16)
    k_end = (k * jnp.exp(btot - b)).astype(BF16)
    dec = jnp.exp(btot)
    vb = v.astype(BF16)
    keep = causal > 0.5

    st = st_ref[...]
    order = range(n_chunks - 1, -1, -1) if reverse else range(n_chunks)
    for ci in order:
        sl = slice(ci * GLA_CHUNK, (ci + 1) * GLA_CHUNK)
        qd = q_dec[sl]
        qbd = (jnp.concatenate([qd] * GLA_HEADS, axis=0) * bd).astype(BF16)
        a = _dot_nt(qbd, k_inv[sl])
        a = jnp.where(keep, a, 0.0)
        rr = _dot(a.astype(BF16), vb[sl])
        o = _dot_nt(qd.astype(BF16), st.astype(BF16))
        for h in range(GLA_HEADS):
            hs = slice(h * GLA_CHUNK, (h + 1) * GLA_CHUNK)
            o = o + rr[hs] * bd[hs]
        o_ref[sl, :] = o
        kvt = _dot_tn(vb[sl], k_end[sl])
        st = st * dec[ci * GLA_CHUNK:ci * GLA_CHUNK + 1, :] + kvt * bd
    st_ref[...] = st


def _gla_kernel(*refs, has_init, emit_final):
    qf, kf, vf, lrf, qb, kb, vb, lrb, wah_ref, wal_ref, ba_ref = refs[:11]
    mask_refs = refs[11:19]
    rest = refs[19:]
    if has_init:
        s0f, s0b = rest[:2]
        rest = rest[2:]
    of_ref, ob_ref = rest[:2]
    rest = rest[2:]
    if emit_final:
        sff, sfb = rest[:2]
        rest = rest[2:]
    stf, stb = rest

    s = pl.program_id(1)

    @pl.when(s == 0)
    def _():
        if has_init:
            stf[...] = s0f[...]
            stb[...] = s0b[...]
        else:
            stf[...] = jnp.zeros_like(stf)
            stb[...] = jnp.zeros_like(stb)

    def decay_pre(lr_ref):
        lr = lr_ref[...]
        return _dot(lr, wah_ref[...]) + _dot(lr, wal_ref[...]) + ba_ref[...]

    f32 = lambda ref: ref[...].astype(F32)

    masks_f = tuple(m[...] for m in mask_refs[:4])
    masks_b = tuple(m[...] for m in mask_refs[4:])
    _gla_segment(f32(qf), f32(kf), f32(vf), decay_pre(lrf)[:, :HEAD_W], stf, of_ref, masks_f, reverse=False)
    _gla_segment(f32(qb), f32(kb), f32(vb), decay_pre(lrb)[:, HEAD_W:], stb, ob_ref, masks_b, reverse=True)

    if emit_final:
        @pl.when(s == pl.num_programs(1) - 1)
        def _():
            sff[...] = stf[...]
            sfb[...] = stb[...]


def _gla(proj3, wa_hi, wa_lo, ba, st0, emit_final):
    bsz, n, _ = proj3.shape
    nseg = n // SEG

    def col(cb, width=HEAD_W, rev=False):
        if rev:
            return pl.BlockSpec((None, SEG, width), lambda b, s: (b, nseg - 1 - s, cb))
        return pl.BlockSpec((None, SEG, width), lambda b, s: (b, s, cb))

    in_specs = [col(COL_Q), col(COL_K), col(COL_V), col(COL_LR, LR_W),
                col(COL_Q, rev=True), col(COL_K, rev=True), col(COL_V, rev=True), col(COL_LR, LR_W, rev=True),
                pl.BlockSpec((LR_W, 2 * HEAD_W), lambda b, s: (0, 0)),
                pl.BlockSpec((LR_W, 2 * HEAD_W), lambda b, s: (0, 0)),
                pl.BlockSpec((1, 2 * HEAD_W), lambda b, s: (0, 0))]
    masks = _gla_masks(False) + _gla_masks(True)
    in_specs += [pl.BlockSpec(m.shape, lambda b, s: (0, 0)) for m in masks]
    args = [proj3] * 8 + [wa_hi, wa_lo, ba] + list(masks)
    st_spec = pl.BlockSpec((None, HEAD_W, HEAD_W), lambda b, s: (b, 0, 0))
    if st0 is not None:
        in_specs += [st_spec, st_spec]
        args += [st0[0], st0[1]]
    out_shape = [jax.ShapeDtypeStruct((bsz, n, HEAD_W), F32)] * 2
    out_specs = [pl.BlockSpec((None, SEG, HEAD_W), lambda b, s: (b, s, 0)),
                 pl.BlockSpec((None, SEG, HEAD_W), lambda b, s: (b, nseg - 1 - s, 0))]
    if emit_final:
        out_shape += [jax.ShapeDtypeStruct((bsz, HEAD_W, HEAD_W), F32)] * 2
        out_specs += [st_spec, st_spec]
    return pl.pallas_call(
        functools.partial(_gla_kernel, has_init=st0 is not None, emit_final=emit_final),
        grid=(bsz, nseg),
        in_specs=in_specs,
        out_specs=out_specs,
        out_shape=out_shape,
        scratch_shapes=[pltpu.VMEM((HEAD_W, HEAD_W), F32), pltpu.VMEM((HEAD_W, HEAD_W), F32)],
        compiler_params=_cparams("parallel", "arbitrary"),
        name="gla",
    )(*args)


def _fft_direct_kernel(zr_ref, zi_ref, cn_ref, sn_ref, o_ref):
    o_ref[...] = _dot(cn_ref[...], zr_ref[...]) + _dot(sn_ref[...], zi_ref[...])


def _fft_direct(zr3, zi3, cn, sn):
    bsz, n, w = zr3.shape
    blk = pl.BlockSpec((None, n, w), lambda b: (b, 0, 0))
    tab = pl.BlockSpec((n, n), lambda b: (0, 0))
    return pl.pallas_call(
        _fft_direct_kernel,
        grid=(bsz,),
        in_specs=[blk, blk, tab, tab],
        out_specs=blk,
        out_shape=jax.ShapeDtypeStruct((bsz, n, w), F32),
        compiler_params=_cparams("parallel"),
        name="fft_direct",
    )(zr3, zi3, cn, sn)


def _fft_a_kernel(zr_ref, zi_ref, c_ref, s_ref, gr_ref, gi_ref):
    zr = zr_ref[...]
    zi = zi_ref[...]
    cm = c_ref[...]
    sm = s_ref[...]
    gr_ref[...] = (_dot(cm, zr) + _dot(sm, zi)).astype(BF16)
    gi_ref[...] = (_dot(cm, zi) - _dot(sm, zr)).astype(BF16)


def _fft_c_kernel(gr_ref, gi_ref, mc_ref, ms_ref, o_ref):
    for j in range(gr_ref.shape[0]):
        o_ref[:, j, :] = _dot(mc_ref[j], gr_ref[j]) + _dot(ms_ref[j], gi_ref[j])


def _fft_two_stage(zr3, zi3, tabs):
    bsz, n, w = zr3.shape
    n1 = FFT_N1
    n2 = n // n1
    c1, s1, mc, ms = tabs
    tn = 2048
    wide = n2 * w
    blk = pl.BlockSpec((None, n1, tn), lambda b, j: (b, 0, j))
    tab = pl.BlockSpec((n1, n1), lambda b, j: (0, 0))
    gr, gi = pl.pallas_call(
        _fft_a_kernel,
        grid=(bsz, wide // tn),
        in_specs=[blk, blk, tab, tab],
        out_specs=[blk, blk],
        out_shape=[jax.ShapeDtypeStruct((bsz, n1, wide), BF16)] * 2,
        compiler_params=_cparams("parallel", "parallel"),
        name="fft_stage_a",
    )(zr3.reshape(bsz, n1, wide), zi3.reshape(bsz, n1, wide), c1, s1)
    kb = 8
    gblk = pl.BlockSpec((None, kb, n2, w), lambda b, j: (b, j, 0, 0))
    mblk = pl.BlockSpec((kb, n2, n2), lambda b, j: (j, 0, 0))
    out = pl.pallas_call(
        _fft_c_kernel,
        grid=(bsz, n1 // kb),
        in_specs=[gblk, gblk, mblk, mblk],
        out_specs=pl.BlockSpec((None, n2, kb, w), lambda b, j: (b, 0, j, 0)),
        out_shape=jax.ShapeDtypeStruct((bsz, n2, n1, w), F32),
        compiler_params=_cparams("parallel", "parallel"),
        name="fft_stage_c",
    )(gr.reshape(bsz, n1, n2, w), gi.reshape(bsz, n1, n2, w), mc, ms)
    return out.reshape(bsz, n, w)


def _mix_masks(tm):
    h = np.arange(HEAD_W) // GLA_DK
    head_mean = (h[:, None] == h[None, :]) / GLA_DK
    rg = np.arange(GMLP_GROUPS * GMLP_CHUNK) // GMLP_CHUNK
    cg = np.arange(HEAD_W) // (HEAD_W // GMLP_GROUPS)
    t = np.arange(tm)
    e = np.arange(N_EXPERTS)
    return (jnp.asarray(head_mean, BF16), jnp.asarray(rg[:, None] == cg[None, :], F32),
            jnp.asarray(t[:, None] < t[None, :], BF16), jnp.asarray(e[None, :] < e[:, None], BF16))


def _route(logits, bias, before, lower):
    tm = logits.shape[1]
    s = _sigmoid(logits)
    biased = s + bias
    neg = -jnp.inf
    rows = lax.broadcasted_iota(jnp.int32, (8, tm), 0)

    def first_argmax(x, ids, sentinel):
        m = jnp.max(x, axis=0, keepdims=True)
        return m, jnp.min(jnp.where(x == m, ids, sentinel), axis=0, keepdims=True)

    gs_rows = []
    for g in range(N_GROUPS):
        x = biased[8 * g:8 * g + 8]
        m1, i1 = first_argmax(x, rows, 8)
        m2 = jnp.max(jnp.where(rows == i1, neg, x), axis=0, keepdims=True)
        gs_rows.append(m1 + m2)
    gs = jnp.concatenate(gs_rows, axis=0)
    gsel = jnp.zeros((N_GROUPS, tm), F32)
    for _ in range(TOPK_GROUPS):
        _, i = first_argmax(gs, rows, 8)
        hit = rows == i
        gsel = jnp.where(hit, 1.0, gsel)
        gs = jnp.where(hit, neg, gs)

    xs = [jnp.where(gsel[g:g + 1] > 0.0, biased[8 * g:8 * g + 8], neg) for g in range(N_GROUPS)]
    ids = [rows + 8 * g for g in range(N_GROUPS)]
    sel = [jnp.zeros((8, tm), F32) for _ in range(N_GROUPS)]
    eids = []
    for _ in range(TOP_K):
        m = xs[0]
        for g in range(1, N_GROUPS):
            m = jnp.maximum(m, xs[g])
        m = jnp.max(m, axis=0, keepdims=True)
        cand = jnp.where(xs[0] == m, ids[0], N_EXPERTS)
        for g in range(1, N_GROUPS):
            cand = jnp.minimum(cand, jnp.where(xs[g] == m, ids[g], N_EXPERTS))
        i = jnp.min(cand, axis=0, keepdims=True)
        eids.append(i)
        for g in range(N_GROUPS):
            hit = ids[g] == i
            sel[g] = jnp.where(hit, 1.0, sel[g])
            xs[g] = jnp.where(hit, neg, xs[g])

    sel_all = jnp.concatenate(sel, axis=0)
    seen = _dot(sel_all.astype(BF16), before)
    counts = jnp.sum(sel_all, axis=1, keepdims=True)

    n_chunks = jnp.ceil(counts * (1.0 / CHUNK))
    run_start = _dot(lower, jnp.broadcast_to(n_chunks, (N_EXPERTS, LANES)).astype(BF16))[:, 0:1] * CHUNK
    local_pos = seen + run_start

    def pick(k, table):
        acc = None
        for g in range(N_GROUPS):
            v = jnp.where(ids[g] == eids[k], table[8 * g:8 * g + 8], 0.0)
            acc = v if acc is None else acc + v
        return jnp.sum(acc, axis=0, keepdims=True)

    w_raw = [pick(k, s) for k in range(TOP_K)]
    lpos = [pick(k, local_pos) for k in range(TOP_K)]
    tot = w_raw[0]
    for k in range(1, TOP_K):
        tot = tot + w_raw[k]
    weights = [w / tot * ROUTED_SCALE for w in w_raw]
    return weights, lpos, n_chunks, run_start


def _mix_out_kernel(x_ref, of_ref, ob_ref, g_ref, su_ref, sv_ref, cb_ref, cc_ref, cx_ref,
                    ccp_ref, cxp_ref, ccn_ref, cxn_ref, ft_ref, mod_ref,
                    glag_ref, sgng_ref, sgnb_ref, wsgu_ref, bsgu_ref, wconv_ref, wout_ref,
                    ln1g_ref, ln1b_ref, wrh_ref, wrl_ref, rb_ref, hmean_ref, sgubd_ref, before_ref, lower_ref,
                    x1_ref, xw_ref, wrow_ref, lpos_ref, tab_ref, cnt_ref, carry_ref, *, seq_len):
    tm = x_ref.shape[0]
    i = pl.program_id(0)

    @pl.when(i == 0)
    def _():
        carry_ref[...] = jnp.zeros_like(carry_ref)

    mod = mod_ref[...]
    g1 = mod[:, 2 * D_MODEL:3 * D_MODEL]
    sh2 = mod[:, 3 * D_MODEL:4 * D_MODEL]
    sc2 = mod[:, 4 * D_MODEL:5 * D_MODEL]

    o = of_ref[...] + ob_ref[...]
    head_mean = hmean_ref[...]
    o2 = o * o
    o2_hi = o2.astype(BF16)
    o2_lo = (o2 - o2_hi.astype(F32)).astype(BF16)
    ms = _dot(o2_hi, head_mean) + _dot(o2_lo, head_mean)
    y_gla = o * lax.rsqrt(ms + RMS_EPS) * glag_ref[...] * _silu(g_ref[...].astype(F32))

    vn = _ln(sv_ref[...].astype(F32)) * sgng_ref[...] + sgnb_ref[...]
    sgu_bd = sgubd_ref[...]
    sp_parts = []
    for j in range(tm // GMLP_CHUNK):
        vc = vn[j * GMLP_CHUNK:(j + 1) * GMLP_CHUNK]
        vbd = (jnp.concatenate([vc] * GMLP_GROUPS, axis=0) * sgu_bd).astype(BF16)
        sp_parts.append(_dot(wsgu_ref[...], vbd) + bsgu_ref[...])
    y_sgu = su_ref[...].astype(F32) * jnp.concatenate(sp_parts, axis=0)

    z = cc_ref[...].astype(F32) * cx_ref[...].astype(F32)
    z_before = (ccp_ref[...].astype(F32) * cxp_ref[...].astype(F32))[HALO - 1:HALO, :]
    z_after = (ccn_ref[...].astype(F32) * cxn_ref[...].astype(F32))[0:1, :]
    row = lax.broadcasted_iota(jnp.int32, (tm, HEAD_W), 0)
    pos = (i * tm + row) & (seq_len - 1)
    z_prev = jnp.where(row == 0, z_before, pltpu.roll(z, 1, 0))
    z_next = jnp.where(row == tm - 1, z_after, pltpu.roll(z, tm - 1, 0))
    z_prev = jnp.where(pos == 0, 0.0, z_prev)
    z_next = jnp.where(pos == seq_len - 1, 0.0, z_next)
    wconv = wconv_ref[...]
    y_conv = cb_ref[...].astype(F32) * (wconv[0:1] * z_prev + wconv[1:2] * z + wconv[2:3] * z_next)

    y = (_dot(y_gla.astype(BF16), wout_ref[0:HEAD_W, :])
         + _dot(y_sgu.astype(BF16), wout_ref[HEAD_W:2 * HEAD_W, :])
         + _dot(y_conv.astype(BF16), wout_ref[2 * HEAD_W:3 * HEAD_W, :])
         + _dot(ft_ref[...].astype(BF16), wout_ref[3 * HEAD_W:4 * HEAD_W, :]))
    x1 = _ln(DEEPNORM_ALPHA * x_ref[...] + g1 * y) * ln1g_ref[...] + ln1b_ref[...]
    x1_ref[...] = x1
    h2 = _ln(x1) * (1.0 + sc2) + sh2
    xw_ref[...] = _pack_bf16_pairs(h2)

    h2_hi = h2.astype(BF16)
    h2_lo = (h2 - h2_hi.astype(F32)).astype(BF16)
    logits = _dot(h2_hi, wrh_ref[...]) + _dot(h2_hi, wrl_ref[...]) + _dot(h2_lo, wrh_ref[...])
    weights, lpos, n_chunks, run_start = _route(logits.T[:N_EXPERTS], rb_ref[...], before_ref[...], lower_ref[...])
    lpos_ref[...] = jnp.concatenate(lpos, axis=0).astype(jnp.int32)
    wrow_ref[...] = jnp.concatenate(weights, axis=0)
    carry = carry_ref[:, 0:1]
    lane = lax.broadcasted_iota(jnp.int32, (N_EXPERTS, LANES), 1)
    total = jnp.sum(n_chunks, axis=0, keepdims=True)
    cols = jnp.where(lane == 0, n_chunks, jnp.where(lane == 1, run_start,
                                                   jnp.where(lane == 2, carry, jnp.where(lane == 3, total, 0.0))))
    tab = jnp.concatenate([cols, jnp.zeros((LANES - N_EXPERTS, LANES), F32)], axis=0).T
    tab_ref[...] = tab[0:8].astype(jnp.int32)
    new_carry = carry_ref[...] + n_chunks
    carry_ref[...] = new_carry
    cnt_ref[...] = new_carry


def _mix_out(x2d, of2d, ob2d, proj, yft2d, mod3, mod_row, lw, seq_len, tm):
    t = x2d.shape[0]
    tm = min(tm, t)
    nt8 = t // HALO
    rows8 = tm // HALO

    def col(cb):
        return pl.BlockSpec((tm, HEAD_W), lambda i: (i, cb))

    def halo_prev(cb):
        return pl.BlockSpec((HALO, HEAD_W), lambda i: (jnp.maximum(i * rows8 - 1, 0), cb))

    def halo_next(cb):
        return pl.BlockSpec((HALO, HEAD_W), lambda i: (jnp.minimum((i + 1) * rows8, nt8 - 1), cb))

    def full(a):
        return pl.BlockSpec(a.shape, lambda i: (0,) * a.ndim)

    tok_d = pl.BlockSpec((tm, D_MODEL), lambda i: (i, 0))
    tok_h = pl.BlockSpec((tm, HEAD_W), lambda i: (i, 0))
    weights = [lw["gla_norm_g"], lw["sgu_norm_g"], lw["sgu_norm_b"], lw["w_sgu_cat"], lw["b_sgu_full"],
               lw["w_conv"], lw["w_out"], lw["ln1_g"], lw["ln1_b"], lw["w_router_hi"], lw["w_router_lo"], lw["router_bias"]]
    weights += list(_mix_masks(tm))
    in_specs = ([tok_d, tok_h, tok_h, col(COL_G), col(COL_SGU), col(COL_SGV), col(COL_CVB), col(COL_CVC),
                 col(COL_CVX), halo_prev(COL_CVC), halo_prev(COL_CVX), halo_next(COL_CVC), halo_next(COL_CVX),
                 tok_h, pl.BlockSpec((None, 1, 6 * D_MODEL), lambda i: (mod_row(i, tm), 0, 0))]
                + [full(w) for w in weights])
    args = [x2d, of2d, ob2d] + [proj] * 10 + [yft2d, mod3] + weights
    return pl.pallas_call(
        functools.partial(_mix_out_kernel, seq_len=seq_len),
        grid=(t // tm,),
        in_specs=in_specs,
        out_specs=[tok_d,
                   pl.BlockSpec((tm, PACK_W), lambda i: (i, 0)),
                   pl.BlockSpec((None, TOP_K, tm), lambda i: (i, 0, 0)),
                   pl.BlockSpec((None, TOP_K, tm), lambda i: (i, 0, 0)),
                   pl.BlockSpec((None, 8, LANES), lambda i: (i, 0, 0)),
                   pl.BlockSpec((N_EXPERTS, LANES), lambda i: (0, 0))],
        out_shape=[jax.ShapeDtypeStruct((t, D_MODEL), F32),
                   jax.ShapeDtypeStruct((t, PACK_W), jnp.uint32),
                   jax.ShapeDtypeStruct((t // tm, TOP_K, tm), F32),
                   jax.ShapeDtypeStruct((t // tm, TOP_K, tm), jnp.int32),
                   jax.ShapeDtypeStruct((t // tm, 8, LANES), jnp.int32),
                   jax.ShapeDtypeStruct((N_EXPERTS, LANES), F32)],
        scratch_shapes=[pltpu.VMEM((N_EXPERTS, LANES), F32)],
        compiler_params=_cparams("arbitrary"),
        name="mix_out",
    )(*args)


def _local_rows(tm):
    need = tm * TOP_K + N_EXPERTS * (CHUNK - 1)
    return -(-need // LBLK) * LBLK


def _run_copy(local_ref, sorted_hbm, sem, local_row, sorted_row, to_sorted, rows=CHUNK):
    loc = local_ref.at[pl.ds(pl.multiple_of(local_row, CHUNK), rows)]
    srt = sorted_hbm.at[pl.ds(pl.multiple_of(sorted_row, CHUNK), rows)]
    return pltpu.make_async_copy(loc, srt, sem) if to_sorted else pltpu.make_async_copy(srt, loc, sem)


def _start_run_copies(tab_ref, gstart_ref, local_ref, sorted_hbm, sem, to_sorted):
    def per_expert(e, totals):
        n = tab_ref[0, e]
        l0 = tab_ref[1, e]
        g0 = (gstart_ref[e] + tab_ref[2, e]) * CHUNK
        n_big = n >> BIG_SHIFT
        n_small = n & ((1 << BIG_SHIFT) - 1)

        def big(j, c):
            _run_copy(local_ref, sorted_hbm, sem, l0 + j * BIG_ROWS, g0 + j * BIG_ROWS, to_sorted, BIG_ROWS).start()
            return c

        lax.fori_loop(0, n_big, big, 0)
        l1 = l0 + n_big * BIG_ROWS
        g1 = g0 + n_big * BIG_ROWS

        def small(j, c):
            _run_copy(local_ref, sorted_hbm, sem, l1 + j * CHUNK, g1 + j * CHUNK, to_sorted).start()
            return c

        lax.fori_loop(0, n_small, small, 0)
        return totals[0] + n_big, totals[1] + n_small

    return lax.fori_loop(0, N_EXPERTS, per_expert, (jnp.int32(0), jnp.int32(0)))


def _get_pending(pending_ref, s):
    return pending_ref[s, 0], pending_ref[s, 1]


def _set_pending(pending_ref, s, counts):
    pending_ref[s, 0] = counts[0]
    pending_ref[s, 1] = counts[1]


def _wait_run_copies(counts, local_ref, sorted_hbm, sem, to_sorted):
    batch = 8

    def wait_n(rows, reps):
        def body(j, c):
            for _ in range(reps):
                _run_copy(local_ref, sorted_hbm, sem, 0, 0, to_sorted, rows).wait()
            return c
        return body

    for count, rows in zip(counts, (BIG_ROWS, CHUNK)):
        lax.fori_loop(0, count >> 3, wait_n(rows, batch), 0)
        lax.fori_loop(0, count & (batch - 1), wait_n(rows, 1), 0)


def _moe_scatter_kernel(gstart_ref, gtail_ref, xw_ref, lpos_ref, tab_ref, sorted_hbm,
                        local_ref, sem, zsem, pending_ref, *, eblk):
    i = pl.program_id(0)
    n = pl.num_programs(0)
    slot = i & 1
    tm = xw_ref.shape[0]
    lrows = local_ref.shape[1]

    def zero_copy(e):
        off = pl.multiple_of(jnp.maximum(gtail_ref[e], 0), eblk)
        return pltpu.make_async_copy(local_ref.at[1, pl.ds(0, eblk)], sorted_hbm.at[pl.ds(off, eblk)], zsem)

    @pl.when(i == 0)
    def _():
        _set_pending(pending_ref, 0, (0, 0))
        _set_pending(pending_ref, 1, (0, 0))
        local_ref[1, 0:eblk, :] = jnp.zeros((eblk, PACK_W), jnp.uint32)

        def z_start(e, c):
            @pl.when(gtail_ref[e] >= 0)
            def _():
                zero_copy(e).start()
            return c

        def z_wait(e, c):
            @pl.when(gtail_ref[e] >= 0)
            def _():
                zero_copy(e).wait()
            return c

        lax.fori_loop(0, N_EXPERTS, z_start, 0)
        lax.fori_loop(0, N_EXPERTS, z_wait, 0)

    local = local_ref.at[slot]
    _wait_run_copies(_get_pending(pending_ref, slot), local, sorted_hbm, sem.at[slot], True)

    x = _unpack_bf16_pairs(xw_ref[...]).astype(BF16)
    lpos = lpos_ref[...].astype(jnp.int16)
    one = jnp.ones((LBLK, tm), BF16)
    used_rows = tab_ref[3, 0] * CHUNK

    def sort_block(b):
        riota = lax.broadcasted_iota(jnp.int16, (LBLK, tm), 0) + b * LBLK
        p = jnp.zeros((LBLK, tm), BF16)
        for k in range(TOP_K):
            p = jnp.where(riota == lpos[k:k + 1, :], one, p)
        local[b * LBLK:(b + 1) * LBLK, :] = _pack_exact_bf16_pairs(_dot(p, x))

    n_blocks = lrows // LBLK
    for b in range(n_blocks - 1):
        sort_block(b)
    pl.when(used_rows > (n_blocks - 1) * LBLK)(functools.partial(sort_block, n_blocks - 1))

    _set_pending(pending_ref, slot, _start_run_copies(tab_ref, gstart_ref, local, sorted_hbm, sem.at[slot], True))

    @pl.when(i == n - 1)
    def _():
        for s in range(2):
            _wait_run_copies(_get_pending(pending_ref, s), local_ref.at[s], sorted_hbm, sem.at[s], True)


def _moe_scatter(gstart, gtail, xw, lpos, tab, n_rows, tm, eblk):
    n_tiles = xw.shape[0] // tm
    lrows = _local_rows(tm)
    any_spec = pl.BlockSpec(memory_space=pl.ANY)
    return pl.pallas_call(
        functools.partial(_moe_scatter_kernel, eblk=eblk),
        grid_spec=pltpu.PrefetchScalarGridSpec(
            num_scalar_prefetch=2,
            grid=(n_tiles,),
            in_specs=[pl.BlockSpec((tm, PACK_W), lambda i, *_: (i, 0)),
                      pl.BlockSpec((None, TOP_K, tm), lambda i, *_: (i, 0, 0)),
                      pl.BlockSpec((None, 8, LANES), lambda i, *_: (i, 0, 0), memory_space=pltpu.SMEM)],
            out_specs=any_spec,
            scratch_shapes=[pltpu.VMEM((2, lrows, PACK_W), jnp.uint32), pltpu.SemaphoreType.DMA((2,)),
                            pltpu.SemaphoreType.DMA, pltpu.SMEM((2, 2), jnp.int32)]),
        out_shape=jax.ShapeDtypeStruct((n_rows, PACK_W), jnp.uint32),
        compiler_params=_cparams("arbitrary"),
        name="moe_scatter",
    )(gstart, gtail, xw, lpos, tab)


def _expert2_kernel(be_ref, nu_ref, xs_ref, wg_ref, wu_ref, wd_ref, ys_ref, wgu_b, wd_b):
    j = pl.program_id(0)

    @pl.when(jnp.logical_or(j == 0, be_ref[j] != be_ref[jnp.maximum(j - 1, 0)]))
    def _():
        wgu_b[:, :EXPERT_FF] = wg_ref[...].astype(BF16)
        wgu_b[:, EXPERT_FF:] = wu_ref[...].astype(BF16)
        wd_b[...] = wd_ref[...].astype(BF16)

    @pl.when(j < nu_ref[0])
    def _():
        x = _unpack_bf16_pairs(xs_ref[...]).astype(BF16)
        gu = _dot(x, wgu_b[...])
        a = _silu(gu[:, :EXPERT_FF]) * gu[:, EXPERT_FF:]
        ys_ref[...] = _pack_bf16_pairs(_dot(a.astype(BF16), wd_b[...]))


def _experts2(xs, block_expert, n_used, lw, layer, eblk):
    n_rows = xs.shape[0]
    ff = EXPERT_FF

    def blk(j, be, nu):
        return (jnp.minimum(j, jnp.maximum(nu[0] - 1, 0)), 0)

    def wblk(j, be, nu):
        return (layer, be[j], 0, 0)

    return pl.pallas_call(
        _expert2_kernel,
        grid_spec=pltpu.PrefetchScalarGridSpec(
            num_scalar_prefetch=2,
            grid=(n_used[0],),
            in_specs=[pl.BlockSpec((eblk, PACK_W), blk),
                      pl.BlockSpec((None, None, D_MODEL, ff), wblk),
                      pl.BlockSpec((None, None, D_MODEL, ff), wblk),
                      pl.BlockSpec((None, None, ff, D_MODEL), wblk)],
            out_specs=pl.BlockSpec((eblk, PACK_W), blk),
            scratch_shapes=[pltpu.VMEM((D_MODEL, 2 * ff), BF16), pltpu.VMEM((ff, D_MODEL), BF16)]),
        out_shape=jax.ShapeDtypeStruct((n_rows, PACK_W), jnp.uint32),
        compiler_params=_cparams("arbitrary"),
        name="moe_experts",
    )(block_expert, n_used, xs, lw["w_exp_gate"], lw["w_exp_up"], lw["w_exp_down"])


def _moe_combine_kernel(gstart_ref, x1_ref, xw_ref, lpos_ref, wrow_ref, tab_ref, tab_next_ref, mod_ref,
                        sg_ref, su_ref, sd_ref, ln2g_ref, ln2b_ref, sorted_hbm, o_ref,
                        local_ref, sem, pending_ref):
    i = pl.program_id(0)
    n = pl.num_programs(0)
    slot = i & 1
    tm = x1_ref.shape[0]
    lrows = local_ref.shape[1]

    @pl.when(i == 0)
    def _():
        local_ref[...] = jnp.zeros_like(local_ref)
        _set_pending(pending_ref, 0,
                     _start_run_copies(tab_ref, gstart_ref, local_ref.at[0], sorted_hbm, sem.at[0], False))

    @pl.when(i + 1 < n)
    def _():
        nxt = 1 - slot
        _set_pending(pending_ref, nxt, _start_run_copies(tab_next_ref, gstart_ref, local_ref.at[nxt], sorted_hbm,
                                                         sem.at[nxt], False))

    h = _unpack_bf16_pairs(xw_ref[...]).astype(BF16)
    a = _silu(_dot(h, sg_ref[...])) * _dot(h, su_ref[...])
    acc = _dot(a.astype(BF16), sd_ref[...])

    local = local_ref.at[slot]
    _wait_run_copies(_get_pending(pending_ref, slot), local, sorted_hbm, sem.at[slot], False)

    lpos = lpos_ref[...].astype(jnp.int16)
    wrow = wrow_ref[...].astype(BF16)

    def unsort_block(b, acc):
        riota = lax.broadcasted_iota(jnp.int16, (LBLK, tm), 0) + b * LBLK
        q = jnp.zeros((LBLK, tm), BF16)
        for k in range(TOP_K):
            q = jnp.where(riota == lpos[k:k + 1, :], jnp.broadcast_to(wrow[k:k + 1, :], (LBLK, tm)), q)
        y = _unpack_bf16_pairs(local[b * LBLK:(b + 1) * LBLK, :]).astype(BF16)
        return acc + _dot_tn(q, y)

    for b in range(lrows // LBLK):
        acc = unsort_block(b, acc)

    g2 = mod_ref[...][:, 5 * D_MODEL:6 * D_MODEL]
    u = DEEPNORM_ALPHA * x1_ref[...] + g2 * acc
    o_ref[...] = _ln(u) * ln2g_ref[...] + ln2b_ref[...]


def _moe_combine(gstart, x1, xw, lpos, wrow, tab, ys, mod3, mod_row, lw, tm):
    t = x1.shape[0]
    n_tiles = t // tm
    lrows = _local_rows(tm)

    def full(a):
        return pl.BlockSpec(a.shape, lambda i, *_: (0,) * a.ndim)

    tab_blk = lambda f: pl.BlockSpec((None, 8, LANES), f, memory_space=pltpu.SMEM)
    return pl.pallas_call(
        _moe_combine_kernel,
        grid_spec=pltpu.PrefetchScalarGridSpec(
            num_scalar_prefetch=1,
            grid=(n_tiles,),
            in_specs=[pl.BlockSpec((tm, D_MODEL), lambda i, *_: (i, 0)),
                      pl.BlockSpec((tm, PACK_W), lambda i, *_: (i, 0)),
                      pl.BlockSpec((None, TOP_K, tm), lambda i, *_: (i, 0, 0)),
                      pl.BlockSpec((None, TOP_K, tm), lambda i, *_: (i, 0, 0)),
                      tab_blk(lambda i, *_: (i, 0, 0)),
                      tab_blk(lambda i, *_: (jnp.minimum(i + 1, n_tiles - 1), 0, 0)),
                      pl.BlockSpec((None, 1, 6 * D_MODEL), lambda i, *_: (mod_row(i, tm), 0, 0)),
                      full(lw["w_sh_gate"]), full(lw["w_sh_up"]), full(lw["w_sh_down"]),
                      full(lw["ln2_g"]), full(lw["ln2_b"]),
                      pl.BlockSpec(memory_space=pl.ANY)],
            out_specs=pl.BlockSpec((tm, D_MODEL), lambda i, *_: (i, 0)),
            scratch_shapes=[pltpu.VMEM((2, lrows, PACK_W), jnp.uint32), pltpu.SemaphoreType.DMA((2,)),
                            pltpu.SMEM((2, 2), jnp.int32)]),
        out_shape=jax.ShapeDtypeStruct((t, D_MODEL), F32),
        compiler_params=_cparams("arbitrary"),
        name="moe_combine",
    )(gstart, x1, xw, lpos, wrow, tab, tab, mod3, lw["w_sh_gate"], lw["w_sh_up"], lw["w_sh_down"],
      lw["ln2_g"], lw["ln2_b"], ys)


def _moe2(x1, xw, wrow, lpos, tab, chunks_f, mod3, mod_row, lw, layer, tm):
    t = x1.shape[0]
    n_tiles = t // tm
    eblk = max(LBLK, min(EBLK_MAX, t * TOP_K // N_EXPERTS))
    region_rows = chunks_f[:, 0].astype(jnp.int32) * CHUNK
    padded = ((region_rows + eblk - 1) // eblk) * eblk
    end = jnp.cumsum(padded)
    start = end - padded
    n_blocks = -(-(t * TOP_K + n_tiles * N_EXPERTS * (CHUNK - 1)) // eblk) + N_EXPERTS
    n_used = end[-1] // eblk
    blk_row = jnp.minimum(jnp.arange(n_blocks, dtype=jnp.int32), jnp.maximum(n_used - 1, 0)) * eblk
    block_expert = jnp.minimum(jnp.sum((end[None, :] <= blk_row[:, None]).astype(jnp.int32), axis=1),
                               N_EXPERTS - 1).astype(jnp.int32)
    gstart = start // CHUNK
    gtail = jnp.where(padded > 0, end - eblk, -1)
    xs = _moe_scatter(gstart, gtail, xw, lpos, tab, n_blocks * eblk, tm, eblk)
    ys = _experts2(xs, block_expert, n_used.reshape(1), lw, layer, eblk)
    return _moe_combine(gstart, x1, xw, lpos, wrow, tab, ys, mod3, mod_row, lw, tm)


def _channel_dft_table():
    k = np.arange(FNET_CH, dtype=np.float64)
    ang = 2.0 * np.pi * np.outer(k, k) / FNET_CH
    eye = np.eye(FNET_GROUPS)
    return np.concatenate([np.kron(eye, np.cos(ang)), -np.kron(eye, np.sin(ang))], axis=1)


def _direct_dft_tables(n):
    k = np.arange(n, dtype=np.float64)
    ang = 2.0 * np.pi * (np.outer(k, k) % n) / n
    scale = 1.0 / math.sqrt(n * FNET_CH)
    return np.cos(ang) * scale, np.sin(ang) * scale


def _two_stage_dft_tables(n):
    n1 = FFT_N1
    n2 = n // n1
    a = np.arange(n1, dtype=np.float64)
    ang1 = 2.0 * np.pi * (np.outer(a, a) % n1) / n1
    k1 = np.arange(n1).reshape(n1, 1, 1)
    k2 = np.arange(n2).reshape(1, n2, 1)
    m2 = np.arange(n2).reshape(1, 1, n2)
    ang2 = 2.0 * np.pi * ((m2 * (k1 + n1 * k2)) % n) / n
    scale = 1.0 / math.sqrt(n * FNET_CH)
    return np.cos(ang1), np.sin(ang1), np.cos(ang2) * scale, np.sin(ang2) * scale


def _grid_sincos_table(rows, d):
    quarter = d // 4
    omega = 1.0 / (POS_BASE ** (np.arange(quarter, dtype=np.float64) / quarter))
    r = np.arange(rows, dtype=np.float64)[:, None] * omega
    c = np.arange(GRID_W, dtype=np.float64)[:, None] * omega
    return (np.concatenate([np.sin(r), np.cos(r)], axis=-1).astype(np.float32),
            np.concatenate([np.sin(c), np.cos(c)], axis=-1).astype(np.float32))


def _layer_weights(l, w_in, w_gla_a, b_gla_a, gla_norm_g, sgu_norm_g, sgu_norm_b, w_sgu, b_sgu, w_conv,
                   w_out, ln1_g, ln1_b, ln2_g, ln2_b, w_router, router_bias,
                   w_exp_gate, w_exp_up, w_exp_down, w_sh_gate, w_sh_up, w_sh_down):
    wi = w_in[l]
    lr0 = 4 * HEAD_W
    w_in_p = jnp.concatenate(
        [wi[:, :lr0], wi[:, lr0 + 2 * GLA_LR:lr0 + 2 * GLA_LR + 5 * HEAD_W], wi[:, lr0:lr0 + 2 * GLA_LR],
         jnp.zeros((D_MODEL, LR_W - 2 * GLA_LR), F32), wi[:, lr0 + 2 * GLA_LR + 5 * HEAD_W:]], axis=1).astype(BF16)
    wa_pad = jnp.zeros((LR_W, 2 * HEAD_W), F32)
    wa_pad = wa_pad.at[:GLA_LR, :HEAD_W].set(w_gla_a[l, 0])
    wa_pad = wa_pad.at[GLA_LR:2 * GLA_LR, HEAD_W:].set(w_gla_a[l, 1])
    w_router_pad = jnp.concatenate([w_router[l], jnp.zeros((D_MODEL, LANES - N_EXPERTS), F32)], axis=1)
    w_router_hi = w_router_pad.astype(BF16)
    row = lambda a: a[l].reshape(1, -1)
    return {
        "w_in_p": w_in_p,
        "wa_hi": wa_pad.astype(BF16), "wa_lo": (wa_pad - wa_pad.astype(BF16).astype(F32)).astype(BF16),
        "ba": jnp.concatenate([b_gla_a[l, 0], b_gla_a[l, 1]]).reshape(1, 2 * HEAD_W),
        "gla_norm_g": row(gla_norm_g), "sgu_norm_g": row(sgu_norm_g), "sgu_norm_b": row(sgu_norm_b),
        "w_sgu_cat": jnp.concatenate([w_sgu[l, g] for g in range(GMLP_GROUPS)], axis=1).astype(BF16),
        "b_sgu_full": jnp.repeat(b_sgu[l].T, HEAD_W // GMLP_GROUPS, axis=1),
        "w_conv": w_conv[l],
        "w_out": w_out[l].astype(BF16),
        "ln1_g": row(ln1_g), "ln1_b": row(ln1_b), "ln2_g": row(ln2_g), "ln2_b": row(ln2_b),
        "w_router_hi": w_router_hi, "w_router_lo": (w_router_pad - w_router_hi.astype(F32)).astype(BF16),
        "router_bias": router_bias[l].reshape(N_EXPERTS, 1),
        "w_exp_gate": w_exp_gate, "w_exp_up": w_exp_up, "w_exp_down": w_exp_down,
        "w_sh_gate": w_sh_gate[l].astype(BF16), "w_sh_up": w_sh_up[l].astype(BF16),
        "w_sh_down": w_sh_down[l].astype(BF16),
    }


def _state_to_blockdiag_t(s):
    bsz = s.shape[0]
    st = jnp.swapaxes(s, 2, 3)
    eye = jnp.eye(GLA_HEADS, dtype=s.dtype)
    return jnp.einsum("bhvd,hg->bhvgd", st, eye).reshape(bsz, HEAD_W, HEAD_W)


def _blockdiag_t_to_state(st):
    bsz = st.shape[0]
    s5 = st.reshape(bsz, GLA_HEADS, GLA_DK, GLA_HEADS, GLA_DK)
    diag = jnp.stack([s5[:, h, :, h, :] for h in range(GLA_HEADS)], axis=1)
    return jnp.swapaxes(diag, 2, 3)


def _trunk_layer(x3, pos, mod3, mod_row, lw, layer, st0, emit_final, tabs):
    bsz, n, _ = x3.shape
    t = bsz * n
    outs = _in_proj(x3.reshape(t, D_MODEL), pos, mod3, mod_row, lw["w_in_p"], tabs["cs"], tm=512)
    if pos is not None:
        proj, zr, zi, x2d = outs
    else:
        proj, zr, zi = outs
        x2d = x3.reshape(t, D_MODEL)
    gla_out = _gla(proj.reshape(bsz, n, STORE_W), lw["wa_hi"], lw["wa_lo"], lw["ba"], st0, emit_final)
    o_f, o_b = gla_out[:2]
    zr3 = zr.reshape(bsz, n, HEAD_W)
    zi3 = zi.reshape(bsz, n, HEAD_W)
    if "two_stage" in tabs:
        yft = _fft_two_stage(zr3, zi3, tabs["two_stage"])
    else:
        yft = _fft_direct(zr3, zi3, *tabs["direct"])
    x1, xw, wrow, lpos, tab, chunks = _mix_out(x2d, o_f.reshape(t, HEAD_W), o_b.reshape(t, HEAD_W), proj,
                                             yft.reshape(t, HEAD_W), mod3, mod_row, lw, seq_len=n, tm=256)
    x2 = _moe2(x1, xw, wrow, lpos, tab, chunks, mod3, mod_row, lw, layer, tm=min(256, t))
    return x2.reshape(bsz, n, D_MODEL), gla_out[2:]


def kernel(x_prompt, x_sample, c, state_gla, c_ctx, w_ada, b_ada, w_in, w_gla_a, b_gla_a, gla_norm_g, sgu_norm_g, sgu_norm_b, w_sgu, b_sgu, w_conv, w_out, ln1_g, ln1_b, ln2_g, ln2_b, w_router, router_bias, w_exp_gate, w_exp_up, w_exp_down, w_sh_gate, w_sh_up, w_sh_down):
    n_layers = w_ada.shape[0]
    bp, np_, _ = x_prompt.shape
    bs, ns, _ = x_sample.shape
    assert bs <= 7

    cond8 = jnp.concatenate([c_ctx[None, :], c, jnp.zeros((7 - bs, D_MODEL), F32)], axis=0)
    mod = _ada_mod(cond8, w_ada, b_ada)

    tabs_p = {"cs": jnp.asarray(_channel_dft_table(), BF16),
              "direct": tuple(jnp.asarray(a, BF16) for a in _direct_dft_tables(np_))}
    tabs_s = {"cs": tabs_p["cs"],
              "two_stage": tuple(jnp.asarray(a, BF16) for a in _two_stage_dft_tables(ns))}
    rtab, ctab = _grid_sincos_table(ns // GRID_W, D_MODEL)
    pos = jnp.concatenate([jnp.repeat(jnp.asarray(rtab), GRID_W, axis=0),
                           jnp.tile(jnp.asarray(ctab), (ns // GRID_W, 1))], axis=-1)

    prompt_row = lambda i, tm: 0
    sample_row = lambda i, tm: 1 + (i * tm) // ns

    y_p = x_prompt
    y_s = x_sample
    finals = []
    for l in range(n_layers):
        lw = _layer_weights(l, w_in, w_gla_a, b_gla_a, gla_norm_g, sgu_norm_g, sgu_norm_b, w_sgu, b_sgu,
                            w_conv, w_out, ln1_g, ln1_b, ln2_g, ln2_b, w_router, router_bias,
                            w_exp_gate, w_exp_up, w_exp_down, w_sh_gate, w_sh_up, w_sh_down)
        mod3 = mod[l].reshape(8, 1, 6 * D_MODEL)
        y_p, fin = _trunk_layer(y_p, None, mod3, prompt_row, lw, l, None, True, tabs_p)
        finals.append(jnp.stack([_blockdiag_t_to_state(fin[0]), _blockdiag_t_to_state(fin[1])], axis=1))
        st0 = jnp.stack([_state_to_blockdiag_t(state_gla[:, l, 0]), _state_to_blockdiag_t(state_gla[:, l, 1])])
        y_s, _ = _trunk_layer(y_s, pos if l == 0 else None, mod3, sample_row, lw, l, st0, False, tabs_s)
    new_state = jnp.stack(finals, axis=1).astype(x_prompt.dtype)
    return (y_p, y_s, new_state)
```

```python
import functools
import math

import numpy as np
import jax
import jax.numpy as jnp
from jax import lax
from jax.experimental import pallas as pl
from jax.experimental.pallas import tpu as pltpu

F32 = jnp.float32
BF16 = jnp.bfloat16

D_MODEL = 1024
DEPTH = 2
GRID_W = 64
HEAD_W = 256
GLA_HEADS = 4
GLA_DK = 64
GLA_LR = 16
GLA_TAU = 16.0
GLA_CHUNK = 64
GMLP_GROUPS = 4
GMLP_CHUNK = 128
FNET_GROUPS = 4
FNET_CH = 64
N_EXPERTS = 64
TOP_K = 8
N_GROUPS = 8
TOPK_GROUPS = 4
EXPERT_FF = 256
ROUTED_SCALE = 2.5
DEEPNORM_ALPHA = (2 * DEPTH) ** 0.25
LN_EPS = 1e-5
RMS_EPS = 1e-6
POS_BASE = 10000.0

COL_Q, COL_K, COL_V, COL_G, COL_SGU, COL_SGV, COL_CVB, COL_CVC, COL_CVX = range(9)
LR_W = 128
STORE_W = 9 * HEAD_W + LR_W
PROJ_W = STORE_W + HEAD_W
COL_LR = (9 * HEAD_W) // LR_W
HALO = 16

LANES = 128
PACK_W = D_MODEL // 2
EBLK_MAX = 1024
LBLK = 256
CHUNK = 8
BIG_SHIFT = 2
BIG_ROWS = CHUNK << BIG_SHIFT
SEG = 256
FFT_N1 = 64
VMEM_LIMIT = 56 * 1024 * 1024


def _cparams(*sem):
    return pltpu.CompilerParams(dimension_semantics=sem, vmem_limit_bytes=VMEM_LIMIT)


def _ln(x):
    mu = jnp.mean(x, axis=-1, keepdims=True)
    xc = x - mu
    var = jnp.mean(xc * xc, axis=-1, keepdims=True)
    return xc * lax.rsqrt(var + LN_EPS)


def _sigmoid(x):
    return 1.0 / (1.0 + jnp.exp(-x))


def _silu(x):
    return x * _sigmoid(x)


def _pack_bf16_pairs(x):
    w = x.shape[1] // 2
    lo = lax.bitcast_convert_type(x[:, :w].astype(BF16).astype(F32), jnp.uint32)
    hi = lax.bitcast_convert_type(x[:, w:].astype(BF16).astype(F32), jnp.uint32)
    return (lo >> 16) | (hi & jnp.uint32(0xFFFF0000))


def _pack_exact_bf16_pairs(x):
    w = x.shape[1] // 2
    lo = lax.bitcast_convert_type(x[:, :w], jnp.uint32)
    hi = lax.bitcast_convert_type(x[:, w:], jnp.uint32)
    return (lo >> 16) | hi


def _unpack_bf16_pairs(u):
    lo = lax.bitcast_convert_type(u << 16, F32)
    hi = lax.bitcast_convert_type(u & jnp.uint32(0xFFFF0000), F32)
    return jnp.concatenate([lo, hi], axis=1)


def _dot(a, b):
    return jnp.dot(a, b, preferred_element_type=F32)


def _dot_nt(a, b):
    return lax.dot_general(a, b, (((1,), (1,)), ((), ())), preferred_element_type=F32)


def _dot_tn(a, b):
    return lax.dot_general(a, b, (((0,), (0,)), ((), ())), preferred_element_type=F32)


def _ada_kernel(c_ref, w_ref, b_ref, o_ref):
    c = c_ref[...]
    o_ref[...] = _dot(_silu(c).astype(BF16), w_ref[...].astype(BF16)) + b_ref[...]


def _ada_mod(cond8, w_ada, b_ada):
    n_l, d, w6 = w_ada.shape
    tn = 1536
    return pl.pallas_call(
        _ada_kernel,
        grid=(n_l, w6 // tn),
        in_specs=[pl.BlockSpec((8, d), lambda l, j: (0, 0)),
                  pl.BlockSpec((None, d, tn), lambda l, j: (l, 0, j)),
                  pl.BlockSpec((None, 1, tn), lambda l, j: (l, 0, j))],
        out_specs=pl.BlockSpec((None, 8, tn), lambda l, j: (l, 0, j)),
        out_shape=jax.ShapeDtypeStruct((n_l, 8, w6), F32),
        compiler_params=_cparams("parallel", "parallel"),
        name="ada_mod",
    )(cond8, w_ada, b_ada.reshape(n_l, 1, w6))


def _in_proj_kernel(*refs, has_pos):
    if has_pos:
        x_ref, pos_ref, mod_ref, w_ref, cs_ref, proj_ref, zr_ref, zi_ref, x0_ref = refs
        x = x_ref[...] + pos_ref[...]
        x0_ref[...] = x
    else:
        x_ref, mod_ref, w_ref, cs_ref, proj_ref, zr_ref, zi_ref = refs
        x = x_ref[...]
    mod = mod_ref[...]
    sh1 = mod[:, 0:D_MODEL]
    sc1 = mod[:, D_MODEL:2 * D_MODEL]
    h = _ln(x) * (1.0 + sc1) + sh1
    proj = _dot(h.astype(BF16), w_ref[...])
    proj_ref[...] = proj[:, :STORE_W].astype(BF16)
    ft = proj[:, STORE_W:].astype(BF16)
    z = _dot(ft, cs_ref[...])
    zr_ref[...] = z[:, :HEAD_W].astype(BF16)
    zi_ref[...] = z[:, HEAD_W:].astype(BF16)


def _in_proj(x2d, pos, mod3, mod_row, w_in_p, cs, tm):
    t = x2d.shape[0]
    tm = min(tm, t)
    in_specs = [pl.BlockSpec((tm, D_MODEL), lambda i: (i, 0))]
    args = [x2d]
    out_shape = [jax.ShapeDtypeStruct((t, STORE_W), BF16),
                 jax.ShapeDtypeStruct((t, HEAD_W), BF16),
                 jax.ShapeDtypeStruct((t, HEAD_W), BF16)]
    out_specs = [pl.BlockSpec((tm, STORE_W), lambda i: (i, 0)),
                 pl.BlockSpec((tm, HEAD_W), lambda i: (i, 0)),
                 pl.BlockSpec((tm, HEAD_W), lambda i: (i, 0))]
    if pos is not None:
        n_pos = pos.shape[0] // tm
        in_specs.append(pl.BlockSpec((tm, D_MODEL), lambda i: (i % n_pos, 0)))
        args.append(pos)
        out_shape.append(jax.ShapeDtypeStruct((t, D_MODEL), F32))
        out_specs.append(pl.BlockSpec((tm, D_MODEL), lambda i: (i, 0)))
    in_specs += [pl.BlockSpec((None, 1, 6 * D_MODEL), lambda i: (mod_row(i, tm), 0, 0)),
                 pl.BlockSpec((D_MODEL, PROJ_W), lambda i: (0, 0)),
                 pl.BlockSpec((HEAD_W, 2 * HEAD_W), lambda i: (0, 0))]
    args += [mod3, w_in_p, cs]
    return pl.pallas_call(
        functools.partial(_in_proj_kernel, has_pos=pos is not None),
        grid=(t // tm,),
        in_specs=in_specs,
        out_specs=out_specs,
        out_shape=out_shape,
        compiler_params=_cparams("parallel"),
        name="in_proj",
    )(*args)


def _gla_masks(reverse):
    i = np.arange(SEG)
    same_chunk = (i[:, None] // GLA_CHUNK) == (i[None, :] // GLA_CHUNK)
    tri = same_chunk & ((i[None, :] >= i[:, None]) if reverse else (i[None, :] <= i[:, None]))
    l_idx = i[:, None] % GLA_CHUNK
    m_idx = np.arange(GLA_CHUNK)[None, :]
    causal = (m_idx >= l_idx) if reverse else (m_idx <= l_idx)
    return (jnp.asarray(tri, BF16), jnp.asarray(same_chunk, BF16), jnp.asarray(same_chunk, F32),
            jnp.asarray(causal, F32))


def _gla_segment(q, k, v, pre, st_ref, o_ref, masks, reverse):
    seg = q.shape[0]
    n_chunks = seg // GLA_CHUNK
    la = (jnp.minimum(pre, 0.0) - jnp.log1p(jnp.exp(-jnp.abs(pre)))) * (1.0 / GLA_TAU)

    tri_m, ones_m, bd, causal = masks
    hi = la.astype(BF16)
    lo = (la - hi.astype(F32)).astype(BF16)
    b = _dot(tri_m, hi) + _dot(tri_m, lo)
    btot = _dot(ones_m, hi) + _dot(ones_m, lo)

    q_dec = q * (GLA_DK ** -0.5) * jnp.exp(b)
    k_inv = (k * jnp.exp(-b)).astype(BF16)
    k_end = (k * jnp.exp(btot - b)).astype(BF16)
    dec = jnp.exp(btot)
    vb = v.astype(BF16)
    keep = causal > 0.5

    st = st_ref[...]
    order = range(n_chunks - 1, -1, -1) if reverse else range(n_chunks)
    for ci in order:
        sl = slice(ci * GLA_CHUNK, (ci + 1) * GLA_CHUNK)
        qd = q_dec[sl]
        qbd = (jnp.concatenate([qd] * GLA_HEADS, axis=0) * bd).astype(BF16)
        a = _dot_nt(qbd, k_inv[sl])
        a = jnp.where(keep, a, 0.0)
        rr = _dot(a.astype(BF16), vb[sl])
        o = _dot_nt(qd.astype(BF16), st.astype(BF16))
        for h in range(GLA_HEADS):
            hs = slice(h * GLA_CHUNK, (h + 1) * GLA_CHUNK)
            o = o + rr[hs] * bd[hs]
        o_ref[sl, :] = o
        kvt = _dot_tn(vb[sl], k_end[sl])
        st = st * dec[ci * GLA_CHUNK:ci * GLA_CHUNK + 1, :] + kvt * bd
    st_ref[...] = st


def _gla_kernel(*refs, has_init, emit_final):
    qf, kf, vf, lrf, qb, kb, vb, lrb, wah_ref, wal_ref, ba_ref = refs[:11]
    mask_refs = refs[11:19]
    rest = refs[19:]
    if has_init:
        s0f, s0b = rest[:2]
        rest = rest[2:]
    of_ref, ob_ref = rest[:2]
    rest = rest[2:]
    if emit_final:
        sff, sfb = rest[:2]
        rest = rest[2:]
    stf, stb = rest

    s = pl.program_id(1)

    @pl.when(s == 0)
    def _():
        if has_init:
            stf[...] = s0f[...]
            stb[...] = s0b[...]
        else:
            stf[...] = jnp.zeros_like(stf)
            stb[...] = jnp.zeros_like(stb)

    def decay_pre(lr_ref):
        lr = lr_ref[...]
        return _dot(lr, wah_ref[...]) + _dot(lr, wal_ref[...]) + ba_ref[...]

    f32 = lambda ref: ref[...].astype(F32)

    masks_f = tuple(m[...] for m in mask_refs[:4])
    masks_b = tuple(m[...] for m in mask_refs[4:])
    _gla_segment(f32(qf), f32(kf), f32(vf), decay_pre(lrf)[:, :HEAD_W], stf, of_ref, masks_f, reverse=False)
    _gla_segment(f32(qb), f32(kb), f32(vb), decay_pre(lrb)[:, HEAD_W:], stb, ob_ref, masks_b, reverse=True)

    if emit_final:
        @pl.when(s == pl.num_programs(1) - 1)
        def _():
            sff[...] = stf[...]
            sfb[...] = stb[...]


def _gla(proj3, wa_hi, wa_lo, ba, st0, emit_final):
    bsz, n, _ = proj3.shape
    nseg = n // SEG

    def col(cb, width=HEAD_W, rev=False):
        if rev:
            return pl.BlockSpec((None, SEG, width), lambda b, s: (b, nseg - 1 - s, cb))
        return pl.BlockSpec((None, SEG, width), lambda b, s: (b, s, cb))

    in_specs = [col(COL_Q), col(COL_K), col(COL_V), col(COL_LR, LR_W),
                col(COL_Q, rev=True), col(COL_K, rev=True), col(COL_V, rev=True), col(COL_LR, LR_W, rev=True),
                pl.BlockSpec((LR_W, 2 * HEAD_W), lambda b, s: (0, 0)),
                pl.BlockSpec((LR_W, 2 * HEAD_W), lambda b, s: (0, 0)),
                pl.BlockSpec((1, 2 * HEAD_W), lambda b, s: (0, 0))]
    masks = _gla_masks(False) + _gla_masks(True)
    in_specs += [pl.BlockSpec(m.shape, lambda b, s: (0, 0)) for m in masks]
    args = [proj3] * 8 + [wa_hi, wa_lo, ba] + list(masks)
    st_spec = pl.BlockSpec((None, HEAD_W, HEAD_W), lambda b, s: (b, 0, 0))
    if st0 is not None:
        in_specs += [st_spec, st_spec]
        args += [st0[0], st0[1]]
    out_shape = [jax.ShapeDtypeStruct((bsz, n, HEAD_W), F32)] * 2
    out_specs = [pl.BlockSpec((None, SEG, HEAD_W), lambda b, s: (b, s, 0)),
                 pl.BlockSpec((None, SEG, HEAD_W), lambda b, s: (b, nseg - 1 - s, 0))]
    if emit_final:
        out_shape += [jax.ShapeDtypeStruct((bsz, HEAD_W, HEAD_W), F32)] * 2
        out_specs += [st_spec, st_spec]
    return pl.pallas_call(
        functools.partial(_gla_kernel, has_init=st0 is not None, emit_final=emit_final),
        grid=(bsz, nseg),
        in_specs=in_specs,
        out_specs=out_specs,
        out_shape=out_shape,
        scratch_shapes=[pltpu.VMEM((HEAD_W, HEAD_W), F32), pltpu.VMEM((HEAD_W, HEAD_W), F32)],
        compiler_params=_cparams("parallel", "arbitrary"),
        name="gla",
    )(*args)


def _fft_direct_kernel(zr_ref, zi_ref, cn_ref, sn_ref, o_ref):
    o_ref[...] = _dot(cn_ref[...], zr_ref[...]) + _dot(sn_ref[...], zi_ref[...])


def _fft_direct(zr3, zi3, cn, sn):
    bsz, n, w = zr3.shape
    blk = pl.BlockSpec((None, n, w), lambda b: (b, 0, 0))
    tab = pl.BlockSpec((n, n), lambda b: (0, 0))
    return pl.pallas_call(
        _fft_direct_kernel,
        grid=(bsz,),
        in_specs=[blk, blk, tab, tab],
        out_specs=blk,
        out_shape=jax.ShapeDtypeStruct((bsz, n, w), F32),
        compiler_params=_cparams("parallel"),
        name="fft_direct",
    )(zr3, zi3, cn, sn)


def _fft_a_kernel(zr_ref, zi_ref, c_ref, s_ref, gr_ref, gi_ref):
    zr = zr_ref[...]
    zi = zi_ref[...]
    cm = c_ref[...]
    sm = s_ref[...]
    gr_ref[...] = (_dot(cm, zr) + _dot(sm, zi)).astype(BF16)
    gi_ref[...] = (_dot(cm, zi) - _dot(sm, zr)).astype(BF16)


def _fft_c_kernel(gr_ref, gi_ref, mc_ref, ms_ref, o_ref):
    for j in range(gr_ref.shape[0]):
        o_ref[:, j, :] = _dot(mc_ref[j], gr_ref[j]) + _dot(ms_ref[j], gi_ref[j])


def _fft_two_stage(zr3, zi3, tabs):
    bsz, n, w = zr3.shape
    n1 = FFT_N1
    n2 = n // n1
    c1, s1, mc, ms = tabs
    tn = 2048
    wide = n2 * w
    blk = pl.BlockSpec((None, n1, tn), lambda b, j: (b, 0, j))
    tab = pl.BlockSpec((n1, n1), lambda b, j: (0, 0))
    gr, gi = pl.pallas_call(
        _fft_a_kernel,
        grid=(bsz, wide // tn),
        in_specs=[blk, blk, tab, tab],
        out_specs=[blk, blk],
        out_shape=[jax.ShapeDtypeStruct((bsz, n1, wide), BF16)] * 2,
        compiler_params=_cparams("parallel", "parallel"),
        name="fft_stage_a",
    )(zr3.reshape(bsz, n1, wide), zi3.reshape(bsz, n1, wide), c1, s1)
    kb = 8
    gblk = pl.BlockSpec((None, kb, n2, w), lambda b, j: (b, j, 0, 0))
    mblk = pl.BlockSpec((kb, n2, n2), lambda b, j: (j, 0, 0))
    out = pl.pallas_call(
        _fft_c_kernel,
        grid=(bsz, n1 // kb),
        in_specs=[gblk, gblk, mblk, mblk],
        out_specs=pl.BlockSpec((None, n2, kb, w), lambda b, j: (b, 0, j, 0)),
        out_shape=jax.ShapeDtypeStruct((bsz, n2, n1, w), F32),
        compiler_params=_cparams("parallel", "parallel"),
        name="fft_stage_c",
    )(gr.reshape(bsz, n1, n2, w), gi.reshape(bsz, n1, n2, w), mc, ms)
    return out.reshape(bsz, n, w)


def _mix_masks(tm):
    h = np.arange(HEAD_W) // GLA_DK
    head_mean = (h[:, None] == h[None, :]) / GLA_DK
    rg = np.arange(GMLP_GROUPS * GMLP_CHUNK) // GMLP_CHUNK
    cg = np.arange(HEAD_W) // (HEAD_W // GMLP_GROUPS)
    t = np.arange(tm)
    e = np.arange(N_EXPERTS)
    return (jnp.asarray(head_mean, BF16), jnp.asarray(rg[:, None] == cg[None, :], F32),
            jnp.asarray(t[:, None] < t[None, :], BF16), jnp.asarray(e[None, :] < e[:, None], BF16))


def _route(logits, bias, before, lower):
    tm = logits.shape[1]
    s = _sigmoid(logits)
    biased = s + bias
    neg = -jnp.inf
    rows = lax.broadcasted_iota(jnp.int32, (8, tm), 0)

    def first_argmax(x, ids, sentinel):
        m = jnp.max(x, axis=0, keepdims=True)
        return m, jnp.min(jnp.where(x == m, ids, sentinel), axis=0, keepdims=True)

    gs_rows = []
    for g in range(N_GROUPS):
        x = biased[8 * g:8 * g + 8]
        m1, i1 = first_argmax(x, rows, 8)
        m2 = jnp.max(jnp.where(rows == i1, neg, x), axis=0, keepdims=True)
        gs_rows.append(m1 + m2)
    gs = jnp.concatenate(gs_rows, axis=0)
    gsel = jnp.zeros((N_GROUPS, tm), F32)
    for _ in range(TOPK_GROUPS):
        _, i = first_argmax(gs, rows, 8)
        hit = rows == i
        gsel = jnp.where(hit, 1.0, gsel)
        gs = jnp.where(hit, neg, gs)

    xs = [jnp.where(gsel[g:g + 1] > 0.0, biased[8 * g:8 * g + 8], neg) for g in range(N_GROUPS)]
    ids = [rows + 8 * g for g in range(N_GROUPS)]
    sel = [jnp.zeros((8, tm), F32) for _ in range(N_GROUPS)]
    eids = []
    for _ in range(TOP_K):
        m = xs[0]
        for g in range(1, N_GROUPS):
            m = jnp.maximum(m, xs[g])
        m = jnp.max(m, axis=0, keepdims=True)
        cand = jnp.where(xs[0] == m, ids[0], N_EXPERTS)
        for g in range(1, N_GROUPS):
            cand = jnp.minimum(cand, jnp.where(xs[g] == m, ids[g], N_EXPERTS))
        i = jnp.min(cand, axis=0, keepdims=True)
        eids.append(i)
        for g in range(N_GROUPS):
            hit = ids[g] == i
            sel[g] = jnp.where(hit, 1.0, sel[g])
            xs[g] = jnp.where(hit, neg, xs[g])

    sel_all = jnp.concatenate(sel, axis=0)
    seen = _dot(sel_all.astype(BF16), before)
    counts = jnp.sum(sel_all, axis=1, keepdims=True)

    n_chunks = jnp.ceil(counts * (1.0 / CHUNK))
    run_start = _dot(lower, jnp.broadcast_to(n_chunks, (N_EXPERTS, LANES)).astype(BF16))[:, 0:1] * CHUNK
    local_pos = seen + run_start

    def pick(k, table):
        acc = None
        for g in range(N_GROUPS):
            v = jnp.where(ids[g] == eids[k], table[8 * g:8 * g + 8], 0.0)
            acc = v if acc is None else acc + v
        return jnp.sum(acc, axis=0, keepdims=True)

    w_raw = [pick(k, s) for k in range(TOP_K)]
    lpos = [pick(k, local_pos) for k in range(TOP_K)]
    tot = w_raw[0]
    for k in range(1, TOP_K):
        tot = tot + w_raw[k]
    weights = [w / tot * ROUTED_SCALE for w in w_raw]
    return weights, lpos, n_chunks, run_start


def _mix_out_kernel(x_ref, of_ref, ob_ref, g_ref, su_ref, sv_ref, cb_ref, cc_ref, cx_ref,
                    ccp_ref, cxp_ref, ccn_ref, cxn_ref, ft_ref, mod_ref,
                    glag_ref, sgng_ref, sgnb_ref, wsgu_ref, bsgu_ref, wconv_ref, wout_ref,
                    ln1g_ref, ln1b_ref, wrh_ref, wrl_ref, rb_ref, hmean_ref, sgubd_ref, before_ref, lower_ref,
                    x1_ref, xw_ref, wrow_ref, lpos_ref, tab_ref, cnt_ref, carry_ref, *, seq_len):
    tm = x_ref.shape[0]
    i = pl.program_id(0)

    @pl.when(i == 0)
    def _():
        carry_ref[...] = jnp.zeros_like(carry_ref)

    mod = mod_ref[...]
    g1 = mod[:, 2 * D_MODEL:3 * D_MODEL]
    sh2 = mod[:, 3 * D_MODEL:4 * D_MODEL]
    sc2 = mod[:, 4 * D_MODEL:5 * D_MODEL]

    o = of_ref[...] + ob_ref[...]
    head_mean = hmean_ref[...]
    o2 = o * o
    o2_hi = o2.astype(BF16)
    o2_lo = (o2 - o2_hi.astype(F32)).astype(BF16)
    ms = _dot(o2_hi, head_mean) + _dot(o2_lo, head_mean)
    y_gla = o * lax.rsqrt(ms + RMS_EPS) * glag_ref[...] * _silu(g_ref[...].astype(F32))

    vn = _ln(sv_ref[...].astype(F32)) * sgng_ref[...] + sgnb_ref[...]
    sgu_bd = sgubd_ref[...]
    sp_parts = []
    for j in range(tm // GMLP_CHUNK):
        vc = vn[j * GMLP_CHUNK:(j + 1) * GMLP_CHUNK]
        vbd = (jnp.concatenate([vc] * GMLP_GROUPS, axis=0) * sgu_bd).astype(BF16)
        sp_parts.append(_dot(wsgu_ref[...], vbd) + bsgu_ref[...])
    y_sgu = su_ref[...].astype(F32) * jnp.concatenate(sp_parts, axis=0)

    z = cc_ref[...].astype(F32) * cx_ref[...].astype(F32)
    z_before = (ccp_ref[...].astype(F32) * cxp_ref[...].astype(F32))[HALO - 1:HALO, :]
    z_after = (ccn_ref[...].astype(F32) * cxn_ref[...].astype(F32))[0:1, :]
    row = lax.broadcasted_iota(jnp.int32, (tm, HEAD_W), 0)
    pos = (i * tm + row) & (seq_len - 1)
    z_prev = jnp.where(row == 0, z_before, pltpu.roll(z, 1, 0))
    z_next = jnp.where(row == tm - 1, z_after, pltpu.roll(z, tm - 1, 0))
    z_prev = jnp.where(pos == 0, 0.0, z_prev)
    z_next = jnp.where(pos == seq_len - 1, 0.0, z_next)
    wconv = wconv_ref[...]
    y_conv = cb_ref[...].astype(F32) * (wconv[0:1] * z_prev + wconv[1:2] * z + wconv[2:3] * z_next)

    y = (_dot(y_gla.astype(BF16), wout_ref[0:HEAD_W, :])
         + _dot(y_sgu.astype(BF16), wout_ref[HEAD_W:2 * HEAD_W, :])
         + _dot(y_conv.astype(BF16), wout_ref[2 * HEAD_W:3 * HEAD_W, :])
         + _dot(ft_ref[...].astype(BF16), wout_ref[3 * HEAD_W:4 * HEAD_W, :]))
    x1 = _ln(DEEPNORM_ALPHA * x_ref[...] + g1 * y) * ln1g_ref[...] + ln1b_ref[...]
    x1_ref[...] = x1
    h2 = _ln(x1) * (1.0 + sc2) + sh2
    xw_ref[...] = _pack_bf16_pairs(h2)

    h2_hi = h2.astype(BF16)
    h2_lo = (h2 - h2_hi.astype(F32)).astype(BF16)
    logits = _dot(h2_hi, wrh_ref[...]) + _dot(h2_hi, wrl_ref[...]) + _dot(h2_lo, wrh_ref[...])
    weights, lpos, n_chunks, run_start = _route(logits.T[:N_EXPERTS], rb_ref[...], before_ref[...], lower_ref[...])
    lpos_ref[...] = jnp.concatenate(lpos, axis=0).astype(jnp.int32)
    wrow_ref[...] = jnp.concatenate(weights, axis=0)
    carry = carry_ref[:, 0:1]
    lane = lax.broadcasted_iota(jnp.int32, (N_EXPERTS, LANES), 1)
    total = jnp.sum(n_chunks, axis=0, keepdims=True)
    cols = jnp.where(lane == 0, n_chunks, jnp.where(lane == 1, run_start,
                                                   jnp.where(lane == 2, carry, jnp.where(lane == 3, total, 0.0))))
    tab = jnp.concatenate([cols, jnp.zeros((LANES - N_EXPERTS, LANES), F32)], axis=0).T
    tab_ref[...] = tab[0:8].astype(jnp.int32)
    new_carry = carry_ref[...] + n_chunks
    carry_ref[...] = new_carry
    cnt_ref[...] = new_carry


def _mix_out(x2d, of2d, ob2d, proj, yft2d, mod3, mod_row, lw, seq_len, tm):
    t = x2d.shape[0]
    tm = min(tm, t)
    nt8 = t // HALO
    rows8 = tm // HALO

    def col(cb):
        return pl.BlockSpec((tm, HEAD_W), lambda i: (i, cb))

    def halo_prev(cb):
        return pl.BlockSpec((HALO, HEAD_W), lambda i: (jnp.maximum(i * rows8 - 1, 0), cb))

    def halo_next(cb):
        return pl.BlockSpec((HALO, HEAD_W), lambda i: (jnp.minimum((i + 1) * rows8, nt8 - 1), cb))

    def full(a):
        return pl.BlockSpec(a.shape, lambda i: (0,) * a.ndim)

    tok_d = pl.BlockSpec((tm, D_MODEL), lambda i: (i, 0))
    tok_h = pl.BlockSpec((tm, HEAD_W), lambda i: (i, 0))
    weights = [lw["gla_norm_g"], lw["sgu_norm_g"], lw["sgu_norm_b"], lw["w_sgu_cat"], lw["b_sgu_full"],
               lw["w_conv"], lw["w_out"], lw["ln1_g"], lw["ln1_b"], lw["w_router_hi"], lw["w_router_lo"], lw["router_bias"]]
    weights += list(_mix_masks(tm))
    in_specs = ([tok_d, tok_h, tok_h, col(COL_G), col(COL_SGU), col(COL_SGV), col(COL_CVB), col(COL_CVC),
                 col(COL_CVX), halo_prev(COL_CVC), halo_prev(COL_CVX), halo_next(COL_CVC), halo_next(COL_CVX),
                 tok_h, pl.BlockSpec((None, 1, 6 * D_MODEL), lambda i: (mod_row(i, tm), 0, 0))]
                + [full(w) for w in weights])
    args = [x2d, of2d, ob2d] + [proj] * 10 + [yft2d, mod3] + weights
    return pl.pallas_call(
        functools.partial(_mix_out_kernel, seq_len=seq_len),
        grid=(t // tm,),
        in_specs=in_specs,
        out_specs=[tok_d,
                   pl.BlockSpec((tm, PACK_W), lambda i: (i, 0)),
                   pl.BlockSpec((None, TOP_K, tm), lambda i: (i, 0, 0)),
                   pl.BlockSpec((None, TOP_K, tm), lambda i: (i, 0, 0)),
                   pl.BlockSpec((None, 8, LANES), lambda i: (i, 0, 0)),
                   pl.BlockSpec((N_EXPERTS, LANES), lambda i: (0, 0))],
        out_shape=[jax.ShapeDtypeStruct((t, D_MODEL), F32),
                   jax.ShapeDtypeStruct((t, PACK_W), jnp.uint32),
                   jax.ShapeDtypeStruct((t // tm, TOP_K, tm), F32),
                   jax.ShapeDtypeStruct((t // tm, TOP_K, tm), jnp.int32),
                   jax.ShapeDtypeStruct((t // tm, 8, LANES), jnp.int32),
                   jax.ShapeDtypeStruct((N_EXPERTS, LANES), F32)],
        scratch_shapes=[pltpu.VMEM((N_EXPERTS, LANES), F32)],
        compiler_params=_cparams("arbitrary"),
        name="mix_out",
    )(*args)


def _local_rows(tm):
    need = tm * TOP_K + N_EXPERTS * (CHUNK - 1)
    return -(-need // LBLK) * LBLK


def _run_copy(local_ref, sorted_hbm, sem, local_row, sorted_row, to_sorted, rows=CHUNK):
    loc = local_ref.at[pl.ds(pl.multiple_of(local_row, CHUNK), rows)]
    srt = sorted_hbm.at[pl.ds(pl.multiple_of(sorted_row, CHUNK), rows)]
    return pltpu.make_async_copy(loc, srt, sem) if to_sorted else pltpu.make_async_copy(srt, loc, sem)


def _start_run_copies(tab_ref, gstart_ref, local_ref, sorted_hbm, sem, to_sorted):
    def per_expert(e, totals):
        n = tab_ref[0, e]
        l0 = tab_ref[1, e]
        g0 = (gstart_ref[e] + tab_ref[2, e]) * CHUNK
        n_big = n >> BIG_SHIFT
        n_small = n & ((1 << BIG_SHIFT) - 1)

        def big(j, c):
            _run_copy(local_ref, sorted_hbm, sem, l0 + j * BIG_ROWS, g0 + j * BIG_ROWS, to_sorted, BIG_ROWS).start()
            return c

        lax.fori_loop(0, n_big, big, 0)
        l1 = l0 + n_big * BIG_ROWS
        g1 = g0 + n_big * BIG_ROWS

        def small(j, c):
            _run_copy(local_ref, sorted_hbm, sem, l1 + j * CHUNK, g1 + j * CHUNK, to_sorted).start()
            return c

        lax.fori_loop(0, n_small, small, 0)
        return totals[0] + n_big, totals[1] + n_small

    return lax.fori_loop(0, N_EXPERTS, per_expert, (jnp.int32(0), jnp.int32(0)))


def _get_pending(pending_ref, s):
    return pending_ref[s, 0], pending_ref[s, 1]


def _set_pending(pending_ref, s, counts):
    pending_ref[s, 0] = counts[0]
    pending_ref[s, 1] = counts[1]


def _wait_run_copies(counts, local_ref, sorted_hbm, sem, to_sorted):
    batch = 8

    def wait_n(rows, reps):
        def body(j, c):
            for _ in range(reps):
                _run_copy(local_ref, sorted_hbm, sem, 0, 0, to_sorted, rows).wait()
            return c
        return body

    for count, rows in zip(counts, (BIG_ROWS, CHUNK)):
        lax.fori_loop(0, count >> 3, wait_n(rows, batch), 0)
        lax.fori_loop(0, count & (batch - 1), wait_n(rows, 1), 0)


def _moe_scatter_kernel(gstart_ref, gend_ref, gfill_ref, xw_ref, lpos_ref, tab_ref, sorted_hbm,
                        local_ref, sem, zsem, pending_ref):
    i = pl.program_id(0)
    n = pl.num_programs(0)
    slot = i & 1
    tm = xw_ref.shape[0]
    lrows = local_ref.shape[1]

    def zero_copy(e, j):
        off = pl.multiple_of(gend_ref[e] - (j + 1) * LBLK, LBLK)
        return pltpu.make_async_copy(local_ref.at[1, pl.ds(0, LBLK)], sorted_hbm.at[pl.ds(off, LBLK)], zsem)

    @pl.when(i == 0)
    def _():
        _set_pending(pending_ref, 0, (0, 0))
        _set_pending(pending_ref, 1, (0, 0))
        local_ref[1, 0:LBLK, :] = jnp.zeros((LBLK, PACK_W), jnp.uint32)

        def z_start(e, c):
            lax.fori_loop(0, gfill_ref[e], lambda j, c2: (zero_copy(e, j).start(), c2)[1], 0)
            return c

        def z_wait(e, c):
            lax.fori_loop(0, gfill_ref[e], lambda j, c2: (zero_copy(e, j).wait(), c2)[1], 0)
            return c

        lax.fori_loop(0, N_EXPERTS, z_start, 0)
        lax.fori_loop(0, N_EXPERTS, z_wait, 0)

    local = local_ref.at[slot]
    _wait_run_copies(_get_pending(pending_ref, slot), local, sorted_hbm, sem.at[slot], True)

    x = _unpack_bf16_pairs(xw_ref[...]).astype(BF16)
    lpos = lpos_ref[...].astype(jnp.int16)
    one = jnp.ones((LBLK, tm), BF16)
    used_rows = tab_ref[3, 0] * CHUNK

    def sort_block(b):
        riota = lax.broadcasted_iota(jnp.int16, (LBLK, tm), 0) + b * LBLK
        p = jnp.zeros((LBLK, tm), BF16)
        for k in range(TOP_K):
            p = jnp.where(riota == lpos[k:k + 1, :], one, p)
        local[b * LBLK:(b + 1) * LBLK, :] = _pack_exact_bf16_pairs(_dot(p, x))

    n_blocks = lrows // LBLK
    for b in range(n_blocks - 1):
        sort_block(b)
    pl.when(used_rows > (n_blocks - 1) * LBLK)(functools.partial(sort_block, n_blocks - 1))

    _set_pending(pending_ref, slot, _start_run_copies(tab_ref, gstart_ref, local, sorted_hbm, sem.at[slot], True))

    @pl.when(i == n - 1)
    def _():
        for s in range(2):
            _wait_run_copies(_get_pending(pending_ref, s), local_ref.at[s], sorted_hbm, sem.at[s], True)


def _moe_scatter(gstart, gend, gfill, xw, lpos, tab, n_rows, tm):
    n_tiles = xw.shape[0] // tm
    lrows = _local_rows(tm)
    any_spec = pl.BlockSpec(memory_space=pl.ANY)
    return pl.pallas_call(
        _moe_scatter_kernel,
        grid_spec=pltpu.PrefetchScalarGridSpec(
            num_scalar_prefetch=3,
            grid=(n_tiles,),
            in_specs=[pl.BlockSpec((tm, PACK_W), lambda i, *_: (i, 0)),
                      pl.BlockSpec((None, TOP_K, tm), lambda i, *_: (i, 0, 0)),
                      pl.BlockSpec((None, 8, LANES), lambda i, *_: (i, 0, 0), memory_space=pltpu.SMEM)],
            out_specs=any_spec,
            scratch_shapes=[pltpu.VMEM((2, lrows, PACK_W), jnp.uint32), pltpu.SemaphoreType.DMA((2,)),
                            pltpu.SemaphoreType.DMA, pltpu.SMEM((2, 2), jnp.int32)]),
        out_shape=jax.ShapeDtypeStruct((n_rows, PACK_W), jnp.uint32),
        compiler_params=_cparams("arbitrary"),
        name="moe_scatter",
    )(gstart, gend, gfill, xw, lpos, tab)


def _expert2_kernel(be_ref, nu_ref, xs_ref, wg_ref, wu_ref, wd_ref, ys_ref, wgu_b, wd_b):
    j = pl.program_id(0)

    @pl.when(jnp.logical_or(j == 0, be_ref[j] != be_ref[jnp.maximum(j - 1, 0)]))
    def _():
        wgu_b[:, :EXPERT_FF] = wg_ref[...].astype(BF16)
        wgu_b[:, EXPERT_FF:] = wu_ref[...].astype(BF16)
        wd_b[...] = wd_ref[...].astype(BF16)

    @pl.when(j < nu_ref[0])
    def _():
        x = _unpack_bf16_pairs(xs_ref[...]).astype(BF16)
        gu = _dot(x, wgu_b[...])
        a = _silu(gu[:, :EXPERT_FF]) * gu[:, EXPERT_FF:]
        ys_ref[...] = _pack_bf16_pairs(_dot(a.astype(BF16), wd_b[...]))


def _experts2(xs, block_expert, n_used, lw, layer, eblk):
    n_rows = xs.shape[0]
    ff = EXPERT_FF

    def blk(j, be, nu):
        return (jnp.minimum(j, jnp.maximum(nu[0] - 1, 0)), 0)

    def wblk(j, be, nu):
        return (layer, be[j], 0, 0)

    return pl.pallas_call(
        _expert2_kernel,
        grid_spec=pltpu.PrefetchScalarGridSpec(
            num_scalar_prefetch=2,
            grid=(n_used[0],),
            in_specs=[pl.BlockSpec((eblk, PACK_W), blk),
                      pl.BlockSpec((None, None, D_MODEL, ff), wblk),
                      pl.BlockSpec((None, None, D_MODEL, ff), wblk),
                      pl.BlockSpec((None, None, ff, D_MODEL), wblk)],
            out_specs=pl.BlockSpec((eblk, PACK_W), blk),
            scratch_shapes=[pltpu.VMEM((D_MODEL, 2 * ff), BF16), pltpu.VMEM((ff, D_MODEL), BF16)]),
        out_shape=jax.ShapeDtypeStruct((n_rows, PACK_W), jnp.uint32),
        compiler_params=_cparams("arbitrary"),
        name="moe_experts",
    )(block_expert, n_used, xs, lw["w_exp_gate"], lw["w_exp_up"], lw["w_exp_down"])


def _moe_combine_kernel(gstart_ref, x1_ref, xw_ref, lpos_ref, wrow_ref, tab_ref, tab_next_ref, mod_ref,
                        sg_ref, su_ref, sd_ref, ln2g_ref, ln2b_ref, sorted_hbm, o_ref,
                        local_ref, sem, pending_ref):
    i = pl.program_id(0)
    n = pl.num_programs(0)
    slot = i & 1
    tm = x1_ref.shape[0]
    lrows = local_ref.shape[1]

    @pl.when(i == 0)
    def _():
        local_ref[...] = jnp.zeros_like(local_ref)
        _set_pending(pending_ref, 0,
                     _start_run_copies(tab_ref, gstart_ref, local_ref.at[0], sorted_hbm, sem.at[0], False))

    @pl.when(i + 1 < n)
    def _():
        nxt = 1 - slot
        _set_pending(pending_ref, nxt, _start_run_copies(tab_next_ref, gstart_ref, local_ref.at[nxt], sorted_hbm,
                                                         sem.at[nxt], False))

    h = _unpack_bf16_pairs(xw_ref[...]).astype(BF16)
    a = _silu(_dot(h, sg_ref[...])) * _dot(h, su_ref[...])
    acc = _dot(a.astype(BF16), sd_ref[...])

    local = local_ref.at[slot]
    _wait_run_copies(_get_pending(pending_ref, slot), local, sorted_hbm, sem.at[slot], False)

    lpos = lpos_ref[...].astype(jnp.int16)
    wrow = wrow_ref[...].astype(BF16)

    def unsort_block(b, acc):
        riota = lax.broadcasted_iota(jnp.int16, (LBLK, tm), 0) + b * LBLK
        q = jnp.zeros((LBLK, tm), BF16)
        for k in range(TOP_K):
            q = jnp.where(riota == lpos[k:k + 1, :], jnp.broadcast_to(wrow[k:k + 1, :], (LBLK, tm)), q)
        y = _unpack_bf16_pairs(local[b * LBLK:(b + 1) * LBLK, :]).astype(BF16)
        return acc + _dot_tn(q, y)

    for b in range(lrows // LBLK):
        acc = unsort_block(b, acc)

    g2 = mod_ref[...][:, 5 * D_MODEL:6 * D_MODEL]
    u = DEEPNORM_ALPHA * x1_ref[...] + g2 * acc
    o_ref[...] = _ln(u) * ln2g_ref[...] + ln2b_ref[...]


def _moe_combine(gstart, x1, xw, lpos, wrow, tab, ys, mod3, mod_row, lw, tm):
    t = x1.shape[0]
    n_tiles = t // tm
    lrows = _local_rows(tm)

    def full(a):
        return pl.BlockSpec(a.shape, lambda i, *_: (0,) * a.ndim)

    tab_blk = lambda f: pl.BlockSpec((None, 8, LANES), f, memory_space=pltpu.SMEM)
    return pl.pallas_call(
        _moe_combine_kernel,
        grid_spec=pltpu.PrefetchScalarGridSpec(
            num_scalar_prefetch=1,
            grid=(n_tiles,),
            in_specs=[pl.BlockSpec((tm, D_MODEL), lambda i, *_: (i, 0)),
                      pl.BlockSpec((tm, PACK_W), lambda i, *_: (i, 0)),
                      pl.BlockSpec((None, TOP_K, tm), lambda i, *_: (i, 0, 0)),
                      pl.BlockSpec((None, TOP_K, tm), lambda i, *_: (i, 0, 0)),
                      tab_blk(lambda i, *_: (i, 0, 0)),
                      tab_blk(lambda i, *_: (jnp.minimum(i + 1, n_tiles - 1), 0, 0)),
                      pl.BlockSpec((None, 1, 6 * D_MODEL), lambda i, *_: (mod_row(i, tm), 0, 0)),
                      full(lw["w_sh_gate"]), full(lw["w_sh_up"]), full(lw["w_sh_down"]),
                      full(lw["ln2_g"]), full(lw["ln2_b"]),
                      pl.BlockSpec(memory_space=pl.ANY)],
            out_specs=pl.BlockSpec((tm, D_MODEL), lambda i, *_: (i, 0)),
            scratch_shapes=[pltpu.VMEM((2, lrows, PACK_W), jnp.uint32), pltpu.SemaphoreType.DMA((2,)),
                            pltpu.SMEM((2, 2), jnp.int32)]),
        out_shape=jax.ShapeDtypeStruct((t, D_MODEL), F32),
        compiler_params=_cparams("arbitrary"),
        name="moe_combine",
    )(gstart, x1, xw, lpos, wrow, tab, tab, mod3, lw["w_sh_gate"], lw["w_sh_up"], lw["w_sh_down"],
      lw["ln2_g"], lw["ln2_b"], ys)


def _moe2(x1, xw, wrow, lpos, tab, chunks_f, mod3, mod_row, lw, layer, tm):
    t = x1.shape[0]
    n_tiles = t // tm
    eblk = max(LBLK, min(EBLK_MAX, t * TOP_K // N_EXPERTS))
    region_rows = chunks_f[:, 0].astype(jnp.int32) * CHUNK
    padded = ((region_rows + eblk - 1) // eblk) * eblk
    end = jnp.cumsum(padded)
    start = end - padded
    n_blocks = -(-(t * TOP_K + n_tiles * N_EXPERTS * (CHUNK - 1)) // eblk) + N_EXPERTS
    n_used = end[-1] // eblk
    blk_row = jnp.minimum(jnp.arange(n_blocks, dtype=jnp.int32), jnp.maximum(n_used - 1, 0)) * eblk
    block_expert = jnp.minimum(jnp.sum((end[None, :] <= blk_row[:, None]).astype(jnp.int32), axis=1),
                               N_EXPERTS - 1).astype(jnp.int32)
    gstart = start // CHUNK
    gfill = (padded - region_rows + LBLK - 1) // LBLK
    xs = _moe_scatter(gstart, end, gfill, xw, lpos, tab, n_blocks * eblk, tm)
    ys = _experts2(xs, block_expert, n_used.reshape(1), lw, layer, eblk)
    return _moe_combine(gstart, x1, xw, lpos, wrow, tab, ys, mod3, mod_row, lw, tm)


def _channel_dft_table():
    k = np.arange(FNET_CH, dtype=np.float64)
    ang = 2.0 * np.pi * np.outer(k, k) / FNET_CH
    eye = np.eye(FNET_GROUPS)
    return np.concatenate([np.kron(eye, np.cos(ang)), -np.kron(eye, np.sin(ang))], axis=1)


def _direct_dft_tables(n):
    k = np.arange(n, dtype=np.float64)
    ang = 2.0 * np.pi * (np.outer(k, k) % n) / n
    scale = 1.0 / math.sqrt(n * FNET_CH)
    return np.cos(ang) * scale, np.sin(ang) * scale


def _two_stage_dft_tables(n):
    n1 = FFT_N1
    n2 = n // n1
    a = np.arange(n1, dtype=np.float64)
    ang1 = 2.0 * np.pi * (np.outer(a, a) % n1) / n1
    k1 = np.arange(n1).reshape(n1, 1, 1)
    k2 = np.arange(n2).reshape(1, n2, 1)
    m2 = np.arange(n2).reshape(1, 1, n2)
    ang2 = 2.0 * np.pi * ((m2 * (k1 + n1 * k2)) % n) / n
    scale = 1.0 / math.sqrt(n * FNET_CH)
    return np.cos(ang1), np.sin(ang1), np.cos(ang2) * scale, np.sin(ang2) * scale


def _grid_sincos_table(rows, d):
    quarter = d // 4
    omega = 1.0 / (POS_BASE ** (np.arange(quarter, dtype=np.float64) / quarter))
    r = np.arange(rows, dtype=np.float64)[:, None] * omega
    c = np.arange(GRID_W, dtype=np.float64)[:, None] * omega
    return (np.concatenate([np.sin(r), np.cos(r)], axis=-1).astype(np.float32),
            np.concatenate([np.sin(c), np.cos(c)], axis=-1).astype(np.float32))


def _layer_weights(l, w_in, w_gla_a, b_gla_a, gla_norm_g, sgu_norm_g, sgu_norm_b, w_sgu, b_sgu, w_conv,
                   w_out, ln1_g, ln1_b, ln2_g, ln2_b, w_router, router_bias,
                   w_exp_gate, w_exp_up, w_exp_down, w_sh_gate, w_sh_up, w_sh_down):
    wi = w_in[l]
    lr0 = 4 * HEAD_W
    w_in_p = jnp.concatenate(
        [wi[:, :lr0], wi[:, lr0 + 2 * GLA_LR:lr0 + 2 * GLA_LR + 5 * HEAD_W], wi[:, lr0:lr0 + 2 * GLA_LR],
         jnp.zeros((D_MODEL, LR_W - 2 * GLA_LR), F32), wi[:, lr0 + 2 * GLA_LR + 5 * HEAD_W:]], axis=1).astype(BF16)
    wa_pad = jnp.zeros((LR_W, 2 * HEAD_W), F32)
    wa_pad = wa_pad.at[:GLA_LR, :HEAD_W].set(w_gla_a[l, 0])
    wa_pad = wa_pad.at[GLA_LR:2 * GLA_LR, HEAD_W:].set(w_gla_a[l, 1])
    w_router_pad = jnp.concatenate([w_router[l], jnp.zeros((D_MODEL, LANES - N_EXPERTS), F32)], axis=1)
    w_router_hi = w_router_pad.astype(BF16)
    row = lambda a: a[l].reshape(1, -1)
    return {
        "w_in_p": w_in_p,
        "wa_hi": wa_pad.astype(BF16), "wa_lo": (wa_pad - wa_pad.astype(BF16).astype(F32)).astype(BF16),
        "ba": jnp.concatenate([b_gla_a[l, 0], b_gla_a[l, 1]]).reshape(1, 2 * HEAD_W),
        "gla_norm_g": row(gla_norm_g), "sgu_norm_g": row(sgu_norm_g), "sgu_norm_b": row(sgu_norm_b),
        "w_sgu_cat": jnp.concatenate([w_sgu[l, g] for g in range(GMLP_GROUPS)], axis=1).astype(BF16),
        "b_sgu_full": jnp.repeat(b_sgu[l].T, HEAD_W // GMLP_GROUPS, axis=1),
        "w_conv": w_conv[l],
        "w_out": w_out[l].astype(BF16),
        "ln1_g": row(ln1_g), "ln1_b": row(ln1_b), "ln2_g": row(ln2_g), "ln2_b": row(ln2_b),
        "w_router_hi": w_router_hi, "w_router_lo": (w_router_pad - w_router_hi.astype(F32)).astype(BF16),
        "router_bias": router_bias[l].reshape(N_EXPERTS, 1),
        "w_exp_gate": w_exp_gate, "w_exp_up": w_exp_up, "w_exp_down": w_exp_down,
        "w_sh_gate": w_sh_gate[l].astype(BF16), "w_sh_up": w_sh_up[l].astype(BF16),
        "w_sh_down": w_sh_down[l].astype(BF16),
    }


def _state_to_blockdiag_t(s):
    bsz = s.shape[0]
    st = jnp.swapaxes(s, 2, 3)
    eye = jnp.eye(GLA_HEADS, dtype=s.dtype)
    return jnp.einsum("bhvd,hg->bhvgd", st, eye).reshape(bsz, HEAD_W, HEAD_W)


def _blockdiag_t_to_state(st):
    bsz = st.shape[0]
    s5 = st.reshape(bsz, GLA_HEADS, GLA_DK, GLA_HEADS, GLA_DK)
    diag = jnp.stack([s5[:, h, :, h, :] for h in range(GLA_HEADS)], axis=1)
    return jnp.swapaxes(diag, 2, 3)


def _trunk_layer(x3, pos, mod3, mod_row, lw, layer, st0, emit_final, tabs):
    bsz, n, _ = x3.shape
    t = bsz * n
    outs = _in_proj(x3.reshape(t, D_MODEL), pos, mod3, mod_row, lw["w_in_p"], tabs["cs"], tm=512)
    if pos is not None:
        proj, zr, zi, x2d = outs
    else:
        proj, zr, zi = outs
        x2d = x3.reshape(t, D_MODEL)
    gla_out = _gla(proj.reshape(bsz, n, STORE_W), lw["wa_hi"], lw["wa_lo"], lw["ba"], st0, emit_final)
    o_f, o_b = gla_out[:2]
    zr3 = zr.reshape(bsz, n, HEAD_W)
    zi3 = zi.reshape(bsz, n, HEAD_W)
    if "two_stage" in tabs:
        yft = _fft_two_stage(zr3, zi3, tabs["two_stage"])
    else:
        yft = _fft_direct(zr3, zi3, *tabs["direct"])
    x1, xw, wrow, lpos, tab, chunks = _mix_out(x2d, o_f.reshape(t, HEAD_W), o_b.reshape(t, HEAD_W), proj,
                                             yft.reshape(t, HEAD_W), mod3, mod_row, lw, seq_len=n, tm=256)
    x2 = _moe2(x1, xw, wrow, lpos, tab, chunks, mod3, mod_row, lw, layer, tm=min(256, t))
    return x2.reshape(bsz, n, D_MODEL), gla_out[2:]


def kernel(x_prompt, x_sample, c, state_gla, c_ctx, w_ada, b_ada, w_in, w_gla_a, b_gla_a, gla_norm_g, sgu_norm_g, sgu_norm_b, w_sgu, b_sgu, w_conv, w_out, ln1_g, ln1_b, ln2_g, ln2_b, w_router, router_bias, w_exp_gate, w_exp_up, w_exp_down, w_sh_gate, w_sh_up, w_sh_down):
    n_layers = w_ada.shape[0]
    bp, np_, _ = x_prompt.shape
    bs, ns, _ = x_sample.shape
    assert bs <= 7

    cond8 = jnp.concatenate([c_ctx[None, :], c, jnp.zeros((7 - bs, D_MODEL), F32)], axis=0)
    mod = _ada_mod(cond8, w_ada, b_ada)

    tabs_p = {"cs": jnp.asarray(_channel_dft_table(), BF16),
              "direct": tuple(jnp.asarray(a, BF16) for a in _direct_dft_tables(np_))}
    tabs_s = {"cs": tabs_p["cs"],
              "two_stage": tuple(jnp.asarray(a, BF16) for a in _two_stage_dft_tables(ns))}
    rtab, ctab = _grid_sincos_table(ns // GRID_W, D_MODEL)
    pos = jnp.concatenate([jnp.repeat(jnp.asarray(rtab), GRID_W, axis=0),
                           jnp.tile(jnp.asarray(ctab), (ns // GRID_W, 1))], axis=-1)

    prompt_row = lambda i, tm: 0
    sample_row = lambda i, tm: 1 + (i * tm) // ns

    y_p = x_prompt
    y_s = x_sample
    finals = []
    for l in range(n_layers):
        lw = _layer_weights(l, w_in, w_gla_a, b_gla_a, gla_norm_g, sgu_norm_g, sgu_norm_b, w_sgu, b_sgu,
                            w_conv, w_out, ln1_g, ln1_b, ln2_g, ln2_b, w_router, router_bias,
                            w_exp_gate, w_exp_up, w_exp_down, w_sh_gate, w_sh_up, w_sh_down)
        mod3 = mod[l].reshape(8, 1, 6 * D_MODEL)
        y_p, fin = _trunk_layer(y_p, None, mod3, prompt_row, lw, l, None, True, tabs_p)
        finals.append(jnp.stack([_blockdiag_t_to_state(fin[0]), _blockdiag_t_to_state(fin[1])], axis=1))
        st0 = jnp.stack([_state_to_blockdiag_t(state_gla[:, l, 0]), _state_to_blockdiag_t(state_gla[:, l, 1])])
        y_s, _ = _trunk_layer(y_s, pos if l == 0 else None, mod3, sample_row, lw, l, st0, False, tabs_s)
    new_state = jnp.stack(finals, axis=1).astype(x_prompt.dtype)
    return (y_p, y_s, new_state)
```

```python
import functools
import math

import numpy as np
import jax
import jax.numpy as jnp
from jax import lax
from jax.experimental import pallas as pl
from jax.experimental.pallas import tpu as pltpu

F32 = jnp.float32
BF16 = jnp.bfloat16

D_MODEL = 1024
DEPTH = 2
GRID_W = 64
HEAD_W = 256
GLA_HEADS = 4
GLA_DK = 64
GLA_LR = 16
GLA_TAU = 16.0
GLA_CHUNK = 64
GMLP_GROUPS = 4
GMLP_CHUNK = 128
FNET_GROUPS = 4
FNET_CH = 64
N_EXPERTS = 64
TOP_K = 8
N_GROUPS = 8
TOPK_GROUPS = 4
EXPERT_FF = 256
ROUTED_SCALE = 2.5
DEEPNORM_ALPHA = (2 * DEPTH) ** 0.25
LN_EPS = 1e-5
RMS_EPS = 1e-6
POS_BASE = 10000.0

COL_Q, COL_K, COL_V, COL_G, COL_SGU, COL_SGV, COL_CVB, COL_CVC, COL_CVX = range(9)
LR_W = 128
STORE_W = 9 * HEAD_W + LR_W
PROJ_W = STORE_W + HEAD_W
COL_LR = (9 * HEAD_W) // LR_W
HALO = 16

LANES = 128
PACK_W = D_MODEL // 2
EBLK_MAX = 1024
LBLK = 256
MOE_TM = 256
CHUNK = 8
BIG_SHIFT = 2
BIG_ROWS = CHUNK << BIG_SHIFT
SEG = 256
FFT_N1 = 64
VMEM_LIMIT = 56 * 1024 * 1024


def _cparams(*sem):
    return pltpu.CompilerParams(dimension_semantics=sem, vmem_limit_bytes=VMEM_LIMIT)


def _ln(x):
    mu = jnp.mean(x, axis=-1, keepdims=True)
    xc = x - mu
    var = jnp.mean(xc * xc, axis=-1, keepdims=True)
    return xc * lax.rsqrt(var + LN_EPS)


def _sigmoid(x):
    return 1.0 / (1.0 + jnp.exp(-x))


def _silu(x):
    return x * _sigmoid(x)


def _pack_bf16_pairs(x):
    w = x.shape[1] // 2
    lo = lax.bitcast_convert_type(x[:, :w].astype(BF16).astype(F32), jnp.uint32)
    hi = lax.bitcast_convert_type(x[:, w:].astype(BF16).astype(F32), jnp.uint32)
    return (lo >> 16) | (hi & jnp.uint32(0xFFFF0000))


def _pack_exact_bf16_pairs(x):
    w = x.shape[1] // 2
    lo = lax.bitcast_convert_type(x[:, :w], jnp.uint32)
    hi = lax.bitcast_convert_type(x[:, w:], jnp.uint32)
    return (lo >> 16) | hi


def _unpack_bf16_pairs(u):
    lo = lax.bitcast_convert_type(u << 16, F32)
    hi = lax.bitcast_convert_type(u & jnp.uint32(0xFFFF0000), F32)
    return jnp.concatenate([lo, hi], axis=1)


def _dot(a, b):
    return jnp.dot(a, b, preferred_element_type=F32)


def _dot_nt(a, b):
    return lax.dot_general(a, b, (((1,), (1,)), ((), ())), preferred_element_type=F32)


def _dot_tn(a, b):
    return lax.dot_general(a, b, (((0,), (0,)), ((), ())), preferred_element_type=F32)


def _ada_kernel(c_ref, w_ref, b_ref, o_ref):
    c = c_ref[...]
    o_ref[...] = _dot(_silu(c).astype(BF16), w_ref[...].astype(BF16)) + b_ref[...]


def _ada_mod(cond8, w_ada, b_ada):
    n_l, d, w6 = w_ada.shape
    tn = 1536
    return pl.pallas_call(
        _ada_kernel,
        grid=(n_l, w6 // tn),
        in_specs=[pl.BlockSpec((8, d), lambda l, j: (0, 0)),
                  pl.BlockSpec((None, d, tn), lambda l, j: (l, 0, j)),
                  pl.BlockSpec((None, 1, tn), lambda l, j: (l, 0, j))],
        out_specs=pl.BlockSpec((None, 8, tn), lambda l, j: (l, 0, j)),
        out_shape=jax.ShapeDtypeStruct((n_l, 8, w6), F32),
        compiler_params=_cparams("parallel", "parallel"),
        name="ada_mod",
    )(cond8, w_ada, b_ada.reshape(n_l, 1, w6))


def _in_proj_kernel(*refs, has_pos):
    if has_pos:
        x_ref, pos_ref, mod_ref, w_ref, cs_ref, proj_ref, zr_ref, zi_ref, x0_ref = refs
        x = x_ref[...] + pos_ref[...]
        x0_ref[...] = x
    else:
        x_ref, mod_ref, w_ref, cs_ref, proj_ref, zr_ref, zi_ref = refs
        x = x_ref[...]
    mod = mod_ref[...]
    sh1 = mod[:, 0:D_MODEL]
    sc1 = mod[:, D_MODEL:2 * D_MODEL]
    h = _ln(x) * (1.0 + sc1) + sh1
    proj = _dot(h.astype(BF16), w_ref[...])
    proj_ref[...] = proj[:, :STORE_W].astype(BF16)
    ft = proj[:, STORE_W:].astype(BF16)
    z = _dot(ft, cs_ref[...])
    zr_ref[...] = z[:, :HEAD_W].astype(BF16)
    zi_ref[...] = z[:, HEAD_W:].astype(BF16)


def _in_proj(x2d, pos, mod3, mod_row, w_in_p, cs, tm):
    t = x2d.shape[0]
    tm = min(tm, t)
    in_specs = [pl.BlockSpec((tm, D_MODEL), lambda i: (i, 0))]
    args = [x2d]
    out_shape = [jax.ShapeDtypeStruct((t, STORE_W), BF16),
                 jax.ShapeDtypeStruct((t, HEAD_W), BF16),
                 jax.ShapeDtypeStruct((t, HEAD_W), BF16)]
    out_specs = [pl.BlockSpec((tm, STORE_W), lambda i: (i, 0)),
                 pl.BlockSpec((tm, HEAD_W), lambda i: (i, 0)),
                 pl.BlockSpec((tm, HEAD_W), lambda i: (i, 0))]
    if pos is not None:
        n_pos = pos.shape[0] // tm
        in_specs.append(pl.BlockSpec((tm, D_MODEL), lambda i: (i % n_pos, 0)))
        args.append(pos)
        out_shape.append(jax.ShapeDtypeStruct((t, D_MODEL), F32))
        out_specs.append(pl.BlockSpec((tm, D_MODEL), lambda i: (i, 0)))
    in_specs += [pl.BlockSpec((None, 1, 6 * D_MODEL), lambda i: (mod_row(i, tm), 0, 0)),
                 pl.BlockSpec((D_MODEL, PROJ_W), lambda i: (0, 0)),
                 pl.BlockSpec((HEAD_W, 2 * HEAD_W), lambda i: (0, 0))]
    args += [mod3, w_in_p, cs]
    return pl.pallas_call(
        functools.partial(_in_proj_kernel, has_pos=pos is not None),
        grid=(t // tm,),
        in_specs=in_specs,
        out_specs=out_specs,
        out_shape=out_shape,
        compiler_params=_cparams("parallel"),
        name="in_proj",
    )(*args)


def _gla_masks(reverse):
    i = np.arange(SEG)
    same_chunk = (i[:, None] // GLA_CHUNK) == (i[None, :] // GLA_CHUNK)
    tri = same_chunk & ((i[None, :] >= i[:, None]) if reverse else (i[None, :] <= i[:, None]))
    l_idx = i[:, None] % GLA_CHUNK
    m_idx = np.arange(GLA_CHUNK)[None, :]
    causal = (m_idx >= l_idx) if reverse else (m_idx <= l_idx)
    return (jnp.asarray(tri, BF16), jnp.asarray(same_chunk, BF16), jnp.asarray(same_chunk, F32),
            jnp.asarray(causal, F32))


def _gla_segment(q, k, v, pre, st_ref, o_ref, masks, reverse):
    seg = q.shape[0]
    n_chunks = seg // GLA_CHUNK
    la = (jnp.minimum(pre, 0.0) - jnp.log1p(jnp.exp(-jnp.abs(pre)))) * (1.0 / GLA_TAU)

    tri_m, ones_m, bd, causal = masks
    hi = la.astype(BF16)
    lo = (la - hi.astype(F32)).astype(BF16)
    b = _dot(tri_m, hi) + _dot(tri_m, lo)
    btot = _dot(ones_m, hi) + _dot(ones_m, lo)

    q_dec = q * (GLA_DK ** -0.5) * jnp.exp(b)
    k_inv = (k * jnp.exp(-b)).astype(BF16)
    k_end = (k * jnp.exp(btot - b)).astype(BF16)
    dec = jnp.exp(btot)
    vb = v.astype(BF16)
    keep = causal > 0.5

    st = st_ref[...]
    order = range(n_chunks - 1, -1, -1) if reverse else range(n_chunks)
    for ci in order:
        sl = slice(ci * GLA_CHUNK, (ci + 1) * GLA_CHUNK)
        qd = q_dec[sl]
        qbd = (jnp.concatenate([qd] * GLA_HEADS, axis=0) * bd).astype(BF16)
        a = _dot_nt(qbd, k_inv[sl])
        a = jnp.where(keep, a, 0.0)
        rr = _dot(a.astype(BF16), vb[sl])
        o = _dot_nt(qd.astype(BF16), st.astype(BF16))
        for h in range(GLA_HEADS):
            hs = slice(h * GLA_CHUNK, (h + 1) * GLA_CHUNK)
            o = o + rr[hs] * bd[hs]
        o_ref[sl, :] = o
        kvt = _dot_tn(vb[sl], k_end[sl])
        st = st * dec[ci * GLA_CHUNK:ci * GLA_CHUNK + 1, :] + kvt * bd
    st_ref[...] = st


def _gla_kernel(*refs, has_init, emit_final):
    qf, kf, vf, lrf, qb, kb, vb, lrb, wah_ref, wal_ref, ba_ref = refs[:11]
    mask_refs = refs[11:19]
    rest = refs[19:]
    if has_init:
        s0f, s0b = rest[:2]
        rest = rest[2:]
    of_ref, ob_ref = rest[:2]
    rest = rest[2:]
    if emit_final:
        sff, sfb = rest[:2]
        rest = rest[2:]
    stf, stb = rest

    s = pl.program_id(1)

    @pl.when(s == 0)
    def _():
        if has_init:
            stf[...] = s0f[...]
            stb[...] = s0b[...]
        else:
            stf[...] = jnp.zeros_like(stf)
            stb[...] = jnp.zeros_like(stb)

    def decay_pre(lr_ref):
        lr = lr_ref[...]
        return _dot(lr, wah_ref[...]) + _dot(lr, wal_ref[...]) + ba_ref[...]

    f32 = lambda ref: ref[...].astype(F32)

    masks_f = tuple(m[...] for m in mask_refs[:4])
    masks_b = tuple(m[...] for m in mask_refs[4:])
    _gla_segment(f32(qf), f32(kf), f32(vf), decay_pre(lrf)[:, :HEAD_W], stf, of_ref, masks_f, reverse=False)
    _gla_segment(f32(qb), f32(kb), f32(vb), decay_pre(lrb)[:, HEAD_W:], stb, ob_ref, masks_b, reverse=True)

    if emit_final:
        @pl.when(s == pl.num_programs(1) - 1)
        def _():
            sff[...] = stf[...]
            sfb[...] = stb[...]


def _gla(proj3, wa_hi, wa_lo, ba, st0, emit_final):
    bsz, n, _ = proj3.shape
    nseg = n // SEG

    def col(cb, width=HEAD_W, rev=False):
        if rev:
            return pl.BlockSpec((None, SEG, width), lambda b, s: (b, nseg - 1 - s, cb))
        return pl.BlockSpec((None, SEG, width), lambda b, s: (b, s, cb))

    in_specs = [col(COL_Q), col(COL_K), col(COL_V), col(COL_LR, LR_W),
                col(COL_Q, rev=True), col(COL_K, rev=True), col(COL_V, rev=True), col(COL_LR, LR_W, rev=True),
                pl.BlockSpec((LR_W, 2 * HEAD_W), lambda b, s: (0, 0)),
                pl.BlockSpec((LR_W, 2 * HEAD_W), lambda b, s: (0, 0)),
                pl.BlockSpec((1, 2 * HEAD_W), lambda b, s: (0, 0))]
    masks = _gla_masks(False) + _gla_masks(True)
    in_specs += [pl.BlockSpec(m.shape, lambda b, s: (0, 0)) for m in masks]
    args = [proj3] * 8 + [wa_hi, wa_lo, ba] + list(masks)
    st_spec = pl.BlockSpec((None, HEAD_W, HEAD_W), lambda b, s: (b, 0, 0))
    if st0 is not None:
        in_specs += [st_spec, st_spec]
        args += [st0[0], st0[1]]
    out_shape = [jax.ShapeDtypeStruct((bsz, n, HEAD_W), F32)] * 2
    out_specs = [pl.BlockSpec((None, SEG, HEAD_W), lambda b, s: (b, s, 0)),
                 pl.BlockSpec((None, SEG, HEAD_W), lambda b, s: (b, nseg - 1 - s, 0))]
    if emit_final:
        out_shape += [jax.ShapeDtypeStruct((bsz, HEAD_W, HEAD_W), F32)] * 2
        out_specs += [st_spec, st_spec]
    return pl.pallas_call(
        functools.partial(_gla_kernel, has_init=st0 is not None, emit_final=emit_final),
        grid=(bsz, nseg),
        in_specs=in_specs,
        out_specs=out_specs,
        out_shape=out_shape,
        scratch_shapes=[pltpu.VMEM((HEAD_W, HEAD_W), F32), pltpu.VMEM((HEAD_W, HEAD_W), F32)],
        compiler_params=_cparams("parallel", "arbitrary"),
        name="gla",
    )(*args)


def _fft_direct_kernel(zr_ref, zi_ref, cn_ref, sn_ref, o_ref):
    o_ref[...] = _dot(cn_ref[...], zr_ref[...]) + _dot(sn_ref[...], zi_ref[...])


def _fft_direct(zr3, zi3, cn, sn):
    bsz, n, w = zr3.shape
    blk = pl.BlockSpec((None, n, w), lambda b: (b, 0, 0))
    tab = pl.BlockSpec((n, n), lambda b: (0, 0))
    return pl.pallas_call(
        _fft_direct_kernel,
        grid=(bsz,),
        in_specs=[blk, blk, tab, tab],
        out_specs=blk,
        out_shape=jax.ShapeDtypeStruct((bsz, n, w), F32),
        compiler_params=_cparams("parallel"),
        name="fft_direct",
    )(zr3, zi3, cn, sn)


def _fft_a_kernel(zr_ref, zi_ref, c_ref, s_ref, gr_ref, gi_ref):
    zr = zr_ref[...]
    zi = zi_ref[...]
    cm = c_ref[...]
    sm = s_ref[...]
    gr_ref[...] = (_dot(cm, zr) + _dot(sm, zi)).astype(BF16)
    gi_ref[...] = (_dot(cm, zi) - _dot(sm, zr)).astype(BF16)


def _fft_c_kernel(gr_ref, gi_ref, mc_ref, ms_ref, o_ref):
    for j in range(gr_ref.shape[0]):
        o_ref[:, j, :] = _dot(mc_ref[j], gr_ref[j]) + _dot(ms_ref[j], gi_ref[j])


def _fft_two_stage(zr3, zi3, tabs):
    bsz, n, w = zr3.shape
    n1 = FFT_N1
    n2 = n // n1
    c1, s1, mc, ms = tabs
    tn = 2048
    wide = n2 * w
    blk = pl.BlockSpec((None, n1, tn), lambda b, j: (b, 0, j))
    tab = pl.BlockSpec((n1, n1), lambda b, j: (0, 0))
    gr, gi = pl.pallas_call(
        _fft_a_kernel,
        grid=(bsz, wide // tn),
        in_specs=[blk, blk, tab, tab],
        out_specs=[blk, blk],
        out_shape=[jax.ShapeDtypeStruct((bsz, n1, wide), BF16)] * 2,
        compiler_params=_cparams("parallel", "parallel"),
        name="fft_stage_a",
    )(zr3.reshape(bsz, n1, wide), zi3.reshape(bsz, n1, wide), c1, s1)
    kb = 8
    gblk = pl.BlockSpec((None, kb, n2, w), lambda b, j: (b, j, 0, 0))
    mblk = pl.BlockSpec((kb, n2, n2), lambda b, j: (j, 0, 0))
    out = pl.pallas_call(
        _fft_c_kernel,
        grid=(bsz, n1 // kb),
        in_specs=[gblk, gblk, mblk, mblk],
        out_specs=pl.BlockSpec((None, n2, kb, w), lambda b, j: (b, 0, j, 0)),
        out_shape=jax.ShapeDtypeStruct((bsz, n2, n1, w), F32),
        compiler_params=_cparams("parallel", "parallel"),
        name="fft_stage_c",
    )(gr.reshape(bsz, n1, n2, w), gi.reshape(bsz, n1, n2, w), mc, ms)
    return out.reshape(bsz, n, w)


def _mix_masks(tm):
    h = np.arange(HEAD_W) // GLA_DK
    head_mean = (h[:, None] == h[None, :]) / GLA_DK
    rg = np.arange(GMLP_GROUPS * GMLP_CHUNK) // GMLP_CHUNK
    cg = np.arange(HEAD_W) // (HEAD_W // GMLP_GROUPS)
    t = np.arange(tm)
    e = np.arange(N_EXPERTS)
    return (jnp.asarray(head_mean, BF16), jnp.asarray(rg[:, None] == cg[None, :], F32),
            jnp.asarray(t[:, None] < t[None, :], BF16), jnp.asarray(e[None, :] < e[:, None], BF16))


def _route(logits, bias, before, lower):
    tm = logits.shape[1]
    s = _sigmoid(logits)
    biased = s + bias
    neg = -jnp.inf
    rows = lax.broadcasted_iota(jnp.int32, (8, tm), 0)

    def first_argmax(x, ids, sentinel):
        m = jnp.max(x, axis=0, keepdims=True)
        return m, jnp.min(jnp.where(x == m, ids, sentinel), axis=0, keepdims=True)

    gs_rows = []
    for g in range(N_GROUPS):
        x = biased[8 * g:8 * g + 8]
        m1, i1 = first_argmax(x, rows, 8)
        m2 = jnp.max(jnp.where(rows == i1, neg, x), axis=0, keepdims=True)
        gs_rows.append(m1 + m2)
    gs = jnp.concatenate(gs_rows, axis=0)
    gsel = jnp.zeros((N_GROUPS, tm), F32)
    for _ in range(TOPK_GROUPS):
        _, i = first_argmax(gs, rows, 8)
        hit = rows == i
        gsel = jnp.where(hit, 1.0, gsel)
        gs = jnp.where(hit, neg, gs)

    xs = [jnp.where(gsel[g:g + 1] > 0.0, biased[8 * g:8 * g + 8], neg) for g in range(N_GROUPS)]
    ids = [rows + 8 * g for g in range(N_GROUPS)]
    sel = [jnp.zeros((8, tm), F32) for _ in range(N_GROUPS)]
    eids = []
    for _ in range(TOP_K):
        m = xs[0]
        for g in range(1, N_GROUPS):
            m = jnp.maximum(m, xs[g])
        m = jnp.max(m, axis=0, keepdims=True)
        cand = jnp.where(xs[0] == m, ids[0], N_EXPERTS)
        for g in range(1, N_GROUPS):
            cand = jnp.minimum(cand, jnp.where(xs[g] == m, ids[g], N_EXPERTS))
        i = jnp.min(cand, axis=0, keepdims=True)
        eids.append(i)
        for g in range(N_GROUPS):
            hit = ids[g] == i
            sel[g] = jnp.where(hit, 1.0, sel[g])
            xs[g] = jnp.where(hit, neg, xs[g])

    sel_all = jnp.concatenate(sel, axis=0)
    seen = _dot(sel_all.astype(BF16), before)
    counts = jnp.sum(sel_all, axis=1, keepdims=True)

    n_chunks = jnp.ceil(counts * (1.0 / CHUNK))
    run_start = _dot(lower, jnp.broadcast_to(n_chunks, (N_EXPERTS, LANES)).astype(BF16))[:, 0:1] * CHUNK
    local_pos = seen + run_start

    def pick(k, table):
        acc = None
        for g in range(N_GROUPS):
            v = jnp.where(ids[g] == eids[k], table[8 * g:8 * g + 8], 0.0)
            acc = v if acc is None else acc + v
        return jnp.sum(acc, axis=0, keepdims=True)

    w_raw = [pick(k, s) for k in range(TOP_K)]
    lpos = [pick(k, local_pos) for k in range(TOP_K)]
    tot = w_raw[0]
    for k in range(1, TOP_K):
        tot = tot + w_raw[k]
    weights = [w / tot * ROUTED_SCALE for w in w_raw]
    return weights, lpos, n_chunks, run_start


def _mix_out_kernel(x_ref, of_ref, ob_ref, g_ref, su_ref, sv_ref, cb_ref, cc_ref, cx_ref,
                    ccp_ref, cxp_ref, ccn_ref, cxn_ref, ft_ref, mod_ref,
                    glag_ref, sgng_ref, sgnb_ref, wsgu_ref, bsgu_ref, wconv_ref, wout_ref,
                    ln1g_ref, ln1b_ref, wrh_ref, wrl_ref, rb_ref, hmean_ref, sgubd_ref, before_ref, lower_ref,
                    x1_ref, xw_ref, wrow_ref, lpos_ref, tab_ref, cnt_ref, carry_ref, *, seq_len):
    tm = x_ref.shape[0]
    i = pl.program_id(0)

    @pl.when(i == 0)
    def _():
        carry_ref[...] = jnp.zeros_like(carry_ref)

    mod = mod_ref[...]
    g1 = mod[:, 2 * D_MODEL:3 * D_MODEL]
    sh2 = mod[:, 3 * D_MODEL:4 * D_MODEL]
    sc2 = mod[:, 4 * D_MODEL:5 * D_MODEL]

    o = of_ref[...] + ob_ref[...]
    head_mean = hmean_ref[...]
    o2 = o * o
    o2_hi = o2.astype(BF16)
    o2_lo = (o2 - o2_hi.astype(F32)).astype(BF16)
    ms = _dot(o2_hi, head_mean) + _dot(o2_lo, head_mean)
    y_gla = o * lax.rsqrt(ms + RMS_EPS) * glag_ref[...] * _silu(g_ref[...].astype(F32))

    vn = _ln(sv_ref[...].astype(F32)) * sgng_ref[...] + sgnb_ref[...]
    sgu_bd = sgubd_ref[...]
    sp_parts = []
    for j in range(tm // GMLP_CHUNK):
        vc = vn[j * GMLP_CHUNK:(j + 1) * GMLP_CHUNK]
        vbd = (jnp.concatenate([vc] * GMLP_GROUPS, axis=0) * sgu_bd).astype(BF16)
        sp_parts.append(_dot(wsgu_ref[...], vbd) + bsgu_ref[...])
    y_sgu = su_ref[...].astype(F32) * jnp.concatenate(sp_parts, axis=0)

    z = cc_ref[...].astype(F32) * cx_ref[...].astype(F32)
    z_before = (ccp_ref[...].astype(F32) * cxp_ref[...].astype(F32))[HALO - 1:HALO, :]
    z_after = (ccn_ref[...].astype(F32) * cxn_ref[...].astype(F32))[0:1, :]
    row = lax.broadcasted_iota(jnp.int32, (tm, HEAD_W), 0)
    pos = (i * tm + row) & (seq_len - 1)
    z_prev = jnp.where(row == 0, z_before, pltpu.roll(z, 1, 0))
    z_next = jnp.where(row == tm - 1, z_after, pltpu.roll(z, tm - 1, 0))
    z_prev = jnp.where(pos == 0, 0.0, z_prev)
    z_next = jnp.where(pos == seq_len - 1, 0.0, z_next)
    wconv = wconv_ref[...]
    y_conv = cb_ref[...].astype(F32) * (wconv[0:1] * z_prev + wconv[1:2] * z + wconv[2:3] * z_next)

    y = (_dot(y_gla.astype(BF16), wout_ref[0:HEAD_W, :])
         + _dot(y_sgu.astype(BF16), wout_ref[HEAD_W:2 * HEAD_W, :])
         + _dot(y_conv.astype(BF16), wout_ref[2 * HEAD_W:3 * HEAD_W, :])
         + _dot(ft_ref[...].astype(BF16), wout_ref[3 * HEAD_W:4 * HEAD_W, :]))
    x1 = _ln(DEEPNORM_ALPHA * x_ref[...] + g1 * y) * ln1g_ref[...] + ln1b_ref[...]
    x1_ref[...] = x1
    h2 = _ln(x1) * (1.0 + sc2) + sh2
    xw_ref[...] = _pack_bf16_pairs(h2)

    h2_hi = h2.astype(BF16)
    h2_lo = (h2 - h2_hi.astype(F32)).astype(BF16)
    logits = _dot(h2_hi, wrh_ref[...]) + _dot(h2_hi, wrl_ref[...]) + _dot(h2_lo, wrh_ref[...])
    weights, lpos, n_chunks, run_start = _route(logits.T[:N_EXPERTS], rb_ref[...], before_ref[...], lower_ref[...])
    lpos_ref[...] = jnp.concatenate(lpos, axis=0).astype(jnp.int32)
    wrow_ref[...] = jnp.concatenate(weights, axis=0)
    carry = carry_ref[:, 0:1]
    lane = lax.broadcasted_iota(jnp.int32, (N_EXPERTS, LANES), 1)
    total = jnp.sum(n_chunks, axis=0, keepdims=True)
    cols = jnp.where(lane == 0, n_chunks, jnp.where(lane == 1, run_start,
                                                   jnp.where(lane == 2, carry, jnp.where(lane == 3, total, 0.0))))
    tab = jnp.concatenate([cols, jnp.zeros((LANES - N_EXPERTS, LANES), F32)], axis=0).T
    tab_ref[...] = tab[0:8].astype(jnp.int32)
    new_carry = carry_ref[...] + n_chunks
    carry_ref[...] = new_carry
    cnt_ref[...] = new_carry


def _mix_out(x2d, of2d, ob2d, proj, yft2d, mod3, mod_row, lw, seq_len, tm):
    t = x2d.shape[0]
    tm = min(tm, t)
    nt8 = t // HALO
    rows8 = tm // HALO

    def col(cb):
        return pl.BlockSpec((tm, HEAD_W), lambda i: (i, cb))

    def halo_prev(cb):
        return pl.BlockSpec((HALO, HEAD_W), lambda i: (jnp.maximum(i * rows8 - 1, 0), cb))

    def halo_next(cb):
        return pl.BlockSpec((HALO, HEAD_W), lambda i: (jnp.minimum((i + 1) * rows8, nt8 - 1), cb))

    def full(a):
        return pl.BlockSpec(a.shape, lambda i: (0,) * a.ndim)

    tok_d = pl.BlockSpec((tm, D_MODEL), lambda i: (i, 0))
    tok_h = pl.BlockSpec((tm, HEAD_W), lambda i: (i, 0))
    weights = [lw["gla_norm_g"], lw["sgu_norm_g"], lw["sgu_norm_b"], lw["w_sgu_cat"], lw["b_sgu_full"],
               lw["w_conv"], lw["w_out"], lw["ln1_g"], lw["ln1_b"], lw["w_router_hi"], lw["w_router_lo"], lw["router_bias"]]
    weights += list(_mix_masks(tm))
    in_specs = ([tok_d, tok_h, tok_h, col(COL_G), col(COL_SGU), col(COL_SGV), col(COL_CVB), col(COL_CVC),
                 col(COL_CVX), halo_prev(COL_CVC), halo_prev(COL_CVX), halo_next(COL_CVC), halo_next(COL_CVX),
                 tok_h, pl.BlockSpec((None, 1, 6 * D_MODEL), lambda i: (mod_row(i, tm), 0, 0))]
                + [full(w) for w in weights])
    args = [x2d, of2d, ob2d] + [proj] * 10 + [yft2d, mod3] + weights
    return pl.pallas_call(
        functools.partial(_mix_out_kernel, seq_len=seq_len),
        grid=(t // tm,),
        in_specs=in_specs,
        out_specs=[tok_d,
                   pl.BlockSpec((tm, PACK_W), lambda i: (i, 0)),
                   pl.BlockSpec((None, TOP_K, tm), lambda i: (i, 0, 0)),
                   pl.BlockSpec((None, TOP_K, tm), lambda i: (i, 0, 0)),
                   pl.BlockSpec((None, 8, LANES), lambda i: (i, 0, 0)),
                   pl.BlockSpec((N_EXPERTS, LANES), lambda i: (0, 0))],
        out_shape=[jax.ShapeDtypeStruct((t, D_MODEL), F32),
                   jax.ShapeDtypeStruct((t, PACK_W), jnp.uint32),
                   jax.ShapeDtypeStruct((t // tm, TOP_K, tm), F32),
                   jax.ShapeDtypeStruct((t // tm, TOP_K, tm), jnp.int32),
                   jax.ShapeDtypeStruct((t // tm, 8, LANES), jnp.int32),
                   jax.ShapeDtypeStruct((N_EXPERTS, LANES), F32)],
        scratch_shapes=[pltpu.VMEM((N_EXPERTS, LANES), F32)],
        compiler_params=_cparams("arbitrary"),
        name="mix_out",
    )(*args)


def _local_rows(tm):
    need = tm * TOP_K + N_EXPERTS * (CHUNK - 1)
    return -(-need // LBLK) * LBLK


def _run_copy(local_ref, sorted_hbm, sem, local_row, sorted_row, to_sorted, rows=CHUNK):
    loc = local_ref.at[pl.ds(pl.multiple_of(local_row, CHUNK), rows)]
    srt = sorted_hbm.at[pl.ds(pl.multiple_of(sorted_row, CHUNK), rows)]
    return pltpu.make_async_copy(loc, srt, sem) if to_sorted else pltpu.make_async_copy(srt, loc, sem)


def _start_run_copies(tab_ref, gstart_ref, local_ref, sorted_hbm, sem, to_sorted):
    def per_expert(e, totals):
        n = tab_ref[0, e]
        l0 = tab_ref[1, e]
        g0 = (gstart_ref[e] + tab_ref[2, e]) * CHUNK
        n_big = n >> BIG_SHIFT
        n_small = n & ((1 << BIG_SHIFT) - 1)

        def big(j, c):
            _run_copy(local_ref, sorted_hbm, sem, l0 + j * BIG_ROWS, g0 + j * BIG_ROWS, to_sorted, BIG_ROWS).start()
            return c

        lax.fori_loop(0, n_big, big, 0)
        l1 = l0 + n_big * BIG_ROWS
        g1 = g0 + n_big * BIG_ROWS

        def small(j, c):
            _run_copy(local_ref, sorted_hbm, sem, l1 + j * CHUNK, g1 + j * CHUNK, to_sorted).start()
            return c

        lax.fori_loop(0, n_small, small, 0)
        return totals[0] + n_big, totals[1] + n_small

    return lax.fori_loop(0, N_EXPERTS, per_expert, (jnp.int32(0), jnp.int32(0)))


def _get_pending(pending_ref, s):
    return pending_ref[s, 0], pending_ref[s, 1]


def _set_pending(pending_ref, s, counts):
    pending_ref[s, 0] = counts[0]
    pending_ref[s, 1] = counts[1]


def _wait_run_copies(counts, local_ref, sorted_hbm, sem, to_sorted):
    batch = 8

    def wait_n(rows, reps):
        def body(j, c):
            for _ in range(reps):
                _run_copy(local_ref, sorted_hbm, sem, 0, 0, to_sorted, rows).wait()
            return c
        return body

    for count, rows in zip(counts, (BIG_ROWS, CHUNK)):
        lax.fori_loop(0, count >> 3, wait_n(rows, batch), 0)
        lax.fori_loop(0, count & (batch - 1), wait_n(rows, 1), 0)


def _moe_scatter_kernel(gstart_ref, gend_ref, gfill_ref, xw_ref, lpos_ref, tab_ref, *refs, has_shared):
    sorted_hbm, local_ref, sem, zsem, pending_ref = refs[1:] if has_shared else refs
    i = pl.program_id(0)
    n = pl.num_programs(0)
    slot = i & 1
    tm = xw_ref.shape[0]
    lrows = local_ref.shape[1]

    def zero_copy(e, j):
        off = pl.multiple_of(gend_ref[e] - (j + 1) * LBLK, LBLK)
        return pltpu.make_async_copy(local_ref.at[1, pl.ds(0, LBLK)], sorted_hbm.at[pl.ds(off, LBLK)], zsem)

    @pl.when(i == 0)
    def _():
        _set_pending(pending_ref, 0, (0, 0))
        _set_pending(pending_ref, 1, (0, 0))
        local_ref[1, 0:LBLK, :] = jnp.zeros((LBLK, PACK_W), jnp.uint32)

        def z_start(e, c):
            lax.fori_loop(0, gfill_ref[e], lambda j, c2: (zero_copy(e, j).start(), c2)[1], 0)
            return c

        def z_wait(e, c):
            lax.fori_loop(0, gfill_ref[e], lambda j, c2: (zero_copy(e, j).wait(), c2)[1], 0)
            return c

        lax.fori_loop(0, N_EXPERTS, z_start, 0)
        lax.fori_loop(0, N_EXPERTS, z_wait, 0)

    local = local_ref.at[slot]
    _wait_run_copies(_get_pending(pending_ref, slot), local, sorted_hbm, sem.at[slot], True)

    x = _unpack_bf16_pairs(xw_ref[...]).astype(BF16)
    lpos = lpos_ref[...].astype(jnp.int16)
    one = jnp.ones((LBLK, tm), BF16)
    used_rows = tab_ref[3, 0] * CHUNK

    def sort_block(b):
        riota = lax.broadcasted_iota(jnp.int16, (LBLK, tm), 0) + b * LBLK
        p = jnp.zeros((LBLK, tm), BF16)
        for k in range(TOP_K):
            p = jnp.where(riota == lpos[k:k + 1, :], one, p)
        local[b * LBLK:(b + 1) * LBLK, :] = _pack_exact_bf16_pairs(_dot(p, x))

    n_blocks = lrows // LBLK
    for b in range(n_blocks - 1):
        sort_block(b)
    pl.when(used_rows > (n_blocks - 1) * LBLK)(functools.partial(sort_block, n_blocks - 1))

    _set_pending(pending_ref, slot, _start_run_copies(tab_ref, gstart_ref, local, sorted_hbm, sem.at[slot], True))

    @pl.when(i == n - 1)
    def _():
        for s in range(2):
            _wait_run_copies(_get_pending(pending_ref, s), local_ref.at[s], sorted_hbm, sem.at[s], True)


def _moe_scatter(gstart, gend, gfill, xw, lpos, tab, n_rows, tm, shared=None):
    n_tiles = xw.shape[0] // tm
    lrows = _local_rows(tm)
    any_spec = pl.BlockSpec(memory_space=pl.ANY)
    in_specs = [pl.BlockSpec((tm, PACK_W), lambda i, *_: (i, 0)),
                pl.BlockSpec((None, TOP_K, tm), lambda i, *_: (i, 0, 0)),
                pl.BlockSpec((None, 8, LANES), lambda i, *_: (i, 0, 0), memory_space=pltpu.SMEM)]
    args = [gstart, gend, gfill, xw, lpos, tab]
    aliases = {}
    if shared is not None:
        in_specs.append(any_spec)
        aliases = {len(args): 0}
        args.append(shared)
    return pl.pallas_call(
        functools.partial(_moe_scatter_kernel, has_shared=shared is not None),
        grid_spec=pltpu.PrefetchScalarGridSpec(
            num_scalar_prefetch=3,
            grid=(n_tiles,),
            in_specs=in_specs,
            out_specs=any_spec,
            scratch_shapes=[pltpu.VMEM((2, lrows, PACK_W), jnp.uint32), pltpu.SemaphoreType.DMA((2,)),
                            pltpu.SemaphoreType.DMA, pltpu.SMEM((2, 2), jnp.int32)]),
        out_shape=jax.ShapeDtypeStruct((n_rows, PACK_W), jnp.uint32),
        input_output_aliases=aliases,
        compiler_params=_cparams("arbitrary"),
        name="moe_scatter",
    )(*args)


def _expert2_kernel(be_ref, nu_ref, xs_ref, wg_ref, wu_ref, wd_ref, ys_ref, wgu_b, wd_b):
    j = pl.program_id(0)

    @pl.when(jnp.logical_or(j == 0, be_ref[j] != be_ref[jnp.maximum(j - 1, 0)]))
    def _():
        wgu_b[:, :EXPERT_FF] = wg_ref[...].astype(BF16)
        wgu_b[:, EXPERT_FF:] = wu_ref[...].astype(BF16)
        wd_b[...] = wd_ref[...].astype(BF16)

    @pl.when(j < nu_ref[0])
    def _():
        x = _unpack_bf16_pairs(xs_ref[...]).astype(BF16)
        gu = _dot(x, wgu_b[...])
        a = _silu(gu[:, :EXPERT_FF]) * gu[:, EXPERT_FF:]
        ys_ref[...] = _pack_bf16_pairs(_dot(a.astype(BF16), wd_b[...]))


def _experts2(xs, block_expert, n_used, lw, layer, eblk):
    n_rows = xs.shape[0]
    ff = EXPERT_FF

    def blk(j, be, nu):
        return (jnp.minimum(j, jnp.maximum(nu[0] - 1, 0)), 0)

    def wblk(j, be, nu):
        return (layer, be[j], 0, 0)

    return pl.pallas_call(
        _expert2_kernel,
        grid_spec=pltpu.PrefetchScalarGridSpec(
            num_scalar_prefetch=2,
            grid=(n_used[0],),
            in_specs=[pl.BlockSpec((eblk, PACK_W), blk),
                      pl.BlockSpec((None, None, D_MODEL, ff), wblk),
                      pl.BlockSpec((None, None, D_MODEL, ff), wblk),
                      pl.BlockSpec((None, None, ff, D_MODEL), wblk)],
            out_specs=pl.BlockSpec((eblk, PACK_W), blk),
            scratch_shapes=[pltpu.VMEM((D_MODEL, 2 * ff), BF16), pltpu.VMEM((ff, D_MODEL), BF16)]),
        out_shape=jax.ShapeDtypeStruct((n_rows, PACK_W), jnp.uint32),
        compiler_params=_cparams("arbitrary"),
        name="moe_experts",
    )(block_expert, n_used, xs, lw["w_exp_gate"], lw["w_exp_up"], lw["w_exp_down"])


def _moe_combine_kernel(gstart_ref, x1_ref, xw_ref, lpos_ref, wrow_ref, tab_ref, tab_next_ref, mod_ref,
                        sg_ref, su_ref, sd_ref, ln2g_ref, ln2b_ref, sorted_hbm, o_ref,
                        local_ref, sem, pending_ref):
    i = pl.program_id(0)
    n = pl.num_programs(0)
    slot = i & 1
    tm = x1_ref.shape[0]
    lrows = local_ref.shape[1]

    @pl.when(i == 0)
    def _():
        local_ref[...] = jnp.zeros_like(local_ref)
        _set_pending(pending_ref, 0,
                     _start_run_copies(tab_ref, gstart_ref, local_ref.at[0], sorted_hbm, sem.at[0], False))

    @pl.when(i + 1 < n)
    def _():
        nxt = 1 - slot
        _set_pending(pending_ref, nxt, _start_run_copies(tab_next_ref, gstart_ref, local_ref.at[nxt], sorted_hbm,
                                                         sem.at[nxt], False))

    h = _unpack_bf16_pairs(xw_ref[...]).astype(BF16)
    a = _silu(_dot(h, sg_ref[...])) * _dot(h, su_ref[...])
    acc = _dot(a.astype(BF16), sd_ref[...])

    local = local_ref.at[slot]
    _wait_run_copies(_get_pending(pending_ref, slot), local, sorted_hbm, sem.at[slot], False)

    lpos = lpos_ref[...].astype(jnp.int16)
    wrow = wrow_ref[...].astype(BF16)

    def unsort_block(b, acc):
        riota = lax.broadcasted_iota(jnp.int16, (LBLK, tm), 0) + b * LBLK
        q = jnp.zeros((LBLK, tm), BF16)
        for k in range(TOP_K):
            q = jnp.where(riota == lpos[k:k + 1, :], jnp.broadcast_to(wrow[k:k + 1, :], (LBLK, tm)), q)
        y = _unpack_bf16_pairs(local[b * LBLK:(b + 1) * LBLK, :]).astype(BF16)
        return acc + _dot_tn(q, y)

    for b in range(lrows // LBLK):
        acc = unsort_block(b, acc)

    g2 = mod_ref[...][:, 5 * D_MODEL:6 * D_MODEL]
    u = DEEPNORM_ALPHA * x1_ref[...] + g2 * acc
    o_ref[...] = _ln(u) * ln2g_ref[...] + ln2b_ref[...]


def _moe_combine(gstart, x1, xw, lpos, wrow, tab, ys, mod3, mod_row, lw, tm):
    t = x1.shape[0]
    n_tiles = t // tm
    lrows = _local_rows(tm)

    def full(a):
        return pl.BlockSpec(a.shape, lambda i, *_: (0,) * a.ndim)

    tab_blk = lambda f: pl.BlockSpec((None, 8, LANES), f, memory_space=pltpu.SMEM)
    return pl.pallas_call(
        _moe_combine_kernel,
        grid_spec=pltpu.PrefetchScalarGridSpec(
            num_scalar_prefetch=1,
            grid=(n_tiles,),
            in_specs=[pl.BlockSpec((tm, D_MODEL), lambda i, *_: (i, 0)),
                      pl.BlockSpec((tm, PACK_W), lambda i, *_: (i, 0)),
                      pl.BlockSpec((None, TOP_K, tm), lambda i, *_: (i, 0, 0)),
                      pl.BlockSpec((None, TOP_K, tm), lambda i, *_: (i, 0, 0)),
                      tab_blk(lambda i, *_: (i, 0, 0)),
                      tab_blk(lambda i, *_: (jnp.minimum(i + 1, n_tiles - 1), 0, 0)),
                      pl.BlockSpec((None, 1, 6 * D_MODEL), lambda i, *_: (mod_row(i, tm), 0, 0)),
                      full(lw["w_sh_gate"]), full(lw["w_sh_up"]), full(lw["w_sh_down"]),
                      full(lw["ln2_g"]), full(lw["ln2_b"]),
                      pl.BlockSpec(memory_space=pl.ANY)],
            out_specs=pl.BlockSpec((tm, D_MODEL), lambda i, *_: (i, 0)),
            scratch_shapes=[pltpu.VMEM((2, lrows, PACK_W), jnp.uint32), pltpu.SemaphoreType.DMA((2,)),
                            pltpu.SMEM((2, 2), jnp.int32)]),
        out_shape=jax.ShapeDtypeStruct((t, D_MODEL), F32),
        compiler_params=_cparams("arbitrary"),
        name="moe_combine",
    )(gstart, x1, xw, lpos, wrow, tab, tab, mod3, lw["w_sh_gate"], lw["w_sh_up"], lw["w_sh_down"],
      lw["ln2_g"], lw["ln2_b"], ys)


def _moe_groups(groups, lw, layer):
    tm = MOE_TM
    chunks = [g["chunks"][:, 0].astype(jnp.int32) for g in groups]
    t_all = sum(g["x1"].shape[0] for g in groups)
    n_tiles = t_all // tm
    eblk = max(LBLK, min(EBLK_MAX, t_all * TOP_K // N_EXPERTS))
    region_rows = sum(chunks) * CHUNK
    padded = ((region_rows + eblk - 1) // eblk) * eblk
    end = jnp.cumsum(padded)
    start = end - padded
    n_blocks = -(-(t_all * TOP_K + n_tiles * N_EXPERTS * (CHUNK - 1)) // eblk) + N_EXPERTS
    n_used = end[-1] // eblk
    blk_row = jnp.minimum(jnp.arange(n_blocks, dtype=jnp.int32), jnp.maximum(n_used - 1, 0)) * eblk
    block_expert = jnp.minimum(jnp.sum((end[None, :] <= blk_row[:, None]).astype(jnp.int32), axis=1),
                               N_EXPERTS - 1).astype(jnp.int32)
    gfill = (padded - region_rows + LBLK - 1) // LBLK
    gstarts = []
    before = jnp.zeros_like(chunks[0])
    for ch in chunks:
        gstarts.append(start // CHUNK + before)
        before = before + ch
    xs = None
    for g, gstart in zip(groups, gstarts):
        fill = gfill if xs is None else jnp.zeros_like(gfill)
        xs = _moe_scatter(gstart, end, fill, g["xw"], g["lpos"], g["tab"], n_blocks * eblk, tm, xs)
    ys = _experts2(xs, block_expert, n_used.reshape(1), lw, layer, eblk)
    return [_moe_combine(gstart, g["x1"], g["xw"], g["lpos"], g["wrow"], g["tab"], ys, g["mod3"], g["mod_row"],
                         lw, tm) for g, gstart in zip(groups, gstarts)]


def _channel_dft_table():
    k = np.arange(FNET_CH, dtype=np.float64)
    ang = 2.0 * np.pi * np.outer(k, k) / FNET_CH
    eye = np.eye(FNET_GROUPS)
    return np.concatenate([np.kron(eye, np.cos(ang)), -np.kron(eye, np.sin(ang))], axis=1)


def _direct_dft_tables(n):
    k = np.arange(n, dtype=np.float64)
    ang = 2.0 * np.pi * (np.outer(k, k) % n) / n
    scale = 1.0 / math.sqrt(n * FNET_CH)
    return np.cos(ang) * scale, np.sin(ang) * scale


def _two_stage_dft_tables(n):
    n1 = FFT_N1
    n2 = n // n1
    a = np.arange(n1, dtype=np.float64)
    ang1 = 2.0 * np.pi * (np.outer(a, a) % n1) / n1
    k1 = np.arange(n1).reshape(n1, 1, 1)
    k2 = np.arange(n2).reshape(1, n2, 1)
    m2 = np.arange(n2).reshape(1, 1, n2)
    ang2 = 2.0 * np.pi * ((m2 * (k1 + n1 * k2)) % n) / n
    scale = 1.0 / math.sqrt(n * FNET_CH)
    return np.cos(ang1), np.sin(ang1), np.cos(ang2) * scale, np.sin(ang2) * scale


def _grid_sincos_table(rows, d):
    quarter = d // 4
    omega = 1.0 / (POS_BASE ** (np.arange(quarter, dtype=np.float64) / quarter))
    r = np.arange(rows, dtype=np.float64)[:, None] * omega
    c = np.arange(GRID_W, dtype=np.float64)[:, None] * omega
    return (np.concatenate([np.sin(r), np.cos(r)], axis=-1).astype(np.float32),
            np.concatenate([np.sin(c), np.cos(c)], axis=-1).astype(np.float32))


def _layer_weights(l, w_in, w_gla_a, b_gla_a, gla_norm_g, sgu_norm_g, sgu_norm_b, w_sgu, b_sgu, w_conv,
                   w_out, ln1_g, ln1_b, ln2_g, ln2_b, w_router, router_bias,
                   w_exp_gate, w_exp_up, w_exp_down, w_sh_gate, w_sh_up, w_sh_down):
    wi = w_in[l]
    lr0 = 4 * HEAD_W
    w_in_p = jnp.concatenate(
        [wi[:, :lr0], wi[:, lr0 + 2 * GLA_LR:lr0 + 2 * GLA_LR + 5 * HEAD_W], wi[:, lr0:lr0 + 2 * GLA_LR],
         jnp.zeros((D_MODEL, LR_W - 2 * GLA_LR), F32), wi[:, lr0 + 2 * GLA_LR + 5 * HEAD_W:]], axis=1).astype(BF16)
    wa_pad = jnp.zeros((LR_W, 2 * HEAD_W), F32)
    wa_pad = wa_pad.at[:GLA_LR, :HEAD_W].set(w_gla_a[l, 0])
    wa_pad = wa_pad.at[GLA_LR:2 * GLA_LR, HEAD_W:].set(w_gla_a[l, 1])
    w_router_pad = jnp.concatenate([w_router[l], jnp.zeros((D_MODEL, LANES - N_EXPERTS), F32)], axis=1)
    w_router_hi = w_router_pad.astype(BF16)
    row = lambda a: a[l].reshape(1, -1)
    return {
        "w_in_p": w_in_p,
        "wa_hi": wa_pad.astype(BF16), "wa_lo": (wa_pad - wa_pad.astype(BF16).astype(F32)).astype(BF16),
        "ba": jnp.concatenate([b_gla_a[l, 0], b_gla_a[l, 1]]).reshape(1, 2 * HEAD_W),
        "gla_norm_g": row(gla_norm_g), "sgu_norm_g": row(sgu_norm_g), "sgu_norm_b": row(sgu_norm_b),
        "w_sgu_cat": jnp.concatenate([w_sgu[l, g] for g in range(GMLP_GROUPS)], axis=1).astype(BF16),
        "b_sgu_full": jnp.repeat(b_sgu[l].T, HEAD_W // GMLP_GROUPS, axis=1),
        "w_conv": w_conv[l],
        "w_out": w_out[l].astype(BF16),
        "ln1_g": row(ln1_g), "ln1_b": row(ln1_b), "ln2_g": row(ln2_g), "ln2_b": row(ln2_b),
        "w_router_hi": w_router_hi, "w_router_lo": (w_router_pad - w_router_hi.astype(F32)).astype(BF16),
        "router_bias": router_bias[l].reshape(N_EXPERTS, 1),
        "w_exp_gate": w_exp_gate, "w_exp_up": w_exp_up, "w_exp_down": w_exp_down,
        "w_sh_gate": w_sh_gate[l].astype(BF16), "w_sh_up": w_sh_up[l].astype(BF16),
        "w_sh_down": w_sh_down[l].astype(BF16),
    }


def _state_to_blockdiag_t(s):
    bsz = s.shape[0]
    st = jnp.swapaxes(s, 2, 3)
    eye = jnp.eye(GLA_HEADS, dtype=s.dtype)
    return jnp.einsum("bhvd,hg->bhvgd", st, eye).reshape(bsz, HEAD_W, HEAD_W)


def _blockdiag_t_to_state(st):
    bsz = st.shape[0]
    s5 = st.reshape(bsz, GLA_HEADS, GLA_DK, GLA_HEADS, GLA_DK)
    diag = jnp.stack([s5[:, h, :, h, :] for h in range(GLA_HEADS)], axis=1)
    return jnp.swapaxes(diag, 2, 3)


def _pre_moe(x3, pos, mod3, mod_row, lw, st0, emit_final, tabs):
    bsz, n, _ = x3.shape
    t = bsz * n
    outs = _in_proj(x3.reshape(t, D_MODEL), pos, mod3, mod_row, lw["w_in_p"], tabs["cs"], tm=512)
    if pos is not None:
        proj, zr, zi, x2d = outs
    else:
        proj, zr, zi = outs
        x2d = x3.reshape(t, D_MODEL)
    gla_out = _gla(proj.reshape(bsz, n, STORE_W), lw["wa_hi"], lw["wa_lo"], lw["ba"], st0, emit_final)
    o_f, o_b = gla_out[:2]
    zr3 = zr.reshape(bsz, n, HEAD_W)
    zi3 = zi.reshape(bsz, n, HEAD_W)
    if "two_stage" in tabs:
        yft = _fft_two_stage(zr3, zi3, tabs["two_stage"])
    else:
        yft = _fft_direct(zr3, zi3, *tabs["direct"])
    x1, xw, wrow, lpos, tab, chunks = _mix_out(x2d, o_f.reshape(t, HEAD_W), o_b.reshape(t, HEAD_W), proj,
                                             yft.reshape(t, HEAD_W), mod3, mod_row, lw, seq_len=n, tm=MOE_TM)
    group = {"x1": x1, "xw": xw, "wrow": wrow, "lpos": lpos, "tab": tab, "chunks": chunks,
             "mod3": mod3, "mod_row": mod_row, "shape": (bsz, n, D_MODEL)}
    return group, gla_out[2:]


def _trunk_layer(x3, pos, mod3, mod_row, lw, layer, st0, emit_final, tabs):
    group, finals = _pre_moe(x3, pos, mod3, mod_row, lw, st0, emit_final, tabs)
    (x2,) = _moe_groups([group], lw, layer)
    return x2.reshape(group["shape"]), finals


def kernel(x_prompt, x_sample, c, state_gla, c_ctx, w_ada, b_ada, w_in, w_gla_a, b_gla_a, gla_norm_g, sgu_norm_g, sgu_norm_b, w_sgu, b_sgu, w_conv, w_out, ln1_g, ln1_b, ln2_g, ln2_b, w_router, router_bias, w_exp_gate, w_exp_up, w_exp_down, w_sh_gate, w_sh_up, w_sh_down):
    n_layers = w_ada.shape[0]
    bp, np_, _ = x_prompt.shape
    bs, ns, _ = x_sample.shape
    assert bs <= 7

    cond8 = jnp.concatenate([c_ctx[None, :], c, jnp.zeros((7 - bs, D_MODEL), F32)], axis=0)
    mod = _ada_mod(cond8, w_ada, b_ada)

    tabs_p = {"cs": jnp.asarray(_channel_dft_table(), BF16),
              "direct": tuple(jnp.asarray(a, BF16) for a in _direct_dft_tables(np_))}
    tabs_s = {"cs": tabs_p["cs"],
              "two_stage": tuple(jnp.asarray(a, BF16) for a in _two_stage_dft_tables(ns))}
    rtab, ctab = _grid_sincos_table(ns // GRID_W, D_MODEL)
    pos = jnp.concatenate([jnp.repeat(jnp.asarray(rtab), GRID_W, axis=0),
                           jnp.tile(jnp.asarray(ctab), (ns // GRID_W, 1))], axis=-1)

    prompt_row = lambda i, tm: 0
    sample_row = lambda i, tm: 1 + (i * tm) // ns

    y_p = x_prompt
    y_s = x_sample
    finals = []
    for l in range(n_layers):
        lw = _layer_weights(l, w_in, w_gla_a, b_gla_a, gla_norm_g, sgu_norm_g, sgu_norm_b, w_sgu, b_sgu,
                            w_conv, w_out, ln1_g, ln1_b, ln2_g, ln2_b, w_router, router_bias,
                            w_exp_gate, w_exp_up, w_exp_down, w_sh_gate, w_sh_up, w_sh_down)
        mod3 = mod[l].reshape(8, 1, 6 * D_MODEL)
        group_p, fin = _pre_moe(y_p, None, mod3, prompt_row, lw, None, True, tabs_p)
        finals.append(jnp.stack([_blockdiag_t_to_state(fin[0]), _blockdiag_t_to_state(fin[1])], axis=1))
        st0 = jnp.stack([_state_to_blockdiag_t(state_gla[:, l, 0]), _state_to_blockdiag_t(state_gla[:, l, 1])])
        group_s, _ = _pre_moe(y_s, pos if l == 0 else None, mod3, sample_row, lw, st0, False, tabs_s)
        y_p, y_s = _moe_groups([group_p, group_s], lw, l)
        y_p = y_p.reshape(group_p["shape"])
        y_s = y_s.reshape(group_s["shape"])
    new_state = jnp.stack(finals, axis=1).astype(x_prompt.dtype)
    return (y_p, y_s, new_state)
```

```python
import functools
import math

import numpy as np
import jax
import jax.numpy as jnp
from jax import lax
from jax.experimental import pallas as pl
from jax.experimental.pallas import tpu as pltpu

F32 = jnp.float32
BF16 = jnp.bfloat16

D_MODEL = 1024
DEPTH = 2
GRID_W = 64
HEAD_W = 256
GLA_HEADS = 4
GLA_DK = 64
GLA_LR = 16
GLA_TAU = 16.0
GLA_CHUNK = 64
GMLP_GROUPS = 4
GMLP_CHUNK = 128
FNET_GROUPS = 4
FNET_CH = 64
N_EXPERTS = 64
TOP_K = 8
N_GROUPS = 8
TOPK_GROUPS = 4
EXPERT_FF = 256
ROUTED_SCALE = 2.5
DEEPNORM_ALPHA = (2 * DEPTH) ** 0.25
LN_EPS = 1e-5
RMS_EPS = 1e-6
POS_BASE = 10000.0

COL_Q, COL_K, COL_V, COL_G, COL_SGU, COL_SGV, COL_CVB, COL_CVC, COL_CVX = range(9)
LR_W = 128
STORE_W = 9 * HEAD_W + LR_W
PROJ_W = STORE_W + HEAD_W
COL_LR = (9 * HEAD_W) // LR_W
HALO = 16

LANES = 128
PACK_W = D_MODEL // 2
EBLK_MAX = 1024
LBLK = 256
MOE_TM = 256
CHUNK = 8
BIG_SHIFT = 2
BIG_ROWS = CHUNK << BIG_SHIFT
SEG = 256
FFT_N1 = 64
VMEM_LIMIT = 56 * 1024 * 1024


def _cparams(*sem):
    return pltpu.CompilerParams(dimension_semantics=sem, vmem_limit_bytes=VMEM_LIMIT)


def _ln(x):
    mu = jnp.mean(x, axis=-1, keepdims=True)
    xc = x - mu
    var = jnp.mean(xc * xc, axis=-1, keepdims=True)
    return xc * lax.rsqrt(var + LN_EPS)


def _sigmoid(x):
    return 1.0 / (1.0 + jnp.exp(-x))


def _silu(x):
    return x * _sigmoid(x)


def _pack_bf16_pairs(x):
    w = x.shape[1] // 2
    lo = lax.bitcast_convert_type(x[:, :w].astype(BF16).astype(F32), jnp.uint32)
    hi = lax.bitcast_convert_type(x[:, w:].astype(BF16).astype(F32), jnp.uint32)
    return (lo >> 16) | (hi & jnp.uint32(0xFFFF0000))


def _pack_exact_bf16_pairs(x):
    w = x.shape[1] // 2
    lo = lax.bitcast_convert_type(x[:, :w], jnp.uint32)
    hi = lax.bitcast_convert_type(x[:, w:], jnp.uint32)
    return (lo >> 16) | hi


def _unpack_bf16_pairs(u):
    lo = lax.bitcast_convert_type(u << 16, F32)
    hi = lax.bitcast_convert_type(u & jnp.uint32(0xFFFF0000), F32)
    return jnp.concatenate([lo, hi], axis=1)


def _dot(a, b):
    return jnp.dot(a, b, preferred_element_type=F32)


def _dot_nt(a, b):
    return lax.dot_general(a, b, (((1,), (1,)), ((), ())), preferred_element_type=F32)


def _dot_tn(a, b):
    return lax.dot_general(a, b, (((0,), (0,)), ((), ())), preferred_element_type=F32)


def _ada_kernel(c_ref, w_ref, b_ref, o_ref):
    c = c_ref[...]
    o_ref[...] = _dot(_silu(c).astype(BF16), w_ref[...].astype(BF16)) + b_ref[...]


def _ada_mod(cond8, w_ada, b_ada):
    n_l, d, w6 = w_ada.shape
    tn = 1536
    return pl.pallas_call(
        _ada_kernel,
        grid=(n_l, w6 // tn),
        in_specs=[pl.BlockSpec((8, d), lambda l, j: (0, 0)),
                  pl.BlockSpec((None, d, tn), lambda l, j: (l, 0, j)),
                  pl.BlockSpec((None, 1, tn), lambda l, j: (l, 0, j))],
        out_specs=pl.BlockSpec((None, 8, tn), lambda l, j: (l, 0, j)),
        out_shape=jax.ShapeDtypeStruct((n_l, 8, w6), F32),
        compiler_params=_cparams("parallel", "parallel"),
        name="ada_mod",
    )(cond8, w_ada, b_ada.reshape(n_l, 1, w6))


def _in_proj_kernel(*refs, has_pos):
    if has_pos:
        x_ref, pos_ref, mod_ref, w_ref, cs_ref, proj_ref, zr_ref, zi_ref, x0_ref = refs
        x = x_ref[...] + pos_ref[...]
        x0_ref[...] = x
    else:
        x_ref, mod_ref, w_ref, cs_ref, proj_ref, zr_ref, zi_ref = refs
        x = x_ref[...]
    mod = mod_ref[...]
    sh1 = mod[:, 0:D_MODEL]
    sc1 = mod[:, D_MODEL:2 * D_MODEL]
    h = _ln(x) * (1.0 + sc1) + sh1
    proj = _dot(h.astype(BF16), w_ref[...])
    proj_ref[...] = proj[:, :STORE_W].astype(BF16)
    ft = proj[:, STORE_W:].astype(BF16)
    z = _dot(ft, cs_ref[...])
    zr_ref[...] = z[:, :HEAD_W].astype(BF16)
    zi_ref[...] = z[:, HEAD_W:].astype(BF16)


def _in_proj(x2d, pos, mod3, mod_row, w_in_p, cs, tm):
    t = x2d.shape[0]
    tm = min(tm, t)
    in_specs = [pl.BlockSpec((tm, D_MODEL), lambda i: (i, 0))]
    args = [x2d]
    out_shape = [jax.ShapeDtypeStruct((t, STORE_W), BF16),
                 jax.ShapeDtypeStruct((t, HEAD_W), BF16),
                 jax.ShapeDtypeStruct((t, HEAD_W), BF16)]
    out_specs = [pl.BlockSpec((tm, STORE_W), lambda i: (i, 0)),
                 pl.BlockSpec((tm, HEAD_W), lambda i: (i, 0)),
                 pl.BlockSpec((tm, HEAD_W), lambda i: (i, 0))]
    if pos is not None:
        n_pos = pos.shape[0] // tm
        in_specs.append(pl.BlockSpec((tm, D_MODEL), lambda i: (i % n_pos, 0)))
        args.append(pos)
        out_shape.append(jax.ShapeDtypeStruct((t, D_MODEL), F32))
        out_specs.append(pl.BlockSpec((tm, D_MODEL), lambda i: (i, 0)))
    in_specs += [pl.BlockSpec((None, 1, 6 * D_MODEL), lambda i: (mod_row(i, tm), 0, 0)),
                 pl.BlockSpec((D_MODEL, PROJ_W), lambda i: (0, 0)),
                 pl.BlockSpec((HEAD_W, 2 * HEAD_W), lambda i: (0, 0))]
    args += [mod3, w_in_p, cs]
    return pl.pallas_call(
        functools.partial(_in_proj_kernel, has_pos=pos is not None),
        grid=(t // tm,),
        in_specs=in_specs,
        out_specs=out_specs,
        out_shape=out_shape,
        compiler_params=_cparams("parallel"),
        name="in_proj",
    )(*args)


def _gla_masks(reverse):
    i = np.arange(SEG)
    same_chunk = (i[:, None] // GLA_CHUNK) == (i[None, :] // GLA_CHUNK)
    tri = same_chunk & ((i[None, :] >= i[:, None]) if reverse else (i[None, :] <= i[:, None]))
    l_idx = i[:, None] % GLA_CHUNK
    m_idx = np.arange(GLA_CHUNK)[None, :]
    causal = (m_idx >= l_idx) if reverse else (m_idx <= l_idx)
    return (jnp.asarray(tri, BF16), jnp.asarray(same_chunk, BF16), jnp.asarray(same_chunk, F32),
            jnp.asarray(causal, F32))


def _gla_segment(q, k, v, pre, st_ref, o_ref, masks, reverse):
    seg = q.shape[0]
    n_chunks = seg // GLA_CHUNK
    la = (jnp.minimum(pre, 0.0) - jnp.log1p(jnp.exp(-jnp.abs(pre)))) * (1.0 / GLA_TAU)

    tri_m, ones_m, bd, causal = masks
    hi = la.astype(BF16)
    lo = (la - hi.astype(F32)).astype(BF16)
    b = _dot(tri_m, hi) + _dot(tri_m, lo)
    btot = _dot(ones_m, hi) + _dot(ones_m, lo)

    q_dec = q * (GLA_DK ** -0.5) * jnp.exp(b)
    k_inv = (k * jnp.exp(-b)).astype(BF16)
    k_end = (k * jnp.exp(btot - b)).astype(BF16)
    dec = jnp.exp(btot)
    vb = v.astype(BF16)
    keep = causal > 0.5

    st = st_ref[...]
    order = range(n_chunks - 1, -1, -1) if reverse else range(n_chunks)
    for ci in order:
        sl = slice(ci * GLA_CHUNK, (ci + 1) * GLA_CHUNK)
        qd = q_dec[sl]
        qbd = (jnp.concatenate([qd] * GLA_HEADS, axis=0) * bd).astype(BF16)
        a = _dot_nt(qbd, k_inv[sl])
        a = jnp.where(keep, a, 0.0)
        rr = _dot(a.astype(BF16), vb[sl])
        o = _dot_nt(qd.astype(BF16), st.astype(BF16))
        for h in range(GLA_HEADS):
            hs = slice(h * GLA_CHUNK, (h + 1) * GLA_CHUNK)
            o = o + rr[hs] * bd[hs]
        o_ref[sl, :] = o
        kvt = _dot_tn(vb[sl], k_end[sl])
        st = st * dec[ci * GLA_CHUNK:ci * GLA_CHUNK + 1, :] + kvt * bd
    st_ref[...] = st


def _gla_kernel(*refs, has_init, emit_final):
    qf, kf, vf, lrf, qb, kb, vb, lrb, wah_ref, wal_ref, ba_ref = refs[:11]
    mask_refs = refs[11:19]
    rest = refs[19:]
    if has_init:
        s0f, s0b = rest[:2]
        rest = rest[2:]
    of_ref, ob_ref = rest[:2]
    rest = rest[2:]
    if emit_final:
        sff, sfb = rest[:2]
        rest = rest[2:]
    stf, stb = rest

    s = pl.program_id(1)

    @pl.when(s == 0)
    def _():
        if has_init:
            stf[...] = s0f[...]
            stb[...] = s0b[...]
        else:
            stf[...] = jnp.zeros_like(stf)
            stb[...] = jnp.zeros_like(stb)

    def decay_pre(lr_ref):
        lr = lr_ref[...]
        return _dot(lr, wah_ref[...]) + _dot(lr, wal_ref[...]) + ba_ref[...]

    f32 = lambda ref: ref[...].astype(F32)

    masks_f = tuple(m[...] for m in mask_refs[:4])
    masks_b = tuple(m[...] for m in mask_refs[4:])
    _gla_segment(f32(qf), f32(kf), f32(vf), decay_pre(lrf)[:, :HEAD_W], stf, of_ref, masks_f, reverse=False)
    _gla_segment(f32(qb), f32(kb), f32(vb), decay_pre(lrb)[:, HEAD_W:], stb, ob_ref, masks_b, reverse=True)

    if emit_final:
        @pl.when(s == pl.num_programs(1) - 1)
        def _():
            sff[...] = stf[...]
            sfb[...] = stb[...]


def _gla(proj3, wa_hi, wa_lo, ba, st0, emit_final):
    bsz, n, _ = proj3.shape
    nseg = n // SEG

    def col(cb, width=HEAD_W, rev=False):
        if rev:
            return pl.BlockSpec((None, SEG, width), lambda b, s: (b, nseg - 1 - s, cb))
        return pl.BlockSpec((None, SEG, width), lambda b, s: (b, s, cb))

    in_specs = [col(COL_Q), col(COL_K), col(COL_V), col(COL_LR, LR_W),
                col(COL_Q, rev=True), col(COL_K, rev=True), col(COL_V, rev=True), col(COL_LR, LR_W, rev=True),
                pl.BlockSpec((LR_W, 2 * HEAD_W), lambda b, s: (0, 0)),
                pl.BlockSpec((LR_W, 2 * HEAD_W), lambda b, s: (0, 0)),
                pl.BlockSpec((1, 2 * HEAD_W), lambda b, s: (0, 0))]
    masks = _gla_masks(False) + _gla_masks(True)
    in_specs += [pl.BlockSpec(m.shape, lambda b, s: (0, 0)) for m in masks]
    args = [proj3] * 8 + [wa_hi, wa_lo, ba] + list(masks)
    st_spec = pl.BlockSpec((None, HEAD_W, HEAD_W), lambda b, s: (b, 0, 0))
    if st0 is not None:
        in_specs += [st_spec, st_spec]
        args += [st0[0], st0[1]]
    out_shape = [jax.ShapeDtypeStruct((bsz, n, HEAD_W), F32)] * 2
    out_specs = [pl.BlockSpec((None, SEG, HEAD_W), lambda b, s: (b, s, 0)),
                 pl.BlockSpec((None, SEG, HEAD_W), lambda b, s: (b, nseg - 1 - s, 0))]
    if emit_final:
        out_shape += [jax.ShapeDtypeStruct((bsz, HEAD_W, HEAD_W), F32)] * 2
        out_specs += [st_spec, st_spec]
    return pl.pallas_call(
        functools.partial(_gla_kernel, has_init=st0 is not None, emit_final=emit_final),
        grid=(bsz, nseg),
        in_specs=in_specs,
        out_specs=out_specs,
        out_shape=out_shape,
        scratch_shapes=[pltpu.VMEM((HEAD_W, HEAD_W), F32), pltpu.VMEM((HEAD_W, HEAD_W), F32)],
        compiler_params=_cparams("parallel", "arbitrary"),
        name="gla",
    )(*args)


def _fft_direct_kernel(zr_ref, zi_ref, cn_ref, sn_ref, o_ref):
    o_ref[...] = _dot(cn_ref[...], zr_ref[...]) + _dot(sn_ref[...], zi_ref[...])


def _fft_direct(zr3, zi3, cn, sn):
    bsz, n, w = zr3.shape
    blk = pl.BlockSpec((None, n, w), lambda b: (b, 0, 0))
    tab = pl.BlockSpec((n, n), lambda b: (0, 0))
    return pl.pallas_call(
        _fft_direct_kernel,
        grid=(bsz,),
        in_specs=[blk, blk, tab, tab],
        out_specs=blk,
        out_shape=jax.ShapeDtypeStruct((bsz, n, w), F32),
        compiler_params=_cparams("parallel"),
        name="fft_direct",
    )(zr3, zi3, cn, sn)


def _fft_a_kernel(zr_ref, zi_ref, c_ref, s_ref, gr_ref, gi_ref):
    zr = zr_ref[...]
    zi = zi_ref[...]
    cm = c_ref[...]
    sm = s_ref[...]
    gr_ref[...] = (_dot(cm, zr) + _dot(sm, zi)).astype(BF16)
    gi_ref[...] = (_dot(cm, zi) - _dot(sm, zr)).astype(BF16)


def _fft_c_kernel(gr_ref, gi_ref, mc_ref, ms_ref, o_ref):
    for j in range(gr_ref.shape[0]):
        o_ref[:, j, :] = _dot(mc_ref[j], gr_ref[j]) + _dot(ms_ref[j], gi_ref[j])


def _fft_two_stage(zr3, zi3, tabs):
    bsz, n, w = zr3.shape
    n1 = FFT_N1
    n2 = n // n1
    c1, s1, mc, ms = tabs
    tn = 2048
    wide = n2 * w
    blk = pl.BlockSpec((None, n1, tn), lambda b, j: (b, 0, j))
    tab = pl.BlockSpec((n1, n1), lambda b, j: (0, 0))
    gr, gi = pl.pallas_call(
        _fft_a_kernel,
        grid=(bsz, wide // tn),
        in_specs=[blk, blk, tab, tab],
        out_specs=[blk, blk],
        out_shape=[jax.ShapeDtypeStruct((bsz, n1, wide), BF16)] * 2,
        compiler_params=_cparams("parallel", "parallel"),
        name="fft_stage_a",
    )(zr3.reshape(bsz, n1, wide), zi3.reshape(bsz, n1, wide), c1, s1)
    kb = 8
    gblk = pl.BlockSpec((None, kb, n2, w), lambda b, j: (b, j, 0, 0))
    mblk = pl.BlockSpec((kb, n2, n2), lambda b, j: (j, 0, 0))
    out = pl.pallas_call(
        _fft_c_kernel,
        grid=(bsz, n1 // kb),
        in_specs=[gblk, gblk, mblk, mblk],
        out_specs=pl.BlockSpec((None, n2, kb, w), lambda b, j: (b, 0, j, 0)),
        out_shape=jax.ShapeDtypeStruct((bsz, n2, n1, w), F32),
        compiler_params=_cparams("parallel", "parallel"),
        name="fft_stage_c",
    )(gr.reshape(bsz, n1, n2, w), gi.reshape(bsz, n1, n2, w), mc, ms)
    return out.reshape(bsz, n, w)


def _mix_masks(tm):
    h = np.arange(HEAD_W) // GLA_DK
    head_mean = (h[:, None] == h[None, :]) / GLA_DK
    rg = np.arange(GMLP_GROUPS * GMLP_CHUNK) // GMLP_CHUNK
    cg = np.arange(HEAD_W) // (HEAD_W // GMLP_GROUPS)
    t = np.arange(tm)
    e = np.arange(N_EXPERTS)
    return (jnp.asarray(head_mean, BF16), jnp.asarray(rg[:, None] == cg[None, :], F32),
            jnp.asarray(t[:, None] < t[None, :], BF16), jnp.asarray(e[None, :] < e[:, None], BF16))


def _route(logits, bias, before, lower):
    tm = logits.shape[1]
    s = _sigmoid(logits)
    biased = s + bias
    neg = -jnp.inf
    rows = lax.broadcasted_iota(jnp.int32, (8, tm), 0)

    def first_argmax(x, ids, sentinel):
        m = jnp.max(x, axis=0, keepdims=True)
        return m, jnp.min(jnp.where(x == m, ids, sentinel), axis=0, keepdims=True)

    gs_rows = []
    for g in range(N_GROUPS):
        x = biased[8 * g:8 * g + 8]
        m1, i1 = first_argmax(x, rows, 8)
        m2 = jnp.max(jnp.where(rows == i1, neg, x), axis=0, keepdims=True)
        gs_rows.append(m1 + m2)
    gs = jnp.concatenate(gs_rows, axis=0)
    gsel = jnp.zeros((N_GROUPS, tm), F32)
    for _ in range(TOPK_GROUPS):
        _, i = first_argmax(gs, rows, 8)
        hit = rows == i
        gsel = jnp.where(hit, 1.0, gsel)
        gs = jnp.where(hit, neg, gs)

    xs = [jnp.where(gsel[g:g + 1] > 0.0, biased[8 * g:8 * g + 8], neg) for g in range(N_GROUPS)]
    ids = [rows + 8 * g for g in range(N_GROUPS)]
    sel = [jnp.zeros((8, tm), F32) for _ in range(N_GROUPS)]
    eids = []
    for _ in range(TOP_K):
        m = xs[0]
        for g in range(1, N_GROUPS):
            m = jnp.maximum(m, xs[g])
        m = jnp.max(m, axis=0, keepdims=True)
        cand = jnp.where(xs[0] == m, ids[0], N_EXPERTS)
        for g in range(1, N_GROUPS):
            cand = jnp.minimum(cand, jnp.where(xs[g] == m, ids[g], N_EXPERTS))
        i = jnp.min(cand, axis=0, keepdims=True)
        eids.append(i)
        for g in range(N_GROUPS):
            hit = ids[g] == i
            sel[g] = jnp.where(hit, 1.0, sel[g])
            xs[g] = jnp.where(hit, neg, xs[g])

    sel_all = jnp.concatenate(sel, axis=0)
    seen = _dot(sel_all.astype(BF16), before)
    counts = jnp.sum(sel_all, axis=1, keepdims=True)

    n_chunks = jnp.ceil(counts * (1.0 / CHUNK))
    run_start = _dot(lower, jnp.broadcast_to(n_chunks, (N_EXPERTS, LANES)).astype(BF16))[:, 0:1] * CHUNK
    local_pos = seen + run_start

    def pick(k, table):
        acc = None
        for g in range(N_GROUPS):
            v = jnp.where(ids[g] == eids[k], table[8 * g:8 * g + 8], 0.0)
            acc = v if acc is None else acc + v
        return jnp.sum(acc, axis=0, keepdims=True)

    w_raw = [pick(k, s) for k in range(TOP_K)]
    lpos = [pick(k, local_pos) for k in range(TOP_K)]
    tot = w_raw[0]
    for k in range(1, TOP_K):
        tot = tot + w_raw[k]
    weights = [w / tot * ROUTED_SCALE for w in w_raw]
    return weights, lpos, n_chunks, run_start


def _mix_out_kernel(x_ref, of_ref, ob_ref, g_ref, su_ref, sv_ref, cb_ref, cc_ref, cx_ref,
                    ccp_ref, cxp_ref, ccn_ref, cxn_ref, ft_ref, mod_ref,
                    glag_ref, sgng_ref, sgnb_ref, wsgu_ref, bsgu_ref, wconv_ref, wout_ref,
                    ln1g_ref, ln1b_ref, wrh_ref, wrl_ref, rb_ref, hmean_ref, sgubd_ref, before_ref, lower_ref,
                    x1_ref, xw_ref, wrow_ref, lpos_ref, tab_ref, cnt_ref, carry_ref, *, seq_len):
    tm = x_ref.shape[0]
    i = pl.program_id(0)

    @pl.when(i == 0)
    def _():
        carry_ref[...] = jnp.zeros_like(carry_ref)

    mod = mod_ref[...]
    g1 = mod[:, 2 * D_MODEL:3 * D_MODEL]
    sh2 = mod[:, 3 * D_MODEL:4 * D_MODEL]
    sc2 = mod[:, 4 * D_MODEL:5 * D_MODEL]

    o = of_ref[...] + ob_ref[...]
    head_mean = hmean_ref[...]
    o2 = o * o
    o2_hi = o2.astype(BF16)
    o2_lo = (o2 - o2_hi.astype(F32)).astype(BF16)
    ms = _dot(o2_hi, head_mean) + _dot(o2_lo, head_mean)
    y_gla = o * lax.rsqrt(ms + RMS_EPS) * glag_ref[...] * _silu(g_ref[...].astype(F32))

    vn = _ln(sv_ref[...].astype(F32)) * sgng_ref[...] + sgnb_ref[...]
    sgu_bd = sgubd_ref[...]
    sp_parts = []
    for j in range(tm // GMLP_CHUNK):
        vc = vn[j * GMLP_CHUNK:(j + 1) * GMLP_CHUNK]
        vbd = (jnp.concatenate([vc] * GMLP_GROUPS, axis=0) * sgu_bd).astype(BF16)
        sp_parts.append(_dot(wsgu_ref[...], vbd) + bsgu_ref[...])
    y_sgu = su_ref[...].astype(F32) * jnp.concatenate(sp_parts, axis=0)

    z = cc_ref[...].astype(F32) * cx_ref[...].astype(F32)
    z_before = (ccp_ref[...].astype(F32) * cxp_ref[...].astype(F32))[HALO - 1:HALO, :]
    z_after = (ccn_ref[...].astype(F32) * cxn_ref[...].astype(F32))[0:1, :]
    row = lax.broadcasted_iota(jnp.int32, (tm, HEAD_W), 0)
    pos = (i * tm + row) & (seq_len - 1)
    z_prev = jnp.where(row == 0, z_before, pltpu.roll(z, 1, 0))
    z_next = jnp.where(row == tm - 1, z_after, pltpu.roll(z, tm - 1, 0))
    z_prev = jnp.where(pos == 0, 0.0, z_prev)
    z_next = jnp.where(pos == seq_len - 1, 0.0, z_next)
    wconv = wconv_ref[...]
    y_conv = cb_ref[...].astype(F32) * (wconv[0:1] * z_prev + wconv[1:2] * z + wconv[2:3] * z_next)

    y = (_dot(y_gla.astype(BF16), wout_ref[0:HEAD_W, :])
         + _dot(y_sgu.astype(BF16), wout_ref[HEAD_W:2 * HEAD_W, :])
         + _dot(y_conv.astype(BF16), wout_ref[2 * HEAD_W:3 * HEAD_W, :])
         + _dot(ft_ref[...].astype(BF16), wout_ref[3 * HEAD_W:4 * HEAD_W, :]))
    x1 = _ln(DEEPNORM_ALPHA * x_ref[...] + g1 * y) * ln1g_ref[...] + ln1b_ref[...]
    x1_ref[...] = x1
    h2 = _ln(x1) * (1.0 + sc2) + sh2
    xw_ref[...] = _pack_bf16_pairs(h2)

    h2_hi = h2.astype(BF16)
    h2_lo = (h2 - h2_hi.astype(F32)).astype(BF16)
    logits = _dot(h2_hi, wrh_ref[...]) + _dot(h2_hi, wrl_ref[...]) + _dot(h2_lo, wrh_ref[...])
    weights, lpos, n_chunks, run_start = _route(logits.T[:N_EXPERTS], rb_ref[...], before_ref[...], lower_ref[...])
    lpos_ref[...] = jnp.concatenate(lpos, axis=0).astype(jnp.int32)
    wrow_ref[...] = jnp.concatenate(weights, axis=0)
    carry = carry_ref[:, 0:1]
    lane = lax.broadcasted_iota(jnp.int32, (N_EXPERTS, LANES), 1)
    total = jnp.sum(n_chunks, axis=0, keepdims=True)
    cols = jnp.where(lane == 0, n_chunks, jnp.where(lane == 1, run_start,
                                                   jnp.where(lane == 2, carry, jnp.where(lane == 3, total, 0.0))))
    tab = jnp.concatenate([cols, jnp.zeros((LANES - N_EXPERTS, LANES), F32)], axis=0).T
    tab_ref[...] = tab[0:8].astype(jnp.int32)
    new_carry = carry_ref[...] + n_chunks
    carry_ref[...] = new_carry
    cnt_ref[...] = new_carry


def _mix_out(x2d, of2d, ob2d, proj, yft2d, mod3, mod_row, lw, seq_len, tm):
    t = x2d.shape[0]
    tm = min(tm, t)
    nt8 = t // HALO
    rows8 = tm // HALO

    def col(cb):
        return pl.BlockSpec((tm, HEAD_W), lambda i: (i, cb))

    def halo_prev(cb):
        return pl.BlockSpec((HALO, HEAD_W), lambda i: (jnp.maximum(i * rows8 - 1, 0), cb))

    def halo_next(cb):
        return pl.BlockSpec((HALO, HEAD_W), lambda i: (jnp.minimum((i + 1) * rows8, nt8 - 1), cb))

    def full(a):
        return pl.BlockSpec(a.shape, lambda i: (0,) * a.ndim)

    tok_d = pl.BlockSpec((tm, D_MODEL), lambda i: (i, 0))
    tok_h = pl.BlockSpec((tm, HEAD_W), lambda i: (i, 0))
    weights = [lw["gla_norm_g"], lw["sgu_norm_g"], lw["sgu_norm_b"], lw["w_sgu_cat"], lw["b_sgu_full"],
               lw["w_conv"], lw["w_out"], lw["ln1_g"], lw["ln1_b"], lw["w_router_hi"], lw["w_router_lo"], lw["router_bias"]]
    weights += list(_mix_masks(tm))
    in_specs = ([tok_d, tok_h, tok_h, col(COL_G), col(COL_SGU), col(COL_SGV), col(COL_CVB), col(COL_CVC),
                 col(COL_CVX), halo_prev(COL_CVC), halo_prev(COL_CVX), halo_next(COL_CVC), halo_next(COL_CVX),
                 tok_h, pl.BlockSpec((None, 1, 6 * D_MODEL), lambda i: (mod_row(i, tm), 0, 0))]
                + [full(w) for w in weights])
    args = [x2d, of2d, ob2d] + [proj] * 10 + [yft2d, mod3] + weights
    return pl.pallas_call(
        functools.partial(_mix_out_kernel, seq_len=seq_len),
        grid=(t // tm,),
        in_specs=in_specs,
        out_specs=[tok_d,
                   pl.BlockSpec((tm, PACK_W), lambda i: (i, 0)),
                   pl.BlockSpec((None, TOP_K, tm), lambda i: (i, 0, 0)),
                   pl.BlockSpec((None, TOP_K, tm), lambda i: (i, 0, 0)),
                   pl.BlockSpec((None, 8, LANES), lambda i: (i, 0, 0)),
                   pl.BlockSpec((N_EXPERTS, LANES), lambda i: (0, 0))],
        out_shape=[jax.ShapeDtypeStruct((t, D_MODEL), F32),
                   jax.ShapeDtypeStruct((t, PACK_W), jnp.uint32),
                   jax.ShapeDtypeStruct((t // tm, TOP_K, tm), F32),
                   jax.ShapeDtypeStruct((t // tm, TOP_K, tm), jnp.int32),
                   jax.ShapeDtypeStruct((t // tm, 8, LANES), jnp.int32),
                   jax.ShapeDtypeStruct((N_EXPERTS, LANES), F32)],
        scratch_shapes=[pltpu.VMEM((N_EXPERTS, LANES), F32)],
        compiler_params=_cparams("arbitrary"),
        name="mix_out",
    )(*args)


def _local_rows(tm):
    need = tm * TOP_K + N_EXPERTS * (CHUNK - 1)
    return -(-need // LBLK) * LBLK


def _run_copy(local_ref, sorted_hbm, sem, local_row, sorted_row, to_sorted, rows=CHUNK):
    loc = local_ref.at[pl.ds(pl.multiple_of(local_row, CHUNK), rows)]
    srt = sorted_hbm.at[pl.ds(pl.multiple_of(sorted_row, CHUNK), rows)]
    return pltpu.make_async_copy(loc, srt, sem) if to_sorted else pltpu.make_async_copy(srt, loc, sem)


def _start_run_copies(tab_ref, gstart_ref, local_ref, sorted_hbm, sem, to_sorted):
    def per_expert(e, totals):
        n = tab_ref[0, e]
        l0 = tab_ref[1, e]
        g0 = (gstart_ref[e] + tab_ref[2, e]) * CHUNK
        n_big = n >> BIG_SHIFT
        n_small = n & ((1 << BIG_SHIFT) - 1)

        def big(j, c):
            _run_copy(local_ref, sorted_hbm, sem, l0 + j * BIG_ROWS, g0 + j * BIG_ROWS, to_sorted, BIG_ROWS).start()
            return c

        lax.fori_loop(0, n_big, big, 0)
        l1 = l0 + n_big * BIG_ROWS
        g1 = g0 + n_big * BIG_ROWS

        for j in range((1 << BIG_SHIFT) - 1):
            @pl.when(n_small > j)
            def _():
                _run_copy(local_ref, sorted_hbm, sem, l1 + j * CHUNK, g1 + j * CHUNK, to_sorted).start()
        return totals[0] + n_big, totals[1] + n_small

    return lax.fori_loop(0, N_EXPERTS, per_expert, (jnp.int32(0), jnp.int32(0)))


def _get_pending(pending_ref, s):
    return pending_ref[s, 0], pending_ref[s, 1]


def _set_pending(pending_ref, s, counts):
    pending_ref[s, 0] = counts[0]
    pending_ref[s, 1] = counts[1]


def _wait_run_copies(counts, local_ref, sorted_hbm, sem, to_sorted):
    batch = 8

    def wait_n(rows, reps):
        def body(j, c):
            for _ in range(reps):
                _run_copy(local_ref, sorted_hbm, sem, 0, 0, to_sorted, rows).wait()
            return c
        return body

    for count, rows in zip(counts, (BIG_ROWS, CHUNK)):
        lax.fori_loop(0, count >> 3, wait_n(rows, batch), 0)
        lax.fori_loop(0, count & (batch - 1), wait_n(rows, 1), 0)


def _moe_scatter_kernel(gstart_ref, gend_ref, gfill_ref, xw_ref, lpos_ref, tab_ref, *refs, has_shared):
    sorted_hbm, local_ref, sem, zsem, pending_ref = refs[1:] if has_shared else refs
    i = pl.program_id(0)
    n = pl.num_programs(0)
    slot = i & 1
    tm = xw_ref.shape[0]
    lrows = local_ref.shape[1]

    def zero_copy(e, j):
        off = pl.multiple_of(gend_ref[e] - (j + 1) * LBLK, LBLK)
        return pltpu.make_async_copy(local_ref.at[1, pl.ds(0, LBLK)], sorted_hbm.at[pl.ds(off, LBLK)], zsem)

    @pl.when(i == 0)
    def _():
        _set_pending(pending_ref, 0, (0, 0))
        _set_pending(pending_ref, 1, (0, 0))
        local_ref[1, 0:LBLK, :] = jnp.zeros((LBLK, PACK_W), jnp.uint32)

        def z_start(e, c):
            lax.fori_loop(0, gfill_ref[e], lambda j, c2: (zero_copy(e, j).start(), c2)[1], 0)
            return c

        def z_wait(e, c):
            lax.fori_loop(0, gfill_ref[e], lambda j, c2: (zero_copy(e, j).wait(), c2)[1], 0)
            return c

        lax.fori_loop(0, N_EXPERTS, z_start, 0)
        lax.fori_loop(0, N_EXPERTS, z_wait, 0)

    local = local_ref.at[slot]
    _wait_run_copies(_get_pending(pending_ref, slot), local, sorted_hbm, sem.at[slot], True)

    x = _unpack_bf16_pairs(xw_ref[...]).astype(BF16)
    lpos = lpos_ref[...].astype(jnp.int16)
    one = jnp.ones((LBLK, tm), BF16)
    used_rows = tab_ref[3, 0] * CHUNK

    def sort_block(b):
        riota = lax.broadcasted_iota(jnp.int16, (LBLK, tm), 0) + b * LBLK
        p = jnp.zeros((LBLK, tm), BF16)
        for k in range(TOP_K):
            p = jnp.where(riota == lpos[k:k + 1, :], one, p)
        local[b * LBLK:(b + 1) * LBLK, :] = _pack_exact_bf16_pairs(_dot(p, x))

    n_blocks = lrows // LBLK
    for b in range(n_blocks - 1):
        sort_block(b)
    pl.when(used_rows > (n_blocks - 1) * LBLK)(functools.partial(sort_block, n_blocks - 1))

    _set_pending(pending_ref, slot, _start_run_copies(tab_ref, gstart_ref, local, sorted_hbm, sem.at[slot], True))

    @pl.when(i == n - 1)
    def _():
        for s in range(2):
            _wait_run_copies(_get_pending(pending_ref, s), local_ref.at[s], sorted_hbm, sem.at[s], True)


def _moe_scatter(gstart, gend, gfill, xw, lpos, tab, n_rows, tm, shared=None):
    n_tiles = xw.shape[0] // tm
    lrows = _local_rows(tm)
    any_spec = pl.BlockSpec(memory_space=pl.ANY)
    in_specs = [pl.BlockSpec((tm, PACK_W), lambda i, *_: (i, 0)),
                pl.BlockSpec((None, TOP_K, tm), lambda i, *_: (i, 0, 0)),
                pl.BlockSpec((None, 8, LANES), lambda i, *_: (i, 0, 0), memory_space=pltpu.SMEM)]
    args = [gstart, gend, gfill, xw, lpos, tab]
    aliases = {}
    if shared is not None:
        in_specs.append(any_spec)
        aliases = {len(args): 0}
        args.append(shared)
    return pl.pallas_call(
        functools.partial(_moe_scatter_kernel, has_shared=shared is not None),
        grid_spec=pltpu.PrefetchScalarGridSpec(
            num_scalar_prefetch=3,
            grid=(n_tiles,),
            in_specs=in_specs,
            out_specs=any_spec,
            scratch_shapes=[pltpu.VMEM((2, lrows, PACK_W), jnp.uint32), pltpu.SemaphoreType.DMA((2,)),
                            pltpu.SemaphoreType.DMA, pltpu.SMEM((2, 2), jnp.int32)]),
        out_shape=jax.ShapeDtypeStruct((n_rows, PACK_W), jnp.uint32),
        input_output_aliases=aliases,
        compiler_params=_cparams("arbitrary"),
        name="moe_scatter",
    )(*args)


def _expert2_kernel(be_ref, nu_ref, xs_ref, wg_ref, wu_ref, wd_ref, ys_ref, wgu_b, wd_b):
    j = pl.program_id(0)

    @pl.when(jnp.logical_or(j == 0, be_ref[j] != be_ref[jnp.maximum(j - 1, 0)]))
    def _():
        wgu_b[:, :EXPERT_FF] = wg_ref[...].astype(BF16)
        wgu_b[:, EXPERT_FF:] = wu_ref[...].astype(BF16)
        wd_b[...] = wd_ref[...].astype(BF16)

    @pl.when(j < nu_ref[0])
    def _():
        x = _unpack_bf16_pairs(xs_ref[...]).astype(BF16)
        gu = _dot(x, wgu_b[...])
        a = _silu(gu[:, :EXPERT_FF]) * gu[:, EXPERT_FF:]
        ys_ref[...] = _pack_bf16_pairs(_dot(a.astype(BF16), wd_b[...]))


def _experts2(xs, block_expert, n_used, lw, layer, eblk):
    n_rows = xs.shape[0]
    ff = EXPERT_FF

    def blk(j, be, nu):
        return (jnp.minimum(j, jnp.maximum(nu[0] - 1, 0)), 0)

    def wblk(j, be, nu):
        return (layer, be[j], 0, 0)

    return pl.pallas_call(
        _expert2_kernel,
        grid_spec=pltpu.PrefetchScalarGridSpec(
            num_scalar_prefetch=2,
            grid=(n_used[0],),
            in_specs=[pl.BlockSpec((eblk, PACK_W), blk),
                      pl.BlockSpec((None, None, D_MODEL, ff), wblk),
                      pl.BlockSpec((None, None, D_MODEL, ff), wblk),
                      pl.BlockSpec((None, None, ff, D_MODEL), wblk)],
            out_specs=pl.BlockSpec((eblk, PACK_W), blk),
            scratch_shapes=[pltpu.VMEM((D_MODEL, 2 * ff), BF16), pltpu.VMEM((ff, D_MODEL), BF16)]),
        out_shape=jax.ShapeDtypeStruct((n_rows, PACK_W), jnp.uint32),
        compiler_params=_cparams("arbitrary"),
        name="moe_experts",
    )(block_expert, n_used, xs, lw["w_exp_gate"], lw["w_exp_up"], lw["w_exp_down"])


def _moe_combine_kernel(gstart_ref, x1_ref, xw_ref, lpos_ref, wrow_ref, tab_ref, tab_next_ref, mod_ref,
                        sg_ref, su_ref, sd_ref, ln2g_ref, ln2b_ref, sorted_hbm, o_ref,
                        local_ref, sem, pending_ref):
    i = pl.program_id(0)
    n = pl.num_programs(0)
    slot = i & 1
    tm = x1_ref.shape[0]
    lrows = local_ref.shape[1]

    @pl.when(i == 0)
    def _():
        local_ref[...] = jnp.zeros_like(local_ref)
        _set_pending(pending_ref, 0,
                     _start_run_copies(tab_ref, gstart_ref, local_ref.at[0], sorted_hbm, sem.at[0], False))

    @pl.when(i + 1 < n)
    def _():
        nxt = 1 - slot
        _set_pending(pending_ref, nxt, _start_run_copies(tab_next_ref, gstart_ref, local_ref.at[nxt], sorted_hbm,
                                                         sem.at[nxt], False))

    h = _unpack_bf16_pairs(xw_ref[...]).astype(BF16)
    a = _silu(_dot(h, sg_ref[...])) * _dot(h, su_ref[...])
    acc = _dot(a.astype(BF16), sd_ref[...])

    local = local_ref.at[slot]
    _wait_run_copies(_get_pending(pending_ref, slot), local, sorted_hbm, sem.at[slot], False)

    lpos = lpos_ref[...].astype(jnp.int16)
    wrow = wrow_ref[...].astype(BF16)

    def unsort_block(b, acc):
        riota = lax.broadcasted_iota(jnp.int16, (LBLK, tm), 0) + b * LBLK
        q = jnp.zeros((LBLK, tm), BF16)
        for k in range(TOP_K):
            q = jnp.where(riota == lpos[k:k + 1, :], jnp.broadcast_to(wrow[k:k + 1, :], (LBLK, tm)), q)
        y = _unpack_bf16_pairs(local[b * LBLK:(b + 1) * LBLK, :]).astype(BF16)
        return acc + _dot_tn(q, y)

    for b in range(lrows // LBLK):
        acc = unsort_block(b, acc)

    g2 = mod_ref[...][:, 5 * D_MODEL:6 * D_MODEL]
    u = DEEPNORM_ALPHA * x1_ref[...] + g2 * acc
    o_ref[...] = _ln(u) * ln2g_ref[...] + ln2b_ref[...]


def _moe_combine(gstart, x1, xw, lpos, wrow, tab, ys, mod3, mod_row, lw, tm):
    t = x1.shape[0]
    n_tiles = t // tm
    lrows = _local_rows(tm)

    def full(a):
        return pl.BlockSpec(a.shape, lambda i, *_: (0,) * a.ndim)

    tab_blk = lambda f: pl.BlockSpec((None, 8, LANES), f, memory_space=pltpu.SMEM)
    return pl.pallas_call(
        _moe_combine_kernel,
        grid_spec=pltpu.PrefetchScalarGridSpec(
            num_scalar_prefetch=1,
            grid=(n_tiles,),
            in_specs=[pl.BlockSpec((tm, D_MODEL), lambda i, *_: (i, 0)),
                      pl.BlockSpec((tm, PACK_W), lambda i, *_: (i, 0)),
                      pl.BlockSpec((None, TOP_K, tm), lambda i, *_: (i, 0, 0)),
                      pl.BlockSpec((None, TOP_K, tm), lambda i, *_: (i, 0, 0)),
                      tab_blk(lambda i, *_: (i, 0, 0)),
                      tab_blk(lambda i, *_: (jnp.minimum(i + 1, n_tiles - 1), 0, 0)),
                      pl.BlockSpec((None, 1, 6 * D_MODEL), lambda i, *_: (mod_row(i, tm), 0, 0)),
                      full(lw["w_sh_gate"]), full(lw["w_sh_up"]), full(lw["w_sh_down"]),
                      full(lw["ln2_g"]), full(lw["ln2_b"]),
                      pl.BlockSpec(memory_space=pl.ANY)],
            out_specs=pl.BlockSpec((tm, D_MODEL), lambda i, *_: (i, 0)),
            scratch_shapes=[pltpu.VMEM((2, lrows, PACK_W), jnp.uint32), pltpu.SemaphoreType.DMA((2,)),
                            pltpu.SMEM((2, 2), jnp.int32)]),
        out_shape=jax.ShapeDtypeStruct((t, D_MODEL), F32),
        compiler_params=_cparams("arbitrary"),
        name="moe_combine",
    )(gstart, x1, xw, lpos, wrow, tab, tab, mod3, lw["w_sh_gate"], lw["w_sh_up"], lw["w_sh_down"],
      lw["ln2_g"], lw["ln2_b"], ys)


def _moe_groups(groups, lw, layer):
    tm = MOE_TM
    chunks = [g["chunks"][:, 0].astype(jnp.int32) for g in groups]
    t_all = sum(g["x1"].shape[0] for g in groups)
    n_tiles = t_all // tm
    eblk = max(LBLK, min(EBLK_MAX, t_all * TOP_K // N_EXPERTS))
    region_rows = sum(chunks) * CHUNK
    padded = ((region_rows + eblk - 1) // eblk) * eblk
    end = jnp.cumsum(padded)
    start = end - padded
    n_blocks = -(-(t_all * TOP_K + n_tiles * N_EXPERTS * (CHUNK - 1)) // eblk) + N_EXPERTS
    n_used = end[-1] // eblk
    blk_row = jnp.minimum(jnp.arange(n_blocks, dtype=jnp.int32), jnp.maximum(n_used - 1, 0)) * eblk
    block_expert = jnp.minimum(jnp.sum((end[None, :] <= blk_row[:, None]).astype(jnp.int32), axis=1),
                               N_EXPERTS - 1).astype(jnp.int32)
    gfill = (padded - region_rows + LBLK - 1) // LBLK
    gstarts = []
    before = jnp.zeros_like(chunks[0])
    for ch in chunks:
        gstarts.append(start // CHUNK + before)
        before = before + ch
    xs = None
    for g, gstart in zip(groups, gstarts):
        fill = gfill if xs is None else jnp.zeros_like(gfill)
        xs = _moe_scatter(gstart, end, fill, g["xw"], g["lpos"], g["tab"], n_blocks * eblk, tm, xs)
    ys = _experts2(xs, block_expert, n_used.reshape(1), lw, layer, eblk)
    return [_moe_combine(gstart, g["x1"], g["xw"], g["lpos"], g["wrow"], g["tab"], ys, g["mod3"], g["mod_row"],
                         lw, tm) for g, gstart in zip(groups, gstarts)]


def _channel_dft_table():
    k = np.arange(FNET_CH, dtype=np.float64)
    ang = 2.0 * np.pi * np.outer(k, k) / FNET_CH
    eye = np.eye(FNET_GROUPS)
    return np.concatenate([np.kron(eye, np.cos(ang)), -np.kron(eye, np.sin(ang))], axis=1)


def _direct_dft_tables(n):
    k = np.arange(n, dtype=np.float64)
    ang = 2.0 * np.pi * (np.outer(k, k) % n) / n
    scale = 1.0 / math.sqrt(n * FNET_CH)
    return np.cos(ang) * scale, np.sin(ang) * scale


def _two_stage_dft_tables(n):
    n1 = FFT_N1
    n2 = n // n1
    a = np.arange(n1, dtype=np.float64)
    ang1 = 2.0 * np.pi * (np.outer(a, a) % n1) / n1
    k1 = np.arange(n1).reshape(n1, 1, 1)
    k2 = np.arange(n2).reshape(1, n2, 1)
    m2 = np.arange(n2).reshape(1, 1, n2)
    ang2 = 2.0 * np.pi * ((m2 * (k1 + n1 * k2)) % n) / n
    scale = 1.0 / math.sqrt(n * FNET_CH)
    return np.cos(ang1), np.sin(ang1), np.cos(ang2) * scale, np.sin(ang2) * scale


def _grid_sincos_table(rows, d):
    quarter = d // 4
    omega = 1.0 / (POS_BASE ** (np.arange(quarter, dtype=np.float64) / quarter))
    r = np.arange(rows, dtype=np.float64)[:, None] * omega
    c = np.arange(GRID_W, dtype=np.float64)[:, None] * omega
    return (np.concatenate([np.sin(r), np.cos(r)], axis=-1).astype(np.float32),
            np.concatenate([np.sin(c), np.cos(c)], axis=-1).astype(np.float32))


def _layer_weights(l, w_in, w_gla_a, b_gla_a, gla_norm_g, sgu_norm_g, sgu_norm_b, w_sgu, b_sgu, w_conv,
                   w_out, ln1_g, ln1_b, ln2_g, ln2_b, w_router, router_bias,
                   w_exp_gate, w_exp_up, w_exp_down, w_sh_gate, w_sh_up, w_sh_down):
    wi = w_in[l]
    lr0 = 4 * HEAD_W
    w_in_p = jnp.concatenate(
        [wi[:, :lr0], wi[:, lr0 + 2 * GLA_LR:lr0 + 2 * GLA_LR + 5 * HEAD_W], wi[:, lr0:lr0 + 2 * GLA_LR],
         jnp.zeros((D_MODEL, LR_W - 2 * GLA_LR), F32), wi[:, lr0 + 2 * GLA_LR + 5 * HEAD_W:]], axis=1).astype(BF16)
    wa_pad = jnp.zeros((LR_W, 2 * HEAD_W), F32)
    wa_pad = wa_pad.at[:GLA_LR, :HEAD_W].set(w_gla_a[l, 0])
    wa_pad = wa_pad.at[GLA_LR:2 * GLA_LR, HEAD_W:].set(w_gla_a[l, 1])
    w_router_pad = jnp.concatenate([w_router[l], jnp.zeros((D_MODEL, LANES - N_EXPERTS), F32)], axis=1)
    w_router_hi = w_router_pad.astype(BF16)
    row = lambda a: a[l].reshape(1, -1)
    return {
        "w_in_p": w_in_p,
        "wa_hi": wa_pad.astype(BF16), "wa_lo": (wa_pad - wa_pad.astype(BF16).astype(F32)).astype(BF16),
        "ba": jnp.concatenate([b_gla_a[l, 0], b_gla_a[l, 1]]).reshape(1, 2 * HEAD_W),
        "gla_norm_g": row(gla_norm_g), "sgu_norm_g": row(sgu_norm_g), "sgu_norm_b": row(sgu_norm_b),
        "w_sgu_cat": jnp.concatenate([w_sgu[l, g] for g in range(GMLP_GROUPS)], axis=1).astype(BF16),
        "b_sgu_full": jnp.repeat(b_sgu[l].T, HEAD_W // GMLP_GROUPS, axis=1),
        "w_conv": w_conv[l],
        "w_out": w_out[l].astype(BF16),
        "ln1_g": row(ln1_g), "ln1_b": row(ln1_b), "ln2_g": row(ln2_g), "ln2_b": row(ln2_b),
        "w_router_hi": w_router_hi, "w_router_lo": (w_router_pad - w_router_hi.astype(F32)).astype(BF16),
        "router_bias": router_bias[l].reshape(N_EXPERTS, 1),
        "w_exp_gate": w_exp_gate, "w_exp_up": w_exp_up, "w_exp_down": w_exp_down,
        "w_sh_gate": w_sh_gate[l].astype(BF16), "w_sh_up": w_sh_up[l].astype(BF16),
        "w_sh_down": w_sh_down[l].astype(BF16),
    }


def _state_to_blockdiag_t(s):
    bsz = s.shape[0]
    st = jnp.swapaxes(s, 2, 3)
    eye = jnp.eye(GLA_HEADS, dtype=s.dtype)
    return jnp.einsum("bhvd,hg->bhvgd", st, eye).reshape(bsz, HEAD_W, HEAD_W)


def _blockdiag_t_to_state(st):
    bsz = st.shape[0]
    s5 = st.reshape(bsz, GLA_HEADS, GLA_DK, GLA_HEADS, GLA_DK)
    diag = jnp.stack([s5[:, h, :, h, :] for h in range(GLA_HEADS)], axis=1)
    return jnp.swapaxes(diag, 2, 3)


def _pre_moe(x3, pos, mod3, mod_row, lw, st0, emit_final, tabs):
    bsz, n, _ = x3.shape
    t = bsz * n
    outs = _in_proj(x3.reshape(t, D_MODEL), pos, mod3, mod_row, lw["w_in_p"], tabs["cs"], tm=512)
    if pos is not None:
        proj, zr, zi, x2d = outs
    else:
        proj, zr, zi = outs
        x2d = x3.reshape(t, D_MODEL)
    gla_out = _gla(proj.reshape(bsz, n, STORE_W), lw["wa_hi"], lw["wa_lo"], lw["ba"], st0, emit_final)
    o_f, o_b = gla_out[:2]
    zr3 = zr.reshape(bsz, n, HEAD_W)
    zi3 = zi.reshape(bsz, n, HEAD_W)
    if "two_stage" in tabs:
        yft = _fft_two_stage(zr3, zi3, tabs["two_stage"])
    else:
        yft = _fft_direct(zr3, zi3, *tabs["direct"])
    x1, xw, wrow, lpos, tab, chunks = _mix_out(x2d, o_f.reshape(t, HEAD_W), o_b.reshape(t, HEAD_W), proj,
                                             yft.reshape(t, HEAD_W), mod3, mod_row, lw, seq_len=n, tm=MOE_TM)
    group = {"x1": x1, "xw": xw, "wrow": wrow, "lpos": lpos, "tab": tab, "chunks": chunks,
             "mod3": mod3, "mod_row": mod_row, "shape": (bsz, n, D_MODEL)}
    return group, gla_out[2:]


def _trunk_layer(x3, pos, mod3, mod_row, lw, layer, st0, emit_final, tabs):
    group, finals = _pre_moe(x3, pos, mod3, mod_row, lw, st0, emit_final, tabs)
    (x2,) = _moe_groups([group], lw, layer)
    return x2.reshape(group["shape"]), finals


def kernel(x_prompt, x_sample, c, state_gla, c_ctx, w_ada, b_ada, w_in, w_gla_a, b_gla_a, gla_norm_g, sgu_norm_g, sgu_norm_b, w_sgu, b_sgu, w_conv, w_out, ln1_g, ln1_b, ln2_g, ln2_b, w_router, router_bias, w_exp_gate, w_exp_up, w_exp_down, w_sh_gate, w_sh_up, w_sh_down):
    n_layers = w_ada.shape[0]
    bp, np_, _ = x_prompt.shape
    bs, ns, _ = x_sample.shape
    assert bs <= 7

    cond8 = jnp.concatenate([c_ctx[None, :], c, jnp.zeros((7 - bs, D_MODEL), F32)], axis=0)
    mod = _ada_mod(cond8, w_ada, b_ada)

    tabs_p = {"cs": jnp.asarray(_channel_dft_table(), BF16),
              "direct": tuple(jnp.asarray(a, BF16) for a in _direct_dft_tables(np_))}
    tabs_s = {"cs": tabs_p["cs"],
              "two_stage": tuple(jnp.asarray(a, BF16) for a in _two_stage_dft_tables(ns))}
    rtab, ctab = _grid_sincos_table(ns // GRID_W, D_MODEL)
    pos = jnp.concatenate([jnp.repeat(jnp.asarray(rtab), GRID_W, axis=0),
                           jnp.tile(jnp.asarray(ctab), (ns // GRID_W, 1))], axis=-1)

    prompt_row = lambda i, tm: 0
    sample_row = lambda i, tm: 1 + (i * tm) // ns

    y_p = x_prompt
    y_s = x_sample
    finals = []
    for l in range(n_layers):
        lw = _layer_weights(l, w_in, w_gla_a, b_gla_a, gla_norm_g, sgu_norm_g, sgu_norm_b, w_sgu, b_sgu,
                            w_conv, w_out, ln1_g, ln1_b, ln2_g, ln2_b, w_router, router_bias,
                            w_exp_gate, w_exp_up, w_exp_down, w_sh_gate, w_sh_up, w_sh_down)
        mod3 = mod[l].reshape(8, 1, 6 * D_MODEL)
        group_p, fin = _pre_moe(y_p, None, mod3, prompt_row, lw, None, True, tabs_p)
        finals.append(jnp.stack([_blockdiag_t_to_state(fin[0]), _blockdiag_t_to_state(fin[1])], axis=1))
        st0 = jnp.stack([_state_to_blockdiag_t(state_gla[:, l, 0]), _state_to_blockdiag_t(state_gla[:, l, 1])])
        group_s, _ = _pre_moe(y_s, pos if l == 0 else None, mod3, sample_row, lw, st0, False, tabs_s)
        y_p, y_s = _moe_groups([group_p, group_s], lw, l)
        y_p = y_p.reshape(group_p["shape"])
        y_s = y_s.reshape(group_s["shape"])
    new_state = jnp.stack(finals, axis=1).astype(x_prompt.dtype)
    return (y_p, y_s, new_state)
```

```python
import functools
import math

import numpy as np
import jax
import jax.numpy as jnp
from jax import lax
from jax.experimental import pallas as pl
from jax.experimental.pallas import tpu as pltpu

F32 = jnp.float32
BF16 = jnp.bfloat16

D_MODEL = 1024
DEPTH = 2
GRID_W = 64
HEAD_W = 256
GLA_HEADS = 4
GLA_DK = 64
GLA_LR = 16
GLA_TAU = 16.0
GLA_CHUNK = 64
GMLP_GROUPS = 4
GMLP_CHUNK = 128
FNET_GROUPS = 4
FNET_CH = 64
N_EXPERTS = 64
TOP_K = 8
N_GROUPS = 8
TOPK_GROUPS = 4
EXPERT_FF = 256
ROUTED_SCALE = 2.5
DEEPNORM_ALPHA = (2 * DEPTH) ** 0.25
LN_EPS = 1e-5
RMS_EPS = 1e-6
POS_BASE = 10000.0

COL_Q, COL_K, COL_V, COL_G, COL_SGU, COL_SGV, COL_CVB, COL_CVC, COL_CVX = range(9)
LR_W = 128
STORE_W = 9 * HEAD_W + LR_W
PROJ_W = STORE_W + HEAD_W
COL_LR = (9 * HEAD_W) // LR_W
HALO = 16

LANES = 128
PACK_W = D_MODEL // 2
EBLK_MAX = 1024
LBLK = 256
MOE_TM = 256
CHUNK = 8
BIG_SHIFT = 2
BIG_ROWS = CHUNK << BIG_SHIFT
SEG = 256
FFT_N1 = 64
VMEM_LIMIT = 56 * 1024 * 1024


def _cparams(*sem):
    return pltpu.CompilerParams(dimension_semantics=sem, vmem_limit_bytes=VMEM_LIMIT)


def _ln(x):
    mu = jnp.mean(x, axis=-1, keepdims=True)
    xc = x - mu
    var = jnp.mean(xc * xc, axis=-1, keepdims=True)
    return xc * lax.rsqrt(var + LN_EPS)


def _sigmoid(x):
    return 1.0 / (1.0 + jnp.exp(-x))


def _silu(x):
    return x * _sigmoid(x)


def _pack_bf16_pairs(x):
    w = x.shape[1] // 2
    lo = lax.bitcast_convert_type(x[:, :w].astype(BF16).astype(F32), jnp.uint32)
    hi = lax.bitcast_convert_type(x[:, w:].astype(BF16).astype(F32), jnp.uint32)
    return (lo >> 16) | (hi & jnp.uint32(0xFFFF0000))


def _pack_exact_bf16_pairs(x):
    w = x.shape[1] // 2
    lo = lax.bitcast_convert_type(x[:, :w], jnp.uint32)
    hi = lax.bitcast_convert_type(x[:, w:], jnp.uint32)
    return (lo >> 16) | hi


def _unpack_bf16_pairs(u):
    lo = lax.bitcast_convert_type(u << 16, F32)
    hi = lax.bitcast_convert_type(u & jnp.uint32(0xFFFF0000), F32)
    return jnp.concatenate([lo, hi], axis=1)


def _dot(a, b):
    return jnp.dot(a, b, preferred_element_type=F32)


def _dot_nt(a, b):
    return lax.dot_general(a, b, (((1,), (1,)), ((), ())), preferred_element_type=F32)


def _dot_tn(a, b):
    return lax.dot_general(a, b, (((0,), (0,)), ((), ())), preferred_element_type=F32)


def _ada_kernel(c_ref, w_ref, b_ref, o_ref):
    c = c_ref[...]
    o_ref[...] = _dot(_silu(c).astype(BF16), w_ref[...].astype(BF16)) + b_ref[...]


def _ada_mod(cond8, w_ada, b_ada):
    n_l, d, w6 = w_ada.shape
    tn = 1536
    return pl.pallas_call(
        _ada_kernel,
        grid=(n_l, w6 // tn),
        in_specs=[pl.BlockSpec((8, d), lambda l, j: (0, 0)),
                  pl.BlockSpec((None, d, tn), lambda l, j: (l, 0, j)),
                  pl.BlockSpec((None, 1, tn), lambda l, j: (l, 0, j))],
        out_specs=pl.BlockSpec((None, 8, tn), lambda l, j: (l, 0, j)),
        out_shape=jax.ShapeDtypeStruct((n_l, 8, w6), F32),
        compiler_params=_cparams("parallel", "parallel"),
        name="ada_mod",
    )(cond8, w_ada, b_ada.reshape(n_l, 1, w6))


def _in_proj_kernel(*refs, has_pos):
    if has_pos:
        x_ref, pos_ref, mod_ref, w_ref, cs_ref, proj_ref, zr_ref, zi_ref, x0_ref = refs
        x = x_ref[...] + pos_ref[...]
        x0_ref[...] = x
    else:
        x_ref, mod_ref, w_ref, cs_ref, proj_ref, zr_ref, zi_ref = refs
        x = x_ref[...]
    mod = mod_ref[...]
    sh1 = mod[:, 0:D_MODEL]
    sc1 = mod[:, D_MODEL:2 * D_MODEL]
    h = _ln(x) * (1.0 + sc1) + sh1
    proj = _dot(h.astype(BF16), w_ref[...])
    proj_ref[...] = proj[:, :STORE_W].astype(BF16)
    ft = proj[:, STORE_W:].astype(BF16)
    z = _dot(ft, cs_ref[...])
    zr_ref[...] = z[:, :HEAD_W].astype(BF16)
    zi_ref[...] = z[:, HEAD_W:].astype(BF16)


def _in_proj(x2d, pos, mod3, mod_row, w_in_p, cs, tm):
    t = x2d.shape[0]
    tm = min(tm, t)
    in_specs = [pl.BlockSpec((tm, D_MODEL), lambda i: (i, 0))]
    args = [x2d]
    out_shape = [jax.ShapeDtypeStruct((t, STORE_W), BF16),
                 jax.ShapeDtypeStruct((t, HEAD_W), BF16),
                 jax.ShapeDtypeStruct((t, HEAD_W), BF16)]
    out_specs = [pl.BlockSpec((tm, STORE_W), lambda i: (i, 0)),
                 pl.BlockSpec((tm, HEAD_W), lambda i: (i, 0)),
                 pl.BlockSpec((tm, HEAD_W), lambda i: (i, 0))]
    if pos is not None:
        n_pos = pos.shape[0] // tm
        in_specs.append(pl.BlockSpec((tm, D_MODEL), lambda i: (i % n_pos, 0)))
        args.append(pos)
        out_shape.append(jax.ShapeDtypeStruct((t, D_MODEL), F32))
        out_specs.append(pl.BlockSpec((tm, D_MODEL), lambda i: (i, 0)))
    in_specs += [pl.BlockSpec((None, 1, 6 * D_MODEL), lambda i: (mod_row(i, tm), 0, 0)),
                 pl.BlockSpec((D_MODEL, PROJ_W), lambda i: (0, 0)),
                 pl.BlockSpec((HEAD_W, 2 * HEAD_W), lambda i: (0, 0))]
    args += [mod3, w_in_p, cs]
    return pl.pallas_call(
        functools.partial(_in_proj_kernel, has_pos=pos is not None),
        grid=(t // tm,),
        in_specs=in_specs,
        out_specs=out_specs,
        out_shape=out_shape,
        compiler_params=_cparams("parallel"),
        name="in_proj",
    )(*args)


def _gla_masks(reverse):
    i = np.arange(SEG)
    same_chunk = (i[:, None] // GLA_CHUNK) == (i[None, :] // GLA_CHUNK)
    tri = same_chunk & ((i[None, :] >= i[:, None]) if reverse else (i[None, :] <= i[:, None]))
    l_idx = i[:, None] % GLA_CHUNK
    m_idx = np.arange(GLA_CHUNK)[None, :]
    causal = (m_idx >= l_idx) if reverse else (m_idx <= l_idx)
    return (jnp.asarray(tri, BF16), jnp.asarray(same_chunk, BF16), jnp.asarray(same_chunk, F32),
            jnp.asarray(causal, F32))


def _gla_segment(q, k, v, pre, st_ref, o_ref, masks, reverse):
    seg = q.shape[0]
    n_chunks = seg // GLA_CHUNK
    la = (jnp.minimum(pre, 0.0) - jnp.log1p(jnp.exp(-jnp.abs(pre)))) * (1.0 / GLA_TAU)

    tri_m, ones_m, bd, causal = masks
    hi = la.astype(BF16)
    lo = (la - hi.astype(F32)).astype(BF16)
    b = _dot(tri_m, hi) + _dot(tri_m, lo)
    btot = _dot(ones_m, hi) + _dot(ones_m, lo)

    q_dec = q * (GLA_DK ** -0.5) * jnp.exp(b)
    k_inv = (k * jnp.exp(-b)).astype(BF16)
    k_end = (k * jnp.exp(btot - b)).astype(BF16)
    dec = jnp.exp(btot)
    vb = v.astype(BF16)
    keep = causal > 0.5

    st = st_ref[...]
    order = range(n_chunks - 1, -1, -1) if reverse else range(n_chunks)
    for ci in order:
        sl = slice(ci * GLA_CHUNK, (ci + 1) * GLA_CHUNK)
        qd = q_dec[sl]
        qbd = (jnp.concatenate([qd] * GLA_HEADS, axis=0) * bd).astype(BF16)
        a = _dot_nt(qbd, k_inv[sl])
        a = jnp.where(keep, a, 0.0)
        rr = _dot(a.astype(BF16), vb[sl])
        o = _dot_nt(qd.astype(BF16), st.astype(BF16))
        for h in range(GLA_HEADS):
            hs = slice(h * GLA_CHUNK, (h + 1) * GLA_CHUNK)
            o = o + rr[hs] * bd[hs]
        o_ref[sl, :] = o
        kvt = _dot_tn(vb[sl], k_end[sl])
        st = st * dec[ci * GLA_CHUNK:ci * GLA_CHUNK + 1, :] + kvt * bd
    st_ref[...] = st


def _gla_kernel(*refs, has_init, emit_final):
    qf, kf, vf, lrf, qb, kb, vb, lrb, wah_ref, wal_ref, ba_ref = refs[:11]
    mask_refs = refs[11:19]
    rest = refs[19:]
    if has_init:
        s0f, s0b = rest[:2]
        rest = rest[2:]
    of_ref, ob_ref = rest[:2]
    rest = rest[2:]
    if emit_final:
        sff, sfb = rest[:2]
        rest = rest[2:]
    stf, stb = rest

    s = pl.program_id(1)

    @pl.when(s == 0)
    def _():
        if has_init:
            stf[...] = s0f[...]
            stb[...] = s0b[...]
        else:
            stf[...] = jnp.zeros_like(stf)
            stb[...] = jnp.zeros_like(stb)

    def decay_pre(lr_ref):
        lr = lr_ref[...]
        return _dot(lr, wah_ref[...]) + _dot(lr, wal_ref[...]) + ba_ref[...]

    f32 = lambda ref: ref[...].astype(F32)

    masks_f = tuple(m[...] for m in mask_refs[:4])
    masks_b = tuple(m[...] for m in mask_refs[4:])
    _gla_segment(f32(qf), f32(kf), f32(vf), decay_pre(lrf)[:, :HEAD_W], stf, of_ref, masks_f, reverse=False)
    _gla_segment(f32(qb), f32(kb), f32(vb), decay_pre(lrb)[:, HEAD_W:], stb, ob_ref, masks_b, reverse=True)

    if emit_final:
        @pl.when(s == pl.num_programs(1) - 1)
        def _():
            sff[...] = stf[...]
            sfb[...] = stb[...]


def _gla(proj3, wa_hi, wa_lo, ba, st0, emit_final):
    bsz, n, _ = proj3.shape
    nseg = n // SEG

    def col(cb, width=HEAD_W, rev=False):
        if rev:
            return pl.BlockSpec((None, SEG, width), lambda b, s: (b, nseg - 1 - s, cb))
        return pl.BlockSpec((None, SEG, width), lambda b, s: (b, s, cb))

    in_specs = [col(COL_Q), col(COL_K), col(COL_V), col(COL_LR, LR_W),
                col(COL_Q, rev=True), col(COL_K, rev=True), col(COL_V, rev=True), col(COL_LR, LR_W, rev=True),
                pl.BlockSpec((LR_W, 2 * HEAD_W), lambda b, s: (0, 0)),
                pl.BlockSpec((LR_W, 2 * HEAD_W), lambda b, s: (0, 0)),
                pl.BlockSpec((1, 2 * HEAD_W), lambda b, s: (0, 0))]
    masks = _gla_masks(False) + _gla_masks(True)
    in_specs += [pl.BlockSpec(m.shape, lambda b, s: (0, 0)) for m in masks]
    args = [proj3] * 8 + [wa_hi, wa_lo, ba] + list(masks)
    st_spec = pl.BlockSpec((None, HEAD_W, HEAD_W), lambda b, s: (b, 0, 0))
    if st0 is not None:
        in_specs += [st_spec, st_spec]
        args += [st0[0], st0[1]]
    out_shape = [jax.ShapeDtypeStruct((bsz, n, HEAD_W), F32)] * 2
    out_specs = [pl.BlockSpec((None, SEG, HEAD_W), lambda b, s: (b, s, 0)),
                 pl.BlockSpec((None, SEG, HEAD_W), lambda b, s: (b, nseg - 1 - s, 0))]
    if emit_final:
        out_shape += [jax.ShapeDtypeStruct((bsz, HEAD_W, HEAD_W), F32)] * 2
        out_specs += [st_spec, st_spec]
    return pl.pallas_call(
        functools.partial(_gla_kernel, has_init=st0 is not None, emit_final=emit_final),
        grid=(bsz, nseg),
        in_specs=in_specs,
        out_specs=out_specs,
        out_shape=out_shape,
        scratch_shapes=[pltpu.VMEM((HEAD_W, HEAD_W), F32), pltpu.VMEM((HEAD_W, HEAD_W), F32)],
        compiler_params=_cparams("parallel", "arbitrary"),
        name="gla",
    )(*args)


def _fft_direct_kernel(zr_ref, zi_ref, cn_ref, sn_ref, o_ref):
    o_ref[...] = _dot(cn_ref[...], zr_ref[...]) + _dot(sn_ref[...], zi_ref[...])


def _fft_direct(zr3, zi3, cn, sn):
    bsz, n, w = zr3.shape
    blk = pl.BlockSpec((None, n, w), lambda b: (b, 0, 0))
    tab = pl.BlockSpec((n, n), lambda b: (0, 0))
    return pl.pallas_call(
        _fft_direct_kernel,
        grid=(bsz,),
        in_specs=[blk, blk, tab, tab],
        out_specs=blk,
        out_shape=jax.ShapeDtypeStruct((bsz, n, w), F32),
        compiler_params=_cparams("parallel"),
        name="fft_direct",
    )(zr3, zi3, cn, sn)


def _fft_a_kernel(zr_ref, zi_ref, c_ref, s_ref, gr_ref, gi_ref):
    zr = zr_ref[...]
    zi = zi_ref[...]
    cm = c_ref[...]
    sm = s_ref[...]
    gr_ref[...] = (_dot(cm, zr) + _dot(sm, zi)).astype(BF16)
    gi_ref[...] = (_dot(cm, zi) - _dot(sm, zr)).astype(BF16)


def _fft_c_kernel(gr_ref, gi_ref, mc_ref, ms_ref, o_ref):
    for j in range(gr_ref.shape[0]):
        o_ref[:, j, :] = _dot(mc_ref[j], gr_ref[j]) + _dot(ms_ref[j], gi_ref[j])


def _fft_two_stage(zr3, zi3, tabs):
    bsz, n, w = zr3.shape
    n1 = FFT_N1
    n2 = n // n1
    c1, s1, mc, ms = tabs
    tn = 2048
    wide = n2 * w
    blk = pl.BlockSpec((None, n1, tn), lambda b, j: (b, 0, j))
    tab = pl.BlockSpec((n1, n1), lambda b, j: (0, 0))
    gr, gi = pl.pallas_call(
        _fft_a_kernel,
        grid=(bsz, wide // tn),
        in_specs=[blk, blk, tab, tab],
        out_specs=[blk, blk],
        out_shape=[jax.ShapeDtypeStruct((bsz, n1, wide), BF16)] * 2,
        compiler_params=_cparams("parallel", "parallel"),
        name="fft_stage_a",
    )(zr3.reshape(bsz, n1, wide), zi3.reshape(bsz, n1, wide), c1, s1)
    kb = 8
    gblk = pl.BlockSpec((None, kb, n2, w), lambda b, j: (b, j, 0, 0))
    mblk = pl.BlockSpec((kb, n2, n2), lambda b, j: (j, 0, 0))
    out = pl.pallas_call(
        _fft_c_kernel,
        grid=(bsz, n1 // kb),
        in_specs=[gblk, gblk, mblk, mblk],
        out_specs=pl.BlockSpec((None, n2, kb, w), lambda b, j: (b, 0, j, 0)),
        out_shape=jax.ShapeDtypeStruct((bsz, n2, n1, w), F32),
        compiler_params=_cparams("parallel", "parallel"),
        name="fft_stage_c",
    )(gr.reshape(bsz, n1, n2, w), gi.reshape(bsz, n1, n2, w), mc, ms)
    return out.reshape(bsz, n, w)


def _mix_masks(tm):
    h = np.arange(HEAD_W) // GLA_DK
    head_mean = (h[:, None] == h[None, :]) / GLA_DK
    rg = np.arange(GMLP_GROUPS * GMLP_CHUNK) // GMLP_CHUNK
    cg = np.arange(HEAD_W) // (HEAD_W // GMLP_GROUPS)
    t = np.arange(tm)
    e = np.arange(N_EXPERTS)
    return (jnp.asarray(head_mean, BF16), jnp.asarray(rg[:, None] == cg[None, :], F32),
            jnp.asarray(t[:, None] < t[None, :], BF16), jnp.asarray(e[None, :] < e[:, None], BF16))


def _route(logits, bias, before, lower):
    tm = logits.shape[1]
    s = _sigmoid(logits)
    biased = s + bias
    neg = -jnp.inf
    rows = lax.broadcasted_iota(jnp.int32, (8, tm), 0)

    def first_argmax(x, ids, sentinel):
        m = jnp.max(x, axis=0, keepdims=True)
        return m, jnp.min(jnp.where(x == m, ids, sentinel), axis=0, keepdims=True)

    gs_rows = []
    for g in range(N_GROUPS):
        x = biased[8 * g:8 * g + 8]
        m1, i1 = first_argmax(x, rows, 8)
        m2 = jnp.max(jnp.where(rows == i1, neg, x), axis=0, keepdims=True)
        gs_rows.append(m1 + m2)
    gs = jnp.concatenate(gs_rows, axis=0)
    gsel = jnp.zeros((N_GROUPS, tm), F32)
    for _ in range(TOPK_GROUPS):
        _, i = first_argmax(gs, rows, 8)
        hit = rows == i
        gsel = jnp.where(hit, 1.0, gsel)
        gs = jnp.where(hit, neg, gs)

    xs = [jnp.where(gsel[g:g + 1] > 0.0, biased[8 * g:8 * g + 8], neg) for g in range(N_GROUPS)]
    ids = [rows + 8 * g for g in range(N_GROUPS)]
    sel = [jnp.zeros((8, tm), F32) for _ in range(N_GROUPS)]
    eids = []
    for _ in range(TOP_K):
        m = xs[0]
        for g in range(1, N_GROUPS):
            m = jnp.maximum(m, xs[g])
        m = jnp.max(m, axis=0, keepdims=True)
        cand = jnp.where(xs[0] == m, ids[0], N_EXPERTS)
        for g in range(1, N_GROUPS):
            cand = jnp.minimum(cand, jnp.where(xs[g] == m, ids[g], N_EXPERTS))
        i = jnp.min(cand, axis=0, keepdims=True)
        eids.append(i)
        for g in range(N_GROUPS):
            hit = ids[g] == i
            sel[g] = jnp.where(hit, 1.0, sel[g])
            xs[g] = jnp.where(hit, neg, xs[g])

    sel_all = jnp.concatenate(sel, axis=0)
    seen = _dot(sel_all.astype(BF16), before)
    counts = jnp.sum(sel_all, axis=1, keepdims=True)

    n_chunks = jnp.ceil(counts * (1.0 / CHUNK))
    run_start = _dot(lower, jnp.broadcast_to(n_chunks, (N_EXPERTS, LANES)).astype(BF16))[:, 0:1] * CHUNK
    local_pos = seen + run_start

    def pick(k, table):
        acc = None
        for g in range(N_GROUPS):
            v = jnp.where(ids[g] == eids[k], table[8 * g:8 * g + 8], 0.0)
            acc = v if acc is None else acc + v
        return jnp.sum(acc, axis=0, keepdims=True)

    w_raw = [pick(k, s) for k in range(TOP_K)]
    lpos = [pick(k, local_pos) for k in range(TOP_K)]
    tot = w_raw[0]
    for k in range(1, TOP_K):
        tot = tot + w_raw[k]
    weights = [w / tot * ROUTED_SCALE for w in w_raw]
    return weights, lpos, n_chunks, run_start


def _mix_out_kernel(x_ref, of_ref, ob_ref, g_ref, su_ref, sv_ref, cb_ref, cc_ref, cx_ref,
                    ccp_ref, cxp_ref, ccn_ref, cxn_ref, ft_ref, mod_ref,
                    glag_ref, sgng_ref, sgnb_ref, wsgu_ref, bsgu_ref, wconv_ref, wout_ref,
                    ln1g_ref, ln1b_ref, wrh_ref, wrl_ref, rb_ref, hmean_ref, sgubd_ref, before_ref, lower_ref,
                    x1_ref, xw_ref, wrow_ref, lpos_ref, tab_ref, cnt_ref, carry_ref, *, seq_len):
    tm = x_ref.shape[0]
    i = pl.program_id(0)

    @pl.when(i == 0)
    def _():
        carry_ref[...] = jnp.zeros_like(carry_ref)

    mod = mod_ref[...]
    g1 = mod[:, 2 * D_MODEL:3 * D_MODEL]
    sh2 = mod[:, 3 * D_MODEL:4 * D_MODEL]
    sc2 = mod[:, 4 * D_MODEL:5 * D_MODEL]

    o = of_ref[...] + ob_ref[...]
    head_mean = hmean_ref[...]
    o2 = o * o
    o2_hi = o2.astype(BF16)
    o2_lo = (o2 - o2_hi.astype(F32)).astype(BF16)
    ms = _dot(o2_hi, head_mean) + _dot(o2_lo, head_mean)
    y_gla = o * lax.rsqrt(ms + RMS_EPS) * glag_ref[...] * _silu(g_ref[...].astype(F32))

    vn = _ln(sv_ref[...].astype(F32)) * sgng_ref[...] + sgnb_ref[...]
    sgu_bd = sgubd_ref[...]
    sp_parts = []
    for j in range(tm // GMLP_CHUNK):
        vc = vn[j * GMLP_CHUNK:(j + 1) * GMLP_CHUNK]
        vbd = (jnp.concatenate([vc] * GMLP_GROUPS, axis=0) * sgu_bd).astype(BF16)
        sp_parts.append(_dot(wsgu_ref[...], vbd) + bsgu_ref[...])
    y_sgu = su_ref[...].astype(F32) * jnp.concatenate(sp_parts, axis=0)

    z = cc_ref[...].astype(F32) * cx_ref[...].astype(F32)
    z_before = (ccp_ref[...].astype(F32) * cxp_ref[...].astype(F32))[HALO - 1:HALO, :]
    z_after = (ccn_ref[...].astype(F32) * cxn_ref[...].astype(F32))[0:1, :]
    row = lax.broadcasted_iota(jnp.int32, (tm, HEAD_W), 0)
    pos = (i * tm + row) & (seq_len - 1)
    z_prev = jnp.where(row == 0, z_before, pltpu.roll(z, 1, 0))
    z_next = jnp.where(row == tm - 1, z_after, pltpu.roll(z, tm - 1, 0))
    z_prev = jnp.where(pos == 0, 0.0, z_prev)
    z_next = jnp.where(pos == seq_len - 1, 0.0, z_next)
    wconv = wconv_ref[...]
    y_conv = cb_ref[...].astype(F32) * (wconv[0:1] * z_prev + wconv[1:2] * z + wconv[2:3] * z_next)

    y = (_dot(y_gla.astype(BF16), wout_ref[0:HEAD_W, :])
         + _dot(y_sgu.astype(BF16), wout_ref[HEAD_W:2 * HEAD_W, :])
         + _dot(y_conv.astype(BF16), wout_ref[2 * HEAD_W:3 * HEAD_W, :])
         + _dot(ft_ref[...].astype(BF16), wout_ref[3 * HEAD_W:4 * HEAD_W, :]))
    x1 = _ln(DEEPNORM_ALPHA * x_ref[...] + g1 * y) * ln1g_ref[...] + ln1b_ref[...]
    x1_ref[...] = x1
    h2 = _ln(x1) * (1.0 + sc2) + sh2
    xw_ref[...] = _pack_bf16_pairs(h2)

    h2_hi = h2.astype(BF16)
    h2_lo = (h2 - h2_hi.astype(F32)).astype(BF16)
    logits = _dot(h2_hi, wrh_ref[...]) + _dot(h2_hi, wrl_ref[...]) + _dot(h2_lo, wrh_ref[...])
    weights, lpos, n_chunks, run_start = _route(logits.T[:N_EXPERTS], rb_ref[...], before_ref[...], lower_ref[...])
    lpos_ref[...] = jnp.concatenate(lpos, axis=0).astype(jnp.int32)
    wrow_ref[...] = jnp.concatenate(weights, axis=0)
    carry = carry_ref[:, 0:1]
    lane = lax.broadcasted_iota(jnp.int32, (N_EXPERTS, LANES), 1)
    total = jnp.sum(n_chunks, axis=0, keepdims=True)
    cols = jnp.where(lane == 0, n_chunks, jnp.where(lane == 1, run_start,
                                                   jnp.where(lane == 2, carry, jnp.where(lane == 3, total, 0.0))))
    tab = jnp.concatenate([cols, jnp.zeros((LANES - N_EXPERTS, LANES), F32)], axis=0).T
    tab_ref[...] = tab[0:8].astype(jnp.int32)
    new_carry = carry_ref[...] + n_chunks
    carry_ref[...] = new_carry
    cnt_ref[...] = new_carry


def _mix_out(x2d, of2d, ob2d, proj, yft2d, mod3, mod_row, lw, seq_len, tm):
    t = x2d.shape[0]
    tm = min(tm, t)
    nt8 = t // HALO
    rows8 = tm // HALO

    def col(cb):
        return pl.BlockSpec((tm, HEAD_W), lambda i: (i, cb))

    def halo_prev(cb):
        return pl.BlockSpec((HALO, HEAD_W), lambda i: (jnp.maximum(i * rows8 - 1, 0), cb))

    def halo_next(cb):
        return pl.BlockSpec((HALO, HEAD_W), lambda i: (jnp.minimum((i + 1) * rows8, nt8 - 1), cb))

    def full(a):
        return pl.BlockSpec(a.shape, lambda i: (0,) * a.ndim)

    tok_d = pl.BlockSpec((tm, D_MODEL), lambda i: (i, 0))
    tok_h = pl.BlockSpec((tm, HEAD_W), lambda i: (i, 0))
    weights = [lw["gla_norm_g"], lw["sgu_norm_g"], lw["sgu_norm_b"], lw["w_sgu_cat"], lw["b_sgu_full"],
               lw["w_conv"], lw["w_out"], lw["ln1_g"], lw["ln1_b"], lw["w_router_hi"], lw["w_router_lo"], lw["router_bias"]]
    weights += list(_mix_masks(tm))
    in_specs = ([tok_d, tok_h, tok_h, col(COL_G), col(COL_SGU), col(COL_SGV), col(COL_CVB), col(COL_CVC),
                 col(COL_CVX), halo_prev(COL_CVC), halo_prev(COL_CVX), halo_next(COL_CVC), halo_next(COL_CVX),
                 tok_h, pl.BlockSpec((None, 1, 6 * D_MODEL), lambda i: (mod_row(i, tm), 0, 0))]
                + [full(w) for w in weights])
    args = [x2d, of2d, ob2d] + [proj] * 10 + [yft2d, mod3] + weights
    return pl.pallas_call(
        functools.partial(_mix_out_kernel, seq_len=seq_len),
        grid=(t // tm,),
        in_specs=in_specs,
        out_specs=[tok_d,
                   pl.BlockSpec((tm, PACK_W), lambda i: (i, 0)),
                   pl.BlockSpec((None, TOP_K, tm), lambda i: (i, 0, 0)),
                   pl.BlockSpec((None, TOP_K, tm), lambda i: (i, 0, 0)),
                   pl.BlockSpec((None, 8, LANES), lambda i: (i, 0, 0)),
                   pl.BlockSpec((N_EXPERTS, LANES), lambda i: (0, 0))],
        out_shape=[jax.ShapeDtypeStruct((t, D_MODEL), F32),
                   jax.ShapeDtypeStruct((t, PACK_W), jnp.uint32),
                   jax.ShapeDtypeStruct((t // tm, TOP_K, tm), F32),
                   jax.ShapeDtypeStruct((t // tm, TOP_K, tm), jnp.int32),
                   jax.ShapeDtypeStruct((t // tm, 8, LANES), jnp.int32),
                   jax.ShapeDtypeStruct((N_EXPERTS, LANES), F32)],
        scratch_shapes=[pltpu.VMEM((N_EXPERTS, LANES), F32)],
        compiler_params=_cparams("arbitrary"),
        name="mix_out",
    )(*args)


def _local_rows(tm):
    need = tm * TOP_K + N_EXPERTS * (CHUNK - 1)
    return -(-need // LBLK) * LBLK


def _run_copy(local_ref, sorted_hbm, sem, local_row, sorted_row, to_sorted, rows=CHUNK):
    loc = local_ref.at[pl.ds(pl.multiple_of(local_row, CHUNK), rows)]
    srt = sorted_hbm.at[pl.ds(pl.multiple_of(sorted_row, CHUNK), rows)]
    return pltpu.make_async_copy(loc, srt, sem) if to_sorted else pltpu.make_async_copy(srt, loc, sem)


def _start_run_copies(tab_ref, gstart_ref, local_ref, sorted_hbm, sem, to_sorted):
    def per_expert(e, totals):
        n = tab_ref[0, e]
        l0 = tab_ref[1, e]
        g0 = (gstart_ref[e] + tab_ref[2, e]) * CHUNK
        n_big = n >> BIG_SHIFT
        n_small = n & ((1 << BIG_SHIFT) - 1)

        def big(j, c):
            _run_copy(local_ref, sorted_hbm, sem, l0 + j * BIG_ROWS, g0 + j * BIG_ROWS, to_sorted, BIG_ROWS).start()
            return c

        lax.fori_loop(0, n_big, big, 0)
        l1 = l0 + n_big * BIG_ROWS
        g1 = g0 + n_big * BIG_ROWS

        for j in range((1 << BIG_SHIFT) - 1):
            @pl.when(n_small > j)
            def _():
                _run_copy(local_ref, sorted_hbm, sem, l1 + j * CHUNK, g1 + j * CHUNK, to_sorted).start()
        return totals[0] + n_big, totals[1] + n_small

    return lax.fori_loop(0, N_EXPERTS, per_expert, (jnp.int32(0), jnp.int32(0)), unroll=2)


def _get_pending(pending_ref, s):
    return pending_ref[s, 0], pending_ref[s, 1]


def _set_pending(pending_ref, s, counts):
    pending_ref[s, 0] = counts[0]
    pending_ref[s, 1] = counts[1]


def _wait_run_copies(counts, local_ref, sorted_hbm, sem, to_sorted):
    batch = 8

    def wait_n(rows, reps):
        def body(j, c):
            for _ in range(reps):
                _run_copy(local_ref, sorted_hbm, sem, 0, 0, to_sorted, rows).wait()
            return c
        return body

    for count, rows in zip(counts, (BIG_ROWS, CHUNK)):
        lax.fori_loop(0, count >> 3, wait_n(rows, batch), 0)
        lax.fori_loop(0, count & (batch - 1), wait_n(rows, 1), 0)


def _moe_scatter_kernel(gstart_ref, gend_ref, gfill_ref, xw_ref, lpos_ref, tab_ref, *refs, has_shared):
    sorted_hbm, local_ref, sem, zsem, pending_ref = refs[1:] if has_shared else refs
    i = pl.program_id(0)
    n = pl.num_programs(0)
    slot = i & 1
    tm = xw_ref.shape[0]
    lrows = local_ref.shape[1]

    def zero_copy(e, j):
        off = pl.multiple_of(gend_ref[e] - (j + 1) * LBLK, LBLK)
        return pltpu.make_async_copy(local_ref.at[1, pl.ds(0, LBLK)], sorted_hbm.at[pl.ds(off, LBLK)], zsem)

    @pl.when(i == 0)
    def _():
        _set_pending(pending_ref, 0, (0, 0))
        _set_pending(pending_ref, 1, (0, 0))
        local_ref[1, 0:LBLK, :] = jnp.zeros((LBLK, PACK_W), jnp.uint32)

        def z_start(e, c):
            lax.fori_loop(0, gfill_ref[e], lambda j, c2: (zero_copy(e, j).start(), c2)[1], 0)
            return c

        def z_wait(e, c):
            lax.fori_loop(0, gfill_ref[e], lambda j, c2: (zero_copy(e, j).wait(), c2)[1], 0)
            return c

        lax.fori_loop(0, N_EXPERTS, z_start, 0)
        lax.fori_loop(0, N_EXPERTS, z_wait, 0)

    local = local_ref.at[slot]
    _wait_run_copies(_get_pending(pending_ref, slot), local, sorted_hbm, sem.at[slot], True)

    x = _unpack_bf16_pairs(xw_ref[...]).astype(BF16)
    lpos = lpos_ref[...].astype(jnp.int16)
    one = jnp.ones((LBLK, tm), BF16)
    used_rows = tab_ref[3, 0] * CHUNK

    def sort_block(b):
        riota = lax.broadcasted_iota(jnp.int16, (LBLK, tm), 0) + b * LBLK
        p = jnp.zeros((LBLK, tm), BF16)
        for k in range(TOP_K):
            p = jnp.where(riota == lpos[k:k + 1, :], one, p)
        local[b * LBLK:(b + 1) * LBLK, :] = _pack_exact_bf16_pairs(_dot(p, x))

    n_blocks = lrows // LBLK
    for b in range(n_blocks - 1):
        sort_block(b)
    pl.when(used_rows > (n_blocks - 1) * LBLK)(functools.partial(sort_block, n_blocks - 1))

    _set_pending(pending_ref, slot, _start_run_copies(tab_ref, gstart_ref, local, sorted_hbm, sem.at[slot], True))

    @pl.when(i == n - 1)
    def _():
        for s in range(2):
            _wait_run_copies(_get_pending(pending_ref, s), local_ref.at[s], sorted_hbm, sem.at[s], True)


def _moe_scatter(gstart, gend, gfill, xw, lpos, tab, n_rows, tm, shared=None):
    n_tiles = xw.shape[0] // tm
    lrows = _local_rows(tm)
    any_spec = pl.BlockSpec(memory_space=pl.ANY)
    in_specs = [pl.BlockSpec((tm, PACK_W), lambda i, *_: (i, 0)),
                pl.BlockSpec((None, TOP_K, tm), lambda i, *_: (i, 0, 0)),
                pl.BlockSpec((None, 8, LANES), lambda i, *_: (i, 0, 0), memory_space=pltpu.SMEM)]
    args = [gstart, gend, gfill, xw, lpos, tab]
    aliases = {}
    if shared is not None:
        in_specs.append(any_spec)
        aliases = {len(args): 0}
        args.append(shared)
    return pl.pallas_call(
        functools.partial(_moe_scatter_kernel, has_shared=shared is not None),
        grid_spec=pltpu.PrefetchScalarGridSpec(
            num_scalar_prefetch=3,
            grid=(n_tiles,),
            in_specs=in_specs,
            out_specs=any_spec,
            scratch_shapes=[pltpu.VMEM((2, lrows, PACK_W), jnp.uint32), pltpu.SemaphoreType.DMA((2,)),
                            pltpu.SemaphoreType.DMA, pltpu.SMEM((2, 2), jnp.int32)]),
        out_shape=jax.ShapeDtypeStruct((n_rows, PACK_W), jnp.uint32),
        input_output_aliases=aliases,
        compiler_params=_cparams("arbitrary"),
        name="moe_scatter",
    )(*args)


def _expert2_kernel(be_ref, nu_ref, xs_ref, wg_ref, wu_ref, wd_ref, ys_ref, wgu_b, wd_b):
    j = pl.program_id(0)

    @pl.when(jnp.logical_or(j == 0, be_ref[j] != be_ref[jnp.maximum(j - 1, 0)]))
    def _():
        wgu_b[:, :EXPERT_FF] = wg_ref[...].astype(BF16)
        wgu_b[:, EXPERT_FF:] = wu_ref[...].astype(BF16)
        wd_b[...] = wd_ref[...].astype(BF16)

    @pl.when(j < nu_ref[0])
    def _():
        x = _unpack_bf16_pairs(xs_ref[...]).astype(BF16)
        gu = _dot(x, wgu_b[...])
        a = _silu(gu[:, :EXPERT_FF]) * gu[:, EXPERT_FF:]
        ys_ref[...] = _pack_bf16_pairs(_dot(a.astype(BF16), wd_b[...]))


def _experts2(xs, block_expert, n_used, lw, layer, eblk):
    n_rows = xs.shape[0]
    ff = EXPERT_FF

    def blk(j, be, nu):
        return (jnp.minimum(j, jnp.maximum(nu[0] - 1, 0)), 0)

    def wblk(j, be, nu):
        return (layer, be[j], 0, 0)

    return pl.pallas_call(
        _expert2_kernel,
        grid_spec=pltpu.PrefetchScalarGridSpec(
            num_scalar_prefetch=2,
            grid=(n_used[0],),
            in_specs=[pl.BlockSpec((eblk, PACK_W), blk),
                      pl.BlockSpec((None, None, D_MODEL, ff), wblk),
                      pl.BlockSpec((None, None, D_MODEL, ff), wblk),
                      pl.BlockSpec((None, None, ff, D_MODEL), wblk)],
            out_specs=pl.BlockSpec((eblk, PACK_W), blk),
            scratch_shapes=[pltpu.VMEM((D_MODEL, 2 * ff), BF16), pltpu.VMEM((ff, D_MODEL), BF16)]),
        out_shape=jax.ShapeDtypeStruct((n_rows, PACK_W), jnp.uint32),
        compiler_params=_cparams("arbitrary"),
        name="moe_experts",
    )(block_expert, n_used, xs, lw["w_exp_gate"], lw["w_exp_up"], lw["w_exp_down"])


def _moe_combine_kernel(gstart_ref, x1_ref, xw_ref, lpos_ref, wrow_ref, tab_ref, tab_next_ref, mod_ref,
                        sg_ref, su_ref, sd_ref, ln2g_ref, ln2b_ref, sorted_hbm, o_ref,
                        local_ref, sem, pending_ref):
    i = pl.program_id(0)
    n = pl.num_programs(0)
    slot = i & 1
    tm = x1_ref.shape[0]
    lrows = local_ref.shape[1]

    @pl.when(i == 0)
    def _():
        local_ref[...] = jnp.zeros_like(local_ref)
        _set_pending(pending_ref, 0,
                     _start_run_copies(tab_ref, gstart_ref, local_ref.at[0], sorted_hbm, sem.at[0], False))

    @pl.when(i + 1 < n)
    def _():
        nxt = 1 - slot
        _set_pending(pending_ref, nxt, _start_run_copies(tab_next_ref, gstart_ref, local_ref.at[nxt], sorted_hbm,
                                                         sem.at[nxt], False))

    h = _unpack_bf16_pairs(xw_ref[...]).astype(BF16)
    a = _silu(_dot(h, sg_ref[...])) * _dot(h, su_ref[...])
    acc = _dot(a.astype(BF16), sd_ref[...])

    local = local_ref.at[slot]
    _wait_run_copies(_get_pending(pending_ref, slot), local, sorted_hbm, sem.at[slot], False)

    lpos = lpos_ref[...].astype(jnp.int16)
    wrow = wrow_ref[...].astype(BF16)

    def unsort_block(b, acc):
        riota = lax.broadcasted_iota(jnp.int16, (LBLK, tm), 0) + b * LBLK
        q = jnp.zeros((LBLK, tm), BF16)
        for k in range(TOP_K):
            q = jnp.where(riota == lpos[k:k + 1, :], jnp.broadcast_to(wrow[k:k + 1, :], (LBLK, tm)), q)
        y = _unpack_bf16_pairs(local[b * LBLK:(b + 1) * LBLK, :]).astype(BF16)
        return acc + _dot_tn(q, y)

    for b in range(lrows // LBLK):
        acc = unsort_block(b, acc)

    g2 = mod_ref[...][:, 5 * D_MODEL:6 * D_MODEL]
    u = DEEPNORM_ALPHA * x1_ref[...] + g2 * acc
    o_ref[...] = _ln(u) * ln2g_ref[...] + ln2b_ref[...]


def _moe_combine(gstart, x1, xw, lpos, wrow, tab, ys, mod3, mod_row, lw, tm):
    t = x1.shape[0]
    n_tiles = t // tm
    lrows = _local_rows(tm)

    def full(a):
        return pl.BlockSpec(a.shape, lambda i, *_: (0,) * a.ndim)

    tab_blk = lambda f: pl.BlockSpec((None, 8, LANES), f, memory_space=pltpu.SMEM)
    return pl.pallas_call(
        _moe_combine_kernel,
        grid_spec=pltpu.PrefetchScalarGridSpec(
            num_scalar_prefetch=1,
            grid=(n_tiles,),
            in_specs=[pl.BlockSpec((tm, D_MODEL), lambda i, *_: (i, 0)),
                      pl.BlockSpec((tm, PACK_W), lambda i, *_: (i, 0)),
                      pl.BlockSpec((None, TOP_K, tm), lambda i, *_: (i, 0, 0)),
                      pl.BlockSpec((None, TOP_K, tm), lambda i, *_: (i, 0, 0)),
                      tab_blk(lambda i, *_: (i, 0, 0)),
                      tab_blk(lambda i, *_: (jnp.minimum(i + 1, n_tiles - 1), 0, 0)),
                      pl.BlockSpec((None, 1, 6 * D_MODEL), lambda i, *_: (mod_row(i, tm), 0, 0)),
                      full(lw["w_sh_gate"]), full(lw["w_sh_up"]), full(lw["w_sh_down"]),
                      full(lw["ln2_g"]), full(lw["ln2_b"]),
                      pl.BlockSpec(memory_space=pl.ANY)],
            out_specs=pl.BlockSpec((tm, D_MODEL), lambda i, *_: (i, 0)),
            scratch_shapes=[pltpu.VMEM((2, lrows, PACK_W), jnp.uint32), pltpu.SemaphoreType.DMA((2,)),
                            pltpu.SMEM((2, 2), jnp.int32)]),
        out_shape=jax.ShapeDtypeStruct((t, D_MODEL), F32),
        compiler_params=_cparams("arbitrary"),
        name="moe_combine",
    )(gstart, x1, xw, lpos, wrow, tab, tab, mod3, lw["w_sh_gate"], lw["w_sh_up"], lw["w_sh_down"],
      lw["ln2_g"], lw["ln2_b"], ys)


def _moe_groups(groups, lw, layer):
    tm = MOE_TM
    chunks = [g["chunks"][:, 0].astype(jnp.int32) for g in groups]
    t_all = sum(g["x1"].shape[0] for g in groups)
    n_tiles = t_all // tm
    eblk = max(LBLK, min(EBLK_MAX, t_all * TOP_K // N_EXPERTS))
    region_rows = sum(chunks) * CHUNK
    padded = ((region_rows + eblk - 1) // eblk) * eblk
    end = jnp.cumsum(padded)
    start = end - padded
    n_blocks = -(-(t_all * TOP_K + n_tiles * N_EXPERTS * (CHUNK - 1)) // eblk) + N_EXPERTS
    n_used = end[-1] // eblk
    blk_row = jnp.minimum(jnp.arange(n_blocks, dtype=jnp.int32), jnp.maximum(n_used - 1, 0)) * eblk
    block_expert = jnp.minimum(jnp.sum((end[None, :] <= blk_row[:, None]).astype(jnp.int32), axis=1),
                               N_EXPERTS - 1).astype(jnp.int32)
    gfill = (padded - region_rows + LBLK - 1) // LBLK
    gstarts = []
    before = jnp.zeros_like(chunks[0])
    for ch in chunks:
        gstarts.append(start // CHUNK + before)
        before = before + ch
    xs = None
    for g, gstart in zip(groups, gstarts):
        fill = gfill if xs is None else jnp.zeros_like(gfill)
        xs = _moe_scatter(gstart, end, fill, g["xw"], g["lpos"], g["tab"], n_blocks * eblk, tm, xs)
    ys = _experts2(xs, block_expert, n_used.reshape(1), lw, layer, eblk)
    return [_moe_combine(gstart, g["x1"], g["xw"], g["lpos"], g["wrow"], g["tab"], ys, g["mod3"], g["mod_row"],
                         lw, tm) for g, gstart in zip(groups, gstarts)]


def _channel_dft_table():
    k = np.arange(FNET_CH, dtype=np.float64)
    ang = 2.0 * np.pi * np.outer(k, k) / FNET_CH
    eye = np.eye(FNET_GROUPS)
    return np.concatenate([np.kron(eye, np.cos(ang)), -np.kron(eye, np.sin(ang))], axis=1)


def _direct_dft_tables(n):
    k = np.arange(n, dtype=np.float64)
    ang = 2.0 * np.pi * (np.outer(k, k) % n) / n
    scale = 1.0 / math.sqrt(n * FNET_CH)
    return np.cos(ang) * scale, np.sin(ang) * scale


def _two_stage_dft_tables(n):
    n1 = FFT_N1
    n2 = n // n1
    a = np.arange(n1, dtype=np.float64)
    ang1 = 2.0 * np.pi * (np.outer(a, a) % n1) / n1
    k1 = np.arange(n1).reshape(n1, 1, 1)
    k2 = np.arange(n2).reshape(1, n2, 1)
    m2 = np.arange(n2).reshape(1, 1, n2)
    ang2 = 2.0 * np.pi * ((m2 * (k1 + n1 * k2)) % n) / n
    scale = 1.0 / math.sqrt(n * FNET_CH)
    return np.cos(ang1), np.sin(ang1), np.cos(ang2) * scale, np.sin(ang2) * scale


def _grid_sincos_table(rows, d):
    quarter = d // 4
    omega = 1.0 / (POS_BASE ** (np.arange(quarter, dtype=np.float64) / quarter))
    r = np.arange(rows, dtype=np.float64)[:, None] * omega
    c = np.arange(GRID_W, dtype=np.float64)[:, None] * omega
    return (np.concatenate([np.sin(r), np.cos(r)], axis=-1).astype(np.float32),
            np.concatenate([np.sin(c), np.cos(c)], axis=-1).astype(np.float32))


def _layer_weights(l, w_in, w_gla_a, b_gla_a, gla_norm_g, sgu_norm_g, sgu_norm_b, w_sgu, b_sgu, w_conv,
                   w_out, ln1_g, ln1_b, ln2_g, ln2_b, w_router, router_bias,
                   w_exp_gate, w_exp_up, w_exp_down, w_sh_gate, w_sh_up, w_sh_down):
    wi = w_in[l]
    lr0 = 4 * HEAD_W
    w_in_p = jnp.concatenate(
        [wi[:, :lr0], wi[:, lr0 + 2 * GLA_LR:lr0 + 2 * GLA_LR + 5 * HEAD_W], wi[:, lr0:lr0 + 2 * GLA_LR],
         jnp.zeros((D_MODEL, LR_W - 2 * GLA_LR), F32), wi[:, lr0 + 2 * GLA_LR + 5 * HEAD_W:]], axis=1).astype(BF16)
    wa_pad = jnp.zeros((LR_W, 2 * HEAD_W), F32)
    wa_pad = wa_pad.at[:GLA_LR, :HEAD_W].set(w_gla_a[l, 0])
    wa_pad = wa_pad.at[GLA_LR:2 * GLA_LR, HEAD_W:].set(w_gla_a[l, 1])
    w_router_pad = jnp.concatenate([w_router[l], jnp.zeros((D_MODEL, LANES - N_EXPERTS), F32)], axis=1)
    w_router_hi = w_router_pad.astype(BF16)
    row = lambda a: a[l].reshape(1, -1)
    return {
        "w_in_p": w_in_p,
        "wa_hi": wa_pad.astype(BF16), "wa_lo": (wa_pad - wa_pad.astype(BF16).astype(F32)).astype(BF16),
        "ba": jnp.concatenate([b_gla_a[l, 0], b_gla_a[l, 1]]).reshape(1, 2 * HEAD_W),
        "gla_norm_g": row(gla_norm_g), "sgu_norm_g": row(sgu_norm_g), "sgu_norm_b": row(sgu_norm_b),
        "w_sgu_cat": jnp.concatenate([w_sgu[l, g] for g in range(GMLP_GROUPS)], axis=1).astype(BF16),
        "b_sgu_full": jnp.repeat(b_sgu[l].T, HEAD_W // GMLP_GROUPS, axis=1),
        "w_conv": w_conv[l],
        "w_out": w_out[l].astype(BF16),
        "ln1_g": row(ln1_g), "ln1_b": row(ln1_b), "ln2_g": row(ln2_g), "ln2_b": row(ln2_b),
        "w_router_hi": w_router_hi, "w_router_lo": (w_router_pad - w_router_hi.astype(F32)).astype(BF16),
        "router_bias": router_bias[l].reshape(N_EXPERTS, 1),
        "w_exp_gate": w_exp_gate, "w_exp_up": w_exp_up, "w_exp_down": w_exp_down,
        "w_sh_gate": w_sh_gate[l].astype(BF16), "w_sh_up": w_sh_up[l].astype(BF16),
        "w_sh_down": w_sh_down[l].astype(BF16),
    }


def _state_to_blockdiag_t(s):
    bsz = s.shape[0]
    st = jnp.swapaxes(s, 2, 3)
    eye = jnp.eye(GLA_HEADS, dtype=s.dtype)
    return jnp.einsum("bhvd,hg->bhvgd", st, eye).reshape(bsz, HEAD_W, HEAD_W)


def _blockdiag_t_to_state(st):
    bsz = st.shape[0]
    s5 = st.reshape(bsz, GLA_HEADS, GLA_DK, GLA_HEADS, GLA_DK)
    diag = jnp.stack([s5[:, h, :, h, :] for h in range(GLA_HEADS)], axis=1)
    return jnp.swapaxes(diag, 2, 3)


def _pre_moe(x3, pos, mod3, mod_row, lw, st0, emit_final, tabs):
    bsz, n, _ = x3.shape
    t = bsz * n
    outs = _in_proj(x3.reshape(t, D_MODEL), pos, mod3, mod_row, lw["w_in_p"], tabs["cs"], tm=512)
    if pos is not None:
        proj, zr, zi, x2d = outs
    else:
        proj, zr, zi = outs
        x2d = x3.reshape(t, D_MODEL)
    gla_out = _gla(proj.reshape(bsz, n, STORE_W), lw["wa_hi"], lw["wa_lo"], lw["ba"], st0, emit_final)
    o_f, o_b = gla_out[:2]
    zr3 = zr.reshape(bsz, n, HEAD_W)
    zi3 = zi.reshape(bsz, n, HEAD_W)
    if "two_stage" in tabs:
        yft = _fft_two_stage(zr3, zi3, tabs["two_stage"])
    else:
        yft = _fft_direct(zr3, zi3, *tabs["direct"])
    x1, xw, wrow, lpos, tab, chunks = _mix_out(x2d, o_f.reshape(t, HEAD_W), o_b.reshape(t, HEAD_W), proj,
                                             yft.reshape(t, HEAD_W), mod3, mod_row, lw, seq_len=n, tm=MOE_TM)
    group = {"x1": x1, "xw": xw, "wrow": wrow, "lpos": lpos, "tab": tab, "chunks": chunks,
             "mod3": mod3, "mod_row": mod_row, "shape": (bsz, n, D_MODEL)}
    return group, gla_out[2:]


def _trunk_layer(x3, pos, mod3, mod_row, lw, layer, st0, emit_final, tabs):
    group, finals = _pre_moe(x3, pos, mod3, mod_row, lw, st0, emit_final, tabs)
    (x2,) = _moe_groups([group], lw, layer)
    return x2.reshape(group["shape"]), finals


def kernel(x_prompt, x_sample, c, state_gla, c_ctx, w_ada, b_ada, w_in, w_gla_a, b_gla_a, gla_norm_g, sgu_norm_g, sgu_norm_b, w_sgu, b_sgu, w_conv, w_out, ln1_g, ln1_b, ln2_g, ln2_b, w_router, router_bias, w_exp_gate, w_exp_up, w_exp_down, w_sh_gate, w_sh_up, w_sh_down):
    n_layers = w_ada.shape[0]
    bp, np_, _ = x_prompt.shape
    bs, ns, _ = x_sample.shape
    assert bs <= 7

    cond8 = jnp.concatenate([c_ctx[None, :], c, jnp.zeros((7 - bs, D_MODEL), F32)], axis=0)
    mod = _ada_mod(cond8, w_ada, b_ada)

    tabs_p = {"cs": jnp.asarray(_channel_dft_table(), BF16),
              "direct": tuple(jnp.asarray(a, BF16) for a in _direct_dft_tables(np_))}
    tabs_s = {"cs": tabs_p["cs"],
              "two_stage": tuple(jnp.asarray(a, BF16) for a in _two_stage_dft_tables(ns))}
    rtab, ctab = _grid_sincos_table(ns // GRID_W, D_MODEL)
    pos = jnp.concatenate([jnp.repeat(jnp.asarray(rtab), GRID_W, axis=0),
                           jnp.tile(jnp.asarray(ctab), (ns // GRID_W, 1))], axis=-1)

    prompt_row = lambda i, tm: 0
    sample_row = lambda i, tm: 1 + (i * tm) // ns

    y_p = x_prompt
    y_s = x_sample
    finals = []
    for l in range(n_layers):
        lw = _layer_weights(l, w_in, w_gla_a, b_gla_a, gla_norm_g, sgu_norm_g, sgu_norm_b, w_sgu, b_sgu,
                            w_conv, w_out, ln1_g, ln1_b, ln2_g, ln2_b, w_router, router_bias,
                            w_exp_gate, w_exp_up, w_exp_down, w_sh_gate, w_sh_up, w_sh_down)
        mod3 = mod[l].reshape(8, 1, 6 * D_MODEL)
        group_p, fin = _pre_moe(y_p, None, mod3, prompt_row, lw, None, True, tabs_p)
        finals.append(jnp.stack([_blockdiag_t_to_state(fin[0]), _blockdiag_t_to_state(fin[1])], axis=1))
        st0 = jnp.stack([_state_to_blockdiag_t(state_gla[:, l, 0]), _state_to_blockdiag_t(state_gla[:, l, 1])])
        group_s, _ = _pre_moe(y_s, pos if l == 0 else None, mod3, sample_row, lw, st0, False, tabs_s)
        y_p, y_s = _moe_groups([group_p, group_s], lw, l)
        y_p = y_p.reshape(group_p["shape"])
        y_s = y_s.reshape(group_s["shape"])
    new_state = jnp.stack(finals, axis=1).astype(x_prompt.dtype)
    return (y_p, y_s, new_state)
```

```python
import functools
import math

import numpy as np
import jax
import jax.numpy as jnp
from jax import lax
from jax.experimental import pallas as pl
from jax.experimental.pallas import tpu as pltpu

F32 = jnp.float32
BF16 = jnp.bfloat16

D_MODEL = 1024
DEPTH = 2
GRID_W = 64
HEAD_W = 256
GLA_HEADS = 4
GLA_DK = 64
GLA_LR = 16
GLA_TAU = 16.0
GLA_CHUNK = 64
GMLP_GROUPS = 4
GMLP_CHUNK = 128
FNET_GROUPS = 4
FNET_CH = 64
N_EXPERTS = 64
TOP_K = 8
N_GROUPS = 8
TOPK_GROUPS = 4
EXPERT_FF = 256
ROUTED_SCALE = 2.5
DEEPNORM_ALPHA = (2 * DEPTH) ** 0.25
LN_EPS = 1e-5
RMS_EPS = 1e-6
POS_BASE = 10000.0

COL_Q, COL_K, COL_V, COL_G, COL_SGU, COL_SGV, COL_CVB, COL_CVC, COL_CVX = range(9)
LR_W = 128
STORE_W = 9 * HEAD_W + LR_W
PROJ_W = STORE_W + HEAD_W
COL_LR = (9 * HEAD_W) // LR_W
HALO = 16

LANES = 128
PACK_W = D_MODEL // 2
EBLK_MAX = 1024
LBLK = 256
MOE_TM = 256
CHUNK = 8
BIG_SHIFT = 2
BIG_ROWS = CHUNK << BIG_SHIFT
SEG = 256
FFT_N1 = 64
VMEM_LIMIT = 56 * 1024 * 1024


def _cparams(*sem):
    return pltpu.CompilerParams(dimension_semantics=sem, vmem_limit_bytes=VMEM_LIMIT)


def _ln(x):
    mu = jnp.mean(x, axis=-1, keepdims=True)
    xc = x - mu
    var = jnp.mean(xc * xc, axis=-1, keepdims=True)
    return xc * lax.rsqrt(var + LN_EPS)


def _sigmoid(x):
    return 1.0 / (1.0 + jnp.exp(-x))


def _silu(x):
    return x * _sigmoid(x)


def _pack_bf16_pairs(x):
    w = x.shape[1] // 2
    lo = lax.bitcast_convert_type(x[:, :w].astype(BF16).astype(F32), jnp.uint32)
    hi = lax.bitcast_convert_type(x[:, w:].astype(BF16).astype(F32), jnp.uint32)
    return (lo >> 16) | (hi & jnp.uint32(0xFFFF0000))


def _pack_exact_bf16_pairs(x):
    w = x.shape[1] // 2
    lo = lax.bitcast_convert_type(x[:, :w], jnp.uint32)
    hi = lax.bitcast_convert_type(x[:, w:], jnp.uint32)
    return (lo >> 16) | hi


def _unpack_bf16_pairs(u):
    lo = lax.bitcast_convert_type(u << 16, F32)
    hi = lax.bitcast_convert_type(u & jnp.uint32(0xFFFF0000), F32)
    return jnp.concatenate([lo, hi], axis=1)


def _dot(a, b):
    return jnp.dot(a, b, preferred_element_type=F32)


def _dot_nt(a, b):
    return lax.dot_general(a, b, (((1,), (1,)), ((), ())), preferred_element_type=F32)


def _dot_tn(a, b):
    return lax.dot_general(a, b, (((0,), (0,)), ((), ())), preferred_element_type=F32)


def _ada_kernel(c_ref, w_ref, b_ref, o_ref):
    c = c_ref[...]
    o_ref[...] = _dot(_silu(c).astype(BF16), w_ref[...].astype(BF16)) + b_ref[...]


def _ada_mod(cond8, w_ada, b_ada):
    n_l, d, w6 = w_ada.shape
    tn = 1536
    return pl.pallas_call(
        _ada_kernel,
        grid=(n_l, w6 // tn),
        in_specs=[pl.BlockSpec((8, d), lambda l, j: (0, 0)),
                  pl.BlockSpec((None, d, tn), lambda l, j: (l, 0, j)),
                  pl.BlockSpec((None, 1, tn), lambda l, j: (l, 0, j))],
        out_specs=pl.BlockSpec((None, 8, tn), lambda l, j: (l, 0, j)),
        out_shape=jax.ShapeDtypeStruct((n_l, 8, w6), F32),
        compiler_params=_cparams("parallel", "parallel"),
        name="ada_mod",
    )(cond8, w_ada, b_ada.reshape(n_l, 1, w6))


def _in_proj_kernel(*refs, has_pos):
    if has_pos:
        x_ref, pos_ref, mod_ref, w_ref, cs_ref, proj_ref, zr_ref, zi_ref, x0_ref = refs
        x = x_ref[...] + pos_ref[...]
        x0_ref[...] = x
    else:
        x_ref, mod_ref, w_ref, cs_ref, proj_ref, zr_ref, zi_ref = refs
        x = x_ref[...]
    mod = mod_ref[...]
    sh1 = mod[:, 0:D_MODEL]
    sc1 = mod[:, D_MODEL:2 * D_MODEL]
    h = _ln(x) * (1.0 + sc1) + sh1
    proj = _dot(h.astype(BF16), w_ref[...])
    proj_ref[...] = proj[:, :STORE_W].astype(BF16)
    ft = proj[:, STORE_W:].astype(BF16)
    z = _dot(ft, cs_ref[...])
    zr_ref[...] = z[:, :HEAD_W].astype(BF16)
    zi_ref[...] = z[:, HEAD_W:].astype(BF16)


def _in_proj(x2d, pos, mod3, mod_row, w_in_p, cs, tm):
    t = x2d.shape[0]
    tm = min(tm, t)
    in_specs = [pl.BlockSpec((tm, D_MODEL), lambda i: (i, 0))]
    args = [x2d]
    out_shape = [jax.ShapeDtypeStruct((t, STORE_W), BF16),
                 jax.ShapeDtypeStruct((t, HEAD_W), BF16),
                 jax.ShapeDtypeStruct((t, HEAD_W), BF16)]
    out_specs = [pl.BlockSpec((tm, STORE_W), lambda i: (i, 0)),
                 pl.BlockSpec((tm, HEAD_W), lambda i: (i, 0)),
                 pl.BlockSpec((tm, HEAD_W), lambda i: (i, 0))]
    if pos is not None:
        n_pos = pos.shape[0] // tm
        in_specs.append(pl.BlockSpec((tm, D_MODEL), lambda i: (i % n_pos, 0)))
        args.append(pos)
        out_shape.append(jax.ShapeDtypeStruct((t, D_MODEL), F32))
        out_specs.append(pl.BlockSpec((tm, D_MODEL), lambda i: (i, 0)))
    in_specs += [pl.BlockSpec((None, 1, 6 * D_MODEL), lambda i: (mod_row(i, tm), 0, 0)),
                 pl.BlockSpec((D_MODEL, PROJ_W), lambda i: (0, 0)),
                 pl.BlockSpec((HEAD_W, 2 * HEAD_W), lambda i: (0, 0))]
    args += [mod3, w_in_p, cs]
    return pl.pallas_call(
        functools.partial(_in_proj_kernel, has_pos=pos is not None),
        grid=(t // tm,),
        in_specs=in_specs,
        out_specs=out_specs,
        out_shape=out_shape,
        compiler_params=_cparams("parallel"),
        name="in_proj",
    )(*args)


def _gla_masks(reverse):
    i = np.arange(SEG)
    same_chunk = (i[:, None] // GLA_CHUNK) == (i[None, :] // GLA_CHUNK)
    tri = same_chunk & ((i[None, :] >= i[:, None]) if reverse else (i[None, :] <= i[:, None]))
    l_idx = i[:, None] % GLA_CHUNK
    m_idx = np.arange(GLA_CHUNK)[None, :]
    causal = (m_idx >= l_idx) if reverse else (m_idx <= l_idx)
    return (jnp.asarray(tri, BF16), jnp.asarray(same_chunk, BF16), jnp.asarray(same_chunk, F32),
            jnp.asarray(causal, F32))


def _gla_segment(q, k, v, pre, st_ref, o_ref, masks, reverse):
    seg = q.shape[0]
    n_chunks = seg // GLA_CHUNK
    la = (jnp.minimum(pre, 0.0) - jnp.log1p(jnp.exp(-jnp.abs(pre)))) * (1.0 / GLA_TAU)

    tri_m, ones_m, bd, causal = masks
    hi = la.astype(BF16)
    lo = (la - hi.astype(F32)).astype(BF16)
    b = _dot(tri_m, hi) + _dot(tri_m, lo)
    btot = _dot(ones_m, hi) + _dot(ones_m, lo)

    q_dec = q * (GLA_DK ** -0.5) * jnp.exp(b)
    k_inv = (k * jnp.exp(-b)).astype(BF16)
    k_end = (k * jnp.exp(btot - b)).astype(BF16)
    dec = jnp.exp(btot)
    vb = v.astype(BF16)
    keep = causal > 0.5

    st = st_ref[...]
    order = range(n_chunks - 1, -1, -1) if reverse else range(n_chunks)
    for ci in order:
        sl = slice(ci * GLA_CHUNK, (ci + 1) * GLA_CHUNK)
        qd = q_dec[sl]
        qbd = (jnp.concatenate([qd] * GLA_HEADS, axis=0) * bd).astype(BF16)
        a = _dot_nt(qbd, k_inv[sl])
        a = jnp.where(keep, a, 0.0)
        rr = _dot(a.astype(BF16), vb[sl])
        o = _dot_nt(qd.astype(BF16), st.astype(BF16))
        for h in range(GLA_HEADS):
            hs = slice(h * GLA_CHUNK, (h + 1) * GLA_CHUNK)
            o = o + rr[hs] * bd[hs]
        o_ref[sl, :] = o
        kvt = _dot_tn(vb[sl], k_end[sl])
        st = st * dec[ci * GLA_CHUNK:ci * GLA_CHUNK + 1, :] + kvt * bd
    st_ref[...] = st


def _gla_kernel(*refs, has_init, emit_final):
    qf, kf, vf, lrf, qb, kb, vb, lrb, wah_ref, wal_ref, ba_ref = refs[:11]
    mask_refs = refs[11:19]
    rest = refs[19:]
    if has_init:
        s0f, s0b = rest[:2]
        rest = rest[2:]
    of_ref, ob_ref = rest[:2]
    rest = rest[2:]
    if emit_final:
        sff, sfb = rest[:2]
        rest = rest[2:]
    stf, stb = rest

    s = pl.program_id(1)

    @pl.when(s == 0)
    def _():
        if has_init:
            stf[...] = s0f[...]
            stb[...] = s0b[...]
        else:
            stf[...] = jnp.zeros_like(stf)
            stb[...] = jnp.zeros_like(stb)

    def decay_pre(lr_ref):
        lr = lr_ref[...]
        return _dot(lr, wah_ref[...]) + _dot(lr, wal_ref[...]) + ba_ref[...]

    f32 = lambda ref: ref[...].astype(F32)

    masks_f = tuple(m[...] for m in mask_refs[:4])
    masks_b = tuple(m[...] for m in mask_refs[4:])
    _gla_segment(f32(qf), f32(kf), f32(vf), decay_pre(lrf)[:, :HEAD_W], stf, of_ref, masks_f, reverse=False)
    _gla_segment(f32(qb), f32(kb), f32(vb), decay_pre(lrb)[:, HEAD_W:], stb, ob_ref, masks_b, reverse=True)

    if emit_final:
        @pl.when(s == pl.num_programs(1) - 1)
        def _():
            sff[...] = stf[...]
            sfb[...] = stb[...]


def _gla(proj3, wa_hi, wa_lo, ba, st0, emit_final):
    bsz, n, _ = proj3.shape
    nseg = n // SEG

    def col(cb, width=HEAD_W, rev=False):
        if rev:
            return pl.BlockSpec((None, SEG, width), lambda b, s: (b, nseg - 1 - s, cb))
        return pl.BlockSpec((None, SEG, width), lambda b, s: (b, s, cb))

    in_specs = [col(COL_Q), col(COL_K), col(COL_V), col(COL_LR, LR_W),
                col(COL_Q, rev=True), col(COL_K, rev=True), col(COL_V, rev=True), col(COL_LR, LR_W, rev=True),
                pl.BlockSpec((LR_W, 2 * HEAD_W), lambda b, s: (0, 0)),
                pl.BlockSpec((LR_W, 2 * HEAD_W), lambda b, s: (0, 0)),
                pl.BlockSpec((1, 2 * HEAD_W), lambda b, s: (0, 0))]
    masks = _gla_masks(False) + _gla_masks(True)
    in_specs += [pl.BlockSpec(m.shape, lambda b, s: (0, 0)) for m in masks]
    args = [proj3] * 8 + [wa_hi, wa_lo, ba] + list(masks)
    st_spec = pl.BlockSpec((None, HEAD_W, HEAD_W), lambda b, s: (b, 0, 0))
    if st0 is not None:
        in_specs += [st_spec, st_spec]
        args += [st0[0], st0[1]]
    out_shape = [jax.ShapeDtypeStruct((bsz, n, HEAD_W), F32)] * 2
    out_specs = [pl.BlockSpec((None, SEG, HEAD_W), lambda b, s: (b, s, 0)),
                 pl.BlockSpec((None, SEG, HEAD_W), lambda b, s: (b, nseg - 1 - s, 0))]
    if emit_final:
        out_shape += [jax.ShapeDtypeStruct((bsz, HEAD_W, HEAD_W), F32)] * 2
        out_specs += [st_spec, st_spec]
    return pl.pallas_call(
        functools.partial(_gla_kernel, has_init=st0 is not None, emit_final=emit_final),
        grid=(bsz, nseg),
        in_specs=in_specs,
        out_specs=out_specs,
        out_shape=out_shape,
        scratch_shapes=[pltpu.VMEM((HEAD_W, HEAD_W), F32), pltpu.VMEM((HEAD_W, HEAD_W), F32)],
        compiler_params=_cparams("parallel", "arbitrary"),
        name="gla",
    )(*args)


def _fft_direct_kernel(zr_ref, zi_ref, cn_ref, sn_ref, o_ref):
    o_ref[...] = _dot(cn_ref[...], zr_ref[...]) + _dot(sn_ref[...], zi_ref[...])


def _fft_direct(zr3, zi3, cn, sn):
    bsz, n, w = zr3.shape
    blk = pl.BlockSpec((None, n, w), lambda b: (b, 0, 0))
    tab = pl.BlockSpec((n, n), lambda b: (0, 0))
    return pl.pallas_call(
        _fft_direct_kernel,
        grid=(bsz,),
        in_specs=[blk, blk, tab, tab],
        out_specs=blk,
        out_shape=jax.ShapeDtypeStruct((bsz, n, w), F32),
        compiler_params=_cparams("parallel"),
        name="fft_direct",
    )(zr3, zi3, cn, sn)


def _fft_a_kernel(zr_ref, zi_ref, c_ref, s_ref, gr_ref, gi_ref):
    zr = zr_ref[...]
    zi = zi_ref[...]
    cm = c_ref[...]
    sm = s_ref[...]
    gr_ref[...] = (_dot(cm, zr) + _dot(sm, zi)).astype(BF16)
    gi_ref[...] = (_dot(cm, zi) - _dot(sm, zr)).astype(BF16)


def _fft_c_kernel(gr_ref, gi_ref, mc_ref, ms_ref, o_ref):
    for j in range(gr_ref.shape[0]):
        o_ref[:, j, :] = _dot(mc_ref[j], gr_ref[j]) + _dot(ms_ref[j], gi_ref[j])


def _fft_two_stage(zr3, zi3, tabs):
    bsz, n, w = zr3.shape
    n1 = FFT_N1
    n2 = n // n1
    c1, s1, mc, ms = tabs
    tn = 2048
    wide = n2 * w
    blk = pl.BlockSpec((None, n1, tn), lambda b, j: (b, 0, j))
    tab = pl.BlockSpec((n1, n1), lambda b, j: (0, 0))
    gr, gi = pl.pallas_call(
        _fft_a_kernel,
        grid=(bsz, wide // tn),
        in_specs=[blk, blk, tab, tab],
        out_specs=[blk, blk],
        out_shape=[jax.ShapeDtypeStruct((bsz, n1, wide), BF16)] * 2,
        compiler_params=_cparams("parallel", "parallel"),
        name="fft_stage_a",
    )(zr3.reshape(bsz, n1, wide), zi3.reshape(bsz, n1, wide), c1, s1)
    kb = 8
    gblk = pl.BlockSpec((None, kb, n2, w), lambda b, j: (b, j, 0, 0))
    mblk = pl.BlockSpec((kb, n2, n2), lambda b, j: (j, 0, 0))
    out = pl.pallas_call(
        _fft_c_kernel,
        grid=(bsz, n1 // kb),
        in_specs=[gblk, gblk, mblk, mblk],
        out_specs=pl.BlockSpec((None, n2, kb, w), lambda b, j: (b, 0, j, 0)),
        out_shape=jax.ShapeDtypeStruct((bsz, n2, n1, w), F32),
        compiler_params=_cparams("parallel", "parallel"),
        name="fft_stage_c",
    )(gr.reshape(bsz, n1, n2, w), gi.reshape(bsz, n1, n2, w), mc, ms)
    return out.reshape(bsz, n, w)


def _mix_masks(tm):
    h = np.arange(HEAD_W) // GLA_DK
    head_mean = (h[:, None] == h[None, :]) / GLA_DK
    rg = np.arange(GMLP_GROUPS * GMLP_CHUNK) // GMLP_CHUNK
    cg = np.arange(HEAD_W) // (HEAD_W // GMLP_GROUPS)
    t = np.arange(tm)
    e = np.arange(N_EXPERTS)
    return (jnp.asarray(head_mean, BF16), jnp.asarray(rg[:, None] == cg[None, :], F32),
            jnp.asarray(t[:, None] < t[None, :], BF16), jnp.asarray(e[None, :] < e[:, None], BF16))


def _route(logits, bias, before, lower):
    tm = logits.shape[1]
    s = _sigmoid(logits)
    biased = s + bias
    neg = -jnp.inf
    rows = lax.broadcasted_iota(jnp.int32, (8, tm), 0)

    def first_argmax(x, ids, sentinel):
        m = jnp.max(x, axis=0, keepdims=True)
        return m, jnp.min(jnp.where(x == m, ids, sentinel), axis=0, keepdims=True)

    gs_rows = []
    for g in range(N_GROUPS):
        x = biased[8 * g:8 * g + 8]
        m1, i1 = first_argmax(x, rows, 8)
        m2 = jnp.max(jnp.where(rows == i1, neg, x), axis=0, keepdims=True)
        gs_rows.append(m1 + m2)
    gs = jnp.concatenate(gs_rows, axis=0)
    gsel = jnp.zeros((N_GROUPS, tm), F32)
    for _ in range(TOPK_GROUPS):
        _, i = first_argmax(gs, rows, 8)
        hit = rows == i
        gsel = jnp.where(hit, 1.0, gsel)
        gs = jnp.where(hit, neg, gs)

    xs = [jnp.where(gsel[g:g + 1] > 0.0, biased[8 * g:8 * g + 8], neg) for g in range(N_GROUPS)]
    ids = [rows + 8 * g for g in range(N_GROUPS)]
    sel = [jnp.zeros((8, tm), F32) for _ in range(N_GROUPS)]
    eids = []
    for _ in range(TOP_K):
        m = xs[0]
        for g in range(1, N_GROUPS):
            m = jnp.maximum(m, xs[g])
        m = jnp.max(m, axis=0, keepdims=True)
        cand = jnp.where(xs[0] == m, ids[0], N_EXPERTS)
        for g in range(1, N_GROUPS):
            cand = jnp.minimum(cand, jnp.where(xs[g] == m, ids[g], N_EXPERTS))
        i = jnp.min(cand, axis=0, keepdims=True)
        eids.append(i)
        for g in range(N_GROUPS):
            hit = ids[g] == i
            sel[g] = jnp.where(hit, 1.0, sel[g])
            xs[g] = jnp.where(hit, neg, xs[g])

    sel_all = jnp.concatenate(sel, axis=0)
    seen = _dot(sel_all.astype(BF16), before)
    counts = jnp.sum(sel_all, axis=1, keepdims=True)

    n_chunks = jnp.ceil(counts * (1.0 / CHUNK))
    run_start = _dot(lower, jnp.broadcast_to(n_chunks, (N_EXPERTS, LANES)).astype(BF16))[:, 0:1] * CHUNK
    local_pos = seen + run_start

    def pick(k, table):
        acc = None
        for g in range(N_GROUPS):
            v = jnp.where(ids[g] == eids[k], table[8 * g:8 * g + 8], 0.0)
            acc = v if acc is None else acc + v
        return jnp.sum(acc, axis=0, keepdims=True)

    w_raw = [pick(k, s) for k in range(TOP_K)]
    lpos = [pick(k, local_pos) for k in range(TOP_K)]
    tot = w_raw[0]
    for k in range(1, TOP_K):
        tot = tot + w_raw[k]
    weights = [w / tot * ROUTED_SCALE for w in w_raw]
    return weights, lpos, n_chunks, run_start


def _mix_out_kernel(x_ref, of_ref, ob_ref, g_ref, su_ref, sv_ref, cb_ref, cc_ref, cx_ref,
                    ccp_ref, cxp_ref, ccn_ref, cxn_ref, ft_ref, mod_ref,
                    glag_ref, sgng_ref, sgnb_ref, wsgu_ref, bsgu_ref, wconv_ref, wout_ref,
                    ln1g_ref, ln1b_ref, wrh_ref, wrl_ref, rb_ref, hmean_ref, sgubd_ref, before_ref, lower_ref,
                    x1_ref, xw_ref, wrow_ref, lpos_ref, tab_ref, cnt_ref, carry_ref, *, seq_len):
    tm = x_ref.shape[0]
    i = pl.program_id(0)

    @pl.when(i == 0)
    def _():
        carry_ref[...] = jnp.zeros_like(carry_ref)

    mod = mod_ref[...]
    g1 = mod[:, 2 * D_MODEL:3 * D_MODEL]
    sh2 = mod[:, 3 * D_MODEL:4 * D_MODEL]
    sc2 = mod[:, 4 * D_MODEL:5 * D_MODEL]

    o = of_ref[...] + ob_ref[...]
    head_mean = hmean_ref[...]
    o2 = o * o
    o2_hi = o2.astype(BF16)
    o2_lo = (o2 - o2_hi.astype(F32)).astype(BF16)
    ms = _dot(o2_hi, head_mean) + _dot(o2_lo, head_mean)
    y_gla = o * lax.rsqrt(ms + RMS_EPS) * glag_ref[...] * _silu(g_ref[...].astype(F32))

    vn = _ln(sv_ref[...].astype(F32)) * sgng_ref[...] + sgnb_ref[...]
    sgu_bd = sgubd_ref[...]
    sp_parts = []
    for j in range(tm // GMLP_CHUNK):
        vc = vn[j * GMLP_CHUNK:(j + 1) * GMLP_CHUNK]
        vbd = (jnp.concatenate([vc] * GMLP_GROUPS, axis=0) * sgu_bd).astype(BF16)
        sp_parts.append(_dot(wsgu_ref[...], vbd) + bsgu_ref[...])
    y_sgu = su_ref[...].astype(F32) * jnp.concatenate(sp_parts, axis=0)

    z = cc_ref[...].astype(F32) * cx_ref[...].astype(F32)
    z_before = (ccp_ref[...].astype(F32) * cxp_ref[...].astype(F32))[HALO - 1:HALO, :]
    z_after = (ccn_ref[...].astype(F32) * cxn_ref[...].astype(F32))[0:1, :]
    row = lax.broadcasted_iota(jnp.int32, (tm, HEAD_W), 0)
    pos = (i * tm + row) & (seq_len - 1)
    z_prev = jnp.where(row == 0, z_before, pltpu.roll(z, 1, 0))
    z_next = jnp.where(row == tm - 1, z_after, pltpu.roll(z, tm - 1, 0))
    z_prev = jnp.where(pos == 0, 0.0, z_prev)
    z_next = jnp.where(pos == seq_len - 1, 0.0, z_next)
    wconv = wconv_ref[...]
    y_conv = cb_ref[...].astype(F32) * (wconv[0:1] * z_prev + wconv[1:2] * z + wconv[2:3] * z_next)

    y = (_dot(y_gla.astype(BF16), wout_ref[0:HEAD_W, :])
         + _dot(y_sgu.astype(BF16), wout_ref[HEAD_W:2 * HEAD_W, :])
         + _dot(y_conv.astype(BF16), wout_ref[2 * HEAD_W:3 * HEAD_W, :])
         + _dot(ft_ref[...].astype(BF16), wout_ref[3 * HEAD_W:4 * HEAD_W, :]))
    x1 = _ln(DEEPNORM_ALPHA * x_ref[...] + g1 * y) * ln1g_ref[...] + ln1b_ref[...]
    x1_ref[...] = x1
    h2 = _ln(x1) * (1.0 + sc2) + sh2
    xw_ref[...] = _pack_bf16_pairs(h2)

    h2_hi = h2.astype(BF16)
    h2_lo = (h2 - h2_hi.astype(F32)).astype(BF16)
    logits = _dot(h2_hi, wrh_ref[...]) + _dot(h2_hi, wrl_ref[...]) + _dot(h2_lo, wrh_ref[...])
    weights, lpos, n_chunks, run_start = _route(logits.T[:N_EXPERTS], rb_ref[...], before_ref[...], lower_ref[...])
    lpos_ref[...] = jnp.concatenate(lpos, axis=0).astype(jnp.int32)
    wrow_ref[...] = jnp.concatenate(weights, axis=0)
    carry = carry_ref[:, 0:1]
    lane = lax.broadcasted_iota(jnp.int32, (N_EXPERTS, LANES), 1)
    total = jnp.sum(n_chunks, axis=0, keepdims=True)
    cols = jnp.where(lane == 0, n_chunks, jnp.where(lane == 1, run_start,
                                                   jnp.where(lane == 2, carry, jnp.where(lane == 3, total, 0.0))))
    tab = jnp.concatenate([cols, jnp.zeros((LANES - N_EXPERTS, LANES), F32)], axis=0).T
    tab_ref[...] = tab[0:8].astype(jnp.int32)
    new_carry = carry_ref[...] + n_chunks
    carry_ref[...] = new_carry
    cnt_ref[...] = new_carry


def _mix_out(x2d, of2d, ob2d, proj, yft2d, mod3, mod_row, lw, seq_len, tm):
    t = x2d.shape[0]
    tm = min(tm, t)
    nt8 = t // HALO
    rows8 = tm // HALO

    def col(cb):
        return pl.BlockSpec((tm, HEAD_W), lambda i: (i, cb))

    def halo_prev(cb):
        return pl.BlockSpec((HALO, HEAD_W), lambda i: (jnp.maximum(i * rows8 - 1, 0), cb))

    def halo_next(cb):
        return pl.BlockSpec((HALO, HEAD_W), lambda i: (jnp.minimum((i + 1) * rows8, nt8 - 1), cb))

    def full(a):
        return pl.BlockSpec(a.shape, lambda i: (0,) * a.ndim)

    tok_d = pl.BlockSpec((tm, D_MODEL), lambda i: (i, 0))
    tok_h = pl.BlockSpec((tm, HEAD_W), lambda i: (i, 0))
    weights = [lw["gla_norm_g"], lw["sgu_norm_g"], lw["sgu_norm_b"], lw["w_sgu_cat"], lw["b_sgu_full"],
               lw["w_conv"], lw["w_out"], lw["ln1_g"], lw["ln1_b"], lw["w_router_hi"], lw["w_router_lo"], lw["router_bias"]]
    weights += list(_mix_masks(tm))
    in_specs = ([tok_d, tok_h, tok_h, col(COL_G), col(COL_SGU), col(COL_SGV), col(COL_CVB), col(COL_CVC),
                 col(COL_CVX), halo_prev(COL_CVC), halo_prev(COL_CVX), halo_next(COL_CVC), halo_next(COL_CVX),
                 tok_h, pl.BlockSpec((None, 1, 6 * D_MODEL), lambda i: (mod_row(i, tm), 0, 0))]
                + [full(w) for w in weights])
    args = [x2d, of2d, ob2d] + [proj] * 10 + [yft2d, mod3] + weights
    return pl.pallas_call(
        functools.partial(_mix_out_kernel, seq_len=seq_len),
        grid=(t // tm,),
        in_specs=in_specs,
        out_specs=[tok_d,
                   pl.BlockSpec((tm, PACK_W), lambda i: (i, 0)),
                   pl.BlockSpec((None, TOP_K, tm), lambda i: (i, 0, 0)),
                   pl.BlockSpec((None, TOP_K, tm), lambda i: (i, 0, 0)),
                   pl.BlockSpec((None, 8, LANES), lambda i: (i, 0, 0)),
                   pl.BlockSpec((N_EXPERTS, LANES), lambda i: (0, 0))],
        out_shape=[jax.ShapeDtypeStruct((t, D_MODEL), F32),
                   jax.ShapeDtypeStruct((t, PACK_W), jnp.uint32),
                   jax.ShapeDtypeStruct((t // tm, TOP_K, tm), F32),
                   jax.ShapeDtypeStruct((t // tm, TOP_K, tm), jnp.int32),
                   jax.ShapeDtypeStruct((t // tm, 8, LANES), jnp.int32),
                   jax.ShapeDtypeStruct((N_EXPERTS, LANES), F32)],
        scratch_shapes=[pltpu.VMEM((N_EXPERTS, LANES), F32)],
        compiler_params=_cparams("arbitrary"),
        name="mix_out",
    )(*args)


def _local_rows(tm):
    need = tm * TOP_K + N_EXPERTS * (CHUNK - 1)
    return -(-need // LBLK) * LBLK


def _run_copy(local_ref, sorted_hbm, sem, local_row, sorted_row, to_sorted, rows=CHUNK):
    loc = local_ref.at[pl.ds(pl.multiple_of(local_row, CHUNK), rows)]
    srt = sorted_hbm.at[pl.ds(pl.multiple_of(sorted_row, CHUNK), rows)]
    return pltpu.make_async_copy(loc, srt, sem) if to_sorted else pltpu.make_async_copy(srt, loc, sem)


def _start_run_copies(tab_ref, gstart_ref, local_ref, sorted_hbm, sem, to_sorted):
    def per_expert(e, totals):
        n = tab_ref[0, e]
        l0 = tab_ref[1, e]
        g0 = (gstart_ref[e] + tab_ref[2, e]) * CHUNK
        n_big = n >> BIG_SHIFT
        n_small = n & ((1 << BIG_SHIFT) - 1)

        def big(j, c):
            _run_copy(local_ref, sorted_hbm, sem, l0 + j * BIG_ROWS, g0 + j * BIG_ROWS, to_sorted, BIG_ROWS).start()
            return c

        lax.fori_loop(0, n_big, big, 0)
        l1 = l0 + n_big * BIG_ROWS
        g1 = g0 + n_big * BIG_ROWS

        for j in range((1 << BIG_SHIFT) - 1):
            @pl.when(n_small > j)
            def _():
                _run_copy(local_ref, sorted_hbm, sem, l1 + j * CHUNK, g1 + j * CHUNK, to_sorted).start()
        return totals[0] + n_big, totals[1] + n_small

    return lax.fori_loop(0, N_EXPERTS, per_expert, (jnp.int32(0), jnp.int32(0)), unroll=4)


def _get_pending(pending_ref, s):
    return pending_ref[s, 0], pending_ref[s, 1]


def _set_pending(pending_ref, s, counts):
    pending_ref[s, 0] = counts[0]
    pending_ref[s, 1] = counts[1]


def _wait_run_copies(counts, local_ref, sorted_hbm, sem, to_sorted):
    batch = 8

    def wait_n(rows, reps):
        def body(j, c):
            for _ in range(reps):
                _run_copy(local_ref, sorted_hbm, sem, 0, 0, to_sorted, rows).wait()
            return c
        return body

    for count, rows in zip(counts, (BIG_ROWS, CHUNK)):
        lax.fori_loop(0, count >> 3, wait_n(rows, batch), 0)
        lax.fori_loop(0, count & (batch - 1), wait_n(rows, 1), 0)


def _moe_scatter_kernel(gstart_ref, gend_ref, gfill_ref, xw_ref, lpos_ref, tab_ref, *refs, has_shared):
    sorted_hbm, local_ref, sem, zsem, pending_ref = refs[1:] if has_shared else refs
    i = pl.program_id(0)
    n = pl.num_programs(0)
    slot = i & 1
    tm = xw_ref.shape[0]
    lrows = local_ref.shape[1]

    def zero_copy(e, j):
        off = pl.multiple_of(gend_ref[e] - (j + 1) * LBLK, LBLK)
        return pltpu.make_async_copy(local_ref.at[1, pl.ds(0, LBLK)], sorted_hbm.at[pl.ds(off, LBLK)], zsem)

    @pl.when(i == 0)
    def _():
        _set_pending(pending_ref, 0, (0, 0))
        _set_pending(pending_ref, 1, (0, 0))
        local_ref[1, 0:LBLK, :] = jnp.zeros((LBLK, PACK_W), jnp.uint32)

        def z_start(e, c):
            lax.fori_loop(0, gfill_ref[e], lambda j, c2: (zero_copy(e, j).start(), c2)[1], 0)
            return c

        def z_wait(e, c):
            lax.fori_loop(0, gfill_ref[e], lambda j, c2: (zero_copy(e, j).wait(), c2)[1], 0)
            return c

        lax.fori_loop(0, N_EXPERTS, z_start, 0)
        lax.fori_loop(0, N_EXPERTS, z_wait, 0)

    local = local_ref.at[slot]
    _wait_run_copies(_get_pending(pending_ref, slot), local, sorted_hbm, sem.at[slot], True)

    x = _unpack_bf16_pairs(xw_ref[...]).astype(BF16)
    lpos = lpos_ref[...].astype(jnp.int16)
    one = jnp.ones((LBLK, tm), BF16)
    used_rows = tab_ref[3, 0] * CHUNK

    def sort_block(b):
        riota = lax.broadcasted_iota(jnp.int16, (LBLK, tm), 0) + b * LBLK
        p = jnp.zeros((LBLK, tm), BF16)
        for k in range(TOP_K):
            p = jnp.where(riota == lpos[k:k + 1, :], one, p)
        local[b * LBLK:(b + 1) * LBLK, :] = _pack_exact_bf16_pairs(_dot(p, x))

    n_blocks = lrows // LBLK
    for b in range(n_blocks - 1):
        sort_block(b)
    pl.when(used_rows > (n_blocks - 1) * LBLK)(functools.partial(sort_block, n_blocks - 1))

    _set_pending(pending_ref, slot, _start_run_copies(tab_ref, gstart_ref, local, sorted_hbm, sem.at[slot], True))

    @pl.when(i == n - 1)
    def _():
        for s in range(2):
            _wait_run_copies(_get_pending(pending_ref, s), local_ref.at[s], sorted_hbm, sem.at[s], True)


def _moe_scatter(gstart, gend, gfill, xw, lpos, tab, n_rows, tm, shared=None):
    n_tiles = xw.shape[0] // tm
    lrows = _local_rows(tm)
    any_spec = pl.BlockSpec(memory_space=pl.ANY)
    in_specs = [pl.BlockSpec((tm, PACK_W), lambda i, *_: (i, 0)),
                pl.BlockSpec((None, TOP_K, tm), lambda i, *_: (i, 0, 0)),
                pl.BlockSpec((None, 8, LANES), lambda i, *_: (i, 0, 0), memory_space=pltpu.SMEM)]
    args = [gstart, gend, gfill, xw, lpos, tab]
    aliases = {}
    if shared is not None:
        in_specs.append(any_spec)
        aliases = {len(args): 0}
        args.append(shared)
    return pl.pallas_call(
        functools.partial(_moe_scatter_kernel, has_shared=shared is not None),
        grid_spec=pltpu.PrefetchScalarGridSpec(
            num_scalar_prefetch=3,
            grid=(n_tiles,),
            in_specs=in_specs,
            out_specs=any_spec,
            scratch_shapes=[pltpu.VMEM((2, lrows, PACK_W), jnp.uint32), pltpu.SemaphoreType.DMA((2,)),
                            pltpu.SemaphoreType.DMA, pltpu.SMEM((2, 2), jnp.int32)]),
        out_shape=jax.ShapeDtypeStruct((n_rows, PACK_W), jnp.uint32),
        input_output_aliases=aliases,
        compiler_params=_cparams("arbitrary"),
        name="moe_scatter",
    )(*args)


def _expert2_kernel(be_ref, nu_ref, xs_ref, wg_ref, wu_ref, wd_ref, ys_ref, wgu_b, wd_b):
    j = pl.program_id(0)

    @pl.when(jnp.logical_or(j == 0, be_ref[j] != be_ref[jnp.maximum(j - 1, 0)]))
    def _():
        wgu_b[:, :EXPERT_FF] = wg_ref[...].astype(BF16)
        wgu_b[:, EXPERT_FF:] = wu_ref[...].astype(BF16)
        wd_b[...] = wd_ref[...].astype(BF16)

    @pl.when(j < nu_ref[0])
    def _():
        x = _unpack_bf16_pairs(xs_ref[...]).astype(BF16)
        gu = _dot(x, wgu_b[...])
        a = _silu(gu[:, :EXPERT_FF]) * gu[:, EXPERT_FF:]
        ys_ref[...] = _pack_bf16_pairs(_dot(a.astype(BF16), wd_b[...]))


def _experts2(xs, block_expert, n_used, lw, layer, eblk):
    n_rows = xs.shape[0]
    ff = EXPERT_FF

    def blk(j, be, nu):
        return (jnp.minimum(j, jnp.maximum(nu[0] - 1, 0)), 0)

    def wblk(j, be, nu):
        return (layer, be[j], 0, 0)

    return pl.pallas_call(
        _expert2_kernel,
        grid_spec=pltpu.PrefetchScalarGridSpec(
            num_scalar_prefetch=2,
            grid=(n_used[0],),
            in_specs=[pl.BlockSpec((eblk, PACK_W), blk),
                      pl.BlockSpec((None, None, D_MODEL, ff), wblk),
                      pl.BlockSpec((None, None, D_MODEL, ff), wblk),
                      pl.BlockSpec((None, None, ff, D_MODEL), wblk)],
            out_specs=pl.BlockSpec((eblk, PACK_W), blk),
            scratch_shapes=[pltpu.VMEM((D_MODEL, 2 * ff), BF16), pltpu.VMEM((ff, D_MODEL), BF16)]),
        out_shape=jax.ShapeDtypeStruct((n_rows, PACK_W), jnp.uint32),
        compiler_params=_cparams("arbitrary"),
        name="moe_experts",
    )(block_expert, n_used, xs, lw["w_exp_gate"], lw["w_exp_up"], lw["w_exp_down"])


def _moe_combine_kernel(gstart_ref, x1_ref, xw_ref, lpos_ref, wrow_ref, tab_ref, tab_next_ref, mod_ref,
                        sg_ref, su_ref, sd_ref, ln2g_ref, ln2b_ref, sorted_hbm, o_ref,
                        local_ref, sem, pending_ref):
    i = pl.program_id(0)
    n = pl.num_programs(0)
    slot = i & 1
    tm = x1_ref.shape[0]
    lrows = local_ref.shape[1]

    @pl.when(i == 0)
    def _():
        local_ref[...] = jnp.zeros_like(local_ref)
        _set_pending(pending_ref, 0,
                     _start_run_copies(tab_ref, gstart_ref, local_ref.at[0], sorted_hbm, sem.at[0], False))

    @pl.when(i + 1 < n)
    def _():
        nxt = 1 - slot
        _set_pending(pending_ref, nxt, _start_run_copies(tab_next_ref, gstart_ref, local_ref.at[nxt], sorted_hbm,
                                                         sem.at[nxt], False))

    h = _unpack_bf16_pairs(xw_ref[...]).astype(BF16)
    a = _silu(_dot(h, sg_ref[...])) * _dot(h, su_ref[...])
    acc = _dot(a.astype(BF16), sd_ref[...])

    local = local_ref.at[slot]
    _wait_run_copies(_get_pending(pending_ref, slot), local, sorted_hbm, sem.at[slot], False)

    lpos = lpos_ref[...].astype(jnp.int16)
    wrow = wrow_ref[...].astype(BF16)

    def unsort_block(b, acc):
        riota = lax.broadcasted_iota(jnp.int16, (LBLK, tm), 0) + b * LBLK
        q = jnp.zeros((LBLK, tm), BF16)
        for k in range(TOP_K):
            q = jnp.where(riota == lpos[k:k + 1, :], jnp.broadcast_to(wrow[k:k + 1, :], (LBLK, tm)), q)
        y = _unpack_bf16_pairs(local[b * LBLK:(b + 1) * LBLK, :]).astype(BF16)
        return acc + _dot_tn(q, y)

    for b in range(lrows // LBLK):
        acc = unsort_block(b, acc)

    g2 = mod_ref[...][:, 5 * D_MODEL:6 * D_MODEL]
    u = DEEPNORM_ALPHA * x1_ref[...] + g2 * acc
    o_ref[...] = _ln(u) * ln2g_ref[...] + ln2b_ref[...]


def _moe_combine(gstart, x1, xw, lpos, wrow, tab, ys, mod3, mod_row, lw, tm):
    t = x1.shape[0]
    n_tiles = t // tm
    lrows = _local_rows(tm)

    def full(a):
        return pl.BlockSpec(a.shape, lambda i, *_: (0,) * a.ndim)

    tab_blk = lambda f: pl.BlockSpec((None, 8, LANES), f, memory_space=pltpu.SMEM)
    return pl.pallas_call(
        _moe_combine_kernel,
        grid_spec=pltpu.PrefetchScalarGridSpec(
            num_scalar_prefetch=1,
            grid=(n_tiles,),
            in_specs=[pl.BlockSpec((tm, D_MODEL), lambda i, *_: (i, 0)),
                      pl.BlockSpec((tm, PACK_W), lambda i, *_: (i, 0)),
                      pl.BlockSpec((None, TOP_K, tm), lambda i, *_: (i, 0, 0)),
                      pl.BlockSpec((None, TOP_K, tm), lambda i, *_: (i, 0, 0)),
                      tab_blk(lambda i, *_: (i, 0, 0)),
                      tab_blk(lambda i, *_: (jnp.minimum(i + 1, n_tiles - 1), 0, 0)),
                      pl.BlockSpec((None, 1, 6 * D_MODEL), lambda i, *_: (mod_row(i, tm), 0, 0)),
                      full(lw["w_sh_gate"]), full(lw["w_sh_up"]), full(lw["w_sh_down"]),
                      full(lw["ln2_g"]), full(lw["ln2_b"]),
                      pl.BlockSpec(memory_space=pl.ANY)],
            out_specs=pl.BlockSpec((tm, D_MODEL), lambda i, *_: (i, 0)),
            scratch_shapes=[pltpu.VMEM((2, lrows, PACK_W), jnp.uint32), pltpu.SemaphoreType.DMA((2,)),
                            pltpu.SMEM((2, 2), jnp.int32)]),
        out_shape=jax.ShapeDtypeStruct((t, D_MODEL), F32),
        compiler_params=_cparams("arbitrary"),
        name="moe_combine",
    )(gstart, x1, xw, lpos, wrow, tab, tab, mod3, lw["w_sh_gate"], lw["w_sh_up"], lw["w_sh_down"],
      lw["ln2_g"], lw["ln2_b"], ys)


def _moe_groups(groups, lw, layer):
    tm = MOE_TM
    chunks = [g["chunks"][:, 0].astype(jnp.int32) for g in groups]
    t_all = sum(g["x1"].shape[0] for g in groups)
    n_tiles = t_all // tm
    eblk = max(LBLK, min(EBLK_MAX, t_all * TOP_K // N_EXPERTS))
    region_rows = sum(chunks) * CHUNK
    padded = ((region_rows + eblk - 1) // eblk) * eblk
    end = jnp.cumsum(padded)
    start = end - padded
    n_blocks = -(-(t_all * TOP_K + n_tiles * N_EXPERTS * (CHUNK - 1)) // eblk) + N_EXPERTS
    n_used = end[-1] // eblk
    blk_row = jnp.minimum(jnp.arange(n_blocks, dtype=jnp.int32), jnp.maximum(n_used - 1, 0)) * eblk
    block_expert = jnp.minimum(jnp.sum((end[None, :] <= blk_row[:, None]).astype(jnp.int32), axis=1),
                               N_EXPERTS - 1).astype(jnp.int32)
    gfill = (padded - region_rows + LBLK - 1) // LBLK
    gstarts = []
    before = jnp.zeros_like(chunks[0])
    for ch in chunks:
        gstarts.append(start // CHUNK + before)
        before = before + ch
    xs = None
    for g, gstart in zip(groups, gstarts):
        fill = gfill if xs is None else jnp.zeros_like(gfill)
        xs = _moe_scatter(gstart, end, fill, g["xw"], g["lpos"], g["tab"], n_blocks * eblk, tm, xs)
    ys = _experts2(xs, block_expert, n_used.reshape(1), lw, layer, eblk)
    return [_moe_combine(gstart, g["x1"], g["xw"], g["lpos"], g["wrow"], g["tab"], ys, g["mod3"], g["mod_row"],
                         lw, tm) for g, gstart in zip(groups, gstarts)]


def _channel_dft_table():
    k = np.arange(FNET_CH, dtype=np.float64)
    ang = 2.0 * np.pi * np.outer(k, k) / FNET_CH
    eye = np.eye(FNET_GROUPS)
    return np.concatenate([np.kron(eye, np.cos(ang)), -np.kron(eye, np.sin(ang))], axis=1)


def _direct_dft_tables(n):
    k = np.arange(n, dtype=np.float64)
    ang = 2.0 * np.pi * (np.outer(k, k) % n) / n
    scale = 1.0 / math.sqrt(n * FNET_CH)
    return np.cos(ang) * scale, np.sin(ang) * scale


def _two_stage_dft_tables(n):
    n1 = FFT_N1
    n2 = n // n1
    a = np.arange(n1, dtype=np.float64)
    ang1 = 2.0 * np.pi * (np.outer(a, a) % n1) / n1
    k1 = np.arange(n1).reshape(n1, 1, 1)
    k2 = np.arange(n2).reshape(1, n2, 1)
    m2 = np.arange(n2).reshape(1, 1, n2)
    ang2 = 2.0 * np.pi * ((m2 * (k1 + n1 * k2)) % n) / n
    scale = 1.0 / math.sqrt(n * FNET_CH)
    return np.cos(ang1), np.sin(ang1), np.cos(ang2) * scale, np.sin(ang2) * scale


def _grid_sincos_table(rows, d):
    quarter = d // 4
    omega = 1.0 / (POS_BASE ** (np.arange(quarter, dtype=np.float64) / quarter))
    r = np.arange(rows, dtype=np.float64)[:, None] * omega
    c = np.arange(GRID_W, dtype=np.float64)[:, None] * omega
    return (np.concatenate([np.sin(r), np.cos(r)], axis=-1).astype(np.float32),
            np.concatenate([np.sin(c), np.cos(c)], axis=-1).astype(np.float32))


def _layer_weights(l, w_in, w_gla_a, b_gla_a, gla_norm_g, sgu_norm_g, sgu_norm_b, w_sgu, b_sgu, w_conv,
                   w_out, ln1_g, ln1_b, ln2_g, ln2_b, w_router, router_bias,
                   w_exp_gate, w_exp_up, w_exp_down, w_sh_gate, w_sh_up, w_sh_down):
    wi = w_in[l]
    lr0 = 4 * HEAD_W
    w_in_p = jnp.concatenate(
        [wi[:, :lr0], wi[:, lr0 + 2 * GLA_LR:lr0 + 2 * GLA_LR + 5 * HEAD_W], wi[:, lr0:lr0 + 2 * GLA_LR],
         jnp.zeros((D_MODEL, LR_W - 2 * GLA_LR), F32), wi[:, lr0 + 2 * GLA_LR + 5 * HEAD_W:]], axis=1).astype(BF16)
    wa_pad = jnp.zeros((LR_W, 2 * HEAD_W), F32)
    wa_pad = wa_pad.at[:GLA_LR, :HEAD_W].set(w_gla_a[l, 0])
    wa_pad = wa_pad.at[GLA_LR:2 * GLA_LR, HEAD_W:].set(w_gla_a[l, 1])
    w_router_pad = jnp.concatenate([w_router[l], jnp.zeros((D_MODEL, LANES - N_EXPERTS), F32)], axis=1)
    w_router_hi = w_router_pad.astype(BF16)
    row = lambda a: a[l].reshape(1, -1)
    return {
        "w_in_p": w_in_p,
        "wa_hi": wa_pad.astype(BF16), "wa_lo": (wa_pad - wa_pad.astype(BF16).astype(F32)).astype(BF16),
        "ba": jnp.concatenate([b_gla_a[l, 0], b_gla_a[l, 1]]).reshape(1, 2 * HEAD_W),
        "gla_norm_g": row(gla_norm_g), "sgu_norm_g": row(sgu_norm_g), "sgu_norm_b": row(sgu_norm_b),
        "w_sgu_cat": jnp.concatenate([w_sgu[l, g] for g in range(GMLP_GROUPS)], axis=1).astype(BF16),
        "b_sgu_full": jnp.repeat(b_sgu[l].T, HEAD_W // GMLP_GROUPS, axis=1),
        "w_conv": w_conv[l],
        "w_out": w_out[l].astype(BF16),
        "ln1_g": row(ln1_g), "ln1_b": row(ln1_b), "ln2_g": row(ln2_g), "ln2_b": row(ln2_b),
        "w_router_hi": w_router_hi, "w_router_lo": (w_router_pad - w_router_hi.astype(F32)).astype(BF16),
        "router_bias": router_bias[l].reshape(N_EXPERTS, 1),
        "w_exp_gate": w_exp_gate, "w_exp_up": w_exp_up, "w_exp_down": w_exp_down,
        "w_sh_gate": w_sh_gate[l].astype(BF16), "w_sh_up": w_sh_up[l].astype(BF16),
        "w_sh_down": w_sh_down[l].astype(BF16),
    }


def _state_to_blockdiag_t(s):
    bsz = s.shape[0]
    st = jnp.swapaxes(s, 2, 3)
    eye = jnp.eye(GLA_HEADS, dtype=s.dtype)
    return jnp.einsum("bhvd,hg->bhvgd", st, eye).reshape(bsz, HEAD_W, HEAD_W)


def _blockdiag_t_to_state(st):
    bsz = st.shape[0]
    s5 = st.reshape(bsz, GLA_HEADS, GLA_DK, GLA_HEADS, GLA_DK)
    diag = jnp.stack([s5[:, h, :, h, :] for h in range(GLA_HEADS)], axis=1)
    return jnp.swapaxes(diag, 2, 3)


def _pre_moe(x3, pos, mod3, mod_row, lw, st0, emit_final, tabs):
    bsz, n, _ = x3.shape
    t = bsz * n
    outs = _in_proj(x3.reshape(t, D_MODEL), pos, mod3, mod_row, lw["w_in_p"], tabs["cs"], tm=512)
    if pos is not None:
        proj, zr, zi, x2d = outs
    else:
        proj, zr, zi = outs
        x2d = x3.reshape(t, D_MODEL)
    gla_out = _gla(proj.reshape(bsz, n, STORE_W), lw["wa_hi"], lw["wa_lo"], lw["ba"], st0, emit_final)
    o_f, o_b = gla_out[:2]
    zr3 = zr.reshape(bsz, n, HEAD_W)
    zi3 = zi.reshape(bsz, n, HEAD_W)
    if "two_stage" in tabs:
        yft = _fft_two_stage(zr3, zi3, tabs["two_stage"])
    else:
        yft = _fft_direct(zr3, zi3, *tabs["direct"])
    x1, xw, wrow, lpos, tab, chunks = _mix_out(x2d, o_f.reshape(t, HEAD_W), o_b.reshape(t, HEAD_W), proj,
                                             yft.reshape(t, HEAD_W), mod3, mod_row, lw, seq_len=n, tm=MOE_TM)
    group = {"x1": x1, "xw": xw, "wrow": wrow, "lpos": lpos, "tab": tab, "chunks": chunks,
             "mod3": mod3, "mod_row": mod_row, "shape": (bsz, n, D_MODEL)}
    return group, gla_out[2:]


def _trunk_layer(x3, pos, mod3, mod_row, lw, layer, st0, emit_final, tabs):
    group, finals = _pre_moe(x3, pos, mod3, mod_row, lw, st0, emit_final, tabs)
    (x2,) = _moe_groups([group], lw, layer)
    return x2.reshape(group["shape"]), finals


def kernel(x_prompt, x_sample, c, state_gla, c_ctx, w_ada, b_ada, w_in, w_gla_a, b_gla_a, gla_norm_g, sgu_norm_g, sgu_norm_b, w_sgu, b_sgu, w_conv, w_out, ln1_g, ln1_b, ln2_g, ln2_b, w_router, router_bias, w_exp_gate, w_exp_up, w_exp_down, w_sh_gate, w_sh_up, w_sh_down):
    n_layers = w_ada.shape[0]
    bp, np_, _ = x_prompt.shape
    bs, ns, _ = x_sample.shape
    assert bs <= 7

    cond8 = jnp.concatenate([c_ctx[None, :], c, jnp.zeros((7 - bs, D_MODEL), F32)], axis=0)
    mod = _ada_mod(cond8, w_ada, b_ada)

    tabs_p = {"cs": jnp.asarray(_channel_dft_table(), BF16),
              "direct": tuple(jnp.asarray(a, BF16) for a in _direct_dft_tables(np_))}
    tabs_s = {"cs": tabs_p["cs"],
              "two_stage": tuple(jnp.asarray(a, BF16) for a in _two_stage_dft_tables(ns))}
    rtab, ctab = _grid_sincos_table(ns // GRID_W, D_MODEL)
    pos = jnp.concatenate([jnp.repeat(jnp.asarray(rtab), GRID_W, axis=0),
                           jnp.tile(jnp.asarray(ctab), (ns // GRID_W, 1))], axis=-1)

    prompt_row = lambda i, tm: 0
    sample_row = lambda i, tm: 1 + (i * tm) // ns

    y_p = x_prompt
    y_s = x_sample
    finals = []
    for l in range(n_layers):
        lw = _layer_weights(l, w_in, w_gla_a, b_gla_a, gla_norm_g, sgu_norm_g, sgu_norm_b, w_sgu, b_sgu,
                            w_conv, w_out, ln1_g, ln1_b, ln2_g, ln2_b, w_router, router_bias,
                            w_exp_gate, w_exp_up, w_exp_down, w_sh_gate, w_sh_up, w_sh_down)
        mod3 = mod[l].reshape(8, 1, 6 * D_MODEL)
        group_p, fin = _pre_moe(y_p, None, mod3, prompt_row, lw, None, True, tabs_p)
        finals.append(jnp.stack([_blockdiag_t_to_state(fin[0]), _blockdiag_t_to_state(fin[1])], axis=1))
        st0 = jnp.stack([_state_to_blockdiag_t(state_gla[:, l, 0]), _state_to_blockdiag_t(state_gla[:, l, 1])])
        group_s, _ = _pre_moe(y_s, pos if l == 0 else None, mod3, sample_row, lw, st0, False, tabs_s)
        y_p, y_s = _moe_groups([group_p, group_s], lw, l)
        y_p = y_p.reshape(group_p["shape"])
        y_s = y_s.reshape(group_s["shape"])
    new_state = jnp.stack(finals, axis=1).astype(x_prompt.dtype)
    return (y_p, y_s, new_state)
```

```python
import functools
import math

import numpy as np
import jax
import jax.numpy as jnp
from jax import lax
from jax.experimental import pallas as pl
from jax.experimental.pallas import tpu as pltpu

F32 = jnp.float32
BF16 = jnp.bfloat16

D_MODEL = 1024
DEPTH = 2
GRID_W = 64
HEAD_W = 256
GLA_HEADS = 4
GLA_DK = 64
GLA_LR = 16
GLA_TAU = 16.0
GLA_CHUNK = 64
GMLP_GROUPS = 4
GMLP_CHUNK = 128
FNET_GROUPS = 4
FNET_CH = 64
N_EXPERTS = 64
TOP_K = 8
N_GROUPS = 8
TOPK_GROUPS = 4
EXPERT_FF = 256
ROUTED_SCALE = 2.5
DEEPNORM_ALPHA = (2 * DEPTH) ** 0.25
LN_EPS = 1e-5
RMS_EPS = 1e-6
POS_BASE = 10000.0

COL_Q, COL_K, COL_V, COL_G, COL_SGU, COL_SGV, COL_CVB, COL_CVC, COL_CVX = range(9)
LR_W = 128
STORE_W = 9 * HEAD_W + LR_W
PROJ_W = STORE_W + HEAD_W
COL_LR = (9 * HEAD_W) // LR_W
HALO = 16

LANES = 128
PACK_W = D_MODEL // 2
EBLK_MAX = 1024
LBLK = 256
MOE_TM = 256
CHUNK = 8
BIG_SHIFT = 2
BIG_ROWS = CHUNK << BIG_SHIFT
SEG = 256
FFT_N1 = 64
VMEM_LIMIT = 56 * 1024 * 1024


def _cparams(*sem):
    return pltpu.CompilerParams(dimension_semantics=sem, vmem_limit_bytes=VMEM_LIMIT)


def _ln(x):
    mu = jnp.mean(x, axis=-1, keepdims=True)
    xc = x - mu
    var = jnp.mean(xc * xc, axis=-1, keepdims=True)
    return xc * lax.rsqrt(var + LN_EPS)


def _sigmoid(x):
    return 1.0 / (1.0 + jnp.exp(-x))


def _silu(x):
    return x * _sigmoid(x)


def _pack_bf16_pairs(x):
    w = x.shape[1] // 2
    lo = lax.bitcast_convert_type(x[:, :w].astype(BF16).astype(F32), jnp.uint32)
    hi = lax.bitcast_convert_type(x[:, w:].astype(BF16).astype(F32), jnp.uint32)
    return (lo >> 16) | (hi & jnp.uint32(0xFFFF0000))


def _pack_exact_bf16_pairs(x):
    w = x.shape[1] // 2
    lo = lax.bitcast_convert_type(x[:, :w], jnp.uint32)
    hi = lax.bitcast_convert_type(x[:, w:], jnp.uint32)
    return (lo >> 16) | hi


def _unpack_bf16_pairs(u):
    lo = lax.bitcast_convert_type(u << 16, F32)
    hi = lax.bitcast_convert_type(u & jnp.uint32(0xFFFF0000), F32)
    return jnp.concatenate([lo, hi], axis=1)


def _dot(a, b):
    return jnp.dot(a, b, preferred_element_type=F32)


def _dot_nt(a, b):
    return lax.dot_general(a, b, (((1,), (1,)), ((), ())), preferred_element_type=F32)


def _dot_tn(a, b):
    return lax.dot_general(a, b, (((0,), (0,)), ((), ())), preferred_element_type=F32)


def _ada_kernel(c_ref, w_ref, b_ref, o_ref):
    c = c_ref[...]
    o_ref[...] = _dot(_silu(c).astype(BF16), w_ref[...].astype(BF16)) + b_ref[...]


def _ada_mod(cond8, w_ada, b_ada):
    n_l, d, w6 = w_ada.shape
    tn = 1536
    return pl.pallas_call(
        _ada_kernel,
        grid=(n_l, w6 // tn),
        in_specs=[pl.BlockSpec((8, d), lambda l, j: (0, 0)),
                  pl.BlockSpec((None, d, tn), lambda l, j: (l, 0, j)),
                  pl.BlockSpec((None, 1, tn), lambda l, j: (l, 0, j))],
        out_specs=pl.BlockSpec((None, 8, tn), lambda l, j: (l, 0, j)),
        out_shape=jax.ShapeDtypeStruct((n_l, 8, w6), F32),
        compiler_params=_cparams("parallel", "parallel"),
        name="ada_mod",
    )(cond8, w_ada, b_ada.reshape(n_l, 1, w6))


def _in_proj_kernel(*refs, has_pos):
    if has_pos:
        x_ref, pos_ref, mod_ref, w_ref, cs_ref, proj_ref, zr_ref, zi_ref, x0_ref = refs
        x = x_ref[...] + pos_ref[...]
        x0_ref[...] = x
    else:
        x_ref, mod_ref, w_ref, cs_ref, proj_ref, zr_ref, zi_ref = refs
        x = x_ref[...]
    mod = mod_ref[...]
    sh1 = mod[:, 0:D_MODEL]
    sc1 = mod[:, D_MODEL:2 * D_MODEL]
    h = _ln(x) * (1.0 + sc1) + sh1
    proj = _dot(h.astype(BF16), w_ref[...])
    proj_ref[...] = proj[:, :STORE_W].astype(BF16)
    ft = proj[:, STORE_W:].astype(BF16)
    z = _dot(ft, cs_ref[...])
    zr_ref[...] = z[:, :HEAD_W].astype(BF16)
    zi_ref[...] = z[:, HEAD_W:].astype(BF16)


def _in_proj(x2d, pos, mod3, mod_row, w_in_p, cs, tm):
    t = x2d.shape[0]
    tm = min(tm, t)
    in_specs = [pl.BlockSpec((tm, D_MODEL), lambda i: (i, 0))]
    args = [x2d]
    out_shape = [jax.ShapeDtypeStruct((t, STORE_W), BF16),
                 jax.ShapeDtypeStruct((t, HEAD_W), BF16),
                 jax.ShapeDtypeStruct((t, HEAD_W), BF16)]
    out_specs = [pl.BlockSpec((tm, STORE_W), lambda i: (i, 0)),
                 pl.BlockSpec((tm, HEAD_W), lambda i: (i, 0)),
                 pl.BlockSpec((tm, HEAD_W), lambda i: (i, 0))]
    if pos is not None:
        n_pos = pos.shape[0] // tm
        in_specs.append(pl.BlockSpec((tm, D_MODEL), lambda i: (i % n_pos, 0)))
        args.append(pos)
        out_shape.append(jax.ShapeDtypeStruct((t, D_MODEL), F32))
        out_specs.append(pl.BlockSpec((tm, D_MODEL), lambda i: (i, 0)))
    in_specs += [pl.BlockSpec((None, 1, 6 * D_MODEL), lambda i: (mod_row(i, tm), 0, 0)),
                 pl.BlockSpec((D_MODEL, PROJ_W), lambda i: (0, 0)),
                 pl.BlockSpec((HEAD_W, 2 * HEAD_W), lambda i: (0, 0))]
    args += [mod3, w_in_p, cs]
    return pl.pallas_call(
        functools.partial(_in_proj_kernel, has_pos=pos is not None),
        grid=(t // tm,),
        in_specs=in_specs,
        out_specs=out_specs,
        out_shape=out_shape,
        compiler_params=_cparams("parallel"),
        name="in_proj",
    )(*args)


def _gla_masks(reverse):
    i = np.arange(SEG)
    same_chunk = (i[:, None] // GLA_CHUNK) == (i[None, :] // GLA_CHUNK)
    tri = same_chunk & ((i[None, :] >= i[:, None]) if reverse else (i[None, :] <= i[:, None]))
    l_idx = i[:, None] % GLA_CHUNK
    m_idx = np.arange(GLA_CHUNK)[None, :]
    causal = (m_idx >= l_idx) if reverse else (m_idx <= l_idx)
    return (jnp.asarray(tri, BF16), jnp.asarray(same_chunk, BF16), jnp.asarray(same_chunk, F32),
            jnp.asarray(causal, F32))


def _gla_segment(q, k, v, pre, st_ref, o_ref, masks, reverse):
    seg = q.shape[0]
    n_chunks = seg // GLA_CHUNK
    la = (jnp.minimum(pre, 0.0) - jnp.log1p(jnp.exp(-jnp.abs(pre)))) * (1.0 / GLA_TAU)

    tri_m, ones_m, bd, causal = masks
    hi = la.astype(BF16)
    lo = (la - hi.astype(F32)).astype(BF16)
    b = _dot(tri_m, hi) + _dot(tri_m, lo)
    btot = _dot(ones_m, hi) + _dot(ones_m, lo)

    q_dec = q * (GLA_DK ** -0.5) * jnp.exp(b)
    k_inv = (k * jnp.exp(-b)).astype(BF16)
    k_end = (k * jnp.exp(btot - b)).astype(BF16)
    dec = jnp.exp(btot)
    vb = v.astype(BF16)
    keep = causal > 0.5

    st = st_ref[...]
    order = range(n_chunks - 1, -1, -1) if reverse else range(n_chunks)
    for ci in order:
        sl = slice(ci * GLA_CHUNK, (ci + 1) * GLA_CHUNK)
        qd = q_dec[sl]
        qbd = (jnp.concatenate([qd] * GLA_HEADS, axis=0) * bd).astype(BF16)
        a = _dot_nt(qbd, k_inv[sl])
        a = jnp.where(keep, a, 0.0)
        rr = _dot(a.astype(BF16), vb[sl])
        o = _dot_nt(qd.astype(BF16), st.astype(BF16))
        for h in range(GLA_HEADS):
            hs = slice(h * GLA_CHUNK, (h + 1) * GLA_CHUNK)
            o = o + rr[hs] * bd[hs]
        o_ref[sl, :] = o
        kvt = _dot_tn(vb[sl], k_end[sl])
        st = st * dec[ci * GLA_CHUNK:ci * GLA_CHUNK + 1, :] + kvt * bd
    st_ref[...] = st


def _gla_kernel(*refs, has_init, emit_final):
    qf, kf, vf, lrf, qb, kb, vb, lrb, wah_ref, wal_ref, ba_ref = refs[:11]
    mask_refs = refs[11:19]
    rest = refs[19:]
    if has_init:
        s0f, s0b = rest[:2]
        rest = rest[2:]
    of_ref, ob_ref = rest[:2]
    rest = rest[2:]
    if emit_final:
        sff, sfb = rest[:2]
        rest = rest[2:]
    stf, stb = rest

    s = pl.program_id(1)

    @pl.when(s == 0)
    def _():
        if has_init:
            stf[...] = s0f[...]
            stb[...] = s0b[...]
        else:
            stf[...] = jnp.zeros_like(stf)
            stb[...] = jnp.zeros_like(stb)

    def decay_pre(lr_ref):
        lr = lr_ref[...]
        return _dot(lr, wah_ref[...]) + _dot(lr, wal_ref[...]) + ba_ref[...]

    f32 = lambda ref: ref[...].astype(F32)

    masks_f = tuple(m[...] for m in mask_refs[:4])
    masks_b = tuple(m[...] for m in mask_refs[4:])
    _gla_segment(f32(qf), f32(kf), f32(vf), decay_pre(lrf)[:, :HEAD_W], stf, of_ref, masks_f, reverse=False)
    _gla_segment(f32(qb), f32(kb), f32(vb), decay_pre(lrb)[:, HEAD_W:], stb, ob_ref, masks_b, reverse=True)

    if emit_final:
        @pl.when(s == pl.num_programs(1) - 1)
        def _():
            sff[...] = stf[...]
            sfb[...] = stb[...]


def _gla(proj3, wa_hi, wa_lo, ba, st0, emit_final):
    bsz, n, _ = proj3.shape
    nseg = n // SEG

    def col(cb, width=HEAD_W, rev=False):
        if rev:
            return pl.BlockSpec((None, SEG, width), lambda b, s: (b, nseg - 1 - s, cb))
        return pl.BlockSpec((None, SEG, width), lambda b, s: (b, s, cb))

    in_specs = [col(COL_Q), col(COL_K), col(COL_V), col(COL_LR, LR_W),
                col(COL_Q, rev=True), col(COL_K, rev=True), col(COL_V, rev=True), col(COL_LR, LR_W, rev=True),
                pl.BlockSpec((LR_W, 2 * HEAD_W), lambda b, s: (0, 0)),
                pl.BlockSpec((LR_W, 2 * HEAD_W), lambda b, s: (0, 0)),
                pl.BlockSpec((1, 2 * HEAD_W), lambda b, s: (0, 0))]
    masks = _gla_masks(False) + _gla_masks(True)
    in_specs += [pl.BlockSpec(m.shape, lambda b, s: (0, 0)) for m in masks]
    args = [proj3] * 8 + [wa_hi, wa_lo, ba] + list(masks)
    st_spec = pl.BlockSpec((None, HEAD_W, HEAD_W), lambda b, s: (b, 0, 0))
    if st0 is not None:
        in_specs += [st_spec, st_spec]
        args += [st0[0], st0[1]]
    out_shape = [jax.ShapeDtypeStruct((bsz, n, HEAD_W), F32)] * 2
    out_specs = [pl.BlockSpec((None, SEG, HEAD_W), lambda b, s: (b, s, 0)),
                 pl.BlockSpec((None, SEG, HEAD_W), lambda b, s: (b, nseg - 1 - s, 0))]
    if emit_final:
        out_shape += [jax.ShapeDtypeStruct((bsz, HEAD_W, HEAD_W), F32)] * 2
        out_specs += [st_spec, st_spec]
    return pl.pallas_call(
        functools.partial(_gla_kernel, has_init=st0 is not None, emit_final=emit_final),
        grid=(bsz, nseg),
        in_specs=in_specs,
        out_specs=out_specs,
        out_shape=out_shape,
        scratch_shapes=[pltpu.VMEM((HEAD_W, HEAD_W), F32), pltpu.VMEM((HEAD_W, HEAD_W), F32)],
        compiler_params=_cparams("parallel", "arbitrary"),
        name="gla",
    )(*args)


def _fft_direct_kernel(zr_ref, zi_ref, cn_ref, sn_ref, o_ref):
    o_ref[...] = _dot(cn_ref[...], zr_ref[...]) + _dot(sn_ref[...], zi_ref[...])


def _fft_direct(zr3, zi3, cn, sn):
    bsz, n, w = zr3.shape
    blk = pl.BlockSpec((None, n, w), lambda b: (b, 0, 0))
    tab = pl.BlockSpec((n, n), lambda b: (0, 0))
    return pl.pallas_call(
        _fft_direct_kernel,
        grid=(bsz,),
        in_specs=[blk, blk, tab, tab],
        out_specs=blk,
        out_shape=jax.ShapeDtypeStruct((bsz, n, w), F32),
        compiler_params=_cparams("parallel"),
        name="fft_direct",
    )(zr3, zi3, cn, sn)


def _fft_a_kernel(zr_ref, zi_ref, c_ref, s_ref, gr_ref, gi_ref):
    zr = zr_ref[...]
    zi = zi_ref[...]
    cm = c_ref[...]
    sm = s_ref[...]
    gr_ref[...] = (_dot(cm, zr) + _dot(sm, zi)).astype(BF16)
    gi_ref[...] = (_dot(cm, zi) - _dot(sm, zr)).astype(BF16)


def _fft_c_kernel(gr_ref, gi_ref, mc_ref, ms_ref, o_ref):
    for j in range(gr_ref.shape[0]):
        o_ref[:, j, :] = _dot(mc_ref[j], gr_ref[j]) + _dot(ms_ref[j], gi_ref[j])


def _fft_two_stage(zr3, zi3, tabs):
    bsz, n, w = zr3.shape
    n1 = FFT_N1
    n2 = n // n1
    c1, s1, mc, ms = tabs
    tn = 2048
    wide = n2 * w
    blk = pl.BlockSpec((None, n1, tn), lambda b, j: (b, 0, j))
    tab = pl.BlockSpec((n1, n1), lambda b, j: (0, 0))
    gr, gi = pl.pallas_call(
        _fft_a_kernel,
        grid=(bsz, wide // tn),
        in_specs=[blk, blk, tab, tab],
        out_specs=[blk, blk],
        out_shape=[jax.ShapeDtypeStruct((bsz, n1, wide), BF16)] * 2,
        compiler_params=_cparams("parallel", "parallel"),
        name="fft_stage_a",
    )(zr3.reshape(bsz, n1, wide), zi3.reshape(bsz, n1, wide), c1, s1)
    kb = 8
    gblk = pl.BlockSpec((None, kb, n2, w), lambda b, j: (b, j, 0, 0))
    mblk = pl.BlockSpec((kb, n2, n2), lambda b, j: (j, 0, 0))
    out = pl.pallas_call(
        _fft_c_kernel,
        grid=(bsz, n1 // kb),
        in_specs=[gblk, gblk, mblk, mblk],
        out_specs=pl.BlockSpec((None, n2, kb, w), lambda b, j: (b, 0, j, 0)),
        out_shape=jax.ShapeDtypeStruct((bsz, n2, n1, w), F32),
        compiler_params=_cparams("parallel", "parallel"),
        name="fft_stage_c",
    )(gr.reshape(bsz, n1, n2, w), gi.reshape(bsz, n1, n2, w), mc, ms)
    return out.reshape(bsz, n, w)


def _mix_masks(tm):
    h = np.arange(HEAD_W) // GLA_DK
    head_mean = (h[:, None] == h[None, :]) / GLA_DK
    rg = np.arange(GMLP_GROUPS * GMLP_CHUNK) // GMLP_CHUNK
    cg = np.arange(HEAD_W) // (HEAD_W // GMLP_GROUPS)
    t = np.arange(tm)
    e = np.arange(N_EXPERTS)
    return (jnp.asarray(head_mean, BF16), jnp.asarray(rg[:, None] == cg[None, :], F32),
            jnp.asarray(t[:, None] < t[None, :], BF16), jnp.asarray(e[None, :] < e[:, None], BF16))


def _route(logits, bias, before, lower):
    tm = logits.shape[1]
    s = _sigmoid(logits)
    biased = s + bias
    neg = -jnp.inf
    rows = lax.broadcasted_iota(jnp.int32, (8, tm), 0)

    def first_argmax(x, ids, sentinel):
        m = jnp.max(x, axis=0, keepdims=True)
        return m, jnp.min(jnp.where(x == m, ids, sentinel), axis=0, keepdims=True)

    gs_rows = []
    for g in range(N_GROUPS):
        x = biased[8 * g:8 * g + 8]
        m1, i1 = first_argmax(x, rows, 8)
        m2 = jnp.max(jnp.where(rows == i1, neg, x), axis=0, keepdims=True)
        gs_rows.append(m1 + m2)
    gs = jnp.concatenate(gs_rows, axis=0)
    gsel = jnp.zeros((N_GROUPS, tm), F32)
    for _ in range(TOPK_GROUPS):
        _, i = first_argmax(gs, rows, 8)
        hit = rows == i
        gsel = jnp.where(hit, 1.0, gsel)
        gs = jnp.where(hit, neg, gs)

    xs = [jnp.where(gsel[g:g + 1] > 0.0, biased[8 * g:8 * g + 8], neg) for g in range(N_GROUPS)]
    ids = [rows + 8 * g for g in range(N_GROUPS)]
    sel = [jnp.zeros((8, tm), F32) for _ in range(N_GROUPS)]
    eids = []
    for _ in range(TOP_K):
        m = xs[0]
        for g in range(1, N_GROUPS):
            m = jnp.maximum(m, xs[g])
        m = jnp.max(m, axis=0, keepdims=True)
        cand = jnp.where(xs[0] == m, ids[0], N_EXPERTS)
        for g in range(1, N_GROUPS):
            cand = jnp.minimum(cand, jnp.where(xs[g] == m, ids[g], N_EXPERTS))
        i = jnp.min(cand, axis=0, keepdims=True)
        eids.append(i)
        for g in range(N_GROUPS):
            hit = ids[g] == i
            sel[g] = jnp.where(hit, 1.0, sel[g])
            xs[g] = jnp.where(hit, neg, xs[g])

    sel_all = jnp.concatenate(sel, axis=0)
    seen = _dot(sel_all.astype(BF16), before)
    counts = jnp.sum(sel_all, axis=1, keepdims=True)

    n_chunks = jnp.ceil(counts * (1.0 / CHUNK))
    run_start = _dot(lower, jnp.broadcast_to(n_chunks, (N_EXPERTS, LANES)).astype(BF16))[:, 0:1] * CHUNK
    local_pos = seen + run_start

    def pick(k, table):
        acc = None
        for g in range(N_GROUPS):
            v = jnp.where(ids[g] == eids[k], table[8 * g:8 * g + 8], 0.0)
            acc = v if acc is None else acc + v
        return jnp.sum(acc, axis=0, keepdims=True)

    w_raw = [pick(k, s) for k in range(TOP_K)]
    lpos = [pick(k, local_pos) for k in range(TOP_K)]
    tot = w_raw[0]
    for k in range(1, TOP_K):
        tot = tot + w_raw[k]
    weights = [w / tot * ROUTED_SCALE for w in w_raw]
    return weights, lpos, n_chunks, run_start


def _mix_out_kernel(x_ref, of_ref, ob_ref, g_ref, su_ref, sv_ref, cb_ref, cc_ref, cx_ref,
                    ccp_ref, cxp_ref, ccn_ref, cxn_ref, ft_ref, mod_ref,
                    glag_ref, sgng_ref, sgnb_ref, wsgu_ref, bsgu_ref, wconv_ref, wout_ref,
                    ln1g_ref, ln1b_ref, wrh_ref, wrl_ref, rb_ref, hmean_ref, sgubd_ref, before_ref, lower_ref,
                    x1_ref, xw_ref, wrow_ref, lpos_ref, tab_ref, cnt_ref, carry_ref, *, seq_len):
    tm = x_ref.shape[0]
    i = pl.program_id(0)

    @pl.when(i == 0)
    def _():
        carry_ref[...] = jnp.zeros_like(carry_ref)

    mod = mod_ref[...]
    g1 = mod[:, 2 * D_MODEL:3 * D_MODEL]
    sh2 = mod[:, 3 * D_MODEL:4 * D_MODEL]
    sc2 = mod[:, 4 * D_MODEL:5 * D_MODEL]

    o = of_ref[...] + ob_ref[...]
    head_mean = hmean_ref[...]
    o2 = o * o
    o2_hi = o2.astype(BF16)
    o2_lo = (o2 - o2_hi.astype(F32)).astype(BF16)
    ms = _dot(o2_hi, head_mean) + _dot(o2_lo, head_mean)
    y_gla = o * lax.rsqrt(ms + RMS_EPS) * glag_ref[...] * _silu(g_ref[...].astype(F32))

    vn = _ln(sv_ref[...].astype(F32)) * sgng_ref[...] + sgnb_ref[...]
    sgu_bd = sgubd_ref[...]
    sp_parts = []
    for j in range(tm // GMLP_CHUNK):
        vc = vn[j * GMLP_CHUNK:(j + 1) * GMLP_CHUNK]
        vbd = (jnp.concatenate([vc] * GMLP_GROUPS, axis=0) * sgu_bd).astype(BF16)
        sp_parts.append(_dot(wsgu_ref[...], vbd) + bsgu_ref[...])
    y_sgu = su_ref[...].astype(F32) * jnp.concatenate(sp_parts, axis=0)

    z = cc_ref[...].astype(F32) * cx_ref[...].astype(F32)
    z_before = (ccp_ref[...].astype(F32) * cxp_ref[...].astype(F32))[HALO - 1:HALO, :]
    z_after = (ccn_ref[...].astype(F32) * cxn_ref[...].astype(F32))[0:1, :]
    row = lax.broadcasted_iota(jnp.int32, (tm, HEAD_W), 0)
    pos = (i * tm + row) & (seq_len - 1)
    z_prev = jnp.where(row == 0, z_before, pltpu.roll(z, 1, 0))
    z_next = jnp.where(row == tm - 1, z_after, pltpu.roll(z, tm - 1, 0))
    z_prev = jnp.where(pos == 0, 0.0, z_prev)
    z_next = jnp.where(pos == seq_len - 1, 0.0, z_next)
    wconv = wconv_ref[...]
    y_conv = cb_ref[...].astype(F32) * (wconv[0:1] * z_prev + wconv[1:2] * z + wconv[2:3] * z_next)

    y = (_dot(y_gla.astype(BF16), wout_ref[0:HEAD_W, :])
         + _dot(y_sgu.astype(BF16), wout_ref[HEAD_W:2 * HEAD_W, :])
         + _dot(y_conv.astype(BF16), wout_ref[2 * HEAD_W:3 * HEAD_W, :])
         + _dot(ft_ref[...].astype(BF16), wout_ref[3 * HEAD_W:4 * HEAD_W, :]))
    x1 = _ln(DEEPNORM_ALPHA * x_ref[...] + g1 * y) * ln1g_ref[...] + ln1b_ref[...]
    x1_ref[...] = x1
    h2 = _ln(x1) * (1.0 + sc2) + sh2
    xw_ref[...] = _pack_bf16_pairs(h2)

    h2_hi = h2.astype(BF16)
    h2_lo = (h2 - h2_hi.astype(F32)).astype(BF16)
    logits = _dot(h2_hi, wrh_ref[...]) + _dot(h2_hi, wrl_ref[...]) + _dot(h2_lo, wrh_ref[...])
    weights, lpos, n_chunks, run_start = _route(logits.T[:N_EXPERTS], rb_ref[...], before_ref[...], lower_ref[...])
    lpos_ref[...] = jnp.concatenate(lpos, axis=0).astype(jnp.int32)
    wrow_ref[...] = jnp.concatenate(weights, axis=0)
    carry = carry_ref[:, 0:1]
    lane = lax.broadcasted_iota(jnp.int32, (N_EXPERTS, LANES), 1)
    total = jnp.sum(n_chunks, axis=0, keepdims=True)
    cols = jnp.where(lane == 0, n_chunks, jnp.where(lane == 1, run_start,
                                                   jnp.where(lane == 2, carry, jnp.where(lane == 3, total, 0.0))))
    tab = jnp.concatenate([cols, jnp.zeros((LANES - N_EXPERTS, LANES), F32)], axis=0).T
    tab_ref[...] = tab[0:8].astype(jnp.int32)
    new_carry = carry_ref[...] + n_chunks
    carry_ref[...] = new_carry
    cnt_ref[...] = new_carry


def _mix_out(x2d, of2d, ob2d, proj, yft2d, mod3, mod_row, lw, seq_len, tm):
    t = x2d.shape[0]
    tm = min(tm, t)
    nt8 = t // HALO
    rows8 = tm // HALO

    def col(cb):
        return pl.BlockSpec((tm, HEAD_W), lambda i: (i, cb))

    def halo_prev(cb):
        return pl.BlockSpec((HALO, HEAD_W), lambda i: (jnp.maximum(i * rows8 - 1, 0), cb))

    def halo_next(cb):
        return pl.BlockSpec((HALO, HEAD_W), lambda i: (jnp.minimum((i + 1) * rows8, nt8 - 1), cb))

    def full(a):
        return pl.BlockSpec(a.shape, lambda i: (0,) * a.ndim)

    tok_d = pl.BlockSpec((tm, D_MODEL), lambda i: (i, 0))
    tok_h = pl.BlockSpec((tm, HEAD_W), lambda i: (i, 0))
    weights = [lw["gla_norm_g"], lw["sgu_norm_g"], lw["sgu_norm_b"], lw["w_sgu_cat"], lw["b_sgu_full"],
               lw["w_conv"], lw["w_out"], lw["ln1_g"], lw["ln1_b"], lw["w_router_hi"], lw["w_router_lo"], lw["router_bias"]]
    weights += list(_mix_masks(tm))
    in_specs = ([tok_d, tok_h, tok_h, col(COL_G), col(COL_SGU), col(COL_SGV), col(COL_CVB), col(COL_CVC),
                 col(COL_CVX), halo_prev(COL_CVC), halo_prev(COL_CVX), halo_next(COL_CVC), halo_next(COL_CVX),
                 tok_h, pl.BlockSpec((None, 1, 6 * D_MODEL), lambda i: (mod_row(i, tm), 0, 0))]
                + [full(w) for w in weights])
    args = [x2d, of2d, ob2d] + [proj] * 10 + [yft2d, mod3] + weights
    return pl.pallas_call(
        functools.partial(_mix_out_kernel, seq_len=seq_len),
        grid=(t // tm,),
        in_specs=in_specs,
        out_specs=[tok_d,
                   pl.BlockSpec((tm, PACK_W), lambda i: (i, 0)),
                   pl.BlockSpec((None, TOP_K, tm), lambda i: (i, 0, 0)),
                   pl.BlockSpec((None, TOP_K, tm), lambda i: (i, 0, 0)),
                   pl.BlockSpec((None, 8, LANES), lambda i: (i, 0, 0)),
                   pl.BlockSpec((N_EXPERTS, LANES), lambda i: (0, 0))],
        out_shape=[jax.ShapeDtypeStruct((t, D_MODEL), F32),
                   jax.ShapeDtypeStruct((t, PACK_W), jnp.uint32),
                   jax.ShapeDtypeStruct((t // tm, TOP_K, tm), F32),
                   jax.ShapeDtypeStruct((t // tm, TOP_K, tm), jnp.int32),
                   jax.ShapeDtypeStruct((t // tm, 8, LANES), jnp.int32),
                   jax.ShapeDtypeStruct((N_EXPERTS, LANES), F32)],
        scratch_shapes=[pltpu.VMEM((N_EXPERTS, LANES), F32)],
        compiler_params=_cparams("arbitrary"),
        name="mix_out",
    )(*args)


def _local_rows(tm):
    need = tm * TOP_K + N_EXPERTS * (CHUNK - 1)
    return -(-need // LBLK) * LBLK


def _run_copy(local_ref, sorted_hbm, sem, local_row, sorted_row, to_sorted, rows=CHUNK):
    loc = local_ref.at[pl.ds(pl.multiple_of(local_row, CHUNK), rows)]
    srt = sorted_hbm.at[pl.ds(pl.multiple_of(sorted_row, CHUNK), rows)]
    return pltpu.make_async_copy(loc, srt, sem) if to_sorted else pltpu.make_async_copy(srt, loc, sem)


def _start_run_copies(tab_ref, gstart_ref, local_ref, sorted_hbm, sem, to_sorted):
    def per_expert(e, totals):
        n = tab_ref[0, e]
        l0 = tab_ref[1, e]
        g0 = (gstart_ref[e] + tab_ref[2, e]) * CHUNK
        n_big = n >> BIG_SHIFT
        n_small = n & ((1 << BIG_SHIFT) - 1)

        def big(j, c):
            _run_copy(local_ref, sorted_hbm, sem, l0 + j * BIG_ROWS, g0 + j * BIG_ROWS, to_sorted, BIG_ROWS).start()
            return c

        lax.fori_loop(0, n_big, big, 0)
        l1 = l0 + n_big * BIG_ROWS
        g1 = g0 + n_big * BIG_ROWS

        for j in range((1 << BIG_SHIFT) - 1):
            @pl.when(n_small > j)
            def _():
                _run_copy(local_ref, sorted_hbm, sem, l1 + j * CHUNK, g1 + j * CHUNK, to_sorted).start()
        return totals[0] + n_big, totals[1] + n_small

    return lax.fori_loop(0, N_EXPERTS, per_expert, (jnp.int32(0), jnp.int32(0)), unroll=8)


def _get_pending(pending_ref, s):
    return pending_ref[s, 0], pending_ref[s, 1]


def _set_pending(pending_ref, s, counts):
    pending_ref[s, 0] = counts[0]
    pending_ref[s, 1] = counts[1]


def _wait_run_copies(counts, local_ref, sorted_hbm, sem, to_sorted):
    batch = 8

    def wait_n(rows, reps):
        def body(j, c):
            for _ in range(reps):
                _run_copy(local_ref, sorted_hbm, sem, 0, 0, to_sorted, rows).wait()
            return c
        return body

    for count, rows in zip(counts, (BIG_ROWS, CHUNK)):
        lax.fori_loop(0, count >> 3, wait_n(rows, batch), 0)
        lax.fori_loop(0, count & (batch - 1), wait_n(rows, 1), 0)


def _moe_scatter_kernel(gstart_ref, gend_ref, gfill_ref, xw_ref, lpos_ref, tab_ref, *refs, has_shared):
    sorted_hbm, local_ref, sem, zsem, pending_ref = refs[1:] if has_shared else refs
    i = pl.program_id(0)
    n = pl.num_programs(0)
    slot = i & 1
    tm = xw_ref.shape[0]
    lrows = local_ref.shape[1]

    def zero_copy(e, j):
        off = pl.multiple_of(gend_ref[e] - (j + 1) * LBLK, LBLK)
        return pltpu.make_async_copy(local_ref.at[1, pl.ds(0, LBLK)], sorted_hbm.at[pl.ds(off, LBLK)], zsem)

    @pl.when(i == 0)
    def _():
        _set_pending(pending_ref, 0, (0, 0))
        _set_pending(pending_ref, 1, (0, 0))
        local_ref[1, 0:LBLK, :] = jnp.zeros((LBLK, PACK_W), jnp.uint32)

        def z_start(e, c):
            lax.fori_loop(0, gfill_ref[e], lambda j, c2: (zero_copy(e, j).start(), c2)[1], 0)
            return c

        def z_wait(e, c):
            lax.fori_loop(0, gfill_ref[e], lambda j, c2: (zero_copy(e, j).wait(), c2)[1], 0)
            return c

        lax.fori_loop(0, N_EXPERTS, z_start, 0)
        lax.fori_loop(0, N_EXPERTS, z_wait, 0)

    local = local_ref.at[slot]
    _wait_run_copies(_get_pending(pending_ref, slot), local, sorted_hbm, sem.at[slot], True)

    x = _unpack_bf16_pairs(xw_ref[...]).astype(BF16)
    lpos = lpos_ref[...].astype(jnp.int16)
    one = jnp.ones((LBLK, tm), BF16)
    used_rows = tab_ref[3, 0] * CHUNK

    def sort_block(b):
        riota = lax.broadcasted_iota(jnp.int16, (LBLK, tm), 0) + b * LBLK
        p = jnp.zeros((LBLK, tm), BF16)
        for k in range(TOP_K):
            p = jnp.where(riota == lpos[k:k + 1, :], one, p)
        local[b * LBLK:(b + 1) * LBLK, :] = _pack_exact_bf16_pairs(_dot(p, x))

    n_blocks = lrows // LBLK
    for b in range(n_blocks - 1):
        sort_block(b)
    pl.when(used_rows > (n_blocks - 1) * LBLK)(functools.partial(sort_block, n_blocks - 1))

    _set_pending(pending_ref, slot, _start_run_copies(tab_ref, gstart_ref, local, sorted_hbm, sem.at[slot], True))

    @pl.when(i == n - 1)
    def _():
        for s in range(2):
            _wait_run_copies(_get_pending(pending_ref, s), local_ref.at[s], sorted_hbm, sem.at[s], True)


def _moe_scatter(gstart, gend, gfill, xw, lpos, tab, n_rows, tm, shared=None):
    n_tiles = xw.shape[0] // tm
    lrows = _local_rows(tm)
    any_spec = pl.BlockSpec(memory_space=pl.ANY)
    in_specs = [pl.BlockSpec((tm, PACK_W), lambda i, *_: (i, 0)),
                pl.BlockSpec((None, TOP_K, tm), lambda i, *_: (i, 0, 0)),
                pl.BlockSpec((None, 8, LANES), lambda i, *_: (i, 0, 0), memory_space=pltpu.SMEM)]
    args = [gstart, gend, gfill, xw, lpos, tab]
    aliases = {}
    if shared is not None:
        in_specs.append(any_spec)
        aliases = {len(args): 0}
        args.append(shared)
    return pl.pallas_call(
        functools.partial(_moe_scatter_kernel, has_shared=shared is not None),
        grid_spec=pltpu.PrefetchScalarGridSpec(
            num_scalar_prefetch=3,
            grid=(n_tiles,),
            in_specs=in_specs,
            out_specs=any_spec,
            scratch_shapes=[pltpu.VMEM((2, lrows, PACK_W), jnp.uint32), pltpu.SemaphoreType.DMA((2,)),
                            pltpu.SemaphoreType.DMA, pltpu.SMEM((2, 2), jnp.int32)]),
        out_shape=jax.ShapeDtypeStruct((n_rows, PACK_W), jnp.uint32),
        input_output_aliases=aliases,
        compiler_params=_cparams("arbitrary"),
        name="moe_scatter",
    )(*args)


def _expert2_kernel(be_ref, nu_ref, xs_ref, wg_ref, wu_ref, wd_ref, ys_ref, wgu_b, wd_b):
    j = pl.program_id(0)

    @pl.when(jnp.logical_or(j == 0, be_ref[j] != be_ref[jnp.maximum(j - 1, 0)]))
    def _():
        wgu_b[:, :EXPERT_FF] = wg_ref[...].astype(BF16)
        wgu_b[:, EXPERT_FF:] = wu_ref[...].astype(BF16)
        wd_b[...] = wd_ref[...].astype(BF16)

    @pl.when(j < nu_ref[0])
    def _():
        x = _unpack_bf16_pairs(xs_ref[...]).astype(BF16)
        gu = _dot(x, wgu_b[...])
        a = _silu(gu[:, :EXPERT_FF]) * gu[:, EXPERT_FF:]
        ys_ref[...] = _pack_bf16_pairs(_dot(a.astype(BF16), wd_b[...]))


def _experts2(xs, block_expert, n_used, lw, layer, eblk):
    n_rows = xs.shape[0]
    ff = EXPERT_FF

    def blk(j, be, nu):
        return (jnp.minimum(j, jnp.maximum(nu[0] - 1, 0)), 0)

    def wblk(j, be, nu):
        return (layer, be[j], 0, 0)

    return pl.pallas_call(
        _expert2_kernel,
        grid_spec=pltpu.PrefetchScalarGridSpec(
            num_scalar_prefetch=2,
            grid=(n_used[0],),
            in_specs=[pl.BlockSpec((eblk, PACK_W), blk),
                      pl.BlockSpec((None, None, D_MODEL, ff), wblk),
                      pl.BlockSpec((None, None, D_MODEL, ff), wblk),
                      pl.BlockSpec((None, None, ff, D_MODEL), wblk)],
            out_specs=pl.BlockSpec((eblk, PACK_W), blk),
            scratch_shapes=[pltpu.VMEM((D_MODEL, 2 * ff), BF16), pltpu.VMEM((ff, D_MODEL), BF16)]),
        out_shape=jax.ShapeDtypeStruct((n_rows, PACK_W), jnp.uint32),
        compiler_params=_cparams("arbitrary"),
        name="moe_experts",
    )(block_expert, n_used, xs, lw["w_exp_gate"], lw["w_exp_up"], lw["w_exp_down"])


def _moe_combine_kernel(gstart_ref, x1_ref, xw_ref, lpos_ref, wrow_ref, tab_ref, tab_next_ref, mod_ref,
                        sg_ref, su_ref, sd_ref, ln2g_ref, ln2b_ref, sorted_hbm, o_ref,
                        local_ref, sem, pending_ref):
    i = pl.program_id(0)
    n = pl.num_programs(0)
    slot = i & 1
    tm = x1_ref.shape[0]
    lrows = local_ref.shape[1]

    @pl.when(i == 0)
    def _():
        local_ref[...] = jnp.zeros_like(local_ref)
        _set_pending(pending_ref, 0,
                     _start_run_copies(tab_ref, gstart_ref, local_ref.at[0], sorted_hbm, sem.at[0], False))

    @pl.when(i + 1 < n)
    def _():
        nxt = 1 - slot
        _set_pending(pending_ref, nxt, _start_run_copies(tab_next_ref, gstart_ref, local_ref.at[nxt], sorted_hbm,
                                                         sem.at[nxt], False))

    h = _unpack_bf16_pairs(xw_ref[...]).astype(BF16)
    a = _silu(_dot(h, sg_ref[...])) * _dot(h, su_ref[...])
    acc = _dot(a.astype(BF16), sd_ref[...])

    local = local_ref.at[slot]
    _wait_run_copies(_get_pending(pending_ref, slot), local, sorted_hbm, sem.at[slot], False)

    lpos = lpos_ref[...].astype(jnp.int16)
    wrow = wrow_ref[...].astype(BF16)

    def unsort_block(b, acc):
        riota = lax.broadcasted_iota(jnp.int16, (LBLK, tm), 0) + b * LBLK
        q = jnp.zeros((LBLK, tm), BF16)
        for k in range(TOP_K):
            q = jnp.where(riota == lpos[k:k + 1, :], jnp.broadcast_to(wrow[k:k + 1, :], (LBLK, tm)), q)
        y = _unpack_bf16_pairs(local[b * LBLK:(b + 1) * LBLK, :]).astype(BF16)
        return acc + _dot_tn(q, y)

    for b in range(lrows // LBLK):
        acc = unsort_block(b, acc)

    g2 = mod_ref[...][:, 5 * D_MODEL:6 * D_MODEL]
    u = DEEPNORM_ALPHA * x1_ref[...] + g2 * acc
    o_ref[...] = _ln(u) * ln2g_ref[...] + ln2b_ref[...]


def _moe_combine(gstart, x1, xw, lpos, wrow, tab, ys, mod3, mod_row, lw, tm):
    t = x1.shape[0]
    n_tiles = t // tm
    lrows = _local_rows(tm)

    def full(a):
        return pl.BlockSpec(a.shape, lambda i, *_: (0,) * a.ndim)

    tab_blk = lambda f: pl.BlockSpec((None, 8, LANES), f, memory_space=pltpu.SMEM)
    return pl.pallas_call(
        _moe_combine_kernel,
        grid_spec=pltpu.PrefetchScalarGridSpec(
            num_scalar_prefetch=1,
            grid=(n_tiles,),
            in_specs=[pl.BlockSpec((tm, D_MODEL), lambda i, *_: (i, 0)),
                      pl.BlockSpec((tm, PACK_W), lambda i, *_: (i, 0)),
                      pl.BlockSpec((None, TOP_K, tm), lambda i, *_: (i, 0, 0)),
                      pl.BlockSpec((None, TOP_K, tm), lambda i, *_: (i, 0, 0)),
                      tab_blk(lambda i, *_: (i, 0, 0)),
                      tab_blk(lambda i, *_: (jnp.minimum(i + 1, n_tiles - 1), 0, 0)),
                      pl.BlockSpec((None, 1, 6 * D_MODEL), lambda i, *_: (mod_row(i, tm), 0, 0)),
                      full(lw["w_sh_gate"]), full(lw["w_sh_up"]), full(lw["w_sh_down"]),
                      full(lw["ln2_g"]), full(lw["ln2_b"]),
                      pl.BlockSpec(memory_space=pl.ANY)],
            out_specs=pl.BlockSpec((tm, D_MODEL), lambda i, *_: (i, 0)),
            scratch_shapes=[pltpu.VMEM((2, lrows, PACK_W), jnp.uint32), pltpu.SemaphoreType.DMA((2,)),
                            pltpu.SMEM((2, 2), jnp.int32)]),
        out_shape=jax.ShapeDtypeStruct((t, D_MODEL), F32),
        compiler_params=_cparams("arbitrary"),
        name="moe_combine",
    )(gstart, x1, xw, lpos, wrow, tab, tab, mod3, lw["w_sh_gate"], lw["w_sh_up"], lw["w_sh_down"],
      lw["ln2_g"], lw["ln2_b"], ys)


def _moe_groups(groups, lw, layer):
    tm = MOE_TM
    chunks = [g["chunks"][:, 0].astype(jnp.int32) for g in groups]
    t_all = sum(g["x1"].shape[0] for g in groups)
    n_tiles = t_all // tm
    eblk = max(LBLK, min(EBLK_MAX, t_all * TOP_K // N_EXPERTS))
    region_rows = sum(chunks) * CHUNK
    padded = ((region_rows + eblk - 1) // eblk) * eblk
    end = jnp.cumsum(padded)
    start = end - padded
    n_blocks = -(-(t_all * TOP_K + n_tiles * N_EXPERTS * (CHUNK - 1)) // eblk) + N_EXPERTS
    n_used = end[-1] // eblk
    blk_row = jnp.minimum(jnp.arange(n_blocks, dtype=jnp.int32), jnp.maximum(n_used - 1, 0)) * eblk
    block_expert = jnp.minimum(jnp.sum((end[None, :] <= blk_row[:, None]).astype(jnp.int32), axis=1),
                               N_EXPERTS - 1).astype(jnp.int32)
    gfill = (padded - region_rows + LBLK - 1) // LBLK
    gstarts = []
    before = jnp.zeros_like(chunks[0])
    for ch in chunks:
        gstarts.append(start // CHUNK + before)
        before = before + ch
    xs = None
    for g, gstart in zip(groups, gstarts):
        fill = gfill if xs is None else jnp.zeros_like(gfill)
        xs = _moe_scatter(gstart, end, fill, g["xw"], g["lpos"], g["tab"], n_blocks * eblk, tm, xs)
    ys = _experts2(xs, block_expert, n_used.reshape(1), lw, layer, eblk)
    return [_moe_combine(gstart, g["x1"], g["xw"], g["lpos"], g["wrow"], g["tab"], ys, g["mod3"], g["mod_row"],
                         lw, tm) for g, gstart in zip(groups, gstarts)]


def _channel_dft_table():
    k = np.arange(FNET_CH, dtype=np.float64)
    ang = 2.0 * np.pi * np.outer(k, k) / FNET_CH
    eye = np.eye(FNET_GROUPS)
    return np.concatenate([np.kron(eye, np.cos(ang)), -np.kron(eye, np.sin(ang))], axis=1)


def _direct_dft_tables(n):
    k = np.arange(n, dtype=np.float64)
    ang = 2.0 * np.pi * (np.outer(k, k) % n) / n
    scale = 1.0 / math.sqrt(n * FNET_CH)
    return np.cos(ang) * scale, np.sin(ang) * scale


def _two_stage_dft_tables(n):
    n1 = FFT_N1
    n2 = n // n1
    a = np.arange(n1, dtype=np.float64)
    ang1 = 2.0 * np.pi * (np.outer(a, a) % n1) / n1
    k1 = np.arange(n1).reshape(n1, 1, 1)
    k2 = np.arange(n2).reshape(1, n2, 1)
    m2 = np.arange(n2).reshape(1, 1, n2)
    ang2 = 2.0 * np.pi * ((m2 * (k1 + n1 * k2)) % n) / n
    scale = 1.0 / math.sqrt(n * FNET_CH)
    return np.cos(ang1), np.sin(ang1), np.cos(ang2) * scale, np.sin(ang2) * scale


def _grid_sincos_table(rows, d):
    quarter = d // 4
    omega = 1.0 / (POS_BASE ** (np.arange(quarter, dtype=np.float64) / quarter))
    r = np.arange(rows, dtype=np.float64)[:, None] * omega
    c = np.arange(GRID_W, dtype=np.float64)[:, None] * omega
    return (np.concatenate([np.sin(r), np.cos(r)], axis=-1).astype(np.float32),
            np.concatenate([np.sin(c), np.cos(c)], axis=-1).astype(np.float32))


def _layer_weights(l, w_in, w_gla_a, b_gla_a, gla_norm_g, sgu_norm_g, sgu_norm_b, w_sgu, b_sgu, w_conv,
                   w_out, ln1_g, ln1_b, ln2_g, ln2_b, w_router, router_bias,
                   w_exp_gate, w_exp_up, w_exp_down, w_sh_gate, w_sh_up, w_sh_down):
    wi = w_in[l]
    lr0 = 4 * HEAD_W
    w_in_p = jnp.concatenate(
        [wi[:, :lr0], wi[:, lr0 + 2 * GLA_LR:lr0 + 2 * GLA_LR + 5 * HEAD_W], wi[:, lr0:lr0 + 2 * GLA_LR],
         jnp.zeros((D_MODEL, LR_W - 2 * GLA_LR), F32), wi[:, lr0 + 2 * GLA_LR + 5 * HEAD_W:]], axis=1).astype(BF16)
    wa_pad = jnp.zeros((LR_W, 2 * HEAD_W), F32)
    wa_pad = wa_pad.at[:GLA_LR, :HEAD_W].set(w_gla_a[l, 0])
    wa_pad = wa_pad.at[GLA_LR:2 * GLA_LR, HEAD_W:].set(w_gla_a[l, 1])
    w_router_pad = jnp.concatenate([w_router[l], jnp.zeros((D_MODEL, LANES - N_EXPERTS), F32)], axis=1)
    w_router_hi = w_router_pad.astype(BF16)
    row = lambda a: a[l].reshape(1, -1)
    return {
        "w_in_p": w_in_p,
        "wa_hi": wa_pad.astype(BF16), "wa_lo": (wa_pad - wa_pad.astype(BF16).astype(F32)).astype(BF16),
        "ba": jnp.concatenate([b_gla_a[l, 0], b_gla_a[l, 1]]).reshape(1, 2 * HEAD_W),
        "gla_norm_g": row(gla_norm_g), "sgu_norm_g": row(sgu_norm_g), "sgu_norm_b": row(sgu_norm_b),
        "w_sgu_cat": jnp.concatenate([w_sgu[l, g] for g in range(GMLP_GROUPS)], axis=1).astype(BF16),
        "b_sgu_full": jnp.repeat(b_sgu[l].T, HEAD_W // GMLP_GROUPS, axis=1),
        "w_conv": w_conv[l],
        "w_out": w_out[l].astype(BF16),
        "ln1_g": row(ln1_g), "ln1_b": row(ln1_b), "ln2_g": row(ln2_g), "ln2_b": row(ln2_b),
        "w_router_hi": w_router_hi, "w_router_lo": (w_router_pad - w_router_hi.astype(F32)).astype(BF16),
        "router_bias": router_bias[l].reshape(N_EXPERTS, 1),
        "w_exp_gate": w_exp_gate, "w_exp_up": w_exp_up, "w_exp_down": w_exp_down,
        "w_sh_gate": w_sh_gate[l].astype(BF16), "w_sh_up": w_sh_up[l].astype(BF16),
        "w_sh_down": w_sh_down[l].astype(BF16),
    }


def _state_to_blockdiag_t(s):
    bsz = s.shape[0]
    st = jnp.swapaxes(s, 2, 3)
    eye = jnp.eye(GLA_HEADS, dtype=s.dtype)
    return jnp.einsum("bhvd,hg->bhvgd", st, eye).reshape(bsz, HEAD_W, HEAD_W)


def _blockdiag_t_to_state(st):
    bsz = st.shape[0]
    s5 = st.reshape(bsz, GLA_HEADS, GLA_DK, GLA_HEADS, GLA_DK)
    diag = jnp.stack([s5[:, h, :, h, :] for h in range(GLA_HEADS)], axis=1)
    return jnp.swapaxes(diag, 2, 3)


def _pre_moe(x3, pos, mod3, mod_row, lw, st0, emit_final, tabs):
    bsz, n, _ = x3.shape
    t = bsz * n
    outs = _in_proj(x3.reshape(t, D_MODEL), pos, mod3, mod_row, lw["w_in_p"], tabs["cs"], tm=512)
    if pos is not None:
        proj, zr, zi, x2d = outs
    else:
        proj, zr, zi = outs
        x2d = x3.reshape(t, D_MODEL)
    gla_out = _gla(proj.reshape(bsz, n, STORE_W), lw["wa_hi"], lw["wa_lo"], lw["ba"], st0, emit_final)
    o_f, o_b = gla_out[:2]
    zr3 = zr.reshape(bsz, n, HEAD_W)
    zi3 = zi.reshape(bsz, n, HEAD_W)
    if "two_stage" in tabs:
        yft = _fft_two_stage(zr3, zi3, tabs["two_stage"])
    else:
        yft = _fft_direct(zr3, zi3, *tabs["direct"])
    x1, xw, wrow, lpos, tab, chunks = _mix_out(x2d, o_f.reshape(t, HEAD_W), o_b.reshape(t, HEAD_W), proj,
                                             yft.reshape(t, HEAD_W), mod3, mod_row, lw, seq_len=n, tm=MOE_TM)
    group = {"x1": x1, "xw": xw, "wrow": wrow, "lpos": lpos, "tab": tab, "chunks": chunks,
             "mod3": mod3, "mod_row": mod_row, "shape": (bsz, n, D_MODEL)}
    return group, gla_out[2:]


def _trunk_layer(x3, pos, mod3, mod_row, lw, layer, st0, emit_final, tabs):
    group, finals = _pre_moe(x3, pos, mod3, mod_row, lw, st0, emit_final, tabs)
    (x2,) = _moe_groups([group], lw, layer)
    return x2.reshape(group["shape"]), finals


def kernel(x_prompt, x_sample, c, state_gla, c_ctx, w_ada, b_ada, w_in, w_gla_a, b_gla_a, gla_norm_g, sgu_norm_g, sgu_norm_b, w_sgu, b_sgu, w_conv, w_out, ln1_g, ln1_b, ln2_g, ln2_b, w_router, router_bias, w_exp_gate, w_exp_up, w_exp_down, w_sh_gate, w_sh_up, w_sh_down):
    n_layers = w_ada.shape[0]
    bp, np_, _ = x_prompt.shape
    bs, ns, _ = x_sample.shape
    assert bs <= 7

    cond8 = jnp.concatenate([c_ctx[None, :], c, jnp.zeros((7 - bs, D_MODEL), F32)], axis=0)
    mod = _ada_mod(cond8, w_ada, b_ada)

    tabs_p = {"cs": jnp.asarray(_channel_dft_table(), BF16),
              "direct": tuple(jnp.asarray(a, BF16) for a in _direct_dft_tables(np_))}
    tabs_s = {"cs": tabs_p["cs"],
              "two_stage": tuple(jnp.asarray(a, BF16) for a in _two_stage_dft_tables(ns))}
    rtab, ctab = _grid_sincos_table(ns // GRID_W, D_MODEL)
    pos = jnp.concatenate([jnp.repeat(jnp.asarray(rtab), GRID_W, axis=0),
                           jnp.tile(jnp.asarray(ctab), (ns // GRID_W, 1))], axis=-1)

    prompt_row = lambda i, tm: 0
    sample_row = lambda i, tm: 1 + (i * tm) // ns

    y_p = x_prompt
    y_s = x_sample
    finals = []
    for l in range(n_layers):
        lw = _layer_weights(l, w_in, w_gla_a, b_gla_a, gla_norm_g, sgu_norm_g, sgu_norm_b, w_sgu, b_sgu,
                            w_conv, w_out, ln1_g, ln1_b, ln2_g, ln2_b, w_router, router_bias,
                            w_exp_gate, w_exp_up, w_exp_down, w_sh_gate, w_sh_up, w_sh_down)
        mod3 = mod[l].reshape(8, 1, 6 * D_MODEL)
        group_p, fin = _pre_moe(y_p, None, mod3, prompt_row, lw, None, True, tabs_p)
        finals.append(jnp.stack([_blockdiag_t_to_state(fin[0]), _blockdiag_t_to_state(fin[1])], axis=1))
        st0 = jnp.stack([_state_to_blockdiag_t(state_gla[:, l, 0]), _state_to_blockdiag_t(state_gla[:, l, 1])])
        group_s, _ = _pre_moe(y_s, pos if l == 0 else None, mod3, sample_row, lw, st0, False, tabs_s)
        y_p, y_s = _moe_groups([group_p, group_s], lw, l)
        y_p = y_p.reshape(group_p["shape"])
        y_s = y_s.reshape(group_s["shape"])
    new_state = jnp.stack(finals, axis=1).astype(x_prompt.dtype)
    return (y_p, y_s, new_state)
```
